```python
import jax, jax.numpy as jnp
from jax import lax
import numpy as np

D_MODEL = 2048
BATCH = 8
SEQ = 2048
DEPTH = 4

CHUNK = 64
N_MEM = 256
D_MIX = D_MODEL
DN_HEAD_DIM = 128
DN_HEADS = (D_MIX // 2) // DN_HEAD_DIM
DN_WIDTH = DN_HEADS * DN_HEAD_DIM
DN_CONV = 4
MLA_NOPE = 128
MLA_ROPE = 64
MLA_V = 128
MLA_HEADS = (D_MIX - DN_WIDTH) // MLA_V
MLA_Q_RANK = 512
MLA_KV_RANK = 256
ROPE_BASE = 10000.0
Q_BLOCK = 128
XA_HEADS = 4
XA_HEAD_DIM = D_MODEL // XA_HEADS
D_FF = ((8 * D_MODEL // 3 + 255) // 256) * 256
FFN_CONV = 3
EPS = 1e-6

kernel_name = "hybrid_deltanet_mla_memxattn_convffn"


def _in_split_sizes():
    return [DN_WIDTH, DN_WIDTH, DN_WIDTH, DN_WIDTH,
            DN_HEADS, DN_HEADS,
            MLA_Q_RANK,
            MLA_KV_RANK + MLA_ROPE]


def _in_cols():
    return sum(_in_split_sizes())


def _split_points():
    return [int(v) for v in np.cumsum(_in_split_sizes())[:-1]]


def rms_norm(x, gain):
    xf = x.astype(jnp.float32)
    y = xf * lax.rsqrt(jnp.mean(xf * xf, axis=-1, keepdims=True) + EPS)
    return (y * gain.astype(jnp.float32)).astype(x.dtype)


def l2_norm(x):
    xf = x.astype(jnp.float32)
    return xf * lax.rsqrt(jnp.sum(xf * xf, axis=-1, keepdims=True) + EPS)


def causal_dwconv(x, w):
    K, C = w.shape
    return lax.conv_general_dilated(
        x, w[:, None, :].astype(x.dtype), window_strides=(1,),
        padding=[(K - 1, 0)], dimension_numbers=("NWC", "WIO", "NWC"),
        feature_group_count=C)


def rope_cos_sin(positions):
    inv = ROPE_BASE ** (-jnp.arange(0, MLA_ROPE, 2, dtype=jnp.float32) / MLA_ROPE)
    ang = positions.astype(jnp.float32)[..., None] * inv
    return jnp.cos(ang), jnp.sin(ang)


def apply_rope(x, cos, sin):
    xf = x.astype(jnp.float32)
    x1, x2 = jnp.split(xf, 2, axis=-1)
    return jnp.concatenate([x1 * cos - x2 * sin, x2 * cos + x1 * sin], axis=-1).astype(x.dtype)


def chunk_gated_delta_rule(q, k, v, g, beta):
    B, S, H, Dk = q.shape
    Dv = v.shape[-1]
    N = S // CHUNK
    f32 = jnp.float32

    def chunks(t):
        return jnp.moveaxis(t.reshape(B, N, CHUNK, H, *t.shape[3:]), 3, 2)

    q = chunks(q.astype(f32)) * (Dk ** -0.5)
    k = chunks(k.astype(f32))
    v = chunks(v.astype(f32))
    beta = chunks(beta.astype(f32))
    G = jnp.cumsum(chunks(g.astype(f32)), axis=-1)

    incl = np.tril(np.ones((CHUNK, CHUNK), dtype=bool))
    strict = np.tril(np.ones((CHUNK, CHUNK), dtype=bool), -1)
    decay = jnp.exp(jnp.where(incl, G[..., :, None] - G[..., None, :], -jnp.inf))

    kb = k * beta[..., None]
    lower = jnp.where(strict, jnp.einsum("bnhik,bnhjk->bnhij", kb, k) * decay, 0.0)
    a_mat = lower + np.eye(CHUNK, dtype=np.float32)
    rhs = jnp.concatenate([v * beta[..., None], kb * jnp.exp(G)[..., None]], axis=-1)
    sol = lax.linalg.triangular_solve(a_mat, rhs, left_side=True, lower=True,
                                      unit_diagonal=True)
    u, w = sol[..., :Dv], sol[..., Dv:]

    attn = jnp.einsum("bnhik,bnhjk->bnhij", q, k) * decay
    q_dec = q * jnp.exp(G)[..., None]
    k_dec = k * jnp.exp(G[..., -1:] - G)[..., None]
    g_last = jnp.exp(G[..., -1])

    def step(state, xs):
        u_c, w_c, a_c, q_c, k_c, gl_c = xs
        v_new = u_c - jnp.einsum("bhck,bhkv->bhcv", w_c, state)
        o_c = (jnp.einsum("bhck,bhkv->bhcv", q_c, state)
               + jnp.einsum("bhij,bhjv->bhiv", a_c, v_new))
        state = state * gl_c[..., None, None] + jnp.einsum("bhck,bhcv->bhkv", k_c, v_new)
        return state, o_c

    xs = tuple(jnp.moveaxis(t, 1, 0) for t in (u, w, attn, q_dec, k_dec, g_last))
    state0 = jnp.zeros((B, H, Dk, Dv), f32)
    _, o = lax.scan(step, state0, xs)
    o = jnp.moveaxis(o, 0, 1)
    return jnp.moveaxis(o, 2, 3).reshape(B, S, H, Dv)


def gated_deltanet(q_raw, k_raw, v_raw, z, b, a, conv_w, a_log, dt_bias, out_norm):
    B, S, _ = q_raw.shape
    qkv = jax.nn.silu(causal_dwconv(jnp.concatenate([q_raw, k_raw, v_raw], axis=-1), conv_w))
    q, k, v = jnp.split(qkv, 3, axis=-1)
    q = l2_norm(q.reshape(B, S, DN_HEADS, DN_HEAD_DIM))
    k = l2_norm(k.reshape(B, S, DN_HEADS, DN_HEAD_DIM))
    v = v.reshape(B, S, DN_HEADS, DN_HEAD_DIM)
    beta = jax.nn.sigmoid(b.astype(jnp.float32))
    g = -jnp.exp(a_log.astype(jnp.float32)) * jax.nn.softplus(
        a.astype(jnp.float32) + dt_bias.astype(jnp.float32))
    o = chunk_gated_delta_rule(q, k, v, g, beta)
    zf = z.reshape(B, S, DN_HEADS, DN_HEAD_DIM).astype(jnp.float32)
    o = rms_norm(o, out_norm) * jax.nn.silu(zf)
    return o.reshape(B, S, DN_WIDTH).astype(q_raw.dtype)


def mla_attention(q_lat, kv_lat, q_norm, w_qb, kv_norm, w_kvb, cos, sin):
    B, S, _ = q_lat.shape
    q = (rms_norm(q_lat, q_norm) @ w_qb).reshape(B, S, MLA_HEADS, MLA_NOPE + MLA_ROPE)
    q_nope = q[..., :MLA_NOPE]
    q_pe = apply_rope(q[..., MLA_NOPE:], cos[:, :, None, :], sin[:, :, None, :])
    c_kv = kv_lat[..., :MLA_KV_RANK]
    k_pe = apply_rope(kv_lat[..., MLA_KV_RANK:], cos, sin)
    kv = (rms_norm(c_kv, kv_norm) @ w_kvb).reshape(B, S, MLA_HEADS, MLA_NOPE + MLA_V)
    k_nope, v = kv[..., :MLA_NOPE], kv[..., MLA_NOPE:]
    scale = (MLA_NOPE + MLA_ROPE) ** -0.5
    outs = []
    for blk in range(S // Q_BLOCK):
        q0 = blk * Q_BLOCK
        kend = q0 + Q_BLOCK
        s = (jnp.einsum("bqhd,bkhd->bhqk", q_nope[:, q0:kend], k_nope[:, :kend])
             + jnp.einsum("bqhr,bkr->bhqk", q_pe[:, q0:kend], k_pe[:, :kend]))
        s = s.astype(jnp.float32) * scale
        q_chunk = (q0 + np.arange(Q_BLOCK)) // CHUNK
        k_chunk = np.arange(kend) // CHUNK
        mask = k_chunk[None, :] <= q_chunk[:, None]
        p = jax.nn.softmax(jnp.where(mask, s, -jnp.inf), axis=-1).astype(v.dtype)
        outs.append(jnp.einsum("bhqk,bkhd->bqhd", p, v[:, :kend]))
    o = jnp.concatenate(outs, axis=1)
    return o.reshape(B, S, MLA_HEADS * MLA_V)


def memory_cross_attention(h, mem_n, wq, wk, wv, wo):
    B, S, _ = h.shape
    M = mem_n.shape[1]
    q = (h @ wq).reshape(B, S, XA_HEADS, XA_HEAD_DIM)
    k = (mem_n @ wk).reshape(B, M, XA_HEADS, XA_HEAD_DIM)
    v = (mem_n @ wv).reshape(B, M, XA_HEADS, XA_HEAD_DIM)
    s = jnp.einsum("bqhd,bmhd->bhqm", q, k).astype(jnp.float32) * (XA_HEAD_DIM ** -0.5)
    p = jax.nn.softmax(s, axis=-1).astype(v.dtype)
    o = jnp.einsum("bhqm,bmhd->bqhd", p, v).reshape(B, S, XA_HEADS * XA_HEAD_DIM)
    return o @ wo


def conv_ffn(h, w_up, conv_w, conv_b, w_down):
    u = causal_dwconv(h @ w_up, conv_w) + conv_b
    gate, up = jnp.split(u, 2, axis=-1)
    return (jax.nn.silu(gate) * up) @ w_down


def _fwd_setup_inputs(seed: int = 0) -> dict:
    key = jax.random.key(seed)
    ks = iter(jax.random.split(key, 40))
    f32 = jnp.float32

    def dense(shape, fan_in, scale=1.0):
        return jax.random.normal(next(ks), shape, f32) * (scale * fan_in ** -0.5)

    def gain(shape):
        return 1.0 + 0.01 * jax.random.normal(next(ks), shape, f32)

    x = jax.random.normal(next(ks), (BATCH, SEQ, D_MODEL), f32)
    mem = jax.random.normal(next(ks), (BATCH, N_MEM, D_MODEL), f32)
    offsets = jax.random.randint(next(ks), (BATCH, 1), 0, 64) * CHUNK
    positions = (offsets + jnp.arange(SEQ, dtype=jnp.int32)[None, :]).astype(jnp.int32)

    a_log = jnp.log(jax.random.uniform(next(ks), (DEPTH, DN_HEADS), f32, 1.0, 16.0))
    dt = jnp.exp(jax.random.uniform(next(ks), (DEPTH, DN_HEADS), f32,
                                    float(np.log(1e-3)), float(np.log(1e-1))))
    dt_bias = dt + jnp.log(-jnp.expm1(-dt))
    out_scale = 0.5
    return {
        "x": x,
        "mem": mem,
        "positions": positions,
        "norm_mix": gain((DEPTH, D_MODEL)),
        "w_in": dense((DEPTH, D_MODEL, _in_cols()), D_MODEL),
        "dn_conv": dense((DEPTH, DN_CONV, 3 * DN_WIDTH), DN_CONV),
        "dn_a_log": a_log,
        "dn_dt_bias": dt_bias,
        "dn_out_norm": gain((DEPTH, DN_HEAD_DIM)),
        "mla_q_norm": gain((DEPTH, MLA_Q_RANK)),
        "mla_w_qb": dense((DEPTH, MLA_Q_RANK, MLA_HEADS * (MLA_NOPE + MLA_ROPE)), MLA_Q_RANK),
        "mla_kv_norm": gain((DEPTH, MLA_KV_RANK)),
        "mla_w_kvb": dense((DEPTH, MLA_KV_RANK, MLA_HEADS * (MLA_NOPE + MLA_V)), MLA_KV_RANK),
        "w_out": dense((DEPTH, D_MIX, D_MODEL), D_MIX, out_scale),
        "mem_norm": gain((D_MODEL,)),
        "norm_xattn": gain((DEPTH, D_MODEL)),
        "xa_wq": dense((DEPTH, D_MODEL, XA_HEADS * XA_HEAD_DIM), D_MODEL),
        "xa_wk": dense((DEPTH, D_MODEL, XA_HEADS * XA_HEAD_DIM), D_MODEL),
        "xa_wv": dense((DEPTH, D_MODEL, XA_HEADS * XA_HEAD_DIM), D_MODEL),
        "xa_wo": dense((DEPTH, XA_HEADS * XA_HEAD_DIM, D_MODEL), D_MODEL, out_scale),
        "norm_ffn": gain((DEPTH, D_MODEL)),
        "ffn_w_up": dense((DEPTH, D_MODEL, 2 * D_FF), D_MODEL),
        "ffn_conv": dense((DEPTH, FFN_CONV, 2 * D_FF), FFN_CONV),
        "ffn_conv_bias": 0.01 * jax.random.normal(next(ks), (DEPTH, 2 * D_FF), f32),
        "ffn_w_down": dense((DEPTH, D_FF, D_MODEL), D_FF, out_scale),
        "norm_final": gain((D_MODEL,)),
    }


def _fwd_reference(x, mem, positions, norm_mix, w_in, dn_conv, dn_a_log, dn_dt_bias,
              dn_out_norm, mla_q_norm, mla_w_qb, mla_kv_norm, mla_w_kvb, w_out,
              mem_norm, norm_xattn, xa_wq, xa_wk, xa_wv, xa_wo, norm_ffn,
              ffn_w_up, ffn_conv, ffn_conv_bias, ffn_w_down, norm_final):
    cos, sin = rope_cos_sin(positions)
    mem_n = rms_norm(mem, mem_norm)
    split_points = _split_points()
    h = x
    for l in range(DEPTH):
        u = rms_norm(h, norm_mix[l])
        proj = u @ w_in[l]
        dq, dk, dv, dz, db, da, mq, mkv = jnp.split(proj, split_points, axis=-1)
        o_dn = gated_deltanet(dq, dk, dv, dz, db, da, dn_conv[l], dn_a_log[l],
                              dn_dt_bias[l], dn_out_norm[l])
        o_mla = mla_attention(mq, mkv, mla_q_norm[l], mla_w_qb[l], mla_kv_norm[l],
                              mla_w_kvb[l], cos, sin)
        h = h + jnp.concatenate([o_dn.astype(h.dtype), o_mla.astype(h.dtype)], axis=-1) @ w_out[l]
        h = h + memory_cross_attention(rms_norm(h, norm_xattn[l]), mem_n, xa_wq[l],
                                       xa_wk[l], xa_wv[l], xa_wo[l])
        h = h + conv_ffn(rms_norm(h, norm_ffn[l]), ffn_w_up[l], ffn_conv[l],
                         ffn_conv_bias[l], ffn_w_down[l])
    return rms_norm(h, norm_final)


import jax as _jax
import jax.numpy as _jnp

TWIN_FORMAT = 'train_step'
FWD_PARAMS = ['x', 'mem', 'positions', 'norm_mix', 'w_in', 'dn_conv', 'dn_a_log', 'dn_dt_bias', 'dn_out_norm', 'mla_q_norm', 'mla_w_qb', 'mla_kv_norm', 'mla_w_kvb', 'w_out', 'mem_norm', 'norm_xattn', 'xa_wq', 'xa_wk', 'xa_wv', 'xa_wo', 'norm_ffn', 'ffn_w_up', 'ffn_conv', 'ffn_conv_bias', 'ffn_w_down', 'norm_final']
TWIN_WEIGHTS = ['norm_mix', 'w_in', 'dn_conv', 'dn_a_log', 'dn_dt_bias', 'dn_out_norm', 'mla_q_norm', 'mla_w_qb', 'mla_kv_norm', 'mla_w_kvb', 'w_out', 'mem_norm', 'norm_xattn', 'xa_wq', 'xa_wk', 'xa_wv', 'xa_wo', 'norm_ffn', 'ffn_w_up', 'ffn_conv', 'ffn_conv_bias', 'ffn_w_down', 'norm_final']
TWIN_DIFF_INPUT = 'x'
TWIN_INPUTS = ['x', 'mem', 'positions', 'norm_mix', 'w_in', 'dn_conv', 'dn_a_log', 'dn_dt_bias', 'dn_out_norm', 'mla_q_norm', 'mla_w_qb', 'mla_kv_norm', 'mla_w_kvb', 'w_out', 'mem_norm', 'norm_xattn', 'xa_wq', 'xa_wk', 'xa_wv', 'xa_wo', 'norm_ffn', 'ffn_w_up', 'ffn_conv', 'ffn_conv_bias', 'ffn_w_down', 'norm_final', 'loss_target', 'm_norm_mix', 'm_w_in', 'm_dn_conv', 'm_dn_a_log', 'm_dn_dt_bias', 'm_dn_out_norm', 'm_mla_q_norm', 'm_mla_w_qb', 'm_mla_kv_norm', 'm_mla_w_kvb', 'm_w_out', 'm_mem_norm', 'm_norm_xattn', 'm_xa_wq', 'm_xa_wk', 'm_xa_wv', 'm_xa_wo', 'm_norm_ffn', 'm_ffn_w_up', 'm_ffn_conv', 'm_ffn_conv_bias', 'm_ffn_w_down', 'm_norm_final', 'v_norm_mix', 'v_w_in', 'v_dn_conv', 'v_dn_a_log', 'v_dn_dt_bias', 'v_dn_out_norm', 'v_mla_q_norm', 'v_mla_w_qb', 'v_mla_kv_norm', 'v_mla_w_kvb', 'v_w_out', 'v_mem_norm', 'v_norm_xattn', 'v_xa_wq', 'v_xa_wk', 'v_xa_wv', 'v_xa_wo', 'v_norm_ffn', 'v_ffn_w_up', 'v_ffn_conv', 'v_ffn_conv_bias', 'v_ffn_w_down', 'v_norm_final']
TWIN_OUTPUTS = ['loss', 'grad_x', 'grad_norm_mix', 'grad_w_in', 'grad_dn_conv', 'grad_dn_a_log', 'grad_dn_dt_bias', 'grad_dn_out_norm', 'grad_mla_q_norm', 'grad_mla_w_qb', 'grad_mla_kv_norm', 'grad_mla_w_kvb', 'grad_w_out', 'grad_mem_norm', 'grad_norm_xattn', 'grad_xa_wq', 'grad_xa_wk', 'grad_xa_wv', 'grad_xa_wo', 'grad_norm_ffn', 'grad_ffn_w_up', 'grad_ffn_conv', 'grad_ffn_conv_bias', 'grad_ffn_w_down', 'grad_norm_final', 'delta_norm_mix', 'delta_w_in', 'delta_dn_conv', 'delta_dn_a_log', 'delta_dn_dt_bias', 'delta_dn_out_norm', 'delta_mla_q_norm', 'delta_mla_w_qb', 'delta_mla_kv_norm', 'delta_mla_w_kvb', 'delta_w_out', 'delta_mem_norm', 'delta_norm_xattn', 'delta_xa_wq', 'delta_xa_wk', 'delta_xa_wv', 'delta_xa_wo', 'delta_norm_ffn', 'delta_ffn_w_up', 'delta_ffn_conv', 'delta_ffn_conv_bias', 'delta_ffn_w_down', 'delta_norm_final', 'new_m_norm_mix', 'new_m_w_in', 'new_m_dn_conv', 'new_m_dn_a_log', 'new_m_dn_dt_bias', 'new_m_dn_out_norm', 'new_m_mla_q_norm', 'new_m_mla_w_qb', 'new_m_mla_kv_norm', 'new_m_mla_w_kvb', 'new_m_w_out', 'new_m_mem_norm', 'new_m_norm_xattn', 'new_m_xa_wq', 'new_m_xa_wk', 'new_m_xa_wv', 'new_m_xa_wo', 'new_m_norm_ffn', 'new_m_ffn_w_up', 'new_m_ffn_conv', 'new_m_ffn_conv_bias', 'new_m_ffn_w_down', 'new_m_norm_final', 'new_v_norm_mix', 'new_v_w_in', 'new_v_dn_conv', 'new_v_dn_a_log', 'new_v_dn_dt_bias', 'new_v_dn_out_norm', 'new_v_mla_q_norm', 'new_v_mla_w_qb', 'new_v_mla_kv_norm', 'new_v_mla_w_kvb', 'new_v_w_out', 'new_v_mem_norm', 'new_v_norm_xattn', 'new_v_xa_wq', 'new_v_xa_wk', 'new_v_xa_wv', 'new_v_xa_wo', 'new_v_norm_ffn', 'new_v_ffn_w_up', 'new_v_ffn_conv', 'new_v_ffn_conv_bias', 'new_v_ffn_w_down', 'new_v_norm_final']
TWIN_LEAF_KINDS = {'loss': 'loss', 'grad_x': 'grad_x', 'grad_norm_mix': 'grad_w', 'grad_w_in': 'grad_w', 'grad_dn_conv': 'grad_w', 'grad_dn_a_log': 'grad_w', 'grad_dn_dt_bias': 'grad_w', 'grad_dn_out_norm': 'grad_w', 'grad_mla_q_norm': 'grad_w', 'grad_mla_w_qb': 'grad_w', 'grad_mla_kv_norm': 'grad_w', 'grad_mla_w_kvb': 'grad_w', 'grad_w_out': 'grad_w', 'grad_mem_norm': 'grad_w', 'grad_norm_xattn': 'grad_w', 'grad_xa_wq': 'grad_w', 'grad_xa_wk': 'grad_w', 'grad_xa_wv': 'grad_w', 'grad_xa_wo': 'grad_w', 'grad_norm_ffn': 'grad_w', 'grad_ffn_w_up': 'grad_w', 'grad_ffn_conv': 'grad_w', 'grad_ffn_conv_bias': 'grad_w', 'grad_ffn_w_down': 'grad_w', 'grad_norm_final': 'grad_w', 'delta_norm_mix': 'delta_w', 'delta_w_in': 'delta_w', 'delta_dn_conv': 'delta_w', 'delta_dn_a_log': 'delta_w', 'delta_dn_dt_bias': 'delta_w', 'delta_dn_out_norm': 'delta_w', 'delta_mla_q_norm': 'delta_w', 'delta_mla_w_qb': 'delta_w', 'delta_mla_kv_norm': 'delta_w', 'delta_mla_w_kvb': 'delta_w', 'delta_w_out': 'delta_w', 'delta_mem_norm': 'delta_w', 'delta_norm_xattn': 'delta_w', 'delta_xa_wq': 'delta_w', 'delta_xa_wk': 'delta_w', 'delta_xa_wv': 'delta_w', 'delta_xa_wo': 'delta_w', 'delta_norm_ffn': 'delta_w', 'delta_ffn_w_up': 'delta_w', 'delta_ffn_conv': 'delta_w', 'delta_ffn_conv_bias': 'delta_w', 'delta_ffn_w_down': 'delta_w', 'delta_norm_final': 'delta_w', 'new_m_norm_mix': 'new_m', 'new_m_w_in': 'new_m', 'new_m_dn_conv': 'new_m', 'new_m_dn_a_log': 'new_m', 'new_m_dn_dt_bias': 'new_m', 'new_m_dn_out_norm': 'new_m', 'new_m_mla_q_norm': 'new_m', 'new_m_mla_w_qb': 'new_m', 'new_m_mla_kv_norm': 'new_m', 'new_m_mla_w_kvb': 'new_m', 'new_m_w_out': 'new_m', 'new_m_mem_norm': 'new_m', 'new_m_norm_xattn': 'new_m', 'new_m_xa_wq': 'new_m', 'new_m_xa_wk': 'new_m', 'new_m_xa_wv': 'new_m', 'new_m_xa_wo': 'new_m', 'new_m_norm_ffn': 'new_m', 'new_m_ffn_w_up': 'new_m', 'new_m_ffn_conv': 'new_m', 'new_m_ffn_conv_bias': 'new_m', 'new_m_ffn_w_down': 'new_m', 'new_m_norm_final': 'new_m', 'new_v_norm_mix': 'new_v', 'new_v_w_in': 'new_v', 'new_v_dn_conv': 'new_v', 'new_v_dn_a_log': 'new_v', 'new_v_dn_dt_bias': 'new_v', 'new_v_dn_out_norm': 'new_v', 'new_v_mla_q_norm': 'new_v', 'new_v_mla_w_qb': 'new_v', 'new_v_mla_kv_norm': 'new_v', 'new_v_mla_w_kvb': 'new_v', 'new_v_w_out': 'new_v', 'new_v_mem_norm': 'new_v', 'new_v_norm_xattn': 'new_v', 'new_v_xa_wq': 'new_v', 'new_v_xa_wk': 'new_v', 'new_v_xa_wv': 'new_v', 'new_v_xa_wo': 'new_v', 'new_v_norm_ffn': 'new_v', 'new_v_ffn_w_up': 'new_v', 'new_v_ffn_conv': 'new_v', 'new_v_ffn_conv_bias': 'new_v', 'new_v_ffn_w_down': 'new_v', 'new_v_norm_final': 'new_v'}


def _forward(args):
    return _fwd_reference(*[args[k] for k in FWD_PARAMS])


def _output_shape():
    out = _jax.eval_shape(lambda: _forward(_fwd_setup_inputs(0)))
    return out.shape, out.dtype

N_MICROBATCH = 1
ADAM_LR = 0.001
ADAM_B1 = 0.9
ADAM_B2 = 0.999
ADAM_EPS = 1e-08
ADAM_WD = 0.01
ADAM_STEP = 10
PER_EXAMPLE_BATCH_AXIS = {'x': 0, 'mem': 0, 'positions': 0, 'loss_target': 0}
SHARED_INPUTS = []
_WEIGHT_DTYPES = {'norm_mix': _jnp.float32, 'w_in': _jnp.float32, 'dn_conv': _jnp.float32, 'dn_a_log': _jnp.float32, 'dn_dt_bias': _jnp.float32, 'dn_out_norm': _jnp.float32, 'mla_q_norm': _jnp.float32, 'mla_w_qb': _jnp.float32, 'mla_kv_norm': _jnp.float32, 'mla_w_kvb': _jnp.float32, 'w_out': _jnp.float32, 'mem_norm': _jnp.float32, 'norm_xattn': _jnp.float32, 'xa_wq': _jnp.float32, 'xa_wk': _jnp.float32, 'xa_wv': _jnp.float32, 'xa_wo': _jnp.float32, 'norm_ffn': _jnp.float32, 'ffn_w_up': _jnp.float32, 'ffn_conv': _jnp.float32, 'ffn_conv_bias': _jnp.float32, 'ffn_w_down': _jnp.float32, 'norm_final': _jnp.float32}
MOMENT_SCALE = {'norm_mix': 2.335187e-02, 'w_in': 1.507084e-02, 'dn_conv': 1.488800e-02, 'dn_a_log': 9.161842e-02, 'dn_dt_bias': 8.830010e-02, 'dn_out_norm': 5.356985e-02, 'mla_q_norm': 6.644281e-03, 'mla_w_qb': 3.880796e-03, 'mla_kv_norm': 1.508003e-02, 'mla_w_kvb': 4.821258e-03, 'w_out': 2.787583e-02, 'mem_norm': 9.888597e-03, 'norm_xattn': 3.324135e-03, 'xa_wq': 3.308198e-03, 'xa_wk': 3.307126e-03, 'xa_wv': 3.478493e-03, 'xa_wo': 6.957884e-03, 'norm_ffn': 2.632035e-02, 'ffn_w_up': 1.132741e-02, 'ffn_conv': 1.132152e-02, 'ffn_conv_bias': 1.120409e-02, 'ffn_w_down': 3.699500e-02, 'norm_final': 7.993137e+00}


def _to_microbatches(a, axis):
    t = _jnp.moveaxis(a, axis, 0)
    t = t.reshape((N_MICROBATCH, t.shape[0] // N_MICROBATCH) + t.shape[1:])
    return _jnp.moveaxis(t, 1, axis + 1)


def setup_inputs(seed: int = 0) -> dict:
    inp = _fwd_setup_inputs(seed)
    key = _jax.random.fold_in(_jax.random.key(seed), 7919)
    shape, _ = _output_shape()
    out = dict(inp)
    out["loss_target"] = _jax.random.normal(_jax.random.fold_in(key, 0), shape, _jnp.float32)
    for i, name in enumerate(TWIN_WEIGHTS):
        w = inp[name].astype(_jnp.float32)
        if MOMENT_SCALE is None:
            s = _jnp.sqrt(_jnp.mean(_jnp.square(w)) + 1e-30)
        else:
            s = MOMENT_SCALE[name]
        km, kv = _jax.random.split(_jax.random.fold_in(key, i + 1))
        out[name] = w
        out["m_" + name] = s * _jax.random.normal(km, w.shape, _jnp.float32)
        out["v_" + name] = (s * s) * _jax.random.uniform(kv, w.shape, _jnp.float32, 0.5, 1.5)
    if N_MICROBATCH > 1:
        for name, axis in PER_EXAMPLE_BATCH_AXIS.items():
            out[name] = _to_microbatches(out[name], axis)
    return {'x': out['x'], 'mem': out['mem'], 'positions': out['positions'], 'norm_mix': out['norm_mix'], 'w_in': out['w_in'], 'dn_conv': out['dn_conv'], 'dn_a_log': out['dn_a_log'], 'dn_dt_bias': out['dn_dt_bias'], 'dn_out_norm': out['dn_out_norm'], 'mla_q_norm': out['mla_q_norm'], 'mla_w_qb': out['mla_w_qb'], 'mla_kv_norm': out['mla_kv_norm'], 'mla_w_kvb': out['mla_w_kvb'], 'w_out': out['w_out'], 'mem_norm': out['mem_norm'], 'norm_xattn': out['norm_xattn'], 'xa_wq': out['xa_wq'], 'xa_wk': out['xa_wk'], 'xa_wv': out['xa_wv'], 'xa_wo': out['xa_wo'], 'norm_ffn': out['norm_ffn'], 'ffn_w_up': out['ffn_w_up'], 'ffn_conv': out['ffn_conv'], 'ffn_conv_bias': out['ffn_conv_bias'], 'ffn_w_down': out['ffn_w_down'], 'norm_final': out['norm_final'], 'loss_target': out['loss_target'], 'm_norm_mix': out['m_norm_mix'], 'm_w_in': out['m_w_in'], 'm_dn_conv': out['m_dn_conv'], 'm_dn_a_log': out['m_dn_a_log'], 'm_dn_dt_bias': out['m_dn_dt_bias'], 'm_dn_out_norm': out['m_dn_out_norm'], 'm_mla_q_norm': out['m_mla_q_norm'], 'm_mla_w_qb': out['m_mla_w_qb'], 'm_mla_kv_norm': out['m_mla_kv_norm'], 'm_mla_w_kvb': out['m_mla_w_kvb'], 'm_w_out': out['m_w_out'], 'm_mem_norm': out['m_mem_norm'], 'm_norm_xattn': out['m_norm_xattn'], 'm_xa_wq': out['m_xa_wq'], 'm_xa_wk': out['m_xa_wk'], 'm_xa_wv': out['m_xa_wv'], 'm_xa_wo': out['m_xa_wo'], 'm_norm_ffn': out['m_norm_ffn'], 'm_ffn_w_up': out['m_ffn_w_up'], 'm_ffn_conv': out['m_ffn_conv'], 'm_ffn_conv_bias': out['m_ffn_conv_bias'], 'm_ffn_w_down': out['m_ffn_w_down'], 'm_norm_final': out['m_norm_final'], 'v_norm_mix': out['v_norm_mix'], 'v_w_in': out['v_w_in'], 'v_dn_conv': out['v_dn_conv'], 'v_dn_a_log': out['v_dn_a_log'], 'v_dn_dt_bias': out['v_dn_dt_bias'], 'v_dn_out_norm': out['v_dn_out_norm'], 'v_mla_q_norm': out['v_mla_q_norm'], 'v_mla_w_qb': out['v_mla_w_qb'], 'v_mla_kv_norm': out['v_mla_kv_norm'], 'v_mla_w_kvb': out['v_mla_w_kvb'], 'v_w_out': out['v_w_out'], 'v_mem_norm': out['v_mem_norm'], 'v_norm_xattn': out['v_norm_xattn'], 'v_xa_wq': out['v_xa_wq'], 'v_xa_wk': out['v_xa_wk'], 'v_xa_wv': out['v_xa_wv'], 'v_xa_wo': out['v_xa_wo'], 'v_norm_ffn': out['v_norm_ffn'], 'v_ffn_w_up': out['v_ffn_w_up'], 'v_ffn_conv': out['v_ffn_conv'], 'v_ffn_conv_bias': out['v_ffn_conv_bias'], 'v_ffn_w_down': out['v_ffn_w_down'], 'v_norm_final': out['v_norm_final']}


def _loss(weights, diff, rest, loss_target):
    with _jax.named_scope("forward"):
        args = {**rest, TWIN_DIFF_INPUT: diff, **{k: w.astype(_WEIGHT_DTYPES[k]) for k, w in weights.items()}}
        y = _forward(args)
    with _jax.named_scope("loss_head"):
        err = _jnp.square(y.astype(_jnp.float32) - loss_target)
        return 0.5 * _jnp.sum(_jnp.mean(err, axis=-1)) if err.ndim else 0.5 * err


def _adamw(w, g, m, v):
    m = ADAM_B1 * m + (1.0 - ADAM_B1) * g
    v = ADAM_B2 * v + (1.0 - ADAM_B2) * _jnp.square(g)
    m_hat = m / (1.0 - ADAM_B1 ** ADAM_STEP)
    v_hat = v / (1.0 - ADAM_B2 ** ADAM_STEP)
    delta = -ADAM_LR * (m_hat / (_jnp.sqrt(v_hat) + ADAM_EPS) + ADAM_WD * w)
    return delta, m, v


def reference(x, mem, positions, norm_mix, w_in, dn_conv, dn_a_log, dn_dt_bias, dn_out_norm, mla_q_norm, mla_w_qb, mla_kv_norm, mla_w_kvb, w_out, mem_norm, norm_xattn, xa_wq, xa_wk, xa_wv, xa_wo, norm_ffn, ffn_w_up, ffn_conv, ffn_conv_bias, ffn_w_down, norm_final, loss_target, m_norm_mix, m_w_in, m_dn_conv, m_dn_a_log, m_dn_dt_bias, m_dn_out_norm, m_mla_q_norm, m_mla_w_qb, m_mla_kv_norm, m_mla_w_kvb, m_w_out, m_mem_norm, m_norm_xattn, m_xa_wq, m_xa_wk, m_xa_wv, m_xa_wo, m_norm_ffn, m_ffn_w_up, m_ffn_conv, m_ffn_conv_bias, m_ffn_w_down, m_norm_final, v_norm_mix, v_w_in, v_dn_conv, v_dn_a_log, v_dn_dt_bias, v_dn_out_norm, v_mla_q_norm, v_mla_w_qb, v_mla_kv_norm, v_mla_w_kvb, v_w_out, v_mem_norm, v_norm_xattn, v_xa_wq, v_xa_wk, v_xa_wv, v_xa_wo, v_norm_ffn, v_ffn_w_up, v_ffn_conv, v_ffn_conv_bias, v_ffn_w_down, v_norm_final):
    given = dict(x=x, mem=mem, positions=positions, norm_mix=norm_mix, w_in=w_in, dn_conv=dn_conv, dn_a_log=dn_a_log, dn_dt_bias=dn_dt_bias, dn_out_norm=dn_out_norm, mla_q_norm=mla_q_norm, mla_w_qb=mla_w_qb, mla_kv_norm=mla_kv_norm, mla_w_kvb=mla_w_kvb, w_out=w_out, mem_norm=mem_norm, norm_xattn=norm_xattn, xa_wq=xa_wq, xa_wk=xa_wk, xa_wv=xa_wv, xa_wo=xa_wo, norm_ffn=norm_ffn, ffn_w_up=ffn_w_up, ffn_conv=ffn_conv, ffn_conv_bias=ffn_conv_bias, ffn_w_down=ffn_w_down, norm_final=norm_final, loss_target=loss_target, m_norm_mix=m_norm_mix, m_w_in=m_w_in, m_dn_conv=m_dn_conv, m_dn_a_log=m_dn_a_log, m_dn_dt_bias=m_dn_dt_bias, m_dn_out_norm=m_dn_out_norm, m_mla_q_norm=m_mla_q_norm, m_mla_w_qb=m_mla_w_qb, m_mla_kv_norm=m_mla_kv_norm, m_mla_w_kvb=m_mla_w_kvb, m_w_out=m_w_out, m_mem_norm=m_mem_norm, m_norm_xattn=m_norm_xattn, m_xa_wq=m_xa_wq, m_xa_wk=m_xa_wk, m_xa_wv=m_xa_wv, m_xa_wo=m_xa_wo, m_norm_ffn=m_norm_ffn, m_ffn_w_up=m_ffn_w_up, m_ffn_conv=m_ffn_conv, m_ffn_conv_bias=m_ffn_conv_bias, m_ffn_w_down=m_ffn_w_down, m_norm_final=m_norm_final, v_norm_mix=v_norm_mix, v_w_in=v_w_in, v_dn_conv=v_dn_conv, v_dn_a_log=v_dn_a_log, v_dn_dt_bias=v_dn_dt_bias, v_dn_out_norm=v_dn_out_norm, v_mla_q_norm=v_mla_q_norm, v_mla_w_qb=v_mla_w_qb, v_mla_kv_norm=v_mla_kv_norm, v_mla_w_kvb=v_mla_w_kvb, v_w_out=v_w_out, v_mem_norm=v_mem_norm, v_norm_xattn=v_norm_xattn, v_xa_wq=v_xa_wq, v_xa_wk=v_xa_wk, v_xa_wv=v_xa_wv, v_xa_wo=v_xa_wo, v_norm_ffn=v_norm_ffn, v_ffn_w_up=v_ffn_w_up, v_ffn_conv=v_ffn_conv, v_ffn_conv_bias=v_ffn_conv_bias, v_ffn_w_down=v_ffn_w_down, v_norm_final=v_norm_final)
    weights = {n: given[n] for n in TWIN_WEIGHTS}
    shared = {n: given[n] for n in SHARED_INPUTS}
    per_example = {n: given[n] for n in ['x', 'mem', 'positions']}
    grad_fn = _jax.value_and_grad(_loss, argnums=(0, 1))

    def one_microbatch(ex, loss_target):
        ex = dict(ex)
        diff = ex.pop(TWIN_DIFF_INPUT)
        return grad_fn(weights, diff, {**shared, **ex}, loss_target)

    if N_MICROBATCH == 1:
        loss, (grad_w, grad_x) = one_microbatch(per_example, given["loss_target"])
    else:
        def body(carry, xs):
            loss_sum, grad_sum = carry
            l_k, (gw_k, gx_k) = one_microbatch(xs[0], xs[1])
            with _jax.named_scope("update"):
                return (loss_sum + l_k, _jax.tree.map(_jnp.add, grad_sum, gw_k)), gx_k

        init = (_jnp.zeros((), _jnp.float32), _jax.tree.map(_jnp.zeros_like, weights))
        (loss, grad_w), grad_x = _jax.lax.scan(body, init, (per_example, given["loss_target"]))
    with _jax.named_scope("update"):
        delta_w, new_m, new_v = {}, {}, {}
        for n in TWIN_WEIGHTS:
            delta_w[n], new_m[n], new_v[n] = _adamw(weights[n], grad_w[n], given["m_" + n], given["v_" + n])
    return (loss, grad_x, *[grad_w[n] for n in TWIN_WEIGHTS], *[delta_w[n] for n in TWIN_WEIGHTS],
            *[new_m[n] for n in TWIN_WEIGHTS], *[new_v[n] for n in TWIN_WEIGHTS])
```

```python
import functools

import numpy as np
import jax
import jax.numpy as jnp
from jax import lax
from jax.experimental import pallas as pl
from jax.experimental.pallas import tpu as pltpu

F32 = jnp.float32
BF16 = jnp.bfloat16
_MXU_DTYPE = BF16
_INTERPRET = False
_VMEM_LIMIT_BYTES = 48 * 1024 * 1024
_HI = lax.Precision.HIGHEST

N_DEV = 8
CHUNK = 64
_CHUNK_SHIFT = 6
DN_HEAD_DIM = 128
DN_CONV = 4
MLA_NOPE = 128
MLA_ROPE = 64
MLA_V = 128
MLA_Q_RANK = 512
MLA_KV_RANK = 256
ROPE_BASE = 10000.0
XA_HEADS = 4
FFN_CONV = 3
EPS = 1e-6
LANE = 128

ADAM_LR = 0.001
ADAM_B1 = 0.9
ADAM_B2 = 0.999
ADAM_EPS = 1e-08
ADAM_WD = 0.01
ADAM_STEP = 10
_ADAM_TILE_ELEMS = 128 * 1024

MESH = pl.DeviceIdType.MESH


def _pcall(body, *, name, out_shape, grid=None, in_specs=None, out_specs=None, scratch=(), sem=None, **kw):
    params = pltpu.CompilerParams(dimension_semantics=sem, vmem_limit_bytes=_VMEM_LIMIT_BYTES)
    args = dict(name=name, out_shape=out_shape, scratch_shapes=list(scratch), compiler_params=params,
                interpret=_INTERPRET, **kw)
    if grid is not None:
        args.update(grid=grid)
    if in_specs is not None:
        args.update(in_specs=in_specs)
    if out_specs is not None:
        args.update(out_specs=out_specs)
    return pl.pallas_call(body, **args)


def _tile(n, pref):
    if n <= pref:
        return n
    t = pref
    while t >= 8:
        if n % t == 0:
            return t
        t //= 2
    return n


def _ein_raw(spec, a, b, hi):
    if hi:
        return jnp.einsum(spec, a.astype(F32), b.astype(F32), precision=_HI, preferred_element_type=F32)
    return jnp.einsum(spec, a.astype(_MXU_DTYPE), b.astype(_MXU_DTYPE), preferred_element_type=F32)


def _make_ein(spec, hi=False, diff_b=True):
    a_s, rest = spec.split(",")
    b_s, o_s = rest.split("->")

    @jax.custom_vjp
    def f(a, b):
        return _ein_raw(spec, a, b, hi)

    def fwd(a, b):
        return f(a, b), (a, b)

    def bwd(res, g):
        a, b = res
        da = _ein_raw(f"{o_s},{b_s}->{a_s}", g, b, hi)
        if not diff_b:
            return da.astype(a.dtype), None
        db = _ein_raw(f"{a_s},{o_s}->{b_s}", a, g, hi)
        return da.astype(a.dtype), db.astype(b.dtype)

    f.defvjp(fwd, bwd)
    return f


_nt = _make_ein("qd,kd->qk")
_nn = _make_ein("qk,kd->qd")
_nn_hi_const = _make_ein("qk,kd->qd", hi=True, diff_b=False)
_bnt = _make_ein("hik,hjk->hij")
_bnt_hi = _make_ein("hik,hjk->hij", hi=True)
_bnn = _make_ein("hij,hjv->hiv")
_bnn_hi = _make_ein("hij,hjv->hiv", hi=True)
_btn = _make_ein("hck,hcv->hkv")


def _mm(a, b, *, ta=False, tb=False, res=None, out_dtype=F32, name="mm"):
    m = a.shape[1] if ta else a.shape[0]
    k = a.shape[0] if ta else a.shape[1]
    n = b.shape[0] if tb else b.shape[1]
    assert (b.shape[1] if tb else b.shape[0]) == k, (a.shape, b.shape, ta, tb)
    tm, tn, tk = _tile(m, 1024), _tile(n, 1024), _tile(k, 512)
    nk = k // tk
    dims = (((0 if ta else 1,), (1 if tb else 0,)), ((), ()))
    has_res = res is not None

    def body(*refs):
        if has_res:
            a_ref, b_ref, r_ref, o_ref, acc = refs
        else:
            a_ref, b_ref, o_ref, acc = refs
        kk = pl.program_id(2)

        @pl.when(kk == 0)
        def _():
            acc[...] = jnp.zeros_like(acc)

        acc[...] += lax.dot_general(a_ref[...].astype(_MXU_DTYPE), b_ref[...].astype(_MXU_DTYPE), dims,
                                    preferred_element_type=F32)

        @pl.when(kk == nk - 1)
        def _():
            r = acc[...]
            if has_res:
                r = r + r_ref[...].astype(F32)
            o_ref[...] = r.astype(out_dtype)

    a_spec = pl.BlockSpec((tk, tm), lambda i, j, kk: (kk, i)) if ta else pl.BlockSpec((tm, tk), lambda i, j, kk: (i, kk))
    b_spec = pl.BlockSpec((tn, tk), lambda i, j, kk: (j, kk)) if tb else pl.BlockSpec((tk, tn), lambda i, j, kk: (kk, j))
    o_spec = pl.BlockSpec((tm, tn), lambda i, j, kk: (i, j))
    in_specs = [a_spec, b_spec] + ([o_spec] if has_res else [])
    args = (a, b) + ((res,) if has_res else ())
    return _pcall(body, name=name, out_shape=jax.ShapeDtypeStruct((m, n), out_dtype),
                  grid=(m // tm, n // tn, nk), in_specs=in_specs, out_specs=o_spec,
                  scratch=[pltpu.VMEM((tm, tn), F32)], sem=("parallel", "parallel", "arbitrary"))(*args)


@functools.partial(jax.custom_vjp, nondiff_argnums=(3,))
def _linear_res(a, w, res, name):
    return _mm(a, w, res=res, name=name + "_fwd")


def _linear_res_fwd(a, w, res, name):
    return _mm(a, w, res=res, name=name + "_fwd"), (a, w)


def _linear_res_bwd(name, saved, g):
    a, w = saved
    da = _mm(g, w, tb=True, out_dtype=a.dtype, name=name + "_da")
    dw = _mm(a, g, ta=True, out_dtype=w.dtype, name=name + "_dw")
    return da, dw, g


_linear_res.defvjp(_linear_res_fwd, _linear_res_bwd)


@functools.partial(jax.custom_vjp, nondiff_argnums=(2,))
def _linear(a, w, name):
    return _mm(a, w, name=name + "_fwd")


def _linear_fwd(a, w, name):
    return _mm(a, w, name=name + "_fwd"), (a, w)


def _linear_bwd(name, saved, g):
    a, w = saved
    da = _mm(g, w, tb=True, out_dtype=a.dtype, name=name + "_da")
    dw = _mm(a, g, ta=True, out_dtype=w.dtype, name=name + "_dw")
    return da, dw


_linear.defvjp(_linear_fwd, _linear_bwd)


def linear(a, w, name, res=None):
    return _linear(a, w, name) if res is None else _linear_res(a, w, res, name)


def _tiled_specs(arrs, kinds, t, axis):
    specs = []
    for a, kind in zip(arrs, kinds):
        if kind == "whole":
            specs.append(pl.BlockSpec(a.shape, lambda i, nd=a.ndim: (0,) * nd))
        elif axis == 0:
            specs.append(pl.BlockSpec((t, a.shape[1]), lambda i: (i, 0)))
        else:
            specs.append(pl.BlockSpec((a.shape[0], t), lambda i: (0, i)))
    return specs


def tilewise(fn, name, args, kinds, outs, *, axis, t, diff):
    n_in = len(args)
    length = next(a.shape[axis] for a, kd in zip(args, kinds) if kd == "tile")
    steps = length // t
    assert steps * t == length, (name, length, t)

    def out_sds(other, dtype):
        return jax.ShapeDtypeStruct((length, other) if axis == 0 else (other, length), dtype)

    def out_spec(other):
        return pl.BlockSpec((t, other), lambda i: (i, 0)) if axis == 0 else pl.BlockSpec((other, t), lambda i: (0, i))

    def run_fwd(*xs):
        def body(*refs):
            vals = fn(*[r[...] for r in refs[:n_in]])
            for o_ref, v in zip(refs[n_in:], vals):
                o_ref[...] = v.astype(o_ref.dtype)

        return _pcall(body, name=name + "_fwd", out_shape=[out_sds(o, d) for o, d in outs], grid=(steps,),
                      in_specs=_tiled_specs(xs, kinds, t, axis), out_specs=[out_spec(o) for o, _ in outs],
                      sem=("parallel",))(*xs)

    didx = [i for i in range(n_in) if diff[i]]

    def run_bwd(xs, cts):
        def body(*refs):
            x_refs, c_refs, g_refs = refs[:n_in], refs[n_in:n_in + len(outs)], refs[n_in + len(outs):]
            vals = [r[...] for r in x_refs]

            def g(*dvals):
                full = list(vals)
                for i, v in zip(didx, dvals):
                    full[i] = v
                return tuple(fn(*full))

            prim_out, vjp = jax.vjp(g, *[vals[i] for i in didx])
            grads = vjp(tuple(c[...].astype(o.dtype) for c, o in zip(c_refs, prim_out)))
            step = pl.program_id(0)
            for i, g_ref, gr in zip(didx, g_refs, grads):
                if kinds[i] == "whole":
                    @pl.when(step == 0)
                    def _(g_ref=g_ref):
                        g_ref[...] = jnp.zeros_like(g_ref)

                    g_ref[...] += gr.astype(g_ref.dtype)
                else:
                    g_ref[...] = gr.astype(g_ref.dtype)

        g_shapes = [jax.ShapeDtypeStruct(xs[i].shape, xs[i].dtype) for i in didx]
        g_specs = _tiled_specs([xs[i] for i in didx], [kinds[i] for i in didx], t, axis)
        ct_specs = [out_spec(o) for o, _ in outs]
        any_whole = any(kinds[i] == "whole" for i in didx)
        return _pcall(body, name=name + "_bwd", out_shape=g_shapes, grid=(steps,),
                      in_specs=_tiled_specs(xs, kinds, t, axis) + ct_specs, out_specs=g_specs,
                      sem=("arbitrary" if any_whole else "parallel",))(*xs, *cts)

    @jax.custom_vjp
    def op(*xs):
        return tuple(run_fwd(*xs))

    def op_fwd(*xs):
        return tuple(run_fwd(*xs)), xs

    def op_bwd(xs, cts):
        grads = run_bwd(xs, cts)
        full = [None] * n_in
        for i, gr in zip(didx, grads):
            full[i] = gr
        return tuple(full)

    op.defvjp(op_fwd, op_bwd)
    return op(*args)


def _silu(x):
    return x * (1.0 / (1.0 + jnp.exp(-x)))


def _softplus(x):
    return jnp.maximum(x, 0.0) + jnp.log(1.0 + jnp.exp(-jnp.abs(x)))


def _rms_tile(x, gain, out_dtype):
    xf = x.astype(F32)
    y = xf * lax.rsqrt(jnp.mean(xf * xf, axis=-1, keepdims=True) + EPS)
    return (y * gain).astype(out_dtype)


def rms_norm(x, gain, name, out_dtype, tr=256):
    def fn(xt, gt):
        return (_rms_tile(xt, gt, out_dtype),)

    return tilewise(fn, name, [x, gain.reshape(1, -1)], ["tile", "whole"], [(x.shape[1], out_dtype)],
                    axis=0, t=_tile(x.shape[0], tr), diff=[True, True])[0]


def _shift_down_raw(x, s):
    rows = lax.broadcasted_iota(jnp.int32, x.shape, 0)
    return jnp.where(rows >= s, pltpu.roll(x, s, 0), 0.0)


def _shift_up_raw(x, s):
    n = x.shape[0]
    rows = lax.broadcasted_iota(jnp.int32, x.shape, 0)
    return jnp.where(rows < n - s, pltpu.roll(x, n - s, 0), 0.0)


@functools.partial(jax.custom_vjp, nondiff_argnums=(1,))
def _shift_down(x, s):
    return _shift_down_raw(x, s)


def _shift_down_fwd(x, s):
    return _shift_down_raw(x, s), None


def _shift_down_bwd(s, _, g):
    return (_shift_up_raw(g, s),)


_shift_down.defvjp(_shift_down_fwd, _shift_down_bwd)


def _causal_dwconv_tile(x, w):
    kk = w.shape[0]
    y = x * w[kk - 1:kk, :]
    for j in range(kk - 1):
        y = y + _shift_down(x, kk - 1 - j) * w[j:j + 1, :]
    return y


def dn_qkv(raw, conv_w, name):
    width = raw.shape[1] // 3
    is_qk = (jnp.arange(raw.shape[1]) < 2 * width).astype(F32).reshape(1, -1)

    def fn(x, w, flag):
        y = _silu(_causal_dwconv_tile(x, w))
        yn = y * lax.rsqrt(jnp.sum(y * y, axis=-1, keepdims=True) + EPS)
        return (jnp.where(flag > 0.5, yn, y),)

    return tilewise(fn, name, [raw, conv_w, is_qk], ["tile", "tile", "tile"], [(raw.shape[0], F32)],
                    axis=1, t=DN_HEAD_DIM, diff=[True, True, False])[0]


def dn_gates(b, a, a_log, dt_bias, name):
    hh = b.shape[1]
    tr = _tile(b.shape[0], 256)

    def fn(bt, at, al, db):
        beta = 1.0 / (1.0 + jnp.exp(-bt))
        g = -jnp.exp(al) * _softplus(at + db)
        pos = lax.broadcasted_iota(jnp.int32, g.shape, 0) & (CHUNK - 1)
        step = 1
        while step < CHUNK:
            g = g + jnp.where(pos >= step, _shift_down(g, step), 0.0)
            step *= 2
        return g, beta

    return tilewise(fn, name, [b, a, a_log.reshape(1, -1), dt_bias.reshape(1, -1)],
                    ["tile", "tile", "whole", "whole"], [(hh, F32), (hh, F32)], axis=0, t=tr, diff=[True] * 4)


def _unit_lower_inv(low):
    c = low.shape[-1]
    ii = lax.broadcasted_iota(jnp.int32, (c, c), 0)
    jj = lax.broadcasted_iota(jnp.int32, (c, c), 1)
    inv = (ii == jj).astype(F32)[None] - low
    p = low
    n = 1
    while 2 * n < c:
        p = _bnn_hi(p, p)
        inv = inv + _bnn_hi(inv, p)
        n *= 2
    return inv


def _dn_chunk(state, q, k, v, gc, gr, gl, bc):
    c = q.shape[1]
    q = q * (q.shape[-1] ** -0.5)
    ii = lax.broadcasted_iota(jnp.int32, (c, c), 0)
    jj = lax.broadcasted_iota(jnp.int32, (c, c), 1)
    incl = (jj <= ii)[None]
    strict = (jj < ii)[None]
    decay = jnp.where(incl, jnp.exp(jnp.where(incl, gc - gr, 0.0)), 0.0)
    kb = k * bc
    low = jnp.where(strict, _bnt_hi(kb, k) * decay, 0.0)
    ainv = _unit_lower_inv(low)
    eg = jnp.exp(gc)
    u = _bnn_hi(ainv, v * bc)
    w = _bnn_hi(ainv, kb * eg)
    attn = _bnt(q, k) * decay
    q_dec = q * eg
    k_dec = k * jnp.exp(gl - gc)
    v_new = u - _bnn(w, state)
    o = _bnn(q_dec, state) + _bnn(attn, v_new)
    new_state = state * jnp.exp(gl) + _btn(k_dec, v_new)
    return new_state, o


def _dn_heads(ref, base, hh):
    return jnp.stack([ref[:, base + h * DN_HEAD_DIM: base + (h + 1) * DN_HEAD_DIM] for h in range(hh)])


def _dn_fwd_call(qkv, gc, gr, gl, bc, name):
    s, w3 = qkv.shape
    width = w3 // 3
    hh = width // DN_HEAD_DIM
    d = DN_HEAD_DIM
    n = s // CHUNK

    def body(qkv_ref, gc_ref, gr_ref, gl_ref, bc_ref, o_ref, st_ref, state):
        @pl.when(pl.program_id(0) == 0)
        def _():
            state[...] = jnp.zeros_like(state)

        s_in = state[...]
        st_ref[0] = s_in
        new_s, o = _dn_chunk(s_in, _dn_heads(qkv_ref, 0, hh), _dn_heads(qkv_ref, width, hh),
                             _dn_heads(qkv_ref, 2 * width, hh), gc_ref[0], gr_ref[0], gl_ref[0], bc_ref[0])
        state[...] = new_s
        for h in range(hh):
            o_ref[:, h * d:(h + 1) * d] = o[h]

    g4 = lambda i: (i, 0, 0, 0)
    return _pcall(
        body, name=name + "_fwd",
        out_shape=[jax.ShapeDtypeStruct((s, width), F32), jax.ShapeDtypeStruct((n, hh, d, d), F32)],
        grid=(n,),
        in_specs=[pl.BlockSpec((CHUNK, w3), lambda i: (i, 0)), pl.BlockSpec((1, hh, CHUNK, 1), g4),
                  pl.BlockSpec((1, hh, 1, CHUNK), g4), pl.BlockSpec((1, hh, 1, 1), g4),
                  pl.BlockSpec((1, hh, CHUNK, 1), g4)],
        out_specs=[pl.BlockSpec((CHUNK, width), lambda i: (i, 0)), pl.BlockSpec((1, hh, d, d), g4)],
        scratch=[pltpu.VMEM((hh, d, d), F32)], sem=("arbitrary",))(qkv, gc, gr, gl, bc)


def _dn_bwd_call(qkv, gc, gr, gl, bc, states, do, name):
    s, w3 = qkv.shape
    width = w3 // 3
    hh = width // DN_HEAD_DIM
    d = DN_HEAD_DIM
    n = s // CHUNK

    def body(qkv_ref, gc_ref, gr_ref, gl_ref, bc_ref, st_ref, do_ref,
             dqkv_ref, dgc_ref, dgr_ref, dgl_ref, dbc_ref, dstate):
        @pl.when(pl.program_id(0) == 0)
        def _():
            dstate[...] = jnp.zeros_like(dstate)

        prim = (st_ref[0], _dn_heads(qkv_ref, 0, hh), _dn_heads(qkv_ref, width, hh),
                _dn_heads(qkv_ref, 2 * width, hh), gc_ref[0], gr_ref[0], gl_ref[0], bc_ref[0])
        _, vjp = jax.vjp(_dn_chunk, *prim)
        ds, dq, dk, dv, dgc, dgr, dgl, dbc = vjp((dstate[...], _dn_heads(do_ref, 0, hh)))
        dstate[...] = ds
        for h in range(hh):
            dqkv_ref[:, h * d:(h + 1) * d] = dq[h]
            dqkv_ref[:, width + h * d: width + (h + 1) * d] = dk[h]
            dqkv_ref[:, 2 * width + h * d: 2 * width + (h + 1) * d] = dv[h]
        dgc_ref[0] = dgc
        dgr_ref[0] = dgr
        dgl_ref[0] = dgl
        dbc_ref[0] = dbc

    r2 = lambda i: (n - 1 - i, 0)
    r4 = lambda i: (n - 1 - i, 0, 0, 0)
    spec_c = pl.BlockSpec((1, hh, CHUNK, 1), r4)
    spec_r = pl.BlockSpec((1, hh, 1, CHUNK), r4)
    spec_l = pl.BlockSpec((1, hh, 1, 1), r4)
    return _pcall(
        body, name=name + "_bwd",
        out_shape=[jax.ShapeDtypeStruct(qkv.shape, F32), jax.ShapeDtypeStruct(gc.shape, F32),
                   jax.ShapeDtypeStruct(gr.shape, F32), jax.ShapeDtypeStruct(gl.shape, F32),
                   jax.ShapeDtypeStruct(bc.shape, F32)],
        grid=(n,),
        in_specs=[pl.BlockSpec((CHUNK, w3), r2), spec_c, spec_r, spec_l, spec_c,
                  pl.BlockSpec((1, hh, d, d), r4), pl.BlockSpec((CHUNK, width), r2)],
        out_specs=[pl.BlockSpec((CHUNK, w3), r2), spec_c, spec_r, spec_l, spec_c],
        scratch=[pltpu.VMEM((hh, d, d), F32)], sem=("arbitrary",))(qkv, gc, gr, gl, bc, states, do)


@functools.partial(jax.custom_vjp, nondiff_argnums=(5,))
def dn_core(qkv, gc, gr, gl, bc, name):
    return _dn_fwd_call(qkv, gc, gr, gl, bc, name)[0]


def _dn_core_fwd(qkv, gc, gr, gl, bc, name):
    o, states = _dn_fwd_call(qkv, gc, gr, gl, bc, name)
    return o, (qkv, gc, gr, gl, bc, states)


def _dn_core_bwd(name, saved, do):
    return tuple(_dn_bwd_call(*saved, do, name))


dn_core.defvjp(_dn_core_fwd, _dn_core_bwd)


def dn_out_gate(o, z, out_norm, name):
    hh = o.shape[1] // DN_HEAD_DIM
    gain = jnp.tile(out_norm.reshape(1, -1), (1, hh))

    def fn(ot, zt, gt):
        y = ot * lax.rsqrt(jnp.mean(ot * ot, axis=-1, keepdims=True) + EPS) * gt
        return (y * _silu(zt),)

    return tilewise(fn, name, [o, z, gain], ["tile", "tile", "tile"], [(o.shape[0], BF16)],
                    axis=1, t=DN_HEAD_DIM, diff=[True, True, True])[0]


def _attn_tile(qn, kn, v, qp, kp, q0, scale, causal):
    s = _nt(qn, kn)
    if qp is not None:
        s = s + _nt(qp, kp)
    s = s * scale
    if causal:
        qpos = q0 + lax.broadcasted_iota(jnp.int32, s.shape, 0)
        kpos = lax.broadcasted_iota(jnp.int32, s.shape, 1)
        s = jnp.where((kpos >> _CHUNK_SHIFT) <= (qpos >> _CHUNK_SHIFT), s, -1e30)
    e = jnp.exp(s - jnp.max(s, axis=-1, keepdims=True))
    p = e / jnp.sum(e, axis=-1, keepdims=True)
    return _nn(p, v)


def _attn_specs(sq, sk, dh, dp, tq):
    q_spec = pl.BlockSpec((tq, dh), lambda h, i: (i, h))
    kv_spec = pl.BlockSpec((sk, dh), lambda h, i: (0, h))
    qp_spec = pl.BlockSpec((1, tq, dp), lambda h, i: (h, i, 0)) if dp else None
    kp_spec = pl.BlockSpec((sk, dp), lambda h, i: (0, 0)) if dp else None
    return q_spec, kv_spec, qp_spec, kp_spec


def _attn_fwd_call(q, k, v, qp, kp, *, dh, scale, causal, tq, name):
    sq, sk = q.shape[0], k.shape[0]
    heads = q.shape[1] // dh
    dp = qp.shape[-1] if qp is not None else 0
    q_spec, kv_spec, qp_spec, kp_spec = _attn_specs(sq, sk, dh, dp, tq)

    def body(*refs):
        if dp:
            q_ref, k_ref, v_ref, qp_ref, kp_ref, o_ref = refs
            o = _attn_tile(q_ref[...], k_ref[...], v_ref[...], qp_ref[0], kp_ref[...],
                           pl.program_id(1) * tq, scale, causal)
        else:
            q_ref, k_ref, v_ref, o_ref = refs
            o = _attn_tile(q_ref[...], k_ref[...], v_ref[...], None, None, pl.program_id(1) * tq, scale, causal)
        o_ref[...] = o

    in_specs = [q_spec, kv_spec, kv_spec] + ([qp_spec, kp_spec] if dp else [])
    args = (q, k, v) + ((qp, kp) if dp else ())
    return _pcall(body, name=name + "_fwd", out_shape=jax.ShapeDtypeStruct((sq, heads * dh), F32),
                  grid=(heads, sq // tq), in_specs=in_specs, out_specs=q_spec,
                  sem=("parallel", "parallel"))(*args)


def _attn_bwd_call(q, k, v, qp, kp, do, *, dh, scale, causal, tq, name):
    sq, sk = q.shape[0], k.shape[0]
    heads = q.shape[1] // dh
    dp = qp.shape[-1] if qp is not None else 0
    q_spec, kv_spec, qp_spec, kp_spec = _attn_specs(sq, sk, dh, dp, tq)

    def body(*refs):
        h, i = pl.program_id(0), pl.program_id(1)
        if dp:
            q_ref, k_ref, v_ref, qp_ref, kp_ref, do_ref, dq_ref, dk_ref, dv_ref, dqp_ref, dkp_ref = refs
            prim = (q_ref[...], k_ref[...], v_ref[...], qp_ref[0], kp_ref[...])
            f = lambda a, b, c, d, e: _attn_tile(a, b, c, d, e, i * tq, scale, causal)
        else:
            q_ref, k_ref, v_ref, do_ref, dq_ref, dk_ref, dv_ref = refs
            prim = (q_ref[...], k_ref[...], v_ref[...])
            f = lambda a, b, c: _attn_tile(a, b, c, None, None, i * tq, scale, causal)
        _, vjp = jax.vjp(f, *prim)
        grads = vjp(do_ref[...])
        dq_ref[...] = grads[0]

        @pl.when(i == 0)
        def _():
            dk_ref[...] = jnp.zeros_like(dk_ref)
            dv_ref[...] = jnp.zeros_like(dv_ref)

        dk_ref[...] += grads[1]
        dv_ref[...] += grads[2]
        if dp:
            dqp_ref[0] = grads[3]

            @pl.when(jnp.logical_and(h == 0, i == 0))
            def _():
                dkp_ref[...] = jnp.zeros_like(dkp_ref)

            dkp_ref[...] += grads[4]

    in_specs = [q_spec, kv_spec, kv_spec] + ([qp_spec, kp_spec] if dp else []) + [q_spec]
    out_shape = [jax.ShapeDtypeStruct(q.shape, F32), jax.ShapeDtypeStruct(k.shape, F32),
                 jax.ShapeDtypeStruct(v.shape, F32)]
    out_specs = [q_spec, kv_spec, kv_spec]
    if dp:
        out_shape += [jax.ShapeDtypeStruct(qp.shape, F32), jax.ShapeDtypeStruct(kp.shape, F32)]
        out_specs += [qp_spec, kp_spec]
    args = (q, k, v) + ((qp, kp) if dp else ()) + (do,)
    return _pcall(body, name=name + "_bwd", out_shape=out_shape, grid=(heads, sq // tq), in_specs=in_specs,
                  out_specs=out_specs, sem=("arbitrary", "arbitrary"))(*args)


def attention(q, k, v, qp=None, kp=None, *, dh, scale, causal, name, tq=256):
    tq = _tile(q.shape[0], tq)
    kw = dict(dh=dh, scale=scale, causal=causal, tq=tq, name=name)
    has_pe = qp is not None

    @jax.custom_vjp
    def op(*xs):
        return _attn_fwd_call(*xs, **kw) if has_pe else _attn_fwd_call(*xs, None, None, **kw)

    def op_fwd(*xs):
        return (_attn_fwd_call(*xs, **kw) if has_pe else _attn_fwd_call(*xs, None, None, **kw)), xs

    def op_bwd(xs, do):
        full = xs if has_pe else xs + (None, None)
        return tuple(_attn_bwd_call(*full, do, **kw))

    op.defvjp(op_fwd, op_bwd)
    return op(q, k, v, qp, kp) if has_pe else op(q, k, v)


def _rope_tables(positions, reps):
    half = MLA_ROPE // 2
    inv = ROPE_BASE ** (-jnp.arange(0, MLA_ROPE, 2, dtype=F32) / MLA_ROPE)
    ang = positions.astype(F32)[:, None] * inv
    cos, sin = jnp.cos(ang), jnp.sin(ang)
    c = jnp.tile(jnp.concatenate([cos, cos], axis=-1), (1, reps))
    s = jnp.tile(jnp.concatenate([sin, sin], axis=-1), (1, reps))
    rot = np.zeros((MLA_ROPE, MLA_ROPE), np.float32)
    for i in range(half):
        rot[i + half, i] = -1.0
        rot[i, i + half] = 1.0
    return c, s, jnp.asarray(np.kron(np.eye(reps, dtype=np.float32), rot))


def rope(x, positions, name):
    c, s, rot = _rope_tables(positions, x.shape[1] // MLA_ROPE)

    def fn(xt, ct, st, rt):
        return (xt * ct + _nn_hi_const(xt, rt) * st,)

    return tilewise(fn, name, [x, c, s, rot], ["tile", "tile", "tile", "whole"], [(x.shape[1], F32)],
                    axis=0, t=_tile(x.shape[0], 256), diff=[True, False, False, False])[0]


def ffn_gate(pre_g, pre_u, conv_g, conv_u, bias_g, bias_u, name):
    def fn(g, u, wg, wu, bg, bu):
        return (_silu(_causal_dwconv_tile(g, wg) + bg) * (_causal_dwconv_tile(u, wu) + bu),)

    return tilewise(fn, name, [pre_g, pre_u, conv_g, conv_u, bias_g, bias_u], ["tile"] * 6,
                    [(pre_g.shape[0], BF16)], axis=1, t=_tile(pre_g.shape[1], 256), diff=[True] * 6)[0]


def loss_rows(h, gain, target, name):
    def fn(ht, tt, gt):
        err = _rms_tile(ht, gt, F32) - tt
        return (0.5 * jnp.mean(err * err, axis=-1, keepdims=True),)

    return tilewise(fn, name, [h, target, gain.reshape(1, -1)], ["tile", "tile", "whole"], [(1, F32)],
                    axis=0, t=_tile(h.shape[0], 256), diff=[True, False, True])[0]


def _col_groups(w, per_head, lo, hi):
    r = w.shape[0]
    return w.reshape(r, -1, per_head)[:, :, lo:hi].reshape(r, -1)


def _layer(h, lw, rep, l, positions, mem_n):
    s, d = h.shape
    width = d // 2
    dn_heads = width // DN_HEAD_DIM
    mla_heads = (d - width) // MLA_V
    nm = f"l{l}_"
    w_in = lw["w_in"]
    c0 = 4 * width
    c1 = c0 + 2 * dn_heads
    c2 = c1 + MLA_Q_RANK
    c3 = c2 + MLA_KV_RANK + MLA_ROPE
    rest_pad = (-(c3 - c0)) % LANE
    w_rest = jnp.pad(w_in[:, c0:c3], ((0, 0), (0, rest_pad)))

    u = rms_norm(h, rep["norm_mix"][l], nm + "norm_mix", BF16)
    qkv_raw = linear(u, w_in[:, :3 * width], nm + "in_qkv")
    z = linear(u, w_in[:, 3 * width:c0], nm + "in_z")
    rest = linear(u, w_rest, nm + "in_rest")

    qkv = dn_qkv(qkv_raw, lw["dn_conv"], nm + "dn_qkv")
    csum, beta = dn_gates(rest[:, :dn_heads], rest[:, dn_heads:2 * dn_heads], rep["dn_a_log"][l],
                          rep["dn_dt_bias"][l], nm + "dn_gates")
    n = s // CHUNK
    g3 = csum.reshape(n, CHUNK, dn_heads).transpose(0, 2, 1)
    b3 = beta.reshape(n, CHUNK, dn_heads).transpose(0, 2, 1)
    o_dn = dn_core(qkv, g3[..., None], g3[:, :, None, :], g3[:, :, CHUNK - 1][..., None, None], b3[..., None],
                   nm + "dn_core")
    o_dn = dn_out_gate(o_dn, z, rep["dn_out_norm"][l], nm + "dn_gate")

    mq = rest[:, c1 - c0:c2 - c0]
    mkv = rest[:, c2 - c0:c3 - c0]
    qn = rms_norm(mq, rep["mla_q_norm"][l], nm + "mla_qnorm", BF16)
    per_q = MLA_NOPE + MLA_ROPE
    q_nope = linear(qn, _col_groups(lw["mla_w_qb"], per_q, 0, MLA_NOPE), nm + "mla_qn")
    q_pe = linear(qn, _col_groups(lw["mla_w_qb"], per_q, MLA_NOPE, per_q), nm + "mla_qp")
    kvn = rms_norm(mkv[:, :MLA_KV_RANK], rep["mla_kv_norm"][l], nm + "mla_kvnorm", BF16)
    per_kv = MLA_NOPE + MLA_V
    k_nope = linear(kvn, _col_groups(lw["mla_w_kvb"], per_kv, 0, MLA_NOPE), nm + "mla_kn")
    v_mla = linear(kvn, _col_groups(lw["mla_w_kvb"], per_kv, MLA_NOPE, per_kv), nm + "mla_v")
    q_pe = rope(q_pe, positions, nm + "rope_q")
    k_pe = rope(mkv[:, MLA_KV_RANK:], positions, nm + "rope_k")
    q_pe = q_pe.reshape(s, mla_heads, MLA_ROPE).transpose(1, 0, 2)
    o_mla = attention(q_nope, k_nope, v_mla, q_pe, k_pe, dh=MLA_NOPE, scale=per_q ** -0.5, causal=True,
                      name=nm + "mla_attn")

    h = linear(o_dn, lw["w_out"][:width], nm + "out_dn", res=h)
    h = linear(o_mla, lw["w_out"][width:], nm + "out_mla", res=h)

    hn = rms_norm(h, rep["norm_xattn"][l], nm + "norm_xattn", BF16)
    xq = linear(hn, lw["xa_wq"], nm + "xa_q")
    xk = linear(mem_n, lw["xa_wk"], nm + "xa_k")
    xv = linear(mem_n, lw["xa_wv"], nm + "xa_v")
    xdh = d // XA_HEADS
    xo = attention(xq, xk, xv, dh=xdh, scale=xdh ** -0.5, causal=False, name=nm + "xattn")
    h = linear(xo, lw["xa_wo"], nm + "xa_o", res=h)

    hn = rms_norm(h, rep["norm_ffn"][l], nm + "norm_ffn", BF16)
    w_up = lw["ffn_w_up"]
    d_ff = w_up.shape[1] // 2
    pre_g = linear(hn, w_up[:, :d_ff], nm + "ffn_upg")
    pre_u = linear(hn, w_up[:, d_ff:], nm + "ffn_upu")
    bias = rep["ffn_conv_bias"][l].reshape(1, -1)
    conv = lw["ffn_conv"]
    act = ffn_gate(pre_g, pre_u, conv[:, :d_ff], conv[:, d_ff:], bias[:, :d_ff], bias[:, d_ff:], nm + "ffn_gate")
    return linear(act, lw["ffn_w_down"], nm + "ffn_down", res=h)


_COL_SHARDED = ("w_in", "mla_w_qb", "mla_w_kvb", "ffn_w_up", "dn_conv", "ffn_conv")
_ROW_SHARDED = ("w_out", "xa_wq", "xa_wk", "xa_wv", "xa_wo", "ffn_w_down")
_BIG = ("w_in", "mla_w_qb", "mla_w_kvb", "w_out", "xa_wq", "xa_wk", "xa_wv", "xa_wo", "ffn_w_up", "ffn_w_down")
_SMALL_SHARDED = ("dn_conv", "ffn_conv")
_REPLICATED = ("norm_mix", "dn_a_log", "dn_dt_bias", "dn_out_norm", "mla_q_norm", "mla_kv_norm", "mem_norm",
               "norm_xattn", "norm_ffn", "ffn_conv_bias", "norm_final")
_WEIGHTS = ("norm_mix", "w_in", "dn_conv", "dn_a_log", "dn_dt_bias", "dn_out_norm", "mla_q_norm", "mla_w_qb",
            "mla_kv_norm", "mla_w_kvb", "w_out", "mem_norm", "norm_xattn", "xa_wq", "xa_wk", "xa_wv", "xa_wo",
            "norm_ffn", "ffn_w_up", "ffn_conv", "ffn_conv_bias", "ffn_w_down", "norm_final")


def _assemble(name, g):
    if name in _ROW_SHARDED:
        return g.reshape((-1,) + g.shape[2:])
    return jnp.moveaxis(g, 0, -2).reshape(g.shape[1:-1] + (-1,))


def _local_loss(x, mem, positions, target, gathered, small, rep):
    depth = len(gathered)
    mem_n = rms_norm(mem, rep["mem_norm"], "mem_norm", BF16)
    h = x
    for l in range(depth):
        lw = {k: _assemble(k, v) for k, v in gathered[l].items()}
        for k in _SMALL_SHARDED:
            lw[k] = _assemble(k, small[k][:, l])
        h = _layer(h, lw, rep, l, positions, mem_n)
    return jnp.sum(loss_rows(h, rep["norm_final"], target, "loss"))


def _my_index():
    return 4 * lax.axis_index("x") + 2 * lax.axis_index("y") + lax.axis_index("c")


def _peer(k):
    x, y, c = lax.axis_index("x"), lax.axis_index("y"), lax.axis_index("c")
    px = (1 - x) if k & 4 else x
    py = (1 - y) if k & 2 else y
    pc = (1 - c) if k & 1 else c
    return (px, py, pc), 4 * px + 2 * py + pc


_ANY = pl.BlockSpec(memory_space=pl.ANY)


def all_gather(shards, name):
    n = len(shards)

    def body(*refs):
        x_refs, o_refs = refs[:n], refs[n:2 * n]
        send_sems, recv_sems, local_sems = refs[2 * n:]
        me = _my_index()
        sib_id, sib = _peer(1)
        chips = [_peer(k) for k in (4, 2, 6)]

        def copy(a, k, block, to, src=None):
            return pltpu.make_async_remote_copy(
                src_ref=o_refs[a].at[block] if src is None else src, dst_ref=o_refs[a].at[block],
                send_sem=send_sems.at[a * 7 + k], recv_sem=recv_sems.at[a * 7 + k],
                device_id=to, device_id_type=MESH)

        mine = [pltpu.make_async_copy(x_refs[a], o_refs[a].at[me], local_sems.at[a]) for a in range(n)]
        for cp in mine:
            cp.start()
        first = []
        for a in range(n):
            first.append(copy(a, 0, me, sib_id, src=x_refs[a]))
            first += [copy(a, 1 + j, me, cid, src=x_refs[a]) for j, (cid, _) in enumerate(chips)]
        for cp in first:
            cp.start()
        passed = []
        for a in range(n):
            for j, (cid, cidx) in enumerate(chips):
                copy(a, 1 + j, cidx, cid).wait_recv()
                fwd = copy(a, 4 + j, cidx, sib_id)
                fwd.start()
                passed.append(fwd)
        for a in range(n):
            copy(a, 0, sib, sib_id).wait_recv()
            for j, (_, cidx) in enumerate(chips):
                copy(a, 4 + j, cidx ^ 1, sib_id).wait_recv()
        for cp in first + passed:
            cp.wait_send()
        for cp in mine:
            cp.wait()

    out_shape = [jax.ShapeDtypeStruct((N_DEV,) + s.shape, s.dtype) for s in shards]
    return _pcall(body, name=name, out_shape=out_shape, in_specs=[_ANY] * n, out_specs=[_ANY] * n,
                  scratch=[pltpu.SemaphoreType.DMA((7 * n,)), pltpu.SemaphoreType.DMA((7 * n,)),
                           pltpu.SemaphoreType.DMA((n,))])(*shards)


def all_to_all(parts, name):
    n = len(parts)

    def body(*refs):
        x_refs, o_refs = refs[:n], refs[n:2 * n]
        send_sems, recv_sems, local_sems = refs[2 * n:]
        me = _my_index()
        mine = [pltpu.make_async_copy(x_refs[a].at[me], o_refs[a].at[me], local_sems.at[a]) for a in range(n)]
        for cp in mine:
            cp.start()
        sends = []
        for a in range(n):
            for k in range(1, N_DEV):
                pid, pidx = _peer(k)
                sends.append(pltpu.make_async_remote_copy(
                    src_ref=x_refs[a].at[pidx], dst_ref=o_refs[a].at[me],
                    send_sem=send_sems.at[a * 7 + k - 1], recv_sem=recv_sems.at[a * 7 + k - 1],
                    device_id=pid, device_id_type=MESH))
        for cp in sends:
            cp.start()
        for a in range(n):
            for k in range(1, N_DEV):
                pid, pidx = _peer(k)
                pltpu.make_async_remote_copy(
                    src_ref=x_refs[a].at[pidx], dst_ref=o_refs[a].at[pidx],
                    send_sem=send_sems.at[a * 7 + k - 1], recv_sem=recv_sems.at[a * 7 + k - 1],
                    device_id=pid, device_id_type=MESH).wait_recv()
        for cp in sends:
            cp.wait_send()
        for cp in mine:
            cp.wait()

    out_shape = [jax.ShapeDtypeStruct(p.shape, p.dtype) for p in parts]
    return _pcall(body, name=name, out_shape=out_shape, in_specs=[_ANY] * n, out_specs=[_ANY] * n,
                  scratch=[pltpu.SemaphoreType.DMA((7 * n,)), pltpu.SemaphoreType.DMA((7 * n,)),
                           pltpu.SemaphoreType.DMA((n,))])(*parts)


def adam_update(recv, w, m, v, name):
    ll, _, r, c = recv.shape
    tr = _tile(r, max(8, 1 << ((_ADAM_TILE_ELEMS // c).bit_length() - 1)))

    def body(g_ref, w_ref, m_ref, v_ref, go_ref, d_ref, mo_ref, vo_ref):
        g = g_ref[0, 0].astype(F32)
        for j in range(1, N_DEV):
            g = g + g_ref[0, j].astype(F32)
        m_new = ADAM_B1 * m_ref[0] + (1.0 - ADAM_B1) * g
        v_new = ADAM_B2 * v_ref[0] + (1.0 - ADAM_B2) * (g * g)
        m_hat = m_new / (1.0 - ADAM_B1 ** ADAM_STEP)
        v_hat = v_new / (1.0 - ADAM_B2 ** ADAM_STEP)
        go_ref[0] = g
        d_ref[0] = -ADAM_LR * (m_hat / (jnp.sqrt(v_hat) + ADAM_EPS) + ADAM_WD * w_ref[0])
        mo_ref[0] = m_new
        vo_ref[0] = v_new

    spec = pl.BlockSpec((1, tr, c), lambda l, i: (l, i, 0))
    sds = jax.ShapeDtypeStruct((ll, r, c), F32)
    return _pcall(body, name=name, out_shape=[sds] * 4, grid=(ll, r // tr),
                  in_specs=[pl.BlockSpec((1, N_DEV, tr, c), lambda l, i: (l, 0, i, 0)), spec, spec, spec],
                  out_specs=[spec] * 4, sem=("parallel", "parallel"))(recv, w, m, v)


def _as3d(a):
    if a.ndim == 1:
        return a.reshape(1, 1, -1)
    if a.ndim == 2:
        return a.reshape(1, a.shape[0], a.shape[1])
    return a.reshape(a.shape[0], -1, a.shape[-1])


def kernel(x, mem, positions, norm_mix, w_in, dn_conv, dn_a_log, dn_dt_bias, dn_out_norm, mla_q_norm, mla_w_qb, mla_kv_norm, mla_w_kvb, w_out, mem_norm, norm_xattn, xa_wq, xa_wk, xa_wv, xa_wo, norm_ffn, ffn_w_up, ffn_conv, ffn_conv_bias, ffn_w_down, norm_final, loss_target, m_norm_mix, m_w_in, m_dn_conv, m_dn_a_log, m_dn_dt_bias, m_dn_out_norm, m_mla_q_norm, m_mla_w_qb, m_mla_kv_norm, m_mla_w_kvb, m_w_out, m_mem_norm, m_norm_xattn, m_xa_wq, m_xa_wk, m_xa_wv, m_xa_wo, m_norm_ffn, m_ffn_w_up, m_ffn_conv, m_ffn_conv_bias, m_ffn_w_down, m_norm_final, v_norm_mix, v_w_in, v_dn_conv, v_dn_a_log, v_dn_dt_bias, v_dn_out_norm, v_mla_q_norm, v_mla_w_qb, v_mla_kv_norm, v_mla_w_kvb, v_w_out, v_mem_norm, v_norm_xattn, v_xa_wq, v_xa_wk, v_xa_wv, v_xa_wo, v_norm_ffn, v_ffn_w_up, v_ffn_conv, v_ffn_conv_bias, v_ffn_w_down, v_norm_final):
    w = dict(norm_mix=norm_mix, w_in=w_in, dn_conv=dn_conv, dn_a_log=dn_a_log, dn_dt_bias=dn_dt_bias,
             dn_out_norm=dn_out_norm, mla_q_norm=mla_q_norm, mla_w_qb=mla_w_qb, mla_kv_norm=mla_kv_norm,
             mla_w_kvb=mla_w_kvb, w_out=w_out, mem_norm=mem_norm, norm_xattn=norm_xattn, xa_wq=xa_wq, xa_wk=xa_wk,
             xa_wv=xa_wv, xa_wo=xa_wo, norm_ffn=norm_ffn, ffn_w_up=ffn_w_up, ffn_conv=ffn_conv,
             ffn_conv_bias=ffn_conv_bias, ffn_w_down=ffn_w_down, norm_final=norm_final)
    mom = dict(norm_mix=m_norm_mix, w_in=m_w_in, dn_conv=m_dn_conv, dn_a_log=m_dn_a_log, dn_dt_bias=m_dn_dt_bias,
               dn_out_norm=m_dn_out_norm, mla_q_norm=m_mla_q_norm, mla_w_qb=m_mla_w_qb, mla_kv_norm=m_mla_kv_norm,
               mla_w_kvb=m_mla_w_kvb, w_out=m_w_out, mem_norm=m_mem_norm, norm_xattn=m_norm_xattn, xa_wq=m_xa_wq,
               xa_wk=m_xa_wk, xa_wv=m_xa_wv, xa_wo=m_xa_wo, norm_ffn=m_norm_ffn, ffn_w_up=m_ffn_w_up,
               ffn_conv=m_ffn_conv, ffn_conv_bias=m_ffn_conv_bias, ffn_w_down=m_ffn_w_down, norm_final=m_norm_final)
    var = dict(norm_mix=v_norm_mix, w_in=v_w_in, dn_conv=v_dn_conv, dn_a_log=v_dn_a_log, dn_dt_bias=v_dn_dt_bias,
               dn_out_norm=v_dn_out_norm, mla_q_norm=v_mla_q_norm, mla_w_qb=v_mla_w_qb, mla_kv_norm=v_mla_kv_norm,
               mla_w_kvb=v_mla_w_kvb, w_out=v_w_out, mem_norm=v_mem_norm, norm_xattn=v_norm_xattn, xa_wq=v_xa_wq,
               xa_wk=v_xa_wk, xa_wv=v_xa_wv, xa_wo=v_xa_wo, norm_ffn=v_norm_ffn, ffn_w_up=v_ffn_w_up,
               ffn_conv=v_ffn_conv, ffn_conv_bias=v_ffn_conv_bias, ffn_w_down=v_ffn_w_down, norm_final=v_norm_final)
    depth = w_in.shape[0]

    gathered = []
    for l in range(depth):
        got = all_gather([w[k][l].astype(BF16) for k in _BIG], f"gather_l{l}")
        gathered.append(dict(zip(_BIG, got)))
    small = dict(zip(_SMALL_SHARDED, all_gather([w[k] for k in _SMALL_SHARDED], "gather_small")))
    rep = {k: w[k] for k in _REPLICATED}

    loss, vjp = jax.vjp(lambda xx, gg, ss, rr: _local_loss(xx, mem[0], positions[0], loss_target[0], gg, ss, rr),
                        x[0], gathered, small, rep)
    d_x, d_gathered, d_small, d_rep = vjp(jnp.ones((), F32))
    loss = lax.psum(loss, ("x", "y", "c"))

    out = {}
    recv_big = [all_to_all([d_gathered[l][k] for k in _BIG], f"scatter_l{l}") for l in range(depth)]
    for i, k in enumerate(_BIG):
        recv = jnp.stack([recv_big[l][i] for l in range(depth)])
        out[k] = adam_update(recv.reshape((depth, N_DEV, -1, recv.shape[-1])), _as3d(w[k]), _as3d(mom[k]),
                             _as3d(var[k]), "adam_" + k)
    recv_small = all_to_all([d_small[k] for k in _SMALL_SHARDED], "scatter_small")
    for k, recv in zip(_SMALL_SHARDED, recv_small):
        r3 = _as3d(w[k])
        out[k] = adam_update(recv.reshape((1, N_DEV, -1, recv.shape[-1])), r3.reshape((1, -1, r3.shape[-1])),
                             _as3d(mom[k]).reshape((1, -1, r3.shape[-1])), _as3d(var[k]).reshape((1, -1, r3.shape[-1])),
                             "adam_" + k)
    recv_rep = all_gather([_as3d(d_rep[k])[0] for k in _REPLICATED], "gather_rep_grads")
    for k, recv in zip(_REPLICATED, recv_rep):
        r3 = _as3d(w[k])
        flat = lambda a, r3=r3: a.reshape((1, -1, r3.shape[-1]))
        out[k] = adam_update(recv[None], flat(w[k]), flat(mom[k]), flat(var[k]), "adam_" + k)

    res = [loss, d_x[None]]
    for j in range(4):
        res += [out[k][j].reshape(w[k].shape) for k in _WEIGHTS]
    return tuple(res)
```

```python
import functools

import numpy as np
import jax
import jax.numpy as jnp
from jax import lax
from jax.experimental import pallas as pl
from jax.experimental.pallas import tpu as pltpu

F32 = jnp.float32
BF16 = jnp.bfloat16
_MXU_DTYPE = BF16
_INTERPRET = False
_VMEM_LIMIT_BYTES = 48 * 1024 * 1024
_HI = lax.Precision.HIGHEST

N_DEV = 8
CHUNK = 64
_CHUNK_SHIFT = 6
DN_HEAD_DIM = 128
DN_CONV = 4
MLA_NOPE = 128
MLA_ROPE = 64
MLA_V = 128
MLA_Q_RANK = 512
MLA_KV_RANK = 256
ROPE_BASE = 10000.0
XA_HEADS = 4
FFN_CONV = 3
EPS = 1e-6
LANE = 128

ADAM_LR = 0.001
ADAM_B1 = 0.9
ADAM_B2 = 0.999
ADAM_EPS = 1e-08
ADAM_WD = 0.01
ADAM_STEP = 10
_ADAM_TILE_ELEMS = 128 * 1024

MESH = pl.DeviceIdType.MESH


def _pcall(body, *, name, out_shape, grid=None, in_specs=None, out_specs=None, scratch=(), sem=None, **kw):
    params = pltpu.CompilerParams(dimension_semantics=sem, vmem_limit_bytes=_VMEM_LIMIT_BYTES)
    args = dict(name=name, out_shape=out_shape, scratch_shapes=list(scratch), compiler_params=params,
                interpret=_INTERPRET, **kw)
    if grid is not None:
        args.update(grid=grid)
    if in_specs is not None:
        args.update(in_specs=in_specs)
    if out_specs is not None:
        args.update(out_specs=out_specs)
    return pl.pallas_call(body, **args)


def _tile(n, pref):
    if n <= pref:
        return n
    t = pref
    while t >= 8:
        if n % t == 0:
            return t
        t //= 2
    return n


def _ein_raw(spec, a, b, hi):
    if hi:
        return jnp.einsum(spec, a.astype(F32), b.astype(F32), precision=_HI, preferred_element_type=F32)
    return jnp.einsum(spec, a.astype(_MXU_DTYPE), b.astype(_MXU_DTYPE), preferred_element_type=F32)


def _make_ein(spec, hi=False, diff_b=True):
    a_s, rest = spec.split(",")
    b_s, o_s = rest.split("->")

    @jax.custom_vjp
    def f(a, b):
        return _ein_raw(spec, a, b, hi)

    def fwd(a, b):
        return f(a, b), (a, b)

    def bwd(res, g):
        a, b = res
        da = _ein_raw(f"{o_s},{b_s}->{a_s}", g, b, hi)
        if not diff_b:
            return da.astype(a.dtype), None
        db = _ein_raw(f"{a_s},{o_s}->{b_s}", a, g, hi)
        return da.astype(a.dtype), db.astype(b.dtype)

    f.defvjp(fwd, bwd)
    return f


_nt = _make_ein("qd,kd->qk")
_nn = _make_ein("qk,kd->qd")
_nn_hi_const = _make_ein("qk,kd->qd", hi=True, diff_b=False)
_bnt = _make_ein("hik,hjk->hij")
_bnt_hi = _make_ein("hik,hjk->hij", hi=True)
_bnn = _make_ein("hij,hjv->hiv")
_bnn_hi = _make_ein("hij,hjv->hiv", hi=True)
_btn = _make_ein("hck,hcv->hkv")


def _mm(a, b, *, ta=False, tb=False, res=None, out_dtype=F32, name="mm"):
    m = a.shape[1] if ta else a.shape[0]
    k = a.shape[0] if ta else a.shape[1]
    n = b.shape[0] if tb else b.shape[1]
    assert (b.shape[1] if tb else b.shape[0]) == k, (a.shape, b.shape, ta, tb)
    tm, tn, tk = _tile(m, 1024), _tile(n, 1024), _tile(k, 512)
    nk = k // tk
    dims = (((0 if ta else 1,), (1 if tb else 0,)), ((), ()))
    has_res = res is not None

    def body(*refs):
        if has_res:
            a_ref, b_ref, r_ref, o_ref, acc = refs
        else:
            a_ref, b_ref, o_ref, acc = refs
        kk = pl.program_id(2)

        @pl.when(kk == 0)
        def _():
            acc[...] = jnp.zeros_like(acc)

        acc[...] += lax.dot_general(a_ref[...].astype(_MXU_DTYPE), b_ref[...].astype(_MXU_DTYPE), dims,
                                    preferred_element_type=F32)

        @pl.when(kk == nk - 1)
        def _():
            r = acc[...]
            if has_res:
                r = r + r_ref[...].astype(F32)
            o_ref[...] = r.astype(out_dtype)

    a_spec = pl.BlockSpec((tk, tm), lambda i, j, kk: (kk, i)) if ta else pl.BlockSpec((tm, tk), lambda i, j, kk: (i, kk))
    b_spec = pl.BlockSpec((tn, tk), lambda i, j, kk: (j, kk)) if tb else pl.BlockSpec((tk, tn), lambda i, j, kk: (kk, j))
    o_spec = pl.BlockSpec((tm, tn), lambda i, j, kk: (i, j))
    in_specs = [a_spec, b_spec] + ([o_spec] if has_res else [])
    args = (a, b) + ((res,) if has_res else ())
    return _pcall(body, name=name, out_shape=jax.ShapeDtypeStruct((m, n), out_dtype),
                  grid=(m // tm, n // tn, nk), in_specs=in_specs, out_specs=o_spec,
                  scratch=[pltpu.VMEM((tm, tn), F32)], sem=("parallel", "parallel", "arbitrary"))(*args)


@functools.partial(jax.custom_vjp, nondiff_argnums=(3,))
def _linear_res(a, w, res, name):
    return _mm(a, w, res=res, name=name + "_fwd")


def _linear_res_fwd(a, w, res, name):
    return _mm(a, w, res=res, name=name + "_fwd"), (a, w)


def _linear_res_bwd(name, saved, g):
    a, w = saved
    da = _mm(g, w, tb=True, out_dtype=a.dtype, name=name + "_da")
    dw = _mm(a, g, ta=True, out_dtype=w.dtype, name=name + "_dw")
    return da, dw, g


_linear_res.defvjp(_linear_res_fwd, _linear_res_bwd)


@functools.partial(jax.custom_vjp, nondiff_argnums=(2,))
def _linear(a, w, name):
    return _mm(a, w, name=name + "_fwd")


def _linear_fwd(a, w, name):
    return _mm(a, w, name=name + "_fwd"), (a, w)


def _linear_bwd(name, saved, g):
    a, w = saved
    da = _mm(g, w, tb=True, out_dtype=a.dtype, name=name + "_da")
    dw = _mm(a, g, ta=True, out_dtype=w.dtype, name=name + "_dw")
    return da, dw


_linear.defvjp(_linear_fwd, _linear_bwd)


def linear(a, w, name, res=None):
    return _linear(a, w, name) if res is None else _linear_res(a, w, res, name)


def _tiled_specs(arrs, kinds, t, axis):
    specs = []
    for a, kind in zip(arrs, kinds):
        if kind == "whole":
            specs.append(pl.BlockSpec(a.shape, lambda i, nd=a.ndim: (0,) * nd))
        elif axis == 0:
            specs.append(pl.BlockSpec((t, a.shape[1]), lambda i: (i, 0)))
        else:
            specs.append(pl.BlockSpec((a.shape[0], t), lambda i: (0, i)))
    return specs


def tilewise(fn, name, args, kinds, outs, *, axis, t, diff):
    n_in = len(args)
    length = next(a.shape[axis] for a, kd in zip(args, kinds) if kd == "tile")
    steps = length // t
    assert steps * t == length, (name, length, t)

    def out_sds(other, dtype):
        return jax.ShapeDtypeStruct((length, other) if axis == 0 else (other, length), dtype)

    def out_spec(other):
        return pl.BlockSpec((t, other), lambda i: (i, 0)) if axis == 0 else pl.BlockSpec((other, t), lambda i: (0, i))

    def run_fwd(*xs):
        def body(*refs):
            vals = fn(*[r[...] for r in refs[:n_in]])
            for o_ref, v in zip(refs[n_in:], vals):
                o_ref[...] = v.astype(o_ref.dtype)

        return _pcall(body, name=name + "_fwd", out_shape=[out_sds(o, d) for o, d in outs], grid=(steps,),
                      in_specs=_tiled_specs(xs, kinds, t, axis), out_specs=[out_spec(o) for o, _ in outs],
                      sem=("parallel",))(*xs)

    didx = [i for i in range(n_in) if diff[i]]

    def run_bwd(xs, cts):
        def body(*refs):
            x_refs, c_refs, g_refs = refs[:n_in], refs[n_in:n_in + len(outs)], refs[n_in + len(outs):]
            vals = [r[...] for r in x_refs]

            def g(*dvals):
                full = list(vals)
                for i, v in zip(didx, dvals):
                    full[i] = v
                return tuple(fn(*full))

            prim_out, vjp = jax.vjp(g, *[vals[i] for i in didx])
            grads = vjp(tuple(c[...].astype(o.dtype) for c, o in zip(c_refs, prim_out)))
            step = pl.program_id(0)
            for i, g_ref, gr in zip(didx, g_refs, grads):
                if kinds[i] == "whole":
                    @pl.when(step == 0)
                    def _(g_ref=g_ref):
                        g_ref[...] = jnp.zeros_like(g_ref)

                    g_ref[...] += gr.astype(g_ref.dtype)
                else:
                    g_ref[...] = gr.astype(g_ref.dtype)

        g_shapes = [jax.ShapeDtypeStruct(xs[i].shape, xs[i].dtype) for i in didx]
        g_specs = _tiled_specs([xs[i] for i in didx], [kinds[i] for i in didx], t, axis)
        ct_specs = [out_spec(o) for o, _ in outs]
        any_whole = any(kinds[i] == "whole" for i in didx)
        return _pcall(body, name=name + "_bwd", out_shape=g_shapes, grid=(steps,),
                      in_specs=_tiled_specs(xs, kinds, t, axis) + ct_specs, out_specs=g_specs,
                      sem=("arbitrary" if any_whole else "parallel",))(*xs, *cts)

    @jax.custom_vjp
    def op(*xs):
        return tuple(run_fwd(*xs))

    def op_fwd(*xs):
        return tuple(run_fwd(*xs)), xs

    def op_bwd(xs, cts):
        grads = run_bwd(xs, cts)
        full = [None] * n_in
        for i, gr in zip(didx, grads):
            full[i] = gr
        return tuple(full)

    op.defvjp(op_fwd, op_bwd)
    return op(*args)


def _silu(x):
    return x * (1.0 / (1.0 + jnp.exp(-x)))


def _softplus(x):
    return jnp.maximum(x, 0.0) + jnp.log(1.0 + jnp.exp(-jnp.abs(x)))


def _rms_tile(x, gain, out_dtype):
    xf = x.astype(F32)
    y = xf * lax.rsqrt(jnp.mean(xf * xf, axis=-1, keepdims=True) + EPS)
    return (y * gain).astype(out_dtype)


def rms_norm(x, gain, name, out_dtype, tr=256):
    def fn(xt, gt):
        return (_rms_tile(xt, gt, out_dtype),)

    return tilewise(fn, name, [x, gain.reshape(1, -1)], ["tile", "whole"], [(x.shape[1], out_dtype)],
                    axis=0, t=_tile(x.shape[0], tr), diff=[True, True])[0]


def _shift_down_raw(x, s):
    rows = lax.broadcasted_iota(jnp.int32, x.shape, 0)
    return jnp.where(rows >= s, pltpu.roll(x, s, 0), 0.0)


def _shift_up_raw(x, s):
    n = x.shape[0]
    rows = lax.broadcasted_iota(jnp.int32, x.shape, 0)
    return jnp.where(rows < n - s, pltpu.roll(x, n - s, 0), 0.0)


@functools.partial(jax.custom_vjp, nondiff_argnums=(1,))
def _shift_down(x, s):
    return _shift_down_raw(x, s)


def _shift_down_fwd(x, s):
    return _shift_down_raw(x, s), None


def _shift_down_bwd(s, _, g):
    return (_shift_up_raw(g, s),)


_shift_down.defvjp(_shift_down_fwd, _shift_down_bwd)


def _causal_dwconv_tile(x, w):
    kk = w.shape[0]
    y = x * w[kk - 1:kk, :]
    for j in range(kk - 1):
        y = y + _shift_down(x, kk - 1 - j) * w[j:j + 1, :]
    return y


def dn_qkv(raw, conv_w, name):
    width = raw.shape[1] // 3
    is_qk = (jnp.arange(raw.shape[1]) < 2 * width).astype(F32).reshape(1, -1)

    def fn(x, w, flag):
        y = _silu(_causal_dwconv_tile(x, w))
        yn = y * lax.rsqrt(jnp.sum(y * y, axis=-1, keepdims=True) + EPS)
        return (jnp.where(flag > 0.5, yn, y),)

    return tilewise(fn, name, [raw, conv_w, is_qk], ["tile", "tile", "tile"], [(raw.shape[0], F32)],
                    axis=1, t=DN_HEAD_DIM, diff=[True, True, False])[0]


def dn_gates(b, a, a_log, dt_bias, name):
    hh = b.shape[1]
    tr = _tile(b.shape[0], 256)

    def fn(bt, at, al, db):
        beta = 1.0 / (1.0 + jnp.exp(-bt))
        g = -jnp.exp(al) * _softplus(at + db)
        pos = lax.broadcasted_iota(jnp.int32, g.shape, 0) & (CHUNK - 1)
        step = 1
        while step < CHUNK:
            g = g + jnp.where(pos >= step, _shift_down(g, step), 0.0)
            step *= 2
        return g, beta

    return tilewise(fn, name, [b, a, a_log.reshape(1, -1), dt_bias.reshape(1, -1)],
                    ["tile", "tile", "whole", "whole"], [(hh, F32), (hh, F32)], axis=0, t=tr, diff=[True] * 4)


def _unit_lower_inv(low):
    c = low.shape[-1]
    ii = lax.broadcasted_iota(jnp.int32, (c, c), 0)
    jj = lax.broadcasted_iota(jnp.int32, (c, c), 1)
    inv = (ii == jj).astype(F32)[None] - low
    p = low
    n = 1
    while 2 * n < c:
        p = _bnn_hi(p, p)
        inv = inv + _bnn_hi(inv, p)
        n *= 2
    return inv


def _dn_chunk(state, q, k, v, gc, gr, gl, bc):
    c = q.shape[1]
    q = q * (q.shape[-1] ** -0.5)
    ii = lax.broadcasted_iota(jnp.int32, (c, c), 0)
    jj = lax.broadcasted_iota(jnp.int32, (c, c), 1)
    incl = (jj <= ii)[None]
    strict = (jj < ii)[None]
    decay = jnp.where(incl, jnp.exp(jnp.where(incl, gc - gr, 0.0)), 0.0)
    kb = k * bc
    low = jnp.where(strict, _bnt_hi(kb, k) * decay, 0.0)
    ainv = _unit_lower_inv(low)
    eg = jnp.exp(gc)
    u = _bnn_hi(ainv, v * bc)
    w = _bnn_hi(ainv, kb * eg)
    attn = _bnt(q, k) * decay
    q_dec = q * eg
    k_dec = k * jnp.exp(gl - gc)
    v_new = u - _bnn(w, state)
    o = _bnn(q_dec, state) + _bnn(attn, v_new)
    new_state = state * jnp.exp(gl) + _btn(k_dec, v_new)
    return new_state, o


def _dn_heads(ref, base, hh):
    return jnp.stack([ref[:, base + h * DN_HEAD_DIM: base + (h + 1) * DN_HEAD_DIM] for h in range(hh)])


def _dn_fwd_call(qkv, gc, gr, gl, bc, name):
    s, w3 = qkv.shape
    width = w3 // 3
    hh = width // DN_HEAD_DIM
    d = DN_HEAD_DIM
    n = s // CHUNK

    def body(qkv_ref, gc_ref, gr_ref, gl_ref, bc_ref, o_ref, st_ref, state):
        @pl.when(pl.program_id(0) == 0)
        def _():
            state[...] = jnp.zeros_like(state)

        s_in = state[...]
        st_ref[0] = s_in
        new_s, o = _dn_chunk(s_in, _dn_heads(qkv_ref, 0, hh), _dn_heads(qkv_ref, width, hh),
                             _dn_heads(qkv_ref, 2 * width, hh), gc_ref[0], gr_ref[0], gl_ref[0], bc_ref[0])
        state[...] = new_s
        for h in range(hh):
            o_ref[:, h * d:(h + 1) * d] = o[h]

    g4 = lambda i: (i, 0, 0, 0)
    return _pcall(
        body, name=name + "_fwd",
        out_shape=[jax.ShapeDtypeStruct((s, width), F32), jax.ShapeDtypeStruct((n, hh, d, d), F32)],
        grid=(n,),
        in_specs=[pl.BlockSpec((CHUNK, w3), lambda i: (i, 0)), pl.BlockSpec((1, hh, CHUNK, 1), g4),
                  pl.BlockSpec((1, hh, 1, CHUNK), g4), pl.BlockSpec((1, hh, 1, 1), g4),
                  pl.BlockSpec((1, hh, CHUNK, 1), g4)],
        out_specs=[pl.BlockSpec((CHUNK, width), lambda i: (i, 0)), pl.BlockSpec((1, hh, d, d), g4)],
        scratch=[pltpu.VMEM((hh, d, d), F32)], sem=("arbitrary",))(qkv, gc, gr, gl, bc)


def _dn_bwd_call(qkv, gc, gr, gl, bc, states, do, name):
    s, w3 = qkv.shape
    width = w3 // 3
    hh = width // DN_HEAD_DIM
    d = DN_HEAD_DIM
    n = s // CHUNK

    def body(qkv_ref, gc_ref, gr_ref, gl_ref, bc_ref, st_ref, do_ref,
             dqkv_ref, dgc_ref, dgr_ref, dgl_ref, dbc_ref, dstate):
        @pl.when(pl.program_id(0) == 0)
        def _():
            dstate[...] = jnp.zeros_like(dstate)

        prim = (st_ref[0], _dn_heads(qkv_ref, 0, hh), _dn_heads(qkv_ref, width, hh),
                _dn_heads(qkv_ref, 2 * width, hh), gc_ref[0], gr_ref[0], gl_ref[0], bc_ref[0])
        _, vjp = jax.vjp(_dn_chunk, *prim)
        ds, dq, dk, dv, dgc, dgr, dgl, dbc = vjp((dstate[...], _dn_heads(do_ref, 0, hh)))
        dstate[...] = ds
        for h in range(hh):
            dqkv_ref[:, h * d:(h + 1) * d] = dq[h]
            dqkv_ref[:, width + h * d: width + (h + 1) * d] = dk[h]
            dqkv_ref[:, 2 * width + h * d: 2 * width + (h + 1) * d] = dv[h]
        dgc_ref[0] = dgc
        dgr_ref[0] = dgr
        dgl_ref[0] = dgl
        dbc_ref[0] = dbc

    r2 = lambda i: (n - 1 - i, 0)
    r4 = lambda i: (n - 1 - i, 0, 0, 0)
    spec_c = pl.BlockSpec((1, hh, CHUNK, 1), r4)
    spec_r = pl.BlockSpec((1, hh, 1, CHUNK), r4)
    spec_l = pl.BlockSpec((1, hh, 1, 1), r4)
    return _pcall(
        body, name=name + "_bwd",
        out_shape=[jax.ShapeDtypeStruct(qkv.shape, F32), jax.ShapeDtypeStruct(gc.shape, F32),
                   jax.ShapeDtypeStruct(gr.shape, F32), jax.ShapeDtypeStruct(gl.shape, F32),
                   jax.ShapeDtypeStruct(bc.shape, F32)],
        grid=(n,),
        in_specs=[pl.BlockSpec((CHUNK, w3), r2), spec_c, spec_r, spec_l, spec_c,
                  pl.BlockSpec((1, hh, d, d), r4), pl.BlockSpec((CHUNK, width), r2)],
        out_specs=[pl.BlockSpec((CHUNK, w3), r2), spec_c, spec_r, spec_l, spec_c],
        scratch=[pltpu.VMEM((hh, d, d), F32)], sem=("arbitrary",))(qkv, gc, gr, gl, bc, states, do)


@functools.partial(jax.custom_vjp, nondiff_argnums=(5,))
def dn_core(qkv, gc, gr, gl, bc, name):
    return _dn_fwd_call(qkv, gc, gr, gl, bc, name)[0]


def _dn_core_fwd(qkv, gc, gr, gl, bc, name):
    o, states = _dn_fwd_call(qkv, gc, gr, gl, bc, name)
    return o, (qkv, gc, gr, gl, bc, states)


def _dn_core_bwd(name, saved, do):
    return tuple(_dn_bwd_call(*saved, do, name))


dn_core.defvjp(_dn_core_fwd, _dn_core_bwd)


def dn_out_gate(o, z, out_norm, name):
    hh = o.shape[1] // DN_HEAD_DIM
    gain = jnp.tile(out_norm.reshape(1, -1), (1, hh))

    def fn(ot, zt, gt):
        y = ot * lax.rsqrt(jnp.mean(ot * ot, axis=-1, keepdims=True) + EPS) * gt
        return (y * _silu(zt),)

    return tilewise(fn, name, [o, z, gain], ["tile", "tile", "tile"], [(o.shape[0], BF16)],
                    axis=1, t=DN_HEAD_DIM, diff=[True, True, True])[0]


def _attn_tile(qn, kn, v, qp, kp, q0, scale, causal):
    s = _nt(qn, kn)
    if qp is not None:
        s = s + _nt(qp, kp)
    s = s * scale
    if causal:
        qpos = q0 + lax.broadcasted_iota(jnp.int32, s.shape, 0)
        kpos = lax.broadcasted_iota(jnp.int32, s.shape, 1)
        s = jnp.where((kpos >> _CHUNK_SHIFT) <= (qpos >> _CHUNK_SHIFT), s, -1e30)
    e = jnp.exp(s - jnp.max(s, axis=-1, keepdims=True))
    p = e / jnp.sum(e, axis=-1, keepdims=True)
    return _nn(p, v)


def _attn_specs(sq, sk, dh, dp, tq):
    q_spec = pl.BlockSpec((tq, dh), lambda h, i: (i, h))
    kv_spec = pl.BlockSpec((sk, dh), lambda h, i: (0, h))
    qp_spec = pl.BlockSpec((1, tq, dp), lambda h, i: (h, i, 0)) if dp else None
    kp_spec = pl.BlockSpec((sk, dp), lambda h, i: (0, 0)) if dp else None
    return q_spec, kv_spec, qp_spec, kp_spec


def _attn_fwd_call(q, k, v, qp, kp, *, dh, scale, causal, tq, name):
    sq, sk = q.shape[0], k.shape[0]
    heads = q.shape[1] // dh
    dp = qp.shape[-1] if qp is not None else 0
    q_spec, kv_spec, qp_spec, kp_spec = _attn_specs(sq, sk, dh, dp, tq)

    def body(*refs):
        if dp:
            q_ref, k_ref, v_ref, qp_ref, kp_ref, o_ref = refs
            o = _attn_tile(q_ref[...], k_ref[...], v_ref[...], qp_ref[0], kp_ref[...],
                           pl.program_id(1) * tq, scale, causal)
        else:
            q_ref, k_ref, v_ref, o_ref = refs
            o = _attn_tile(q_ref[...], k_ref[...], v_ref[...], None, None, pl.program_id(1) * tq, scale, causal)
        o_ref[...] = o

    in_specs = [q_spec, kv_spec, kv_spec] + ([qp_spec, kp_spec] if dp else [])
    args = (q, k, v) + ((qp, kp) if dp else ())
    return _pcall(body, name=name + "_fwd", out_shape=jax.ShapeDtypeStruct((sq, heads * dh), F32),
                  grid=(heads, sq // tq), in_specs=in_specs, out_specs=q_spec,
                  sem=("parallel", "parallel"))(*args)


def _attn_bwd_call(q, k, v, qp, kp, do, *, dh, scale, causal, tq, name):
    sq, sk = q.shape[0], k.shape[0]
    heads = q.shape[1] // dh
    dp = qp.shape[-1] if qp is not None else 0
    q_spec, kv_spec, qp_spec, kp_spec = _attn_specs(sq, sk, dh, dp, tq)

    def body(*refs):
        h, i = pl.program_id(0), pl.program_id(1)
        if dp:
            q_ref, k_ref, v_ref, qp_ref, kp_ref, do_ref, dq_ref, dk_ref, dv_ref, dqp_ref, dkp_ref = refs
            prim = (q_ref[...], k_ref[...], v_ref[...], qp_ref[0], kp_ref[...])
            f = lambda a, b, c, d, e: _attn_tile(a, b, c, d, e, i * tq, scale, causal)
        else:
            q_ref, k_ref, v_ref, do_ref, dq_ref, dk_ref, dv_ref = refs
            prim = (q_ref[...], k_ref[...], v_ref[...])
            f = lambda a, b, c: _attn_tile(a, b, c, None, None, i * tq, scale, causal)
        _, vjp = jax.vjp(f, *prim)
        grads = vjp(do_ref[...])
        dq_ref[...] = grads[0]

        @pl.when(i == 0)
        def _():
            dk_ref[...] = jnp.zeros_like(dk_ref)
            dv_ref[...] = jnp.zeros_like(dv_ref)

        dk_ref[...] += grads[1]
        dv_ref[...] += grads[2]
        if dp:
            dqp_ref[0] = grads[3]

            @pl.when(jnp.logical_and(h == 0, i == 0))
            def _():
                dkp_ref[...] = jnp.zeros_like(dkp_ref)

            dkp_ref[...] += grads[4]

    in_specs = [q_spec, kv_spec, kv_spec] + ([qp_spec, kp_spec] if dp else []) + [q_spec]
    out_shape = [jax.ShapeDtypeStruct(q.shape, F32), jax.ShapeDtypeStruct(k.shape, F32),
                 jax.ShapeDtypeStruct(v.shape, F32)]
    out_specs = [q_spec, kv_spec, kv_spec]
    if dp:
        out_shape += [jax.ShapeDtypeStruct(qp.shape, F32), jax.ShapeDtypeStruct(kp.shape, F32)]
        out_specs += [qp_spec, kp_spec]
    args = (q, k, v) + ((qp, kp) if dp else ()) + (do,)
    return _pcall(body, name=name + "_bwd", out_shape=out_shape, grid=(heads, sq // tq), in_specs=in_specs,
                  out_specs=out_specs, sem=("arbitrary", "arbitrary"))(*args)


def attention(q, k, v, qp=None, kp=None, *, dh, scale, causal, name, tq=256):
    tq = _tile(q.shape[0], tq)
    kw = dict(dh=dh, scale=scale, causal=causal, tq=tq, name=name)
    has_pe = qp is not None

    @jax.custom_vjp
    def op(*xs):
        return _attn_fwd_call(*xs, **kw) if has_pe else _attn_fwd_call(*xs, None, None, **kw)

    def op_fwd(*xs):
        return (_attn_fwd_call(*xs, **kw) if has_pe else _attn_fwd_call(*xs, None, None, **kw)), xs

    def op_bwd(xs, do):
        full = xs if has_pe else xs + (None, None)
        return tuple(_attn_bwd_call(*full, do, **kw))

    op.defvjp(op_fwd, op_bwd)
    return op(q, k, v, qp, kp) if has_pe else op(q, k, v)


def _rope_tables(positions, reps):
    half = MLA_ROPE // 2
    inv = ROPE_BASE ** (-jnp.arange(0, MLA_ROPE, 2, dtype=F32) / MLA_ROPE)
    ang = positions.astype(F32)[:, None] * inv
    cos, sin = jnp.cos(ang), jnp.sin(ang)
    c = jnp.tile(jnp.concatenate([cos, cos], axis=-1), (1, reps))
    s = jnp.tile(jnp.concatenate([sin, sin], axis=-1), (1, reps))
    rot = np.zeros((MLA_ROPE, MLA_ROPE), np.float32)
    for i in range(half):
        rot[i + half, i] = -1.0
        rot[i, i + half] = 1.0
    return c, s, jnp.asarray(np.kron(np.eye(reps, dtype=np.float32), rot))


def rope(x, positions, name):
    c, s, rot = _rope_tables(positions, x.shape[1] // MLA_ROPE)

    def fn(xt, ct, st, rt):
        return (xt * ct + _nn_hi_const(xt, rt) * st,)

    return tilewise(fn, name, [x, c, s, rot], ["tile", "tile", "tile", "whole"], [(x.shape[1], F32)],
                    axis=0, t=_tile(x.shape[0], 256), diff=[True, False, False, False])[0]


def ffn_gate(pre_g, pre_u, conv_g, conv_u, bias_g, bias_u, name):
    def fn(g, u, wg, wu, bg, bu):
        return (_silu(_causal_dwconv_tile(g, wg) + bg) * (_causal_dwconv_tile(u, wu) + bu),)

    return tilewise(fn, name, [pre_g, pre_u, conv_g, conv_u, bias_g, bias_u], ["tile"] * 6,
                    [(pre_g.shape[0], BF16)], axis=1, t=_tile(pre_g.shape[1], 256), diff=[True] * 6)[0]


def loss_rows(h, gain, target, name):
    def fn(ht, tt, gt):
        err = _rms_tile(ht, gt, F32) - tt
        return (0.5 * jnp.mean(err * err, axis=-1, keepdims=True),)

    return tilewise(fn, name, [h, target, gain.reshape(1, -1)], ["tile", "tile", "whole"], [(1, F32)],
                    axis=0, t=_tile(h.shape[0], 256), diff=[True, False, True])[0]


def _col_groups(w, per_head, lo, hi):
    r = w.shape[0]
    return w.reshape(r, -1, per_head)[:, :, lo:hi].reshape(r, -1)


def _layer_mix(h, lw, rp, l, positions):
    s, d = h.shape
    width = d // 2
    dn_heads = width // DN_HEAD_DIM
    mla_heads = (d - width) // MLA_V
    nm = f"l{l}_"
    w_in = lw["w_in"]
    c0 = 4 * width
    c1 = c0 + 2 * dn_heads
    c2 = c1 + MLA_Q_RANK
    c3 = c2 + MLA_KV_RANK + MLA_ROPE
    rest_pad = (-(c3 - c0)) % LANE
    w_rest = jnp.pad(w_in[:, c0:c3], ((0, 0), (0, rest_pad)))

    u = rms_norm(h, rp["norm_mix"], nm + "norm_mix", BF16)
    qkv_raw = linear(u, w_in[:, :3 * width], nm + "in_qkv")
    z = linear(u, w_in[:, 3 * width:c0], nm + "in_z")
    rest = linear(u, w_rest, nm + "in_rest")

    qkv = dn_qkv(qkv_raw, lw["dn_conv"], nm + "dn_qkv")
    csum, beta = dn_gates(rest[:, :dn_heads], rest[:, dn_heads:2 * dn_heads], rp["dn_a_log"],
                          rp["dn_dt_bias"], nm + "dn_gates")
    n = s // CHUNK
    g3 = csum.reshape(n, CHUNK, dn_heads).transpose(0, 2, 1)
    b3 = beta.reshape(n, CHUNK, dn_heads).transpose(0, 2, 1)
    o_dn = dn_core(qkv, g3[..., None], g3[:, :, None, :], g3[:, :, CHUNK - 1][..., None, None], b3[..., None],
                   nm + "dn_core")
    o_dn = dn_out_gate(o_dn, z, rp["dn_out_norm"], nm + "dn_gate")

    mq = rest[:, c1 - c0:c2 - c0]
    mkv = rest[:, c2 - c0:c3 - c0]
    qn = rms_norm(mq, rp["mla_q_norm"], nm + "mla_qnorm", BF16)
    per_q = MLA_NOPE + MLA_ROPE
    q_nope = linear(qn, _col_groups(lw["mla_w_qb"], per_q, 0, MLA_NOPE), nm + "mla_qn")
    q_pe = linear(qn, _col_groups(lw["mla_w_qb"], per_q, MLA_NOPE, per_q), nm + "mla_qp")
    kvn = rms_norm(mkv[:, :MLA_KV_RANK], rp["mla_kv_norm"], nm + "mla_kvnorm", BF16)
    per_kv = MLA_NOPE + MLA_V
    k_nope = linear(kvn, _col_groups(lw["mla_w_kvb"], per_kv, 0, MLA_NOPE), nm + "mla_kn")
    v_mla = linear(kvn, _col_groups(lw["mla_w_kvb"], per_kv, MLA_NOPE, per_kv), nm + "mla_v")
    q_pe = rope(q_pe, positions, nm + "rope_q")
    k_pe = rope(mkv[:, MLA_KV_RANK:], positions, nm + "rope_k")
    q_pe = q_pe.reshape(s, mla_heads, MLA_ROPE).transpose(1, 0, 2)
    o_mla = attention(q_nope, k_nope, v_mla, q_pe, k_pe, dh=MLA_NOPE, scale=per_q ** -0.5, causal=True,
                      name=nm + "mla_attn")

    h = linear(o_dn, lw["w_out"][:width], nm + "out_dn", res=h)
    return linear(o_mla, lw["w_out"][width:], nm + "out_mla", res=h)


def _layer_tail(h, lw, rp, l, mem_n):
    d = h.shape[1]
    nm = f"l{l}_"
    hn = rms_norm(h, rp["norm_xattn"], nm + "norm_xattn", BF16)
    xq = linear(hn, lw["xa_wq"], nm + "xa_q")
    xk = linear(mem_n, lw["xa_wk"], nm + "xa_k")
    xv = linear(mem_n, lw["xa_wv"], nm + "xa_v")
    xdh = d // XA_HEADS
    xo = attention(xq, xk, xv, dh=xdh, scale=xdh ** -0.5, causal=False, name=nm + "xattn")
    h = linear(xo, lw["xa_wo"], nm + "xa_o", res=h)

    hn = rms_norm(h, rp["norm_ffn"], nm + "norm_ffn", BF16)
    w_up = lw["ffn_w_up"]
    d_ff = w_up.shape[1] // 2
    pre_g = linear(hn, w_up[:, :d_ff], nm + "ffn_upg")
    pre_u = linear(hn, w_up[:, d_ff:], nm + "ffn_upu")
    bias = rp["ffn_conv_bias"].reshape(1, -1)
    conv = lw["ffn_conv"]
    act = ffn_gate(pre_g, pre_u, conv[:, :d_ff], conv[:, d_ff:], bias[:, :d_ff], bias[:, d_ff:], nm + "ffn_gate")
    return linear(act, lw["ffn_w_down"], nm + "ffn_down", res=h)


_COL_SHARDED = ("w_in", "mla_w_qb", "mla_w_kvb", "ffn_w_up", "dn_conv", "ffn_conv")
_ROW_SHARDED = ("w_out", "xa_wq", "xa_wk", "xa_wv", "xa_wo", "ffn_w_down")
_BIG = ("w_in", "mla_w_qb", "mla_w_kvb", "w_out", "xa_wq", "xa_wk", "xa_wv", "xa_wo", "ffn_w_up", "ffn_w_down")
_MIX_KEYS = ("w_in", "mla_w_qb", "mla_w_kvb", "w_out")
_TAIL_KEYS = ("xa_wq", "xa_wk", "xa_wv", "xa_wo", "ffn_w_up", "ffn_w_down")
_SMALL_SHARDED = ("dn_conv", "ffn_conv")
_REP_MIX = ("norm_mix", "dn_a_log", "dn_dt_bias", "dn_out_norm", "mla_q_norm", "mla_kv_norm")
_REP_TAIL = ("norm_xattn", "norm_ffn", "ffn_conv_bias")
_REPLICATED = ("norm_mix", "dn_a_log", "dn_dt_bias", "dn_out_norm", "mla_q_norm", "mla_kv_norm", "mem_norm",
               "norm_xattn", "norm_ffn", "ffn_conv_bias", "norm_final")
_WEIGHTS = ("norm_mix", "w_in", "dn_conv", "dn_a_log", "dn_dt_bias", "dn_out_norm", "mla_q_norm", "mla_w_qb",
            "mla_kv_norm", "mla_w_kvb", "w_out", "mem_norm", "norm_xattn", "xa_wq", "xa_wk", "xa_wv", "xa_wo",
            "norm_ffn", "ffn_w_up", "ffn_conv", "ffn_conv_bias", "ffn_w_down", "norm_final")


def _assemble(name, g):
    if name in _ROW_SHARDED:
        return g.reshape((-1,) + g.shape[2:])
    return jnp.moveaxis(g, 0, -2).reshape(g.shape[1:-1] + (-1,))


def _mix(h, gw, conv, rp, l, positions):
    lw = {k: _assemble(k, v) for k, v in gw.items()}
    lw["dn_conv"] = _assemble("dn_conv", conv)
    return _layer_mix(h, lw, rp, l, positions)


def _tail(h, gw, conv, rp, l, mem_n):
    lw = {k: _assemble(k, v) for k, v in gw.items()}
    lw["ffn_conv"] = _assemble("ffn_conv", conv)
    return _layer_tail(h, lw, rp, l, mem_n)


def _my_index():
    return 4 * lax.axis_index("x") + 2 * lax.axis_index("y") + lax.axis_index("c")


def _peer(k):
    x, y, c = lax.axis_index("x"), lax.axis_index("y"), lax.axis_index("c")
    px = (1 - x) if k & 4 else x
    py = (1 - y) if k & 2 else y
    pc = (1 - c) if k & 1 else c
    return (px, py, pc), 4 * px + 2 * py + pc


_ANY = pl.BlockSpec(memory_space=pl.ANY)


def all_gather(shards, name):
    n = len(shards)

    def body(*refs):
        x_refs, o_refs = refs[:n], refs[n:2 * n]
        send_sems, recv_sems, local_sems = refs[2 * n:]
        me = _my_index()
        sib_id, sib = _peer(1)
        chips = [_peer(k) for k in (4, 2, 6)]

        def copy(a, k, block, to, src=None):
            return pltpu.make_async_remote_copy(
                src_ref=o_refs[a].at[block] if src is None else src, dst_ref=o_refs[a].at[block],
                send_sem=send_sems.at[a * 7 + k], recv_sem=recv_sems.at[a * 7 + k],
                device_id=to, device_id_type=MESH)

        mine = [pltpu.make_async_copy(x_refs[a], o_refs[a].at[me], local_sems.at[a]) for a in range(n)]
        for cp in mine:
            cp.start()
        first = []
        for a in range(n):
            first.append(copy(a, 0, me, sib_id, src=x_refs[a]))
            first += [copy(a, 1 + j, me, cid, src=x_refs[a]) for j, (cid, _) in enumerate(chips)]
        for cp in first:
            cp.start()
        passed = []
        for a in range(n):
            for j, (cid, cidx) in enumerate(chips):
                copy(a, 1 + j, cidx, cid).wait_recv()
                fwd = copy(a, 4 + j, cidx, sib_id)
                fwd.start()
                passed.append(fwd)
        for a in range(n):
            copy(a, 0, sib, sib_id).wait_recv()
            for j, (_, cidx) in enumerate(chips):
                copy(a, 4 + j, cidx ^ 1, sib_id).wait_recv()
        for cp in first + passed:
            cp.wait_send()
        for cp in mine:
            cp.wait()

    out_shape = [jax.ShapeDtypeStruct((N_DEV,) + s.shape, s.dtype) for s in shards]
    return _pcall(body, name=name, out_shape=out_shape, in_specs=[_ANY] * n, out_specs=[_ANY] * n,
                  scratch=[pltpu.SemaphoreType.DMA((7 * n,)), pltpu.SemaphoreType.DMA((7 * n,)),
                           pltpu.SemaphoreType.DMA((n,))])(*shards)


def all_to_all(parts, name):
    n = len(parts)

    def body(*refs):
        x_refs, o_refs = refs[:n], refs[n:2 * n]
        send_sems, recv_sems, local_sems = refs[2 * n:]
        me = _my_index()
        mine = [pltpu.make_async_copy(x_refs[a].at[me], o_refs[a].at[me], local_sems.at[a]) for a in range(n)]
        for cp in mine:
            cp.start()
        sends = []
        for a in range(n):
            for k in range(1, N_DEV):
                pid, pidx = _peer(k)
                sends.append(pltpu.make_async_remote_copy(
                    src_ref=x_refs[a].at[pidx], dst_ref=o_refs[a].at[me],
                    send_sem=send_sems.at[a * 7 + k - 1], recv_sem=recv_sems.at[a * 7 + k - 1],
                    device_id=pid, device_id_type=MESH))
        for cp in sends:
            cp.start()
        for a in range(n):
            for k in range(1, N_DEV):
                pid, pidx = _peer(k)
                pltpu.make_async_remote_copy(
                    src_ref=x_refs[a].at[pidx], dst_ref=o_refs[a].at[pidx],
                    send_sem=send_sems.at[a * 7 + k - 1], recv_sem=recv_sems.at[a * 7 + k - 1],
                    device_id=pid, device_id_type=MESH).wait_recv()
        for cp in sends:
            cp.wait_send()
        for cp in mine:
            cp.wait()

    out_shape = [jax.ShapeDtypeStruct(p.shape, p.dtype) for p in parts]
    return _pcall(body, name=name, out_shape=out_shape, in_specs=[_ANY] * n, out_specs=[_ANY] * n,
                  scratch=[pltpu.SemaphoreType.DMA((7 * n,)), pltpu.SemaphoreType.DMA((7 * n,)),
                           pltpu.SemaphoreType.DMA((n,))])(*parts)


_HBM = pl.BlockSpec(memory_space=pltpu.HBM)
_SEM = pl.BlockSpec(memory_space=pltpu.SEMAPHORE)
_EFFECT = pltpu.SideEffectType.DATAFLOW_SIDE_EFFECTING


def _plan_gather_first(n):
    def plan(refs):
        me = _my_index()
        sib_id, sib = _peer(1)
        chips = [_peer(k) for k in (4, 2, 6)]
        out = []
        for a in range(n):
            x, land = refs[a], refs[n + a]
            out.append((x, land.at[me], sib_id, land.at[sib]))
            out += [(x, land.at[me], cid, land.at[cidx]) for cid, cidx in chips]
        return out

    return plan, 4 * n


def _plan_gather_pass(n):
    def plan(refs):
        sib_id, _ = _peer(1)
        chips = [_peer(k) for k in (4, 2, 6)]
        return [(refs[a].at[cidx], refs[a].at[cidx], sib_id, refs[a].at[cidx ^ 1])
                for a in range(n) for _, cidx in chips]

    return plan, 3 * n


def _plan_scatter(n):
    def plan(refs):
        me = _my_index()
        out = []
        for a in range(n):
            for k in range(1, N_DEV):
                pid, pidx = _peer(k)
                out.append((refs[a].at[pidx], refs[n + a].at[me], pid, refs[n + a].at[pidx]))
        return out

    return plan, 7 * n


def _remote_copy(src, dst, send_sems, recv_sems, i, dev):
    return pltpu.make_async_remote_copy(src_ref=src, dst_ref=dst, send_sem=send_sems.at[i], recv_sem=recv_sems.at[i],
                                        device_id=dev, device_id_type=MESH)


def exchange_start(bufs, plan_n, name, after=None):
    plan, n = plan_n
    nb = len(bufs)
    n_in = nb + (after is not None)

    def body(*refs):
        send_sems, recv_sems = refs[n_in], refs[n_in + 1]
        for i, (src, dst, dev, _) in enumerate(plan(refs[:nb])):
            _remote_copy(src, dst, send_sems, recv_sems, i, dev).start()
        refs[-1][...] = jnp.zeros_like(refs[-1])

    out_shape = (pltpu.SemaphoreType.DMA((n,)), pltpu.SemaphoreType.DMA((n,)),
                 *[pltpu.HBM(b.shape, b.dtype) for b in bufs], jax.ShapeDtypeStruct((8, LANE), F32))
    args = [pltpu.with_memory_space_constraint(b, pltpu.HBM) for b in bufs] + ([after] if after is not None else [])
    res = pl.pallas_call(
        body, name=name, out_shape=out_shape,
        in_specs=[_HBM] * nb + ([_ANY] if after is not None else []),
        out_specs=(_SEM, _SEM, *[_HBM] * nb, pl.BlockSpec(memory_space=pltpu.VMEM)),
        input_output_aliases={i: 2 + i for i in range(nb)},
        compiler_params=pltpu.CompilerParams(has_side_effects=_EFFECT), interpret=_INTERPRET)(*args)
    return (res[0], res[1], list(res[2:2 + nb])), res[-1]


def exchange_wait(flight, plan_n, name, after):
    plan, _ = plan_n
    send_sems, recv_sems, bufs = flight
    nb = len(bufs)

    def body(*refs):
        s_sems, r_sems = refs[nb], refs[nb + 1]
        for i, (src, _, dev, arrival) in enumerate(plan(refs[:nb])):
            cp = _remote_copy(src, arrival, s_sems, r_sems, i, dev)
            cp.wait_send()
            cp.wait_recv()

    return list(pl.pallas_call(
        body, name=name, out_shape=tuple(pltpu.HBM(b.shape, b.dtype) for b in bufs),
        in_specs=[_HBM] * nb + [_SEM, _SEM, _ANY], out_specs=tuple([_HBM] * nb),
        input_output_aliases={i: i for i in range(nb)},
        compiler_params=pltpu.CompilerParams(has_side_effects=_EFFECT), interpret=_INTERPRET)(
            *bufs, send_sems, recv_sems, after))


def adam_update(recv, w, m, v, name):
    ll, _, r, c = recv.shape
    tr = _tile(r, max(8, 1 << ((_ADAM_TILE_ELEMS // c).bit_length() - 1)))

    def body(g_ref, w_ref, m_ref, v_ref, go_ref, d_ref, mo_ref, vo_ref):
        g = g_ref[0, 0].astype(F32)
        for j in range(1, N_DEV):
            g = g + g_ref[0, j].astype(F32)
        m_new = ADAM_B1 * m_ref[0] + (1.0 - ADAM_B1) * g
        v_new = ADAM_B2 * v_ref[0] + (1.0 - ADAM_B2) * (g * g)
        m_hat = m_new / (1.0 - ADAM_B1 ** ADAM_STEP)
        v_hat = v_new / (1.0 - ADAM_B2 ** ADAM_STEP)
        go_ref[0] = g
        d_ref[0] = -ADAM_LR * (m_hat / (jnp.sqrt(v_hat) + ADAM_EPS) + ADAM_WD * w_ref[0])
        mo_ref[0] = m_new
        vo_ref[0] = v_new

    spec = pl.BlockSpec((1, tr, c), lambda l, i: (l, i, 0))
    sds = jax.ShapeDtypeStruct((ll, r, c), F32)
    return _pcall(body, name=name, out_shape=[sds] * 4, grid=(ll, r // tr),
                  in_specs=[pl.BlockSpec((1, N_DEV, tr, c), lambda l, i: (l, 0, i, 0)), spec, spec, spec],
                  out_specs=[spec] * 4, sem=("parallel", "parallel"))(recv, w, m, v)


def _as3d(a):
    if a.ndim == 1:
        return a.reshape(1, 1, -1)
    if a.ndim == 2:
        return a.reshape(1, a.shape[0], a.shape[1])
    return a.reshape(a.shape[0], -1, a.shape[-1])


def kernel(x, mem, positions, norm_mix, w_in, dn_conv, dn_a_log, dn_dt_bias, dn_out_norm, mla_q_norm, mla_w_qb, mla_kv_norm, mla_w_kvb, w_out, mem_norm, norm_xattn, xa_wq, xa_wk, xa_wv, xa_wo, norm_ffn, ffn_w_up, ffn_conv, ffn_conv_bias, ffn_w_down, norm_final, loss_target, m_norm_mix, m_w_in, m_dn_conv, m_dn_a_log, m_dn_dt_bias, m_dn_out_norm, m_mla_q_norm, m_mla_w_qb, m_mla_kv_norm, m_mla_w_kvb, m_w_out, m_mem_norm, m_norm_xattn, m_xa_wq, m_xa_wk, m_xa_wv, m_xa_wo, m_norm_ffn, m_ffn_w_up, m_ffn_conv, m_ffn_conv_bias, m_ffn_w_down, m_norm_final, v_norm_mix, v_w_in, v_dn_conv, v_dn_a_log, v_dn_dt_bias, v_dn_out_norm, v_mla_q_norm, v_mla_w_qb, v_mla_kv_norm, v_mla_w_kvb, v_w_out, v_mem_norm, v_norm_xattn, v_xa_wq, v_xa_wk, v_xa_wv, v_xa_wo, v_norm_ffn, v_ffn_w_up, v_ffn_conv, v_ffn_conv_bias, v_ffn_w_down, v_norm_final):
    w = dict(norm_mix=norm_mix, w_in=w_in, dn_conv=dn_conv, dn_a_log=dn_a_log, dn_dt_bias=dn_dt_bias,
             dn_out_norm=dn_out_norm, mla_q_norm=mla_q_norm, mla_w_qb=mla_w_qb, mla_kv_norm=mla_kv_norm,
             mla_w_kvb=mla_w_kvb, w_out=w_out, mem_norm=mem_norm, norm_xattn=norm_xattn, xa_wq=xa_wq, xa_wk=xa_wk,
             xa_wv=xa_wv, xa_wo=xa_wo, norm_ffn=norm_ffn, ffn_w_up=ffn_w_up, ffn_conv=ffn_conv,
             ffn_conv_bias=ffn_conv_bias, ffn_w_down=ffn_w_down, norm_final=norm_final)
    mom = dict(norm_mix=m_norm_mix, w_in=m_w_in, dn_conv=m_dn_conv, dn_a_log=m_dn_a_log, dn_dt_bias=m_dn_dt_bias,
               dn_out_norm=m_dn_out_norm, mla_q_norm=m_mla_q_norm, mla_w_qb=m_mla_w_qb, mla_kv_norm=m_mla_kv_norm,
               mla_w_kvb=m_mla_w_kvb, w_out=m_w_out, mem_norm=m_mem_norm, norm_xattn=m_norm_xattn, xa_wq=m_xa_wq,
               xa_wk=m_xa_wk, xa_wv=m_xa_wv, xa_wo=m_xa_wo, norm_ffn=m_norm_ffn, ffn_w_up=m_ffn_w_up,
               ffn_conv=m_ffn_conv, ffn_conv_bias=m_ffn_conv_bias, ffn_w_down=m_ffn_w_down, norm_final=m_norm_final)
    var = dict(norm_mix=v_norm_mix, w_in=v_w_in, dn_conv=v_dn_conv, dn_a_log=v_dn_a_log, dn_dt_bias=v_dn_dt_bias,
               dn_out_norm=v_dn_out_norm, mla_q_norm=v_mla_q_norm, mla_w_qb=v_mla_w_qb, mla_kv_norm=v_mla_kv_norm,
               mla_w_kvb=v_mla_w_kvb, w_out=v_w_out, mem_norm=v_mem_norm, norm_xattn=v_norm_xattn, xa_wq=v_xa_wq,
               xa_wk=v_xa_wk, xa_wv=v_xa_wv, xa_wo=v_xa_wo, norm_ffn=v_norm_ffn, ffn_w_up=v_ffn_w_up,
               ffn_conv=v_ffn_conv, ffn_conv_bias=v_ffn_conv_bias, ffn_w_down=v_ffn_w_down, norm_final=v_norm_final)
    depth = w_in.shape[0]
    n_big = len(_BIG)
    me = _my_index()
    rep = {k: w[k] for k in _REPLICATED}
    x0, mem0, pos0, target0 = x[0], mem[0], positions[0], loss_target[0]

    def with_own_block(block):
        return lax.dynamic_update_index_in_dim(lax.empty((N_DEV,) + block.shape, block.dtype), block, me, 0)

    plan_first, plan_pass, plan_scatter = _plan_gather_first(n_big), _plan_gather_pass(n_big), _plan_scatter(n_big)

    def gather_first_start(l, after):
        shards = [w[k][l].astype(BF16) for k in _BIG]
        return exchange_start(shards + [with_own_block(sh) for sh in shards], plan_first, f"gather_l{l}_first_start", after)

    small = dict(zip(_SMALL_SHARDED, all_gather([w[k] for k in _SMALL_SHARDED], "gather_small")))

    flight, token = gather_first_start(0, None)
    lands = exchange_wait(flight, plan_first, "gather_l0_first_wait", token)[n_big:]
    flight, token = exchange_start(lands, plan_pass, "gather_l0_pass_start")
    lands = exchange_wait(flight, plan_pass, "gather_l0_pass_wait", token)

    mem_n, vjp_mem = jax.vjp(lambda g: rms_norm(mem0, g, "mem_norm", BF16), rep["mem_norm"])

    h = x0
    tapes = []
    for l in range(depth):
        gathered = dict(zip(_BIG, lands))
        more = l + 1 < depth
        rp_mix = {k: rep[k][l] for k in _REP_MIX}
        rp_tail = {k: rep[k][l] for k in _REP_TAIL}
        if more:
            flight, token = gather_first_start(l + 1, h)
            rp_mix["norm_mix"] = rp_mix["norm_mix"] + token[0, 0]
        h1, vjp_mix = jax.vjp(lambda hh, gw, cv, rp: _mix(hh, gw, cv, rp, l, pos0), h,
                              {k: gathered[k] for k in _MIX_KEYS}, small["dn_conv"][:, l], rp_mix)
        if more:
            lands = exchange_wait(flight, plan_first, f"gather_l{l + 1}_first_wait", h1)[n_big:]
            flight, token = exchange_start(lands, plan_pass, f"gather_l{l + 1}_pass_start")
            rp_tail["norm_xattn"] = rp_tail["norm_xattn"] + token[0, 0]
        h2, vjp_tail = jax.vjp(lambda hh, gw, cv, rp, mn: _tail(hh, gw, cv, rp, l, mn), h1,
                               {k: gathered[k] for k in _TAIL_KEYS}, small["ffn_conv"][:, l], rp_tail, mem_n)
        if more:
            lands = exchange_wait(flight, plan_pass, f"gather_l{l + 1}_pass_wait", h2)
        tapes.append((vjp_mix, vjp_tail))
        h = h2

    rows, vjp_loss = jax.vjp(lambda hh, g: loss_rows(hh, g, target0, "loss"), h, rep["norm_final"])
    loss = lax.psum(jnp.sum(rows), ("x", "y", "c"))
    dh, d_norm_final = vjp_loss(jnp.ones_like(rows))

    d_rep = {k: [None] * depth for k in _REP_MIX + _REP_TAIL}
    d_conv = {k: [None] * depth for k in _SMALL_SHARDED}
    recv_big = [None] * depth
    d_mem_n = None
    flight = None
    for l in reversed(range(depth)):
        if flight is not None:
            dh = dh + token[0, 0]
        vjp_mix, vjp_tail = tapes[l]
        dh, dg_tail, d_conv["ffn_conv"][l], d_rt, d_mn = vjp_tail(dh)
        dh, dg_mix, d_conv["dn_conv"][l], d_rm = vjp_mix(dh)
        d_mem_n = d_mn if d_mem_n is None else d_mem_n + d_mn
        for k, v in {**d_rm, **d_rt}.items():
            d_rep[k][l] = v
        if flight is not None:
            recv_big[l + 1] = exchange_wait(flight, plan_scatter, f"scatter_l{l + 1}_wait", dh)[n_big:]
        parts = [{**dg_mix, **dg_tail}[k] for k in _BIG]
        own = [with_own_block(lax.dynamic_index_in_dim(p, me, 0, keepdims=False)) for p in parts]
        flight, token = exchange_start(parts + own, plan_scatter, f"scatter_l{l}_start")

    (d_mem_norm,) = vjp_mem(d_mem_n)
    d_rep_all = {k: jnp.stack(v) for k, v in d_rep.items()}
    d_rep_all["mem_norm"] = d_mem_norm
    d_rep_all["norm_final"] = d_norm_final + token[0, 0]
    out = {}
    recv_small = all_to_all([jnp.stack(d_conv[k], axis=1) for k in _SMALL_SHARDED], "scatter_small")
    for k, recv in zip(_SMALL_SHARDED, recv_small):
        r3 = _as3d(w[k])
        out[k] = adam_update(recv.reshape((1, N_DEV, -1, recv.shape[-1])), r3.reshape((1, -1, r3.shape[-1])),
                             _as3d(mom[k]).reshape((1, -1, r3.shape[-1])), _as3d(var[k]).reshape((1, -1, r3.shape[-1])),
                             "adam_" + k)
    recv_rep = all_gather([_as3d(d_rep_all[k])[0] for k in _REPLICATED], "gather_rep_grads")
    for k, recv in zip(_REPLICATED, recv_rep):
        r3 = _as3d(w[k])
        flat = lambda a, r3=r3: a.reshape((1, -1, r3.shape[-1]))
        out[k] = adam_update(recv[None], flat(w[k]), flat(mom[k]), flat(var[k]), "adam_" + k)
    recv_big[0] = exchange_wait(flight, plan_scatter, "scatter_l0_wait", out["norm_final"][0])[n_big:]

    for i, k in enumerate(_BIG):
        recv = jnp.stack([recv_big[l][i] for l in range(depth)])
        out[k] = adam_update(recv.reshape((depth, N_DEV, -1, recv.shape[-1])), _as3d(w[k]), _as3d(mom[k]),
                             _as3d(var[k]), "adam_" + k)

    res = [loss, dh[None]]
    for j in range(4):
        res += [out[k][j].reshape(w[k].shape) for k in _WEIGHTS]
    return tuple(res)
```

```python
import functools

import numpy as np
import jax
import jax.numpy as jnp
from jax import lax
from jax.experimental import pallas as pl
from jax.experimental.pallas import tpu as pltpu

F32 = jnp.float32
BF16 = jnp.bfloat16
_MXU_DTYPE = BF16
_INTERPRET = False
_VMEM_LIMIT_BYTES = 48 * 1024 * 1024
_MM_VMEM_BUDGET_BYTES = 36 * 1024 * 1024
_MM_TILE_CAP = 2048
_HI = lax.Precision.HIGHEST

N_DEV = 8
CHUNK = 64
_CHUNK_SHIFT = 6
DN_HEAD_DIM = 128
DN_CONV = 4
MLA_NOPE = 128
MLA_ROPE = 64
MLA_V = 128
MLA_Q_RANK = 512
MLA_KV_RANK = 256
ROPE_BASE = 10000.0
XA_HEADS = 4
FFN_CONV = 3
EPS = 1e-6
LANE = 128

ADAM_LR = 0.001
ADAM_B1 = 0.9
ADAM_B2 = 0.999
ADAM_EPS = 1e-08
ADAM_WD = 0.01
ADAM_STEP = 10
_ADAM_TILE_ELEMS = 128 * 1024

MESH = pl.DeviceIdType.MESH


def _pcall(body, *, name, out_shape, grid=None, in_specs=None, out_specs=None, scratch=(), sem=None, **kw):
    params = pltpu.CompilerParams(dimension_semantics=sem, vmem_limit_bytes=_VMEM_LIMIT_BYTES)
    args = dict(name=name, out_shape=out_shape, scratch_shapes=list(scratch), compiler_params=params,
                interpret=_INTERPRET, **kw)
    if grid is not None:
        args.update(grid=grid)
    if in_specs is not None:
        args.update(in_specs=in_specs)
    if out_specs is not None:
        args.update(out_specs=out_specs)
    return pl.pallas_call(body, **args)


def _tile(n, pref):
    if n <= pref:
        return n
    t = pref
    while t >= 8:
        if n % t == 0:
            return t
        t //= 2
    return n


def _ein_raw(spec, a, b, hi):
    if hi:
        return jnp.einsum(spec, a.astype(F32), b.astype(F32), precision=_HI, preferred_element_type=F32)
    return jnp.einsum(spec, a.astype(_MXU_DTYPE), b.astype(_MXU_DTYPE), preferred_element_type=F32)


def _make_ein(spec, hi=False, diff_b=True):
    a_s, rest = spec.split(",")
    b_s, o_s = rest.split("->")

    @jax.custom_vjp
    def f(a, b):
        return _ein_raw(spec, a, b, hi)

    def fwd(a, b):
        return f(a, b), (a, b)

    def bwd(res, g):
        a, b = res
        da = _ein_raw(f"{o_s},{b_s}->{a_s}", g, b, hi)
        if not diff_b:
            return da.astype(a.dtype), None
        db = _ein_raw(f"{a_s},{o_s}->{b_s}", a, g, hi)
        return da.astype(a.dtype), db.astype(b.dtype)

    f.defvjp(fwd, bwd)
    return f


_nt = _make_ein("qd,kd->qk")
_nn = _make_ein("qk,kd->qd")
_nn_hi_const = _make_ein("qk,kd->qd", hi=True, diff_b=False)
_bnt = _make_ein("hik,hjk->hij")
_bnt_hi = _make_ein("hik,hjk->hij", hi=True)
_bnn = _make_ein("hij,hjv->hiv")
_bnn_hi = _make_ein("hij,hjv->hiv", hi=True)
_btn = _make_ein("hck,hcv->hkv")


def _divisor_tiles(n, cap):
    out = [n] if n <= cap else []
    t = cap
    while t >= LANE:
        if t < n and n % t == 0:
            out.append(t)
        t //= 2
    return out or [n]


def _mm_tiles(m, n, k, a_bytes, b_bytes, o_bytes, r_bytes):
    best = None
    for tk in _divisor_tiles(k, _MM_TILE_CAP):
        for tm in _divisor_tiles(m, _MM_TILE_CAP):
            for tn in _divisor_tiles(n, _MM_TILE_CAP):
                need = 2 * (tm * tk * a_bytes + tk * tn * b_bytes + tm * tn * (o_bytes + r_bytes))
                need += tm * tn * 4 * (1 if tk == k else 2)
                if need > _MM_VMEM_BUDGET_BYTES:
                    continue
                score = (tm * tn * tk, tk, tm)
                if best is None or score > best[0]:
                    best = (score, (tm, tn, tk))
    assert best is not None, (m, n, k)
    return best[1]


def _mm(a, b, *, ta=False, tb=False, res=None, out_dtype=F32, name="mm"):
    m = a.shape[1] if ta else a.shape[0]
    k = a.shape[0] if ta else a.shape[1]
    n = b.shape[0] if tb else b.shape[1]
    assert (b.shape[1] if tb else b.shape[0]) == k, (a.shape, b.shape, ta, tb)
    has_res = res is not None
    tm, tn, tk = _mm_tiles(m, n, k, a.dtype.itemsize, b.dtype.itemsize, jnp.dtype(out_dtype).itemsize,
                           res.dtype.itemsize if has_res else 0)
    nk = k // tk
    dims = (((0 if ta else 1,), (1 if tb else 0,)), ((), ()))

    def body(*refs):
        a_ref, b_ref = refs[0], refs[1]
        r_ref = refs[2] if has_res else None
        o_ref = refs[2 + has_res]

        def finish(r):
            if has_res:
                r = r + r_ref[...].astype(F32)
            o_ref[...] = r.astype(out_dtype)

        part = lax.dot_general(a_ref[...].astype(_MXU_DTYPE), b_ref[...].astype(_MXU_DTYPE), dims,
                               preferred_element_type=F32)
        if nk == 1:
            finish(part)
            return
        acc = refs[-1]
        kk = pl.program_id(2)

        @pl.when(kk == 0)
        def _():
            acc[...] = part

        @pl.when(kk > 0)
        def _():
            acc[...] += part

        @pl.when(kk == nk - 1)
        def _():
            finish(acc[...])

    a_spec = pl.BlockSpec((tk, tm), lambda i, j, kk: (kk, i)) if ta else pl.BlockSpec((tm, tk), lambda i, j, kk: (i, kk))
    b_spec = pl.BlockSpec((tn, tk), lambda i, j, kk: (j, kk)) if tb else pl.BlockSpec((tk, tn), lambda i, j, kk: (kk, j))
    o_spec = pl.BlockSpec((tm, tn), lambda i, j, kk: (i, j))
    in_specs = [a_spec, b_spec] + ([o_spec] if has_res else [])
    args = (a, b) + ((res,) if has_res else ())
    return _pcall(body, name=name, out_shape=jax.ShapeDtypeStruct((m, n), out_dtype),
                  grid=(m // tm, n // tn, nk), in_specs=in_specs, out_specs=o_spec,
                  scratch=[pltpu.VMEM((tm, tn), F32)] if nk > 1 else [],
                  sem=("parallel", "parallel", "arbitrary"))(*args)


@functools.partial(jax.custom_vjp, nondiff_argnums=(3,))
def _linear_res(a, w, res, name):
    return _mm(a, w, res=res, name=name + "_fwd")


def _linear_res_fwd(a, w, res, name):
    return _mm(a, w, res=res, name=name + "_fwd"), (a, w)


def _linear_res_bwd(name, saved, g):
    a, w = saved
    gm = g.astype(_MXU_DTYPE)
    da = _mm(gm, w, tb=True, out_dtype=a.dtype, name=name + "_da")
    dw = _mm(a, gm, ta=True, out_dtype=w.dtype, name=name + "_dw")
    return da, dw, g


_linear_res.defvjp(_linear_res_fwd, _linear_res_bwd)


@functools.partial(jax.custom_vjp, nondiff_argnums=(2,))
def _linear(a, w, name):
    return _mm(a, w, name=name + "_fwd")


def _linear_fwd(a, w, name):
    return _mm(a, w, name=name + "_fwd"), (a, w)


def _linear_bwd(name, saved, g):
    a, w = saved
    gm = g.astype(_MXU_DTYPE)
    da = _mm(gm, w, tb=True, out_dtype=a.dtype, name=name + "_da")
    dw = _mm(a, gm, ta=True, out_dtype=w.dtype, name=name + "_dw")
    return da, dw


_linear.defvjp(_linear_fwd, _linear_bwd)


def linear(a, w, name, res=None):
    return _linear(a, w, name) if res is None else _linear_res(a, w, res, name)


def _tiled_specs(arrs, kinds, t, axis):
    specs = []
    for a, kind in zip(arrs, kinds):
        if kind == "whole":
            specs.append(pl.BlockSpec(a.shape, lambda i, nd=a.ndim: (0,) * nd))
        elif axis == 0:
            specs.append(pl.BlockSpec((t, a.shape[1]), lambda i: (i, 0)))
        else:
            specs.append(pl.BlockSpec((a.shape[0], t), lambda i: (0, i)))
    return specs


def tilewise(fn, name, args, kinds, outs, *, axis, t, diff):
    n_in = len(args)
    length = next(a.shape[axis] for a, kd in zip(args, kinds) if kd == "tile")
    steps = length // t
    assert steps * t == length, (name, length, t)

    def out_sds(other, dtype):
        return jax.ShapeDtypeStruct((length, other) if axis == 0 else (other, length), dtype)

    def out_spec(other):
        return pl.BlockSpec((t, other), lambda i: (i, 0)) if axis == 0 else pl.BlockSpec((other, t), lambda i: (0, i))

    def run_fwd(*xs):
        def body(*refs):
            vals = fn(*[r[...] for r in refs[:n_in]])
            for o_ref, v in zip(refs[n_in:], vals):
                o_ref[...] = v.astype(o_ref.dtype)

        return _pcall(body, name=name + "_fwd", out_shape=[out_sds(o, d) for o, d in outs], grid=(steps,),
                      in_specs=_tiled_specs(xs, kinds, t, axis), out_specs=[out_spec(o) for o, _ in outs],
                      sem=("parallel",))(*xs)

    didx = [i for i in range(n_in) if diff[i]]

    def run_bwd(xs, cts):
        def body(*refs):
            x_refs, c_refs, g_refs = refs[:n_in], refs[n_in:n_in + len(outs)], refs[n_in + len(outs):]
            vals = [r[...] for r in x_refs]

            def g(*dvals):
                full = list(vals)
                for i, v in zip(didx, dvals):
                    full[i] = v
                return tuple(fn(*full))

            prim_out, vjp = jax.vjp(g, *[vals[i] for i in didx])
            grads = vjp(tuple(c[...].astype(o.dtype) for c, o in zip(c_refs, prim_out)))
            step = pl.program_id(0)
            for i, g_ref, gr in zip(didx, g_refs, grads):
                if kinds[i] == "whole":
                    @pl.when(step == 0)
                    def _(g_ref=g_ref):
                        g_ref[...] = jnp.zeros_like(g_ref)

                    g_ref[...] += gr.astype(g_ref.dtype)
                else:
                    g_ref[...] = gr.astype(g_ref.dtype)

        g_shapes = [jax.ShapeDtypeStruct(xs[i].shape, xs[i].dtype) for i in didx]
        g_specs = _tiled_specs([xs[i] for i in didx], [kinds[i] for i in didx], t, axis)
        ct_specs = [out_spec(o) for o, _ in outs]
        any_whole = any(kinds[i] == "whole" for i in didx)
        return _pcall(body, name=name + "_bwd", out_shape=g_shapes, grid=(steps,),
                      in_specs=_tiled_specs(xs, kinds, t, axis) + ct_specs, out_specs=g_specs,
                      sem=("arbitrary" if any_whole else "parallel",))(*xs, *cts)

    @jax.custom_vjp
    def op(*xs):
        return tuple(run_fwd(*xs))

    def op_fwd(*xs):
        return tuple(run_fwd(*xs)), xs

    def op_bwd(xs, cts):
        grads = run_bwd(xs, cts)
        full = [None] * n_in
        for i, gr in zip(didx, grads):
            full[i] = gr
        return tuple(full)

    op.defvjp(op_fwd, op_bwd)
    return op(*args)


def _silu(x):
    return x * (1.0 / (1.0 + jnp.exp(-x)))


def _softplus(x):
    return jnp.maximum(x, 0.0) + jnp.log(1.0 + jnp.exp(-jnp.abs(x)))


def _rms_tile(x, gain, out_dtype):
    xf = x.astype(F32)
    y = xf * lax.rsqrt(jnp.mean(xf * xf, axis=-1, keepdims=True) + EPS)
    return (y * gain).astype(out_dtype)


def rms_norm(x, gain, name, out_dtype, tr=256):
    def fn(xt, gt):
        return (_rms_tile(xt, gt, out_dtype),)

    return tilewise(fn, name, [x, gain.reshape(1, -1)], ["tile", "whole"], [(x.shape[1], out_dtype)],
                    axis=0, t=_tile(x.shape[0], tr), diff=[True, True])[0]


def _shift_down_raw(x, s):
    rows = lax.broadcasted_iota(jnp.int32, x.shape, 0)
    return jnp.where(rows >= s, pltpu.roll(x, s, 0), 0.0)


def _shift_up_raw(x, s):
    n = x.shape[0]
    rows = lax.broadcasted_iota(jnp.int32, x.shape, 0)
    return jnp.where(rows < n - s, pltpu.roll(x, n - s, 0), 0.0)


@functools.partial(jax.custom_vjp, nondiff_argnums=(1,))
def _shift_down(x, s):
    return _shift_down_raw(x, s)


def _shift_down_fwd(x, s):
    return _shift_down_raw(x, s), None


def _shift_down_bwd(s, _, g):
    return (_shift_up_raw(g, s),)


_shift_down.defvjp(_shift_down_fwd, _shift_down_bwd)


def _causal_dwconv_tile(x, w):
    kk = w.shape[0]
    y = x * w[kk - 1:kk, :]
    for j in range(kk - 1):
        y = y + _shift_down(x, kk - 1 - j) * w[j:j + 1, :]
    return y


def dn_qkv(raw, conv_w, name):
    width = raw.shape[1] // 3
    is_qk = (jnp.arange(raw.shape[1]) < 2 * width).astype(F32).reshape(1, -1)

    def fn(x, w, flag):
        y = _silu(_causal_dwconv_tile(x, w))
        yn = y * lax.rsqrt(jnp.sum(y * y, axis=-1, keepdims=True) + EPS)
        return (jnp.where(flag > 0.5, yn, y),)

    return tilewise(fn, name, [raw, conv_w, is_qk], ["tile", "tile", "tile"], [(raw.shape[0], F32)],
                    axis=1, t=DN_HEAD_DIM, diff=[True, True, False])[0]


def dn_gates(b, a, a_log, dt_bias, name):
    hh = b.shape[1]
    tr = _tile(b.shape[0], 256)

    def fn(bt, at, al, db):
        beta = 1.0 / (1.0 + jnp.exp(-bt))
        g = -jnp.exp(al) * _softplus(at + db)
        pos = lax.broadcasted_iota(jnp.int32, g.shape, 0) & (CHUNK - 1)
        step = 1
        while step < CHUNK:
            g = g + jnp.where(pos >= step, _shift_down(g, step), 0.0)
            step *= 2
        return g, beta

    return tilewise(fn, name, [b, a, a_log.reshape(1, -1), dt_bias.reshape(1, -1)],
                    ["tile", "tile", "whole", "whole"], [(hh, F32), (hh, F32)], axis=0, t=tr, diff=[True] * 4)


def _unit_lower_inv(low):
    c = low.shape[-1]
    ii = lax.broadcasted_iota(jnp.int32, (c, c), 0)
    jj = lax.broadcasted_iota(jnp.int32, (c, c), 1)
    inv = (ii == jj).astype(F32)[None] - low
    p = low
    n = 1
    while 2 * n < c:
        p = _bnn_hi(p, p)
        inv = inv + _bnn_hi(inv, p)
        n *= 2
    return inv


def _dn_chunk(state, q, k, v, gc, gr, gl, bc):
    c = q.shape[1]
    q = q * (q.shape[-1] ** -0.5)
    ii = lax.broadcasted_iota(jnp.int32, (c, c), 0)
    jj = lax.broadcasted_iota(jnp.int32, (c, c), 1)
    incl = (jj <= ii)[None]
    strict = (jj < ii)[None]
    decay = jnp.where(incl, jnp.exp(jnp.where(incl, gc - gr, 0.0)), 0.0)
    kb = k * bc
    low = jnp.where(strict, _bnt_hi(kb, k) * decay, 0.0)
    ainv = _unit_lower_inv(low)
    eg = jnp.exp(gc)
    u = _bnn_hi(ainv, v * bc)
    w = _bnn_hi(ainv, kb * eg)
    attn = _bnt(q, k) * decay
    q_dec = q * eg
    k_dec = k * jnp.exp(gl - gc)
    v_new = u - _bnn(w, state)
    o = _bnn(q_dec, state) + _bnn(attn, v_new)
    new_state = state * jnp.exp(gl) + _btn(k_dec, v_new)
    return new_state, o


def _dn_heads(ref, base, hh):
    return jnp.stack([ref[:, base + h * DN_HEAD_DIM: base + (h + 1) * DN_HEAD_DIM] for h in range(hh)])


def _dn_fwd_call(qkv, gc, gr, gl, bc, name):
    s, w3 = qkv.shape
    width = w3 // 3
    hh = width // DN_HEAD_DIM
    d = DN_HEAD_DIM
    n = s // CHUNK

    def body(qkv_ref, gc_ref, gr_ref, gl_ref, bc_ref, o_ref, st_ref, state):
        @pl.when(pl.program_id(0) == 0)
        def _():
            state[...] = jnp.zeros_like(state)

        s_in = state[...]
        st_ref[0] = s_in
        new_s, o = _dn_chunk(s_in, _dn_heads(qkv_ref, 0, hh), _dn_heads(qkv_ref, width, hh),
                             _dn_heads(qkv_ref, 2 * width, hh), gc_ref[0], gr_ref[0], gl_ref[0], bc_ref[0])
        state[...] = new_s
        for h in range(hh):
            o_ref[:, h * d:(h + 1) * d] = o[h]

    g4 = lambda i: (i, 0, 0, 0)
    return _pcall(
        body, name=name + "_fwd",
        out_shape=[jax.ShapeDtypeStruct((s, width), F32), jax.ShapeDtypeStruct((n, hh, d, d), F32)],
        grid=(n,),
        in_specs=[pl.BlockSpec((CHUNK, w3), lambda i: (i, 0)), pl.BlockSpec((1, hh, CHUNK, 1), g4),
                  pl.BlockSpec((1, hh, 1, CHUNK), g4), pl.BlockSpec((1, hh, 1, 1), g4),
                  pl.BlockSpec((1, hh, CHUNK, 1), g4)],
        out_specs=[pl.BlockSpec((CHUNK, width), lambda i: (i, 0)), pl.BlockSpec((1, hh, d, d), g4)],
        scratch=[pltpu.VMEM((hh, d, d), F32)], sem=("arbitrary",))(qkv, gc, gr, gl, bc)


def _dn_bwd_call(qkv, gc, gr, gl, bc, states, do, name):
    s, w3 = qkv.shape
    width = w3 // 3
    hh = width // DN_HEAD_DIM
    d = DN_HEAD_DIM
    n = s // CHUNK

    def body(qkv_ref, gc_ref, gr_ref, gl_ref, bc_ref, st_ref, do_ref,
             dqkv_ref, dgc_ref, dgr_ref, dgl_ref, dbc_ref, dstate):
        @pl.when(pl.program_id(0) == 0)
        def _():
            dstate[...] = jnp.zeros_like(dstate)

        prim = (st_ref[0], _dn_heads(qkv_ref, 0, hh), _dn_heads(qkv_ref, width, hh),
                _dn_heads(qkv_ref, 2 * width, hh), gc_ref[0], gr_ref[0], gl_ref[0], bc_ref[0])
        _, vjp = jax.vjp(_dn_chunk, *prim)
        ds, dq, dk, dv, dgc, dgr, dgl, dbc = vjp((dstate[...], _dn_heads(do_ref, 0, hh)))
        dstate[...] = ds
        for h in range(hh):
            dqkv_ref[:, h * d:(h + 1) * d] = dq[h]
            dqkv_ref[:, width + h * d: width + (h + 1) * d] = dk[h]
            dqkv_ref[:, 2 * width + h * d: 2 * width + (h + 1) * d] = dv[h]
        dgc_ref[0] = dgc
        dgr_ref[0] = dgr
        dgl_ref[0] = dgl
        dbc_ref[0] = dbc

    r2 = lambda i: (n - 1 - i, 0)
    r4 = lambda i: (n - 1 - i, 0, 0, 0)
    spec_c = pl.BlockSpec((1, hh, CHUNK, 1), r4)
    spec_r = pl.BlockSpec((1, hh, 1, CHUNK), r4)
    spec_l = pl.BlockSpec((1, hh, 1, 1), r4)
    return _pcall(
        body, name=name + "_bwd",
        out_shape=[jax.ShapeDtypeStruct(qkv.shape, F32), jax.ShapeDtypeStruct(gc.shape, F32),
                   jax.ShapeDtypeStruct(gr.shape, F32), jax.ShapeDtypeStruct(gl.shape, F32),
                   jax.ShapeDtypeStruct(bc.shape, F32)],
        grid=(n,),
        in_specs=[pl.BlockSpec((CHUNK, w3), r2), spec_c, spec_r, spec_l, spec_c,
                  pl.BlockSpec((1, hh, d, d), r4), pl.BlockSpec((CHUNK, width), r2)],
        out_specs=[pl.BlockSpec((CHUNK, w3), r2), spec_c, spec_r, spec_l, spec_c],
        scratch=[pltpu.VMEM((hh, d, d), F32)], sem=("arbitrary",))(qkv, gc, gr, gl, bc, states, do)


@functools.partial(jax.custom_vjp, nondiff_argnums=(5,))
def dn_core(qkv, gc, gr, gl, bc, name):
    return _dn_fwd_call(qkv, gc, gr, gl, bc, name)[0]


def _dn_core_fwd(qkv, gc, gr, gl, bc, name):
    o, states = _dn_fwd_call(qkv, gc, gr, gl, bc, name)
    return o, (qkv, gc, gr, gl, bc, states)


def _dn_core_bwd(name, saved, do):
    return tuple(_dn_bwd_call(*saved, do, name))


dn_core.defvjp(_dn_core_fwd, _dn_core_bwd)


def dn_out_gate(o, z, out_norm, name):
    hh = o.shape[1] // DN_HEAD_DIM
    gain = jnp.tile(out_norm.reshape(1, -1), (1, hh))

    def fn(ot, zt, gt):
        y = ot * lax.rsqrt(jnp.mean(ot * ot, axis=-1, keepdims=True) + EPS) * gt
        return (y * _silu(zt),)

    return tilewise(fn, name, [o, z, gain], ["tile", "tile", "tile"], [(o.shape[0], BF16)],
                    axis=1, t=DN_HEAD_DIM, diff=[True, True, True])[0]


def _attn_tile(qn, kn, v, qp, kp, q0, scale, causal):
    s = _nt(qn, kn)
    if qp is not None:
        s = s + _nt(qp, kp)
    s = s * scale
    if causal:
        qpos = q0 + lax.broadcasted_iota(jnp.int32, s.shape, 0)
        kpos = lax.broadcasted_iota(jnp.int32, s.shape, 1)
        s = jnp.where((kpos >> _CHUNK_SHIFT) <= (qpos >> _CHUNK_SHIFT), s, -1e30)
    e = jnp.exp(s - jnp.max(s, axis=-1, keepdims=True))
    p = e / jnp.sum(e, axis=-1, keepdims=True)
    return _nn(p, v)


def _attn_specs(sq, sk, dh, dp, tb):
    q_spec = pl.BlockSpec((tb, dh), lambda h, i: (i, h))
    kv_spec = pl.BlockSpec((sk, dh), lambda h, i: (0, h))
    qp_spec = pl.BlockSpec((1, tb, dp), lambda h, i: (h, i, 0)) if dp else None
    kp_spec = pl.BlockSpec((sk, dp), lambda h, i: (0, 0)) if dp else None
    return q_spec, kv_spec, qp_spec, kp_spec


def _attn_plan(sq, sk, tq, causal):
    tb = sq if causal else tq
    subs = [(slice(i * tq, (i + 1) * tq), (i + 1) * tq if causal else sk) for i in range(tb // tq)]
    return tb, subs


def _attn_fwd_call(q, k, v, qp, kp, *, dh, scale, causal, tq, name):
    sq, sk = q.shape[0], k.shape[0]
    heads = q.shape[1] // dh
    dp = qp.shape[-1] if qp is not None else 0
    tb, subs = _attn_plan(sq, sk, tq, causal)
    q_spec, kv_spec, qp_spec, kp_spec = _attn_specs(sq, sk, dh, dp, tb)

    def body(*refs):
        q_ref, k_ref, v_ref = refs[:3]
        qp_ref, kp_ref = (refs[3], refs[4]) if dp else (None, None)
        o_ref = refs[-1]
        row0 = pl.program_id(1) * tb
        for rows, ke in subs:
            o = _attn_tile(q_ref[rows, :], k_ref[:ke, :], v_ref[:ke, :], qp_ref[0, rows, :] if dp else None,
                           kp_ref[:ke, :] if dp else None, row0 + rows.start, scale, causal)
            o_ref[rows, :] = o.astype(o_ref.dtype)

    in_specs = [q_spec, kv_spec, kv_spec] + ([qp_spec, kp_spec] if dp else [])
    args = (q, k, v) + ((qp, kp) if dp else ())
    return _pcall(body, name=name + "_fwd", out_shape=jax.ShapeDtypeStruct((sq, heads * dh), BF16),
                  grid=(heads, sq // tb), in_specs=in_specs, out_specs=q_spec,
                  sem=("parallel", "parallel"))(*args)


def _attn_bwd_call(q, k, v, qp, kp, do, *, dh, scale, causal, tq, name):
    sq, sk = q.shape[0], k.shape[0]
    heads = q.shape[1] // dh
    dp = qp.shape[-1] if qp is not None else 0
    tb, subs = _attn_plan(sq, sk, tq, causal)
    q_spec, kv_spec, qp_spec, kp_spec = _attn_specs(sq, sk, dh, dp, tb)

    def body(*refs):
        h, i = pl.program_id(0), pl.program_id(1)
        if dp:
            q_ref, k_ref, v_ref, qp_ref, kp_ref, do_ref, dq_ref, dk_ref, dv_ref, dqp_ref, dkp_ref = refs
        else:
            q_ref, k_ref, v_ref, do_ref, dq_ref, dk_ref, dv_ref = refs

        @pl.when(i == 0)
        def _():
            dk_ref[...] = jnp.zeros_like(dk_ref)
            dv_ref[...] = jnp.zeros_like(dv_ref)

        if dp:
            @pl.when(jnp.logical_and(h == 0, i == 0))
            def _():
                dkp_ref[...] = jnp.zeros_like(dkp_ref)

        for rows, ke in subs:
            q0 = i * tb + rows.start
            if dp:
                prim = (q_ref[rows, :], k_ref[:ke, :], v_ref[:ke, :], qp_ref[0, rows, :], kp_ref[:ke, :])
                f = lambda a, b, c, d, e, q0=q0: _attn_tile(a, b, c, d, e, q0, scale, causal)
            else:
                prim = (q_ref[rows, :], k_ref[:ke, :], v_ref[:ke, :])
                f = lambda a, b, c, q0=q0: _attn_tile(a, b, c, None, None, q0, scale, causal)
            _, vjp = jax.vjp(f, *prim)
            grads = vjp(do_ref[rows, :].astype(F32))
            dq_ref[rows, :] = grads[0]
            dk_ref[:ke, :] += grads[1]
            dv_ref[:ke, :] += grads[2]
            if dp:
                dqp_ref[0, rows, :] = grads[3]
                dkp_ref[:ke, :] += grads[4]

    in_specs = [q_spec, kv_spec, kv_spec] + ([qp_spec, kp_spec] if dp else []) + [q_spec]
    out_shape = [jax.ShapeDtypeStruct(q.shape, F32), jax.ShapeDtypeStruct(k.shape, F32),
                 jax.ShapeDtypeStruct(v.shape, F32)]
    out_specs = [q_spec, kv_spec, kv_spec]
    if dp:
        out_shape += [jax.ShapeDtypeStruct(qp.shape, F32), jax.ShapeDtypeStruct(kp.shape, F32)]
        out_specs += [qp_spec, kp_spec]
    args = (q, k, v) + ((qp, kp) if dp else ()) + (do,)
    return _pcall(body, name=name + "_bwd", out_shape=out_shape, grid=(heads, sq // tb), in_specs=in_specs,
                  out_specs=out_specs, sem=("arbitrary", "arbitrary"))(*args)


def attention(q, k, v, qp=None, kp=None, *, dh, scale, causal, name, tq=256):
    tq = _tile(q.shape[0], tq)
    kw = dict(dh=dh, scale=scale, causal=causal, tq=tq, name=name)
    has_pe = qp is not None

    @jax.custom_vjp
    def op(*xs):
        return _attn_fwd_call(*xs, **kw) if has_pe else _attn_fwd_call(*xs, None, None, **kw)

    def op_fwd(*xs):
        return (_attn_fwd_call(*xs, **kw) if has_pe else _attn_fwd_call(*xs, None, None, **kw)), xs

    def op_bwd(xs, do):
        full = xs if has_pe else xs + (None, None)
        return tuple(_attn_bwd_call(*full, do, **kw))

    op.defvjp(op_fwd, op_bwd)
    return op(q, k, v, qp, kp) if has_pe else op(q, k, v)


def _rope_tables(positions, reps):
    half = MLA_ROPE // 2
    inv = ROPE_BASE ** (-jnp.arange(0, MLA_ROPE, 2, dtype=F32) / MLA_ROPE)
    ang = positions.astype(F32)[:, None] * inv
    cos, sin = jnp.cos(ang), jnp.sin(ang)
    c = jnp.tile(jnp.concatenate([cos, cos], axis=-1), (1, reps))
    s = jnp.tile(jnp.concatenate([sin, sin], axis=-1), (1, reps))
    rot = np.zeros((MLA_ROPE, MLA_ROPE), np.float32)
    for i in range(half):
        rot[i + half, i] = -1.0
        rot[i, i + half] = 1.0
    return c, s, jnp.asarray(np.kron(np.eye(reps, dtype=np.float32), rot))


def rope(x, positions, name):
    c, s, rot = _rope_tables(positions, x.shape[1] // MLA_ROPE)

    def fn(xt, ct, st, rt):
        return (xt * ct + _nn_hi_const(xt, rt) * st,)

    return tilewise(fn, name, [x, c, s, rot], ["tile", "tile", "tile", "whole"], [(x.shape[1], F32)],
                    axis=0, t=_tile(x.shape[0], 256), diff=[True, False, False, False])[0]


def ffn_gate(pre_g, pre_u, conv_g, conv_u, bias_g, bias_u, name):
    def fn(g, u, wg, wu, bg, bu):
        return (_silu(_causal_dwconv_tile(g, wg) + bg) * (_causal_dwconv_tile(u, wu) + bu),)

    return tilewise(fn, name, [pre_g, pre_u, conv_g, conv_u, bias_g, bias_u], ["tile"] * 6,
                    [(pre_g.shape[0], BF16)], axis=1, t=_tile(pre_g.shape[1], 256), diff=[True] * 6)[0]


def loss_rows(h, gain, target, name):
    def fn(ht, tt, gt):
        err = _rms_tile(ht, gt, F32) - tt
        return (0.5 * jnp.mean(err * err, axis=-1, keepdims=True),)

    return tilewise(fn, name, [h, target, gain.reshape(1, -1)], ["tile", "tile", "whole"], [(1, F32)],
                    axis=0, t=_tile(h.shape[0], 256), diff=[True, False, True])[0]


def _col_groups(w, per_head, lo, hi):
    r = w.shape[0]
    return w.reshape(r, -1, per_head)[:, :, lo:hi].reshape(r, -1)


def _layer_mix(h, lw, rp, l, positions):
    s, d = h.shape
    width = d // 2
    dn_heads = width // DN_HEAD_DIM
    mla_heads = (d - width) // MLA_V
    nm = f"l{l}_"
    w_in = lw["w_in"]
    c0 = 4 * width
    c1 = c0 + 2 * dn_heads
    c2 = c1 + MLA_Q_RANK
    c3 = c2 + MLA_KV_RANK + MLA_ROPE
    rest_pad = (-(c3 - c0)) % LANE
    w_rest = jnp.pad(w_in[:, c0:c3], ((0, 0), (0, rest_pad)))

    u = rms_norm(h, rp["norm_mix"], nm + "norm_mix", BF16)
    qkv_raw = linear(u, w_in[:, :3 * width], nm + "in_qkv")
    z = linear(u, w_in[:, 3 * width:c0], nm + "in_z")
    rest = linear(u, w_rest, nm + "in_rest")

    qkv = dn_qkv(qkv_raw, lw["dn_conv"], nm + "dn_qkv")
    csum, beta = dn_gates(rest[:, :dn_heads], rest[:, dn_heads:2 * dn_heads], rp["dn_a_log"],
                          rp["dn_dt_bias"], nm + "dn_gates")
    n = s // CHUNK
    g3 = csum.reshape(n, CHUNK, dn_heads).transpose(0, 2, 1)
    b3 = beta.reshape(n, CHUNK, dn_heads).transpose(0, 2, 1)
    o_dn = dn_core(qkv, g3[..., None], g3[:, :, None, :], g3[:, :, CHUNK - 1][..., None, None], b3[..., None],
                   nm + "dn_core")
    o_dn = dn_out_gate(o_dn, z, rp["dn_out_norm"], nm + "dn_gate")

    mq = rest[:, c1 - c0:c2 - c0]
    mkv = rest[:, c2 - c0:c3 - c0]
    qn = rms_norm(mq, rp["mla_q_norm"], nm + "mla_qnorm", BF16)
    per_q = MLA_NOPE + MLA_ROPE
    q_nope = linear(qn, _col_groups(lw["mla_w_qb"], per_q, 0, MLA_NOPE), nm + "mla_qn")
    q_pe = linear(qn, _col_groups(lw["mla_w_qb"], per_q, MLA_NOPE, per_q), nm + "mla_qp")
    kvn = rms_norm(mkv[:, :MLA_KV_RANK], rp["mla_kv_norm"], nm + "mla_kvnorm", BF16)
    per_kv = MLA_NOPE + MLA_V
    k_nope = linear(kvn, _col_groups(lw["mla_w_kvb"], per_kv, 0, MLA_NOPE), nm + "mla_kn")
    v_mla = linear(kvn, _col_groups(lw["mla_w_kvb"], per_kv, MLA_NOPE, per_kv), nm + "mla_v")
    q_pe = rope(q_pe, positions, nm + "rope_q")
    k_pe = rope(mkv[:, MLA_KV_RANK:], positions, nm + "rope_k")
    q_pe = q_pe.reshape(s, mla_heads, MLA_ROPE).transpose(1, 0, 2)
    o_mla = attention(q_nope, k_nope, v_mla, q_pe, k_pe, dh=MLA_NOPE, scale=per_q ** -0.5, causal=True,
                      name=nm + "mla_attn")

    h = linear(o_dn, lw["w_out"][:width], nm + "out_dn", res=h)
    return linear(o_mla, lw["w_out"][width:], nm + "out_mla", res=h)


def _layer_tail(h, lw, rp, l, mem_n):
    d = h.shape[1]
    nm = f"l{l}_"
    hn = rms_norm(h, rp["norm_xattn"], nm + "norm_xattn", BF16)
    xq = linear(hn, lw["xa_wq"], nm + "xa_q")
    xk = linear(mem_n, lw["xa_wk"], nm + "xa_k")
    xv = linear(mem_n, lw["xa_wv"], nm + "xa_v")
    xdh = d // XA_HEADS
    xo = attention(xq, xk, xv, dh=xdh, scale=xdh ** -0.5, causal=False, name=nm + "xattn")
    h = linear(xo, lw["xa_wo"], nm + "xa_o", res=h)

    hn = rms_norm(h, rp["norm_ffn"], nm + "norm_ffn", BF16)
    w_up = lw["ffn_w_up"]
    d_ff = w_up.shape[1] // 2
    pre_g = linear(hn, w_up[:, :d_ff], nm + "ffn_upg")
    pre_u = linear(hn, w_up[:, d_ff:], nm + "ffn_upu")
    bias = rp["ffn_conv_bias"].reshape(1, -1)
    conv = lw["ffn_conv"]
    act = ffn_gate(pre_g, pre_u, conv[:, :d_ff], conv[:, d_ff:], bias[:, :d_ff], bias[:, d_ff:], nm + "ffn_gate")
    return linear(act, lw["ffn_w_down"], nm + "ffn_down", res=h)


_COL_SHARDED = ("w_in", "mla_w_qb", "mla_w_kvb", "ffn_w_up", "dn_conv", "ffn_conv")
_ROW_SHARDED = ("w_out", "xa_wq", "xa_wk", "xa_wv", "xa_wo", "ffn_w_down")
_BIG = ("w_in", "mla_w_qb", "mla_w_kvb", "w_out", "xa_wq", "xa_wk", "xa_wv", "xa_wo", "ffn_w_up", "ffn_w_down")
_MIX_KEYS = ("w_in", "mla_w_qb", "mla_w_kvb", "w_out")
_TAIL_KEYS = ("xa_wq", "xa_wk", "xa_wv", "xa_wo", "ffn_w_up", "ffn_w_down")
_SMALL_SHARDED = ("dn_conv", "ffn_conv")
_REP_MIX = ("norm_mix", "dn_a_log", "dn_dt_bias", "dn_out_norm", "mla_q_norm", "mla_kv_norm")
_REP_TAIL = ("norm_xattn", "norm_ffn", "ffn_conv_bias")
_REPLICATED = ("norm_mix", "dn_a_log", "dn_dt_bias", "dn_out_norm", "mla_q_norm", "mla_kv_norm", "mem_norm",
               "norm_xattn", "norm_ffn", "ffn_conv_bias", "norm_final")
_WEIGHTS = ("norm_mix", "w_in", "dn_conv", "dn_a_log", "dn_dt_bias", "dn_out_norm", "mla_q_norm", "mla_w_qb",
            "mla_kv_norm", "mla_w_kvb", "w_out", "mem_norm", "norm_xattn", "xa_wq", "xa_wk", "xa_wv", "xa_wo",
            "norm_ffn", "ffn_w_up", "ffn_conv", "ffn_conv_bias", "ffn_w_down", "norm_final")


def _assemble(name, g):
    if name in _ROW_SHARDED:
        return g.reshape((-1,) + g.shape[2:])
    return jnp.moveaxis(g, 0, -2).reshape(g.shape[1:-1] + (-1,))


def _mix(h, gw, conv, rp, l, positions):
    lw = {k: _assemble(k, v) for k, v in gw.items()}
    lw["dn_conv"] = _assemble("dn_conv", conv)
    return _layer_mix(h, lw, rp, l, positions)


def _tail(h, gw, conv, rp, l, mem_n):
    lw = {k: _assemble(k, v) for k, v in gw.items()}
    lw["ffn_conv"] = _assemble("ffn_conv", conv)
    return _layer_tail(h, lw, rp, l, mem_n)


def _my_index():
    return 4 * lax.axis_index("x") + 2 * lax.axis_index("y") + lax.axis_index("c")


def _peer(k):
    x, y, c = lax.axis_index("x"), lax.axis_index("y"), lax.axis_index("c")
    px = (1 - x) if k & 4 else x
    py = (1 - y) if k & 2 else y
    pc = (1 - c) if k & 1 else c
    return (px, py, pc), 4 * px + 2 * py + pc


_ANY = pl.BlockSpec(memory_space=pl.ANY)


def all_gather(shards, name):
    n = len(shards)

    def body(*refs):
        x_refs, o_refs = refs[:n], refs[n:2 * n]
        send_sems, recv_sems, local_sems = refs[2 * n:]
        me = _my_index()
        sib_id, sib = _peer(1)
        chips = [_peer(k) for k in (4, 2, 6)]

        def copy(a, k, block, to, src=None):
            return pltpu.make_async_remote_copy(
                src_ref=o_refs[a].at[block] if src is None else src, dst_ref=o_refs[a].at[block],
                send_sem=send_sems.at[a * 7 + k], recv_sem=recv_sems.at[a * 7 + k],
                device_id=to, device_id_type=MESH)

        mine = [pltpu.make_async_copy(x_refs[a], o_refs[a].at[me], local_sems.at[a]) for a in range(n)]
        for cp in mine:
            cp.start()
        first = []
        for a in range(n):
            first.append(copy(a, 0, me, sib_id, src=x_refs[a]))
            first += [copy(a, 1 + j, me, cid, src=x_refs[a]) for j, (cid, _) in enumerate(chips)]
        for cp in first:
            cp.start()
        passed = []
        for a in range(n):
            for j, (cid, cidx) in enumerate(chips):
                copy(a, 1 + j, cidx, cid).wait_recv()
                fwd = copy(a, 4 + j, cidx, sib_id)
                fwd.start()
                passed.append(fwd)
        for a in range(n):
            copy(a, 0, sib, sib_id).wait_recv()
            for j, (_, cidx) in enumerate(chips):
                copy(a, 4 + j, cidx ^ 1, sib_id).wait_recv()
        for cp in first + passed:
            cp.wait_send()
        for cp in mine:
            cp.wait()

    out_shape = [jax.ShapeDtypeStruct((N_DEV,) + s.shape, s.dtype) for s in shards]
    return _pcall(body, name=name, out_shape=out_shape, in_specs=[_ANY] * n, out_specs=[_ANY] * n,
                  scratch=[pltpu.SemaphoreType.DMA((7 * n,)), pltpu.SemaphoreType.DMA((7 * n,)),
                           pltpu.SemaphoreType.DMA((n,))])(*shards)


def all_to_all(parts, name):
    n = len(parts)

    def body(*refs):
        x_refs, o_refs = refs[:n], refs[n:2 * n]
        send_sems, recv_sems, local_sems = refs[2 * n:]
        me = _my_index()
        mine = [pltpu.make_async_copy(x_refs[a].at[me], o_refs[a].at[me], local_sems.at[a]) for a in range(n)]
        for cp in mine:
            cp.start()
        sends = []
        for a in range(n):
            for k in range(1, N_DEV):
                pid, pidx = _peer(k)
                sends.append(pltpu.make_async_remote_copy(
                    src_ref=x_refs[a].at[pidx], dst_ref=o_refs[a].at[me],
                    send_sem=send_sems.at[a * 7 + k - 1], recv_sem=recv_sems.at[a * 7 + k - 1],
                    device_id=pid, device_id_type=MESH))
        for cp in sends:
            cp.start()
        for a in range(n):
            for k in range(1, N_DEV):
                pid, pidx = _peer(k)
                pltpu.make_async_remote_copy(
                    src_ref=x_refs[a].at[pidx], dst_ref=o_refs[a].at[pidx],
                    send_sem=send_sems.at[a * 7 + k - 1], recv_sem=recv_sems.at[a * 7 + k - 1],
                    device_id=pid, device_id_type=MESH).wait_recv()
        for cp in sends:
            cp.wait_send()
        for cp in mine:
            cp.wait()

    out_shape = [jax.ShapeDtypeStruct(p.shape, p.dtype) for p in parts]
    return _pcall(body, name=name, out_shape=out_shape, in_specs=[_ANY] * n, out_specs=[_ANY] * n,
                  scratch=[pltpu.SemaphoreType.DMA((7 * n,)), pltpu.SemaphoreType.DMA((7 * n,)),
                           pltpu.SemaphoreType.DMA((n,))])(*parts)


_HBM = pl.BlockSpec(memory_space=pltpu.HBM)
_SEM = pl.BlockSpec(memory_space=pltpu.SEMAPHORE)
_EFFECT = pltpu.SideEffectType.DATAFLOW_SIDE_EFFECTING


def _plan_gather_first(n):
    def plan(refs):
        me = _my_index()
        sib_id, sib = _peer(1)
        chips = [_peer(k) for k in (4, 2, 6)]
        out = []
        for a in range(n):
            x, land = refs[a], refs[n + a]
            out.append((x, land.at[me], sib_id, land.at[sib]))
            out += [(x, land.at[me], cid, land.at[cidx]) for cid, cidx in chips]
        return out

    return plan, 4 * n


def _plan_gather_pass(n):
    def plan(refs):
        sib_id, _ = _peer(1)
        chips = [_peer(k) for k in (4, 2, 6)]
        return [(refs[a].at[cidx], refs[a].at[cidx], sib_id, refs[a].at[cidx ^ 1])
                for a in range(n) for _, cidx in chips]

    return plan, 3 * n


def _plan_scatter(n):
    def plan(refs):
        me = _my_index()
        out = []
        for a in range(n):
            for k in range(1, N_DEV):
                pid, pidx = _peer(k)
                out.append((refs[a].at[pidx], refs[n + a].at[me], pid, refs[n + a].at[pidx]))
        return out

    return plan, 7 * n


def _remote_copy(src, dst, send_sems, recv_sems, i, dev):
    return pltpu.make_async_remote_copy(src_ref=src, dst_ref=dst, send_sem=send_sems.at[i], recv_sem=recv_sems.at[i],
                                        device_id=dev, device_id_type=MESH)


def exchange_start(bufs, plan_n, name, after=None):
    plan, n = plan_n
    nb = len(bufs)
    n_in = nb + (after is not None)

    def body(*refs):
        send_sems, recv_sems = refs[n_in], refs[n_in + 1]
        for i, (src, dst, dev, _) in enumerate(plan(refs[:nb])):
            _remote_copy(src, dst, send_sems, recv_sems, i, dev).start()
        refs[-1][...] = jnp.zeros_like(refs[-1])

    out_shape = (pltpu.SemaphoreType.DMA((n,)), pltpu.SemaphoreType.DMA((n,)),
                 *[pltpu.HBM(b.shape, b.dtype) for b in bufs], jax.ShapeDtypeStruct((8, LANE), F32))
    args = [pltpu.with_memory_space_constraint(b, pltpu.HBM) for b in bufs] + ([after] if after is not None else [])
    res = pl.pallas_call(
        body, name=name, out_shape=out_shape,
        in_specs=[_HBM] * nb + ([_ANY] if after is not None else []),
        out_specs=(_SEM, _SEM, *[_HBM] * nb, pl.BlockSpec(memory_space=pltpu.VMEM)),
        input_output_aliases={i: 2 + i for i in range(nb)},
        compiler_params=pltpu.CompilerParams(has_side_effects=_EFFECT), interpret=_INTERPRET)(*args)
    return (res[0], res[1], list(res[2:2 + nb])), res[-1]


def exchange_wait(flight, plan_n, name, after):
    plan, _ = plan_n
    send_sems, recv_sems, bufs = flight
    nb = len(bufs)

    def body(*refs):
        s_sems, r_sems = refs[nb], refs[nb + 1]
        for i, (src, _, dev, arrival) in enumerate(plan(refs[:nb])):
            cp = _remote_copy(src, arrival, s_sems, r_sems, i, dev)
            cp.wait_send()
            cp.wait_recv()

    return list(pl.pallas_call(
        body, name=name, out_shape=tuple(pltpu.HBM(b.shape, b.dtype) for b in bufs),
        in_specs=[_HBM] * nb + [_SEM, _SEM, _ANY], out_specs=tuple([_HBM] * nb),
        input_output_aliases={i: i for i in range(nb)},
        compiler_params=pltpu.CompilerParams(has_side_effects=_EFFECT), interpret=_INTERPRET)(
            *bufs, send_sems, recv_sems, after))


def adam_update(recv, w, m, v, name):
    ll, _, r, c = recv.shape
    tr = _tile(r, max(8, 1 << ((_ADAM_TILE_ELEMS // c).bit_length() - 1)))

    def body(g_ref, w_ref, m_ref, v_ref, go_ref, d_ref, mo_ref, vo_ref):
        g = g_ref[0, 0].astype(F32)
        for j in range(1, N_DEV):
            g = g + g_ref[0, j].astype(F32)
        m_new = ADAM_B1 * m_ref[0] + (1.0 - ADAM_B1) * g
        v_new = ADAM_B2 * v_ref[0] + (1.0 - ADAM_B2) * (g * g)
        m_hat = m_new / (1.0 - ADAM_B1 ** ADAM_STEP)
        v_hat = v_new / (1.0 - ADAM_B2 ** ADAM_STEP)
        go_ref[0] = g
        d_ref[0] = -ADAM_LR * (m_hat / (jnp.sqrt(v_hat) + ADAM_EPS) + ADAM_WD * w_ref[0])
        mo_ref[0] = m_new
        vo_ref[0] = v_new

    spec = pl.BlockSpec((1, tr, c), lambda l, i: (l, i, 0))
    sds = jax.ShapeDtypeStruct((ll, r, c), F32)
    return _pcall(body, name=name, out_shape=[sds] * 4, grid=(ll, r // tr),
                  in_specs=[pl.BlockSpec((1, N_DEV, tr, c), lambda l, i: (l, 0, i, 0)), spec, spec, spec],
                  out_specs=[spec] * 4, sem=("parallel", "parallel"))(recv, w, m, v)


def _as3d(a):
    if a.ndim == 1:
        return a.reshape(1, 1, -1)
    if a.ndim == 2:
        return a.reshape(1, a.shape[0], a.shape[1])
    return a.reshape(a.shape[0], -1, a.shape[-1])


def kernel(x, mem, positions, norm_mix, w_in, dn_conv, dn_a_log, dn_dt_bias, dn_out_norm, mla_q_norm, mla_w_qb, mla_kv_norm, mla_w_kvb, w_out, mem_norm, norm_xattn, xa_wq, xa_wk, xa_wv, xa_wo, norm_ffn, ffn_w_up, ffn_conv, ffn_conv_bias, ffn_w_down, norm_final, loss_target, m_norm_mix, m_w_in, m_dn_conv, m_dn_a_log, m_dn_dt_bias, m_dn_out_norm, m_mla_q_norm, m_mla_w_qb, m_mla_kv_norm, m_mla_w_kvb, m_w_out, m_mem_norm, m_norm_xattn, m_xa_wq, m_xa_wk, m_xa_wv, m_xa_wo, m_norm_ffn, m_ffn_w_up, m_ffn_conv, m_ffn_conv_bias, m_ffn_w_down, m_norm_final, v_norm_mix, v_w_in, v_dn_conv, v_dn_a_log, v_dn_dt_bias, v_dn_out_norm, v_mla_q_norm, v_mla_w_qb, v_mla_kv_norm, v_mla_w_kvb, v_w_out, v_mem_norm, v_norm_xattn, v_xa_wq, v_xa_wk, v_xa_wv, v_xa_wo, v_norm_ffn, v_ffn_w_up, v_ffn_conv, v_ffn_conv_bias, v_ffn_w_down, v_norm_final):
    w = dict(norm_mix=norm_mix, w_in=w_in, dn_conv=dn_conv, dn_a_log=dn_a_log, dn_dt_bias=dn_dt_bias,
             dn_out_norm=dn_out_norm, mla_q_norm=mla_q_norm, mla_w_qb=mla_w_qb, mla_kv_norm=mla_kv_norm,
             mla_w_kvb=mla_w_kvb, w_out=w_out, mem_norm=mem_norm, norm_xattn=norm_xattn, xa_wq=xa_wq, xa_wk=xa_wk,
             xa_wv=xa_wv, xa_wo=xa_wo, norm_ffn=norm_ffn, ffn_w_up=ffn_w_up, ffn_conv=ffn_conv,
             ffn_conv_bias=ffn_conv_bias, ffn_w_down=ffn_w_down, norm_final=norm_final)
    mom = dict(norm_mix=m_norm_mix, w_in=m_w_in, dn_conv=m_dn_conv, dn_a_log=m_dn_a_log, dn_dt_bias=m_dn_dt_bias,
               dn_out_norm=m_dn_out_norm, mla_q_norm=m_mla_q_norm, mla_w_qb=m_mla_w_qb, mla_kv_norm=m_mla_kv_norm,
               mla_w_kvb=m_mla_w_kvb, w_out=m_w_out, mem_norm=m_mem_norm, norm_xattn=m_norm_xattn, xa_wq=m_xa_wq,
               xa_wk=m_xa_wk, xa_wv=m_xa_wv, xa_wo=m_xa_wo, norm_ffn=m_norm_ffn, ffn_w_up=m_ffn_w_up,
               ffn_conv=m_ffn_conv, ffn_conv_bias=m_ffn_conv_bias, ffn_w_down=m_ffn_w_down, norm_final=m_norm_final)
    var = dict(norm_mix=v_norm_mix, w_in=v_w_in, dn_conv=v_dn_conv, dn_a_log=v_dn_a_log, dn_dt_bias=v_dn_dt_bias,
               dn_out_norm=v_dn_out_norm, mla_q_norm=v_mla_q_norm, mla_w_qb=v_mla_w_qb, mla_kv_norm=v_mla_kv_norm,
               mla_w_kvb=v_mla_w_kvb, w_out=v_w_out, mem_norm=v_mem_norm, norm_xattn=v_norm_xattn, xa_wq=v_xa_wq,
               xa_wk=v_xa_wk, xa_wv=v_xa_wv, xa_wo=v_xa_wo, norm_ffn=v_norm_ffn, ffn_w_up=v_ffn_w_up,
               ffn_conv=v_ffn_conv, ffn_conv_bias=v_ffn_conv_bias, ffn_w_down=v_ffn_w_down, norm_final=v_norm_final)
    depth = w_in.shape[0]
    me = _my_index()
    rep = {k: w[k] for k in _REPLICATED}
    x0, mem0, pos0, target0 = x[0], mem[0], positions[0], loss_target[0]

    def with_own_block(block):
        return lax.dynamic_update_index_in_dim(lax.empty((N_DEV,) + block.shape, block.dtype), block, me, 0)

    def gather_start(l, keys, tag, after):
        shards = [w[k][l].astype(BF16) for k in keys]
        plan = _plan_gather_first(len(keys))
        flight, token = exchange_start(shards + [with_own_block(sh) for sh in shards], plan,
                                       f"gather_l{l}{tag}_first_start", after)
        return (flight, plan, keys, f"gather_l{l}{tag}"), token

    def gather_pass(state, after):
        flight, plan, keys, name = state
        lands = exchange_wait(flight, plan, name + "_first_wait", after)[len(keys):]
        plan = _plan_gather_pass(len(keys))
        flight, token = exchange_start(lands, plan, name + "_pass_start")
        return (flight, plan, keys, name), token

    def gather_finish(state, after):
        flight, plan, keys, name = state
        return dict(zip(keys, exchange_wait(flight, plan, name + "_pass_wait", after)))

    small = dict(zip(_SMALL_SHARDED, all_gather([w[k] for k in _SMALL_SHARDED], "gather_small")))
    mem_n, vjp_mem = jax.vjp(lambda g: rms_norm(mem0, g, "mem_norm", BF16), rep["mem_norm"])

    st_mix, token = gather_start(0, _MIX_KEYS, "_mix", small["dn_conv"])
    st_tail, token_tail = gather_start(0, _TAIL_KEYS, "_tail", token)
    st_mix, token = gather_pass(st_mix, token_tail)
    gathered = gather_finish(st_mix, token)

    h = x0
    tapes = []
    st_next = None
    for l in range(depth):
        more = l + 1 < depth
        rp_mix = {k: rep[k][l] for k in _REP_MIX}
        rp_tail = {k: rep[k][l] for k in _REP_TAIL}
        if more:
            st_next, token = gather_start(l + 1, _BIG, "", gathered["w_in"] if l == 0 else h)
            rp_mix["norm_mix"] = rp_mix["norm_mix"] + token[0, 0]
        h1, vjp_mix = jax.vjp(lambda hh, gw, cv, rp: _mix(hh, gw, cv, rp, l, pos0), h,
                              {k: gathered[k] for k in _MIX_KEYS}, small["dn_conv"][:, l], rp_mix)
        if l == 0:
            st_tail, token = gather_pass(st_tail, h1)
            gathered.update(gather_finish(st_tail, token))
        elif more:
            st_next, token = gather_pass(st_next, h1)
            rp_tail["norm_xattn"] = rp_tail["norm_xattn"] + token[0, 0]
        h2, vjp_tail = jax.vjp(lambda hh, gw, cv, rp, mn: _tail(hh, gw, cv, rp, l, mn), h1,
                               {k: gathered[k] for k in _TAIL_KEYS}, small["ffn_conv"][:, l], rp_tail, mem_n)
        if more:
            if l == 0:
                st_next, token = gather_pass(st_next, h2)
                gathered = gather_finish(st_next, token)
            else:
                gathered = gather_finish(st_next, h2)
        tapes.append((vjp_mix, vjp_tail))
        h = h2

    rows, vjp_loss = jax.vjp(lambda hh, g: loss_rows(hh, g, target0, "loss"), h, rep["norm_final"])
    loss = lax.psum(jnp.sum(rows), ("x", "y", "c"))
    dh, d_norm_final = vjp_loss(jnp.ones_like(rows))

    d_rep = {k: [None] * depth for k in _REP_MIX + _REP_TAIL}
    d_conv = {k: [None] * depth for k in _SMALL_SHARDED}
    recv_big = [dict() for _ in range(depth)]
    d_mem_n = None

    def scatter_start(l, grads, keys, tag):
        parts = [grads[k] for k in keys]
        own = [with_own_block(lax.dynamic_index_in_dim(p, me, 0, keepdims=False)) for p in parts]
        plan = _plan_scatter(len(keys))
        flight, token = exchange_start(parts + own, plan, f"scatter_l{l}{tag}_start")
        return (flight, plan, keys, l, f"scatter_l{l}{tag}_wait"), token

    def scatter_finish(state, after):
        flight, plan, keys, l, name = state
        recv_big[l].update(zip(keys, exchange_wait(flight, plan, name, after)[len(keys):]))

    pend_tail = pend_mix = None
    token = None
    for l in reversed(range(depth)):
        vjp_mix, vjp_tail = tapes[l]
        if token is not None:
            dh = dh + token[0, 0]
        dh, dg_tail, d_conv["ffn_conv"][l], d_rt, d_mn = vjp_tail(dh)
        if pend_tail is not None:
            scatter_finish(pend_tail, dh)
        pend_tail, token = scatter_start(l, dg_tail, _TAIL_KEYS, "_tail")
        dh, dg_mix, d_conv["dn_conv"][l], d_rm = vjp_mix(dh + token[0, 0])
        if pend_mix is not None:
            scatter_finish(pend_mix, dh)
        pend_mix, token = scatter_start(l, dg_mix, _MIX_KEYS, "_mix")
        d_mem_n = d_mn if d_mem_n is None else d_mem_n + d_mn
        for k, v in {**d_rm, **d_rt}.items():
            d_rep[k][l] = v

    (d_mem_norm,) = vjp_mem(d_mem_n)
    d_rep_all = {k: jnp.stack(v) for k, v in d_rep.items()}
    d_rep_all["mem_norm"] = d_mem_norm
    d_rep_all["norm_final"] = d_norm_final + token[0, 0]
    out = {}
    recv_small = all_to_all([jnp.stack(d_conv[k], axis=1) for k in _SMALL_SHARDED], "scatter_small")
    for k, recv in zip(_SMALL_SHARDED, recv_small):
        r3 = _as3d(w[k])
        out[k] = adam_update(recv.reshape((1, N_DEV, -1, recv.shape[-1])), r3.reshape((1, -1, r3.shape[-1])),
                             _as3d(mom[k]).reshape((1, -1, r3.shape[-1])), _as3d(var[k]).reshape((1, -1, r3.shape[-1])),
                             "adam_" + k)
    recv_rep = all_gather([_as3d(d_rep_all[k])[0] for k in _REPLICATED], "gather_rep_grads")
    for k, recv in zip(_REPLICATED, recv_rep):
        r3 = _as3d(w[k])
        flat = lambda a, r3=r3: a.reshape((1, -1, r3.shape[-1]))
        out[k] = adam_update(recv[None], flat(w[k]), flat(mom[k]), flat(var[k]), "adam_" + k)
    scatter_finish(pend_tail, out["norm_final"][0])
    scatter_finish(pend_mix, out["norm_final"][0])

    for k in _BIG:
        recv = jnp.stack([recv_big[l][k] for l in range(depth)])
        out[k] = adam_update(recv.reshape((depth, N_DEV, -1, recv.shape[-1])), _as3d(w[k]), _as3d(mom[k]),
                             _as3d(var[k]), "adam_" + k)

    res = [loss, dh[None]]
    for j in range(4):
        res += [out[k][j].reshape(w[k].shape) for k in _WEIGHTS]
    return tuple(res)
```

```python
import functools

import numpy as np
import jax
import jax.numpy as jnp
from jax import lax
from jax.experimental import pallas as pl
from jax.experimental.pallas import tpu as pltpu

F32 = jnp.float32
BF16 = jnp.bfloat16
_MXU_DTYPE = BF16
_INTERPRET = False
_VMEM_LIMIT_BYTES = 48 * 1024 * 1024
_MM_VMEM_BUDGET_BYTES = 36 * 1024 * 1024
_MM_TILE_CAP = 2048
_HI = lax.Precision.HIGHEST

N_DEV = 8
CHUNK = 64
_CHUNK_SHIFT = 6
DN_HEAD_DIM = 128
DN_CONV = 4
MLA_NOPE = 128
MLA_ROPE = 64
MLA_V = 128
MLA_Q_RANK = 512
MLA_KV_RANK = 256
ROPE_BASE = 10000.0
XA_HEADS = 4
FFN_CONV = 3
EPS = 1e-6
LANE = 128

ADAM_LR = 0.001
ADAM_B1 = 0.9
ADAM_B2 = 0.999
ADAM_EPS = 1e-08
ADAM_WD = 0.01
ADAM_STEP = 10
_ADAM_TILE_ELEMS = 128 * 1024

MESH = pl.DeviceIdType.MESH


def _pcall(body, *, name, out_shape, grid=None, in_specs=None, out_specs=None, scratch=(), sem=None, **kw):
    params = pltpu.CompilerParams(dimension_semantics=sem, vmem_limit_bytes=_VMEM_LIMIT_BYTES)
    args = dict(name=name, out_shape=out_shape, scratch_shapes=list(scratch), compiler_params=params,
                interpret=_INTERPRET, **kw)
    if grid is not None:
        args.update(grid=grid)
    if in_specs is not None:
        args.update(in_specs=in_specs)
    if out_specs is not None:
        args.update(out_specs=out_specs)
    return pl.pallas_call(body, **args)


def _tile(n, pref):
    if n <= pref:
        return n
    t = pref
    while t >= 8:
        if n % t == 0:
            return t
        t //= 2
    return n


def _ein_raw(spec, a, b, hi):
    if hi:
        return jnp.einsum(spec, a.astype(F32), b.astype(F32), precision=_HI, preferred_element_type=F32)
    return jnp.einsum(spec, a.astype(_MXU_DTYPE), b.astype(_MXU_DTYPE), preferred_element_type=F32)


def _make_ein(spec, hi=False, diff_b=True):
    a_s, rest = spec.split(",")
    b_s, o_s = rest.split("->")

    @jax.custom_vjp
    def f(a, b):
        return _ein_raw(spec, a, b, hi)

    def fwd(a, b):
        return f(a, b), (a, b)

    def bwd(res, g):
        a, b = res
        da = _ein_raw(f"{o_s},{b_s}->{a_s}", g, b, hi)
        if not diff_b:
            return da.astype(a.dtype), None
        db = _ein_raw(f"{a_s},{o_s}->{b_s}", a, g, hi)
        return da.astype(a.dtype), db.astype(b.dtype)

    f.defvjp(fwd, bwd)
    return f


_nt = _make_ein("qd,kd->qk")
_nn = _make_ein("qk,kd->qd")
_nn_hi_const = _make_ein("qk,kd->qd", hi=True, diff_b=False)
_bnt = _make_ein("hik,hjk->hij")
_bnt_hi = _make_ein("hik,hjk->hij", hi=True)
_bnn = _make_ein("hij,hjv->hiv")
_bnn_hi = _make_ein("hij,hjv->hiv", hi=True)
_btn = _make_ein("hck,hcv->hkv")


def _divisor_tiles(n, cap):
    out = [n] if n <= cap else []
    t = cap
    while t >= LANE:
        if t < n and n % t == 0:
            out.append(t)
        t //= 2
    return out or [n]


def _mm_tiles(m, n, k, a_bytes, b_bytes, o_bytes, r_bytes):
    best = None
    for tk in _divisor_tiles(k, _MM_TILE_CAP):
        for tm in _divisor_tiles(m, _MM_TILE_CAP):
            for tn in _divisor_tiles(n, _MM_TILE_CAP):
                need = 2 * (tm * tk * a_bytes + tk * tn * b_bytes + tm * tn * (o_bytes + r_bytes))
                need += tm * tn * 4 * (1 if tk == k else 2)
                if need > _MM_VMEM_BUDGET_BYTES:
                    continue
                score = (tm * tn * tk, tk, tm)
                if best is None or score > best[0]:
                    best = (score, (tm, tn, tk))
    assert best is not None, (m, n, k)
    return best[1]


def _mm(a, b, *, ta=False, tb=False, res=None, out_dtype=F32, name="mm"):
    m = a.shape[1] if ta else a.shape[0]
    k = a.shape[0] if ta else a.shape[1]
    n = b.shape[0] if tb else b.shape[1]
    assert (b.shape[1] if tb else b.shape[0]) == k, (a.shape, b.shape, ta, tb)
    has_res = res is not None
    tm, tn, tk = _mm_tiles(m, n, k, a.dtype.itemsize, b.dtype.itemsize, jnp.dtype(out_dtype).itemsize,
                           res.dtype.itemsize if has_res else 0)
    nk = k // tk
    dims = (((0 if ta else 1,), (1 if tb else 0,)), ((), ()))

    def body(*refs):
        a_ref, b_ref = refs[0], refs[1]
        r_ref = refs[2] if has_res else None
        o_ref = refs[2 + has_res]

        def finish(r):
            if has_res:
                r = r + r_ref[...].astype(F32)
            o_ref[...] = r.astype(out_dtype)

        part = lax.dot_general(a_ref[...].astype(_MXU_DTYPE), b_ref[...].astype(_MXU_DTYPE), dims,
                               preferred_element_type=F32)
        if nk == 1:
            finish(part)
            return
        acc = refs[-1]
        kk = pl.program_id(2)

        @pl.when(kk == 0)
        def _():
            acc[...] = part

        @pl.when(kk > 0)
        def _():
            acc[...] += part

        @pl.when(kk == nk - 1)
        def _():
            finish(acc[...])

    a_spec = pl.BlockSpec((tk, tm), lambda i, j, kk: (kk, i)) if ta else pl.BlockSpec((tm, tk), lambda i, j, kk: (i, kk))
    b_spec = pl.BlockSpec((tn, tk), lambda i, j, kk: (j, kk)) if tb else pl.BlockSpec((tk, tn), lambda i, j, kk: (kk, j))
    o_spec = pl.BlockSpec((tm, tn), lambda i, j, kk: (i, j))
    in_specs = [a_spec, b_spec] + ([o_spec] if has_res else [])
    args = (a, b) + ((res,) if has_res else ())
    return _pcall(body, name=name, out_shape=jax.ShapeDtypeStruct((m, n), out_dtype),
                  grid=(m // tm, n // tn, nk), in_specs=in_specs, out_specs=o_spec,
                  scratch=[pltpu.VMEM((tm, tn), F32)] if nk > 1 else [],
                  sem=("parallel", "parallel", "arbitrary"))(*args)


@functools.partial(jax.custom_vjp, nondiff_argnums=(3,))
def _linear_res(a, w, res, name):
    return _mm(a, w, res=res, name=name + "_fwd")


def _linear_res_fwd(a, w, res, name):
    return _mm(a, w, res=res, name=name + "_fwd"), (a, w)


def _linear_res_bwd(name, saved, g):
    a, w = saved
    gm = g.astype(_MXU_DTYPE)
    da = _mm(gm, w, tb=True, out_dtype=a.dtype, name=name + "_da")
    dw = _mm(a, gm, ta=True, out_dtype=w.dtype, name=name + "_dw")
    return da, dw, g


_linear_res.defvjp(_linear_res_fwd, _linear_res_bwd)


@functools.partial(jax.custom_vjp, nondiff_argnums=(2,))
def _linear(a, w, name):
    return _mm(a, w, name=name + "_fwd")


def _linear_fwd(a, w, name):
    return _mm(a, w, name=name + "_fwd"), (a, w)


def _linear_bwd(name, saved, g):
    a, w = saved
    gm = g.astype(_MXU_DTYPE)
    da = _mm(gm, w, tb=True, out_dtype=a.dtype, name=name + "_da")
    dw = _mm(a, gm, ta=True, out_dtype=w.dtype, name=name + "_dw")
    return da, dw


_linear.defvjp(_linear_fwd, _linear_bwd)


def linear(a, w, name, res=None):
    return _linear(a, w, name) if res is None else _linear_res(a, w, res, name)


def _tiled_specs(arrs, kinds, t, axis):
    specs = []
    for a, kind in zip(arrs, kinds):
        if kind == "whole":
            specs.append(pl.BlockSpec(a.shape, lambda i, nd=a.ndim: (0,) * nd))
        elif axis == 0:
            specs.append(pl.BlockSpec((t, a.shape[1]), lambda i: (i, 0)))
        else:
            specs.append(pl.BlockSpec((a.shape[0], t), lambda i: (0, i)))
    return specs


def tilewise(fn, name, args, kinds, outs, *, axis, t, diff):
    n_in = len(args)
    length = next(a.shape[axis] for a, kd in zip(args, kinds) if kd == "tile")
    steps = length // t
    assert steps * t == length, (name, length, t)

    def out_sds(other, dtype):
        return jax.ShapeDtypeStruct((length, other) if axis == 0 else (other, length), dtype)

    def out_spec(other):
        return pl.BlockSpec((t, other), lambda i: (i, 0)) if axis == 0 else pl.BlockSpec((other, t), lambda i: (0, i))

    def run_fwd(*xs):
        def body(*refs):
            vals = fn(*[r[...] for r in refs[:n_in]])
            for o_ref, v in zip(refs[n_in:], vals):
                o_ref[...] = v.astype(o_ref.dtype)

        return _pcall(body, name=name + "_fwd", out_shape=[out_sds(o, d) for o, d in outs], grid=(steps,),
                      in_specs=_tiled_specs(xs, kinds, t, axis), out_specs=[out_spec(o) for o, _ in outs],
                      sem=("parallel",))(*xs)

    didx = [i for i in range(n_in) if diff[i]]

    def run_bwd(xs, cts):
        def body(*refs):
            x_refs, c_refs, g_refs = refs[:n_in], refs[n_in:n_in + len(outs)], refs[n_in + len(outs):]
            vals = [r[...] for r in x_refs]

            def g(*dvals):
                full = list(vals)
                for i, v in zip(didx, dvals):
                    full[i] = v
                return tuple(fn(*full))

            prim_out, vjp = jax.vjp(g, *[vals[i] for i in didx])
            grads = vjp(tuple(c[...].astype(o.dtype) for c, o in zip(c_refs, prim_out)))
            step = pl.program_id(0)
            for i, g_ref, gr in zip(didx, g_refs, grads):
                if kinds[i] == "whole":
                    @pl.when(step == 0)
                    def _(g_ref=g_ref):
                        g_ref[...] = jnp.zeros_like(g_ref)

                    g_ref[...] += gr.astype(g_ref.dtype)
                else:
                    g_ref[...] = gr.astype(g_ref.dtype)

        g_shapes = [jax.ShapeDtypeStruct(xs[i].shape, xs[i].dtype) for i in didx]
        g_specs = _tiled_specs([xs[i] for i in didx], [kinds[i] for i in didx], t, axis)
        ct_specs = [out_spec(o) for o, _ in outs]
        any_whole = any(kinds[i] == "whole" for i in didx)
        return _pcall(body, name=name + "_bwd", out_shape=g_shapes, grid=(steps,),
                      in_specs=_tiled_specs(xs, kinds, t, axis) + ct_specs, out_specs=g_specs,
                      sem=("arbitrary" if any_whole else "parallel",))(*xs, *cts)

    @jax.custom_vjp
    def op(*xs):
        return tuple(run_fwd(*xs))

    def op_fwd(*xs):
        return tuple(run_fwd(*xs)), xs

    def op_bwd(xs, cts):
        grads = run_bwd(xs, cts)
        full = [None] * n_in
        for i, gr in zip(didx, grads):
            full[i] = gr
        return tuple(full)

    op.defvjp(op_fwd, op_bwd)
    return op(*args)


def _silu(x):
    return x * (1.0 / (1.0 + jnp.exp(-x)))


def _softplus(x):
    return jnp.maximum(x, 0.0) + jnp.log(1.0 + jnp.exp(-jnp.abs(x)))


def _rms_tile(x, gain, out_dtype):
    xf = x.astype(F32)
    y = xf * lax.rsqrt(jnp.mean(xf * xf, axis=-1, keepdims=True) + EPS)
    return (y * gain).astype(out_dtype)


def rms_norm(x, gain, name, out_dtype, tr=256):
    def fn(xt, gt):
        return (_rms_tile(xt, gt, out_dtype),)

    return tilewise(fn, name, [x, gain.reshape(1, -1)], ["tile", "whole"], [(x.shape[1], out_dtype)],
                    axis=0, t=_tile(x.shape[0], tr), diff=[True, True])[0]


def _shift_down_raw(x, s):
    rows = lax.broadcasted_iota(jnp.int32, x.shape, 0)
    return jnp.where(rows >= s, pltpu.roll(x, s, 0), 0.0)


def _shift_up_raw(x, s):
    n = x.shape[0]
    rows = lax.broadcasted_iota(jnp.int32, x.shape, 0)
    return jnp.where(rows < n - s, pltpu.roll(x, n - s, 0), 0.0)


@functools.partial(jax.custom_vjp, nondiff_argnums=(1,))
def _shift_down(x, s):
    return _shift_down_raw(x, s)


def _shift_down_fwd(x, s):
    return _shift_down_raw(x, s), None


def _shift_down_bwd(s, _, g):
    return (_shift_up_raw(g, s),)


_shift_down.defvjp(_shift_down_fwd, _shift_down_bwd)


def _causal_dwconv_tile(x, w):
    kk = w.shape[0]
    y = x * w[kk - 1:kk, :]
    for j in range(kk - 1):
        y = y + _shift_down(x, kk - 1 - j) * w[j:j + 1, :]
    return y


def dn_qkv(raw, conv_w, name):
    width = raw.shape[1] // 3
    is_qk = (jnp.arange(raw.shape[1]) < 2 * width).astype(F32).reshape(1, -1)

    def fn(x, w, flag):
        y = _silu(_causal_dwconv_tile(x, w))
        yn = y * lax.rsqrt(jnp.sum(y * y, axis=-1, keepdims=True) + EPS)
        return (jnp.where(flag > 0.5, yn, y),)

    return tilewise(fn, name, [raw, conv_w, is_qk], ["tile", "tile", "tile"], [(raw.shape[0], F32)],
                    axis=1, t=DN_HEAD_DIM, diff=[True, True, False])[0]


def dn_gates(b, a, a_log, dt_bias, name):
    hh = b.shape[1]
    tr = _tile(b.shape[0], 256)

    def fn(bt, at, al, db):
        beta = 1.0 / (1.0 + jnp.exp(-bt))
        g = -jnp.exp(al) * _softplus(at + db)
        pos = lax.broadcasted_iota(jnp.int32, g.shape, 0) & (CHUNK - 1)
        step = 1
        while step < CHUNK:
            g = g + jnp.where(pos >= step, _shift_down(g, step), 0.0)
            step *= 2
        return g, beta

    return tilewise(fn, name, [b, a, a_log.reshape(1, -1), dt_bias.reshape(1, -1)],
                    ["tile", "tile", "whole", "whole"], [(hh, F32), (hh, F32)], axis=0, t=tr, diff=[True] * 4)


def _unit_lower_inv(low):
    c = low.shape[-1]
    ii = lax.broadcasted_iota(jnp.int32, (c, c), 0)
    jj = lax.broadcasted_iota(jnp.int32, (c, c), 1)
    inv = (ii == jj).astype(F32)[None] - low
    p = low
    n = 1
    while 2 * n < c:
        p = _bnn_hi(p, p)
        inv = inv + _bnn_hi(inv, p)
        n *= 2
    return inv


def _dn_chunk(state, q, k, v, gc, gr, gl, bc):
    c = q.shape[1]
    q = q * (q.shape[-1] ** -0.5)
    ii = lax.broadcasted_iota(jnp.int32, (c, c), 0)
    jj = lax.broadcasted_iota(jnp.int32, (c, c), 1)
    incl = (jj <= ii)[None]
    strict = (jj < ii)[None]
    decay = jnp.where(incl, jnp.exp(jnp.where(incl, gc - gr, 0.0)), 0.0)
    kb = k * bc
    low = jnp.where(strict, _bnt_hi(kb, k) * decay, 0.0)
    ainv = _unit_lower_inv(low)
    eg = jnp.exp(gc)
    u = _bnn_hi(ainv, v * bc)
    w = _bnn_hi(ainv, kb * eg)
    attn = _bnt(q, k) * decay
    q_dec = q * eg
    k_dec = k * jnp.exp(gl - gc)
    v_new = u - _bnn(w, state)
    o = _bnn(q_dec, state) + _bnn(attn, v_new)
    new_state = state * jnp.exp(gl) + _btn(k_dec, v_new)
    return new_state, o


def _dn_heads(ref, base, hh):
    return jnp.stack([ref[:, base + h * DN_HEAD_DIM: base + (h + 1) * DN_HEAD_DIM] for h in range(hh)])


def _dn_fwd_call(qkv, gc, gr, gl, bc, name):
    s, w3 = qkv.shape
    width = w3 // 3
    hh = width // DN_HEAD_DIM
    d = DN_HEAD_DIM
    n = s // CHUNK

    def body(qkv_ref, gc_ref, gr_ref, gl_ref, bc_ref, o_ref, st_ref, state):
        @pl.when(pl.program_id(0) == 0)
        def _():
            state[...] = jnp.zeros_like(state)

        s_in = state[...]
        st_ref[0] = s_in
        new_s, o = _dn_chunk(s_in, _dn_heads(qkv_ref, 0, hh), _dn_heads(qkv_ref, width, hh),
                             _dn_heads(qkv_ref, 2 * width, hh), gc_ref[0], gr_ref[0], gl_ref[0], bc_ref[0])
        state[...] = new_s
        for h in range(hh):
            o_ref[:, h * d:(h + 1) * d] = o[h]

    g4 = lambda i: (i, 0, 0, 0)
    return _pcall(
        body, name=name + "_fwd",
        out_shape=[jax.ShapeDtypeStruct((s, width), F32), jax.ShapeDtypeStruct((n, hh, d, d), F32)],
        grid=(n,),
        in_specs=[pl.BlockSpec((CHUNK, w3), lambda i: (i, 0)), pl.BlockSpec((1, hh, CHUNK, 1), g4),
                  pl.BlockSpec((1, hh, 1, CHUNK), g4), pl.BlockSpec((1, hh, 1, 1), g4),
                  pl.BlockSpec((1, hh, CHUNK, 1), g4)],
        out_specs=[pl.BlockSpec((CHUNK, width), lambda i: (i, 0)), pl.BlockSpec((1, hh, d, d), g4)],
        scratch=[pltpu.VMEM((hh, d, d), F32)], sem=("arbitrary",))(qkv, gc, gr, gl, bc)


def _dn_bwd_call(qkv, gc, gr, gl, bc, states, do, name):
    s, w3 = qkv.shape
    width = w3 // 3
    hh = width // DN_HEAD_DIM
    d = DN_HEAD_DIM
    n = s // CHUNK

    def body(qkv_ref, gc_ref, gr_ref, gl_ref, bc_ref, st_ref, do_ref,
             dqkv_ref, dgc_ref, dgr_ref, dgl_ref, dbc_ref, dstate):
        @pl.when(pl.program_id(0) == 0)
        def _():
            dstate[...] = jnp.zeros_like(dstate)

        prim = (st_ref[0], _dn_heads(qkv_ref, 0, hh), _dn_heads(qkv_ref, width, hh),
                _dn_heads(qkv_ref, 2 * width, hh), gc_ref[0], gr_ref[0], gl_ref[0], bc_ref[0])
        _, vjp = jax.vjp(_dn_chunk, *prim)
        ds, dq, dk, dv, dgc, dgr, dgl, dbc = vjp((dstate[...], _dn_heads(do_ref, 0, hh)))
        dstate[...] = ds
        for h in range(hh):
            dqkv_ref[:, h * d:(h + 1) * d] = dq[h]
            dqkv_ref[:, width + h * d: width + (h + 1) * d] = dk[h]
            dqkv_ref[:, 2 * width + h * d: 2 * width + (h + 1) * d] = dv[h]
        dgc_ref[0] = dgc
        dgr_ref[0] = dgr
        dgl_ref[0] = dgl
        dbc_ref[0] = dbc

    r2 = lambda i: (n - 1 - i, 0)
    r4 = lambda i: (n - 1 - i, 0, 0, 0)
    spec_c = pl.BlockSpec((1, hh, CHUNK, 1), r4)
    spec_r = pl.BlockSpec((1, hh, 1, CHUNK), r4)
    spec_l = pl.BlockSpec((1, hh, 1, 1), r4)
    return _pcall(
        body, name=name + "_bwd",
        out_shape=[jax.ShapeDtypeStruct(qkv.shape, F32), jax.ShapeDtypeStruct(gc.shape, F32),
                   jax.ShapeDtypeStruct(gr.shape, F32), jax.ShapeDtypeStruct(gl.shape, F32),
                   jax.ShapeDtypeStruct(bc.shape, F32)],
        grid=(n,),
        in_specs=[pl.BlockSpec((CHUNK, w3), r2), spec_c, spec_r, spec_l, spec_c,
                  pl.BlockSpec((1, hh, d, d), r4), pl.BlockSpec((CHUNK, width), r2)],
        out_specs=[pl.BlockSpec((CHUNK, w3), r2), spec_c, spec_r, spec_l, spec_c],
        scratch=[pltpu.VMEM((hh, d, d), F32)], sem=("arbitrary",))(qkv, gc, gr, gl, bc, states, do)


@functools.partial(jax.custom_vjp, nondiff_argnums=(5,))
def dn_core(qkv, gc, gr, gl, bc, name):
    return _dn_fwd_call(qkv, gc, gr, gl, bc, name)[0]


def _dn_core_fwd(qkv, gc, gr, gl, bc, name):
    o, states = _dn_fwd_call(qkv, gc, gr, gl, bc, name)
    return o, (qkv, gc, gr, gl, bc, states)


def _dn_core_bwd(name, saved, do):
    return tuple(_dn_bwd_call(*saved, do, name))


dn_core.defvjp(_dn_core_fwd, _dn_core_bwd)


def dn_out_gate(o, z, out_norm, name):
    hh = o.shape[1] // DN_HEAD_DIM
    gain = jnp.tile(out_norm.reshape(1, -1), (1, hh))

    def fn(ot, zt, gt):
        y = ot * lax.rsqrt(jnp.mean(ot * ot, axis=-1, keepdims=True) + EPS) * gt
        return (y * _silu(zt),)

    return tilewise(fn, name, [o, z, gain], ["tile", "tile", "tile"], [(o.shape[0], BF16)],
                    axis=1, t=DN_HEAD_DIM, diff=[True, True, True])[0]


def _attn_tile(qn, kn, v, qp, kp, q0, scale, causal):
    s = _nt(qn, kn)
    if qp is not None:
        s = s + _nt(qp, kp)
    s = s * scale
    if causal:
        qpos = q0 + lax.broadcasted_iota(jnp.int32, s.shape, 0)
        kpos = lax.broadcasted_iota(jnp.int32, s.shape, 1)
        s = jnp.where((kpos >> _CHUNK_SHIFT) <= (qpos >> _CHUNK_SHIFT), s, -1e30)
    e = jnp.exp(s - jnp.max(s, axis=-1, keepdims=True))
    p = e / jnp.sum(e, axis=-1, keepdims=True)
    return _nn(p, v)


def _attn_specs(sq, sk, dh, dp, tb):
    q_spec = pl.BlockSpec((tb, dh), lambda h, i: (i, h))
    kv_spec = pl.BlockSpec((sk, dh), lambda h, i: (0, h))
    qp_spec = pl.BlockSpec((1, tb, dp), lambda h, i: (h, i, 0)) if dp else None
    kp_spec = pl.BlockSpec((sk, dp), lambda h, i: (0, 0)) if dp else None
    return q_spec, kv_spec, qp_spec, kp_spec


def _attn_plan(sq, sk, tq, causal):
    tb = sq if causal else tq
    subs = [(slice(i * tq, (i + 1) * tq), (i + 1) * tq if causal else sk) for i in range(tb // tq)]
    return tb, subs


def _attn_fwd_call(q, k, v, qp, kp, *, dh, scale, causal, tq, name):
    sq, sk = q.shape[0], k.shape[0]
    heads = q.shape[1] // dh
    dp = qp.shape[-1] if qp is not None else 0
    tb, subs = _attn_plan(sq, sk, tq, causal)
    q_spec, kv_spec, qp_spec, kp_spec = _attn_specs(sq, sk, dh, dp, tb)

    def body(*refs):
        q_ref, k_ref, v_ref = refs[:3]
        qp_ref, kp_ref = (refs[3], refs[4]) if dp else (None, None)
        o_ref = refs[-1]
        row0 = pl.program_id(1) * tb
        for rows, ke in subs:
            o = _attn_tile(q_ref[rows, :], k_ref[:ke, :], v_ref[:ke, :], qp_ref[0, rows, :] if dp else None,
                           kp_ref[:ke, :] if dp else None, row0 + rows.start, scale, causal)
            o_ref[rows, :] = o.astype(o_ref.dtype)

    in_specs = [q_spec, kv_spec, kv_spec] + ([qp_spec, kp_spec] if dp else [])
    args = (q, k, v) + ((qp, kp) if dp else ())
    return _pcall(body, name=name + "_fwd", out_shape=jax.ShapeDtypeStruct((sq, heads * dh), BF16),
                  grid=(heads, sq // tb), in_specs=in_specs, out_specs=q_spec,
                  sem=("parallel", "parallel"))(*args)


def _attn_bwd_call(q, k, v, qp, kp, do, *, dh, scale, causal, tq, name):
    sq, sk = q.shape[0], k.shape[0]
    heads = q.shape[1] // dh
    dp = qp.shape[-1] if qp is not None else 0
    tb, subs = _attn_plan(sq, sk, tq, causal)
    q_spec, kv_spec, qp_spec, kp_spec = _attn_specs(sq, sk, dh, dp, tb)

    def body(*refs):
        h, i = pl.program_id(0), pl.program_id(1)
        if dp:
            q_ref, k_ref, v_ref, qp_ref, kp_ref, do_ref, dq_ref, dk_ref, dv_ref, dqp_ref, dkp_ref = refs
        else:
            q_ref, k_ref, v_ref, do_ref, dq_ref, dk_ref, dv_ref = refs

        @pl.when(i == 0)
        def _():
            dk_ref[...] = jnp.zeros_like(dk_ref)
            dv_ref[...] = jnp.zeros_like(dv_ref)

        if dp:
            @pl.when(jnp.logical_and(h == 0, i == 0))
            def _():
                dkp_ref[...] = jnp.zeros_like(dkp_ref)

        for rows, ke in subs:
            q0 = i * tb + rows.start
            if dp:
                prim = (q_ref[rows, :], k_ref[:ke, :], v_ref[:ke, :], qp_ref[0, rows, :], kp_ref[:ke, :])
                f = lambda a, b, c, d, e, q0=q0: _attn_tile(a, b, c, d, e, q0, scale, causal)
            else:
                prim = (q_ref[rows, :], k_ref[:ke, :], v_ref[:ke, :])
                f = lambda a, b, c, q0=q0: _attn_tile(a, b, c, None, None, q0, scale, causal)
            _, vjp = jax.vjp(f, *prim)
            grads = vjp(do_ref[rows, :].astype(F32))
            dq_ref[rows, :] = grads[0]
            dk_ref[:ke, :] += grads[1]
            dv_ref[:ke, :] += grads[2]
            if dp:
                dqp_ref[0, rows, :] = grads[3]
                dkp_ref[:ke, :] += grads[4]

    in_specs = [q_spec, kv_spec, kv_spec] + ([qp_spec, kp_spec] if dp else []) + [q_spec]
    out_shape = [jax.ShapeDtypeStruct(q.shape, F32), jax.ShapeDtypeStruct(k.shape, F32),
                 jax.ShapeDtypeStruct(v.shape, F32)]
    out_specs = [q_spec, kv_spec, kv_spec]
    if dp:
        out_shape += [jax.ShapeDtypeStruct(qp.shape, F32), jax.ShapeDtypeStruct(kp.shape, F32)]
        out_specs += [qp_spec, kp_spec]
    args = (q, k, v) + ((qp, kp) if dp else ()) + (do,)
    return _pcall(body, name=name + "_bwd", out_shape=out_shape, grid=(heads, sq // tb), in_specs=in_specs,
                  out_specs=out_specs, sem=("arbitrary", "arbitrary"))(*args)


def attention(q, k, v, qp=None, kp=None, *, dh, scale, causal, name, tq=256):
    tq = _tile(q.shape[0], tq)
    kw = dict(dh=dh, scale=scale, causal=causal, tq=tq, name=name)
    has_pe = qp is not None

    @jax.custom_vjp
    def op(*xs):
        return _attn_fwd_call(*xs, **kw) if has_pe else _attn_fwd_call(*xs, None, None, **kw)

    def op_fwd(*xs):
        return (_attn_fwd_call(*xs, **kw) if has_pe else _attn_fwd_call(*xs, None, None, **kw)), xs

    def op_bwd(xs, do):
        full = xs if has_pe else xs + (None, None)
        return tuple(_attn_bwd_call(*full, do, **kw))

    op.defvjp(op_fwd, op_bwd)
    return op(q, k, v, qp, kp) if has_pe else op(q, k, v)


def _rope_tables(positions, reps):
    half = MLA_ROPE // 2
    inv = ROPE_BASE ** (-jnp.arange(0, MLA_ROPE, 2, dtype=F32) / MLA_ROPE)
    ang = positions.astype(F32)[:, None] * inv
    cos, sin = jnp.cos(ang), jnp.sin(ang)
    c = jnp.tile(jnp.concatenate([cos, cos], axis=-1), (1, reps))
    s = jnp.tile(jnp.concatenate([sin, sin], axis=-1), (1, reps))
    rot = np.zeros((MLA_ROPE, MLA_ROPE), np.float32)
    for i in range(half):
        rot[i + half, i] = -1.0
        rot[i, i + half] = 1.0
    return c, s, jnp.asarray(np.kron(np.eye(reps, dtype=np.float32), rot))


def rope(x, positions, name):
    c, s, rot = _rope_tables(positions, x.shape[1] // MLA_ROPE)

    def fn(xt, ct, st, rt):
        return (xt * ct + _nn_hi_const(xt, rt) * st,)

    return tilewise(fn, name, [x, c, s, rot], ["tile", "tile", "tile", "whole"], [(x.shape[1], F32)],
                    axis=0, t=_tile(x.shape[0], 256), diff=[True, False, False, False])[0]


def ffn_gate(pre_g, pre_u, conv_g, conv_u, bias_g, bias_u, name):
    def fn(g, u, wg, wu, bg, bu):
        return (_silu(_causal_dwconv_tile(g, wg) + bg) * (_causal_dwconv_tile(u, wu) + bu),)

    return tilewise(fn, name, [pre_g, pre_u, conv_g, conv_u, bias_g, bias_u], ["tile"] * 6,
                    [(pre_g.shape[0], BF16)], axis=1, t=_tile(pre_g.shape[1], 256), diff=[True] * 6)[0]


def loss_rows(h, gain, target, name):
    def fn(ht, tt, gt):
        err = _rms_tile(ht, gt, F32) - tt
        return (0.5 * jnp.mean(err * err, axis=-1, keepdims=True),)

    return tilewise(fn, name, [h, target, gain.reshape(1, -1)], ["tile", "tile", "whole"], [(1, F32)],
                    axis=0, t=_tile(h.shape[0], 256), diff=[True, False, True])[0]


def _col_groups(w, per_head, lo, hi):
    r = w.shape[0]
    return w.reshape(r, -1, per_head)[:, :, lo:hi].reshape(r, -1)


def _layer_mix(h, lw, rp, l, positions):
    s, d = h.shape
    width = d // 2
    dn_heads = width // DN_HEAD_DIM
    mla_heads = (d - width) // MLA_V
    nm = f"l{l}_"
    w_in = lw["w_in"]
    c0 = 4 * width
    c1 = c0 + 2 * dn_heads
    c2 = c1 + MLA_Q_RANK
    c3 = c2 + MLA_KV_RANK + MLA_ROPE
    rest_pad = (-(c3 - c0)) % LANE
    w_rest = jnp.pad(w_in[:, c0:c3], ((0, 0), (0, rest_pad)))

    u = rms_norm(h, rp["norm_mix"], nm + "norm_mix", BF16)
    qkv_raw = linear(u, w_in[:, :3 * width], nm + "in_qkv")
    z = linear(u, w_in[:, 3 * width:c0], nm + "in_z")
    rest = linear(u, w_rest, nm + "in_rest")

    qkv = dn_qkv(qkv_raw, lw["dn_conv"], nm + "dn_qkv")
    csum, beta = dn_gates(rest[:, :dn_heads], rest[:, dn_heads:2 * dn_heads], rp["dn_a_log"],
                          rp["dn_dt_bias"], nm + "dn_gates")
    n = s // CHUNK
    g3 = csum.reshape(n, CHUNK, dn_heads).transpose(0, 2, 1)
    b3 = beta.reshape(n, CHUNK, dn_heads).transpose(0, 2, 1)
    o_dn = dn_core(qkv, g3[..., None], g3[:, :, None, :], g3[:, :, CHUNK - 1][..., None, None], b3[..., None],
                   nm + "dn_core")
    o_dn = dn_out_gate(o_dn, z, rp["dn_out_norm"], nm + "dn_gate")

    mq = rest[:, c1 - c0:c2 - c0]
    mkv = rest[:, c2 - c0:c3 - c0]
    qn = rms_norm(mq, rp["mla_q_norm"], nm + "mla_qnorm", BF16)
    per_q = MLA_NOPE + MLA_ROPE
    q_nope = linear(qn, _col_groups(lw["mla_w_qb"], per_q, 0, MLA_NOPE), nm + "mla_qn")
    q_pe = linear(qn, _col_groups(lw["mla_w_qb"], per_q, MLA_NOPE, per_q), nm + "mla_qp")
    kvn = rms_norm(mkv[:, :MLA_KV_RANK], rp["mla_kv_norm"], nm + "mla_kvnorm", BF16)
    per_kv = MLA_NOPE + MLA_V
    k_nope = linear(kvn, _col_groups(lw["mla_w_kvb"], per_kv, 0, MLA_NOPE), nm + "mla_kn")
    v_mla = linear(kvn, _col_groups(lw["mla_w_kvb"], per_kv, MLA_NOPE, per_kv), nm + "mla_v")
    q_pe = rope(q_pe, positions, nm + "rope_q")
    k_pe = rope(mkv[:, MLA_KV_RANK:], positions, nm + "rope_k")
    q_pe = q_pe.reshape(s, mla_heads, MLA_ROPE).transpose(1, 0, 2)
    o_mla = attention(q_nope, k_nope, v_mla, q_pe, k_pe, dh=MLA_NOPE, scale=per_q ** -0.5, causal=True,
                      name=nm + "mla_attn")

    h = linear(o_dn, lw["w_out"][:width], nm + "out_dn", res=h)
    return linear(o_mla, lw["w_out"][width:], nm + "out_mla", res=h)


def _layer_tail(h, lw, rp, l, mem_n):
    d = h.shape[1]
    nm = f"l{l}_"
    hn = rms_norm(h, rp["norm_xattn"], nm + "norm_xattn", BF16)
    xq = linear(hn, lw["xa_wq"], nm + "xa_q")
    xk = linear(mem_n, lw["xa_wk"], nm + "xa_k")
    xv = linear(mem_n, lw["xa_wv"], nm + "xa_v")
    xdh = d // XA_HEADS
    xo = attention(xq, xk, xv, dh=xdh, scale=xdh ** -0.5, causal=False, name=nm + "xattn")
    h = linear(xo, lw["xa_wo"], nm + "xa_o", res=h)

    hn = rms_norm(h, rp["norm_ffn"], nm + "norm_ffn", BF16)
    w_up = lw["ffn_w_up"]
    d_ff = w_up.shape[1] // 2
    pre_g = linear(hn, w_up[:, :d_ff], nm + "ffn_upg")
    pre_u = linear(hn, w_up[:, d_ff:], nm + "ffn_upu")
    bias = rp["ffn_conv_bias"].reshape(1, -1)
    conv = lw["ffn_conv"]
    act = ffn_gate(pre_g, pre_u, conv[:, :d_ff], conv[:, d_ff:], bias[:, :d_ff], bias[:, d_ff:], nm + "ffn_gate")
    return linear(act, lw["ffn_w_down"], nm + "ffn_down", res=h)


_COL_SHARDED = ("w_in", "mla_w_qb", "mla_w_kvb", "ffn_w_up", "dn_conv", "ffn_conv")
_ROW_SHARDED = ("w_out", "xa_wq", "xa_wk", "xa_wv", "xa_wo", "ffn_w_down")
_BIG = ("w_in", "mla_w_qb", "mla_w_kvb", "w_out", "xa_wq", "xa_wk", "xa_wv", "xa_wo", "ffn_w_up", "ffn_w_down")
_MIX_KEYS = ("w_in", "mla_w_qb", "mla_w_kvb", "w_out")
_TAIL_KEYS = ("xa_wq", "xa_wk", "xa_wv", "xa_wo", "ffn_w_up", "ffn_w_down")
_SMALL_SHARDED = ("dn_conv", "ffn_conv")
_REP_MIX = ("norm_mix", "dn_a_log", "dn_dt_bias", "dn_out_norm", "mla_q_norm", "mla_kv_norm")
_REP_TAIL = ("norm_xattn", "norm_ffn", "ffn_conv_bias")
_REPLICATED = ("norm_mix", "dn_a_log", "dn_dt_bias", "dn_out_norm", "mla_q_norm", "mla_kv_norm", "mem_norm",
               "norm_xattn", "norm_ffn", "ffn_conv_bias", "norm_final")
_WEIGHTS = ("norm_mix", "w_in", "dn_conv", "dn_a_log", "dn_dt_bias", "dn_out_norm", "mla_q_norm", "mla_w_qb",
            "mla_kv_norm", "mla_w_kvb", "w_out", "mem_norm", "norm_xattn", "xa_wq", "xa_wk", "xa_wv", "xa_wo",
            "norm_ffn", "ffn_w_up", "ffn_conv", "ffn_conv_bias", "ffn_w_down", "norm_final")


def _assemble(name, g):
    if name in _ROW_SHARDED:
        return g.reshape((-1,) + g.shape[2:])
    return jnp.moveaxis(g, 0, -2).reshape(g.shape[1:-1] + (-1,))


def _mix(h, gw, conv, rp, l, positions):
    lw = {k: _assemble(k, v) for k, v in gw.items()}
    lw["dn_conv"] = _assemble("dn_conv", conv)
    return _layer_mix(h, lw, rp, l, positions)


def _tail(h, gw, conv, rp, l, mem_n):
    lw = {k: _assemble(k, v) for k, v in gw.items()}
    lw["ffn_conv"] = _assemble("ffn_conv", conv)
    return _layer_tail(h, lw, rp, l, mem_n)


def _my_index():
    return 4 * lax.axis_index("x") + 2 * lax.axis_index("y") + lax.axis_index("c")


def _peer(k):
    x, y, c = lax.axis_index("x"), lax.axis_index("y"), lax.axis_index("c")
    px = (1 - x) if k & 4 else x
    py = (1 - y) if k & 2 else y
    pc = (1 - c) if k & 1 else c
    return (px, py, pc), 4 * px + 2 * py + pc


_ANY = pl.BlockSpec(memory_space=pl.ANY)


def all_gather(shards, name):
    n = len(shards)

    def body(*refs):
        x_refs, o_refs = refs[:n], refs[n:2 * n]
        send_sems, recv_sems, local_sems = refs[2 * n:]
        me = _my_index()
        sib_id, sib = _peer(1)
        chips = [_peer(k) for k in (4, 2, 6)]

        def copy(a, k, block, to, src=None):
            return pltpu.make_async_remote_copy(
                src_ref=o_refs[a].at[block] if src is None else src, dst_ref=o_refs[a].at[block],
                send_sem=send_sems.at[a * 7 + k], recv_sem=recv_sems.at[a * 7 + k],
                device_id=to, device_id_type=MESH)

        mine = [pltpu.make_async_copy(x_refs[a], o_refs[a].at[me], local_sems.at[a]) for a in range(n)]
        for cp in mine:
            cp.start()
        first = []
        for a in range(n):
            first.append(copy(a, 0, me, sib_id, src=x_refs[a]))
            first += [copy(a, 1 + j, me, cid, src=x_refs[a]) for j, (cid, _) in enumerate(chips)]
        for cp in first:
            cp.start()
        passed = []
        for a in range(n):
            for j, (cid, cidx) in enumerate(chips):
                copy(a, 1 + j, cidx, cid).wait_recv()
                fwd = copy(a, 4 + j, cidx, sib_id)
                fwd.start()
                passed.append(fwd)
        for a in range(n):
            copy(a, 0, sib, sib_id).wait_recv()
            for j, (_, cidx) in enumerate(chips):
                copy(a, 4 + j, cidx ^ 1, sib_id).wait_recv()
        for cp in first + passed:
            cp.wait_send()
        for cp in mine:
            cp.wait()

    out_shape = [jax.ShapeDtypeStruct((N_DEV,) + s.shape, s.dtype) for s in shards]
    return _pcall(body, name=name, out_shape=out_shape, in_specs=[_ANY] * n, out_specs=[_ANY] * n,
                  scratch=[pltpu.SemaphoreType.DMA((7 * n,)), pltpu.SemaphoreType.DMA((7 * n,)),
                           pltpu.SemaphoreType.DMA((n,))])(*shards)


_HBM = pl.BlockSpec(memory_space=pltpu.HBM)
_SEM = pl.BlockSpec(memory_space=pltpu.SEMAPHORE)
_EFFECT = pltpu.SideEffectType.DATAFLOW_SIDE_EFFECTING


def _plan_gather_first(n):
    def plan(refs):
        me = _my_index()
        sib_id, sib = _peer(1)
        chips = [_peer(k) for k in (4, 2, 6)]
        out = []
        for a in range(n):
            x, land = refs[a], refs[n + a]
            out.append((x, land.at[me], sib_id, land.at[sib]))
            out += [(x, land.at[me], cid, land.at[cidx]) for cid, cidx in chips]
        return out

    return plan, 4 * n


def _plan_gather_pass(n):
    def plan(refs):
        sib_id, _ = _peer(1)
        chips = [_peer(k) for k in (4, 2, 6)]
        return [(refs[a].at[cidx], refs[a].at[cidx], sib_id, refs[a].at[cidx ^ 1])
                for a in range(n) for _, cidx in chips]

    return plan, 3 * n


def _plan_gather_direct(n):
    def plan(refs):
        me = _my_index()
        out = []
        for a in range(n):
            for k in range(1, N_DEV):
                pid, pidx = _peer(k)
                out.append((refs[a], refs[n + a].at[me], pid, refs[n + a].at[pidx]))
        return out

    return plan, 7 * n


def _plan_scatter(n):
    def plan(refs):
        me = _my_index()
        out = []
        for a in range(n):
            for k in range(1, N_DEV):
                pid, pidx = _peer(k)
                out.append((refs[a].at[pidx], refs[n + a].at[me], pid, refs[n + a].at[pidx]))
        return out

    return plan, 7 * n


def _remote_copy(src, dst, send_sems, recv_sems, i, dev):
    return pltpu.make_async_remote_copy(src_ref=src, dst_ref=dst, send_sem=send_sems.at[i], recv_sem=recv_sems.at[i],
                                        device_id=dev, device_id_type=MESH)


def exchange_start(bufs, plan_n, name, after=None):
    plan, n = plan_n
    nb = len(bufs)
    n_in = nb + (after is not None)

    def body(*refs):
        send_sems, recv_sems = refs[n_in], refs[n_in + 1]
        for i, (src, dst, dev, _) in enumerate(plan(refs[:nb])):
            _remote_copy(src, dst, send_sems, recv_sems, i, dev).start()
        refs[-1][...] = jnp.zeros_like(refs[-1])

    out_shape = (pltpu.SemaphoreType.DMA((n,)), pltpu.SemaphoreType.DMA((n,)),
                 *[pltpu.HBM(b.shape, b.dtype) for b in bufs], jax.ShapeDtypeStruct((8, LANE), F32))
    args = [pltpu.with_memory_space_constraint(b, pltpu.HBM) for b in bufs] + ([after] if after is not None else [])
    res = pl.pallas_call(
        body, name=name, out_shape=out_shape,
        in_specs=[_HBM] * nb + ([_ANY] if after is not None else []),
        out_specs=(_SEM, _SEM, *[_HBM] * nb, pl.BlockSpec(memory_space=pltpu.VMEM)),
        input_output_aliases={i: 2 + i for i in range(nb)},
        compiler_params=pltpu.CompilerParams(has_side_effects=_EFFECT), interpret=_INTERPRET)(*args)
    return (res[0], res[1], list(res[2:2 + nb])), res[-1]


def exchange_wait(flight, plan_n, name, after):
    plan, _ = plan_n
    send_sems, recv_sems, bufs = flight
    nb = len(bufs)

    def body(*refs):
        s_sems, r_sems = refs[nb], refs[nb + 1]
        for i, (src, _, dev, arrival) in enumerate(plan(refs[:nb])):
            cp = _remote_copy(src, arrival, s_sems, r_sems, i, dev)
            cp.wait_send()
            cp.wait_recv()

    return list(pl.pallas_call(
        body, name=name, out_shape=tuple(pltpu.HBM(b.shape, b.dtype) for b in bufs),
        in_specs=[_HBM] * nb + [_SEM, _SEM, _ANY], out_specs=tuple([_HBM] * nb),
        input_output_aliases={i: i for i in range(nb)},
        compiler_params=pltpu.CompilerParams(has_side_effects=_EFFECT), interpret=_INTERPRET)(
            *bufs, send_sems, recv_sems, after))


def _adam_tile(contribs, w, m, v):
    g = contribs[0].astype(F32)
    for part in contribs[1:]:
        g = g + part.astype(F32)
    m_new = ADAM_B1 * m + (1.0 - ADAM_B1) * g
    v_new = ADAM_B2 * v + (1.0 - ADAM_B2) * (g * g)
    m_hat = m_new / (1.0 - ADAM_B1 ** ADAM_STEP)
    v_hat = v_new / (1.0 - ADAM_B2 ** ADAM_STEP)
    return g, -ADAM_LR * (m_hat / (jnp.sqrt(v_hat) + ADAM_EPS) + ADAM_WD * w), m_new, v_new


def adam_update(recv, w, m, v, name):
    ll, _, r, c = recv.shape
    tr = _tile(r, max(8, 1 << ((_ADAM_TILE_ELEMS // c).bit_length() - 1)))

    def body(g_ref, w_ref, m_ref, v_ref, go_ref, d_ref, mo_ref, vo_ref):
        go_ref[0], d_ref[0], mo_ref[0], vo_ref[0] = _adam_tile([g_ref[0, j] for j in range(N_DEV)], w_ref[0],
                                                               m_ref[0], v_ref[0])

    spec = pl.BlockSpec((1, tr, c), lambda l, i: (l, i, 0))
    sds = jax.ShapeDtypeStruct((ll, r, c), F32)
    return _pcall(body, name=name, out_shape=[sds] * 4, grid=(ll, r // tr),
                  in_specs=[pl.BlockSpec((1, N_DEV, tr, c), lambda l, i: (l, 0, i, 0)), spec, spec, spec],
                  out_specs=[spec] * 4, sem=("parallel", "parallel"))(recv, w, m, v)


def adam_layer(recv, w, m, v, prev, l, name):
    _, r, c = recv.shape
    tr = _tile(r, max(8, 1 << ((_ADAM_TILE_ELEMS // c).bit_length() - 1)))
    if prev is None:
        prev = [lax.empty(w.shape, F32) for _ in range(4)]

    def body(g_ref, w_ref, m_ref, v_ref, *rest):
        go_ref, d_ref, mo_ref, vo_ref = rest[4:]
        go_ref[0], d_ref[0], mo_ref[0], vo_ref[0] = _adam_tile([g_ref[j] for j in range(N_DEV)], w_ref[0], m_ref[0],
                                                               v_ref[0])

    spec = pl.BlockSpec((1, tr, c), lambda i: (l, i, 0))
    return _pcall(body, name=name, out_shape=[jax.ShapeDtypeStruct(w.shape, F32)] * 4, grid=(r // tr,),
                  in_specs=[pl.BlockSpec((N_DEV, tr, c), lambda i: (0, i, 0)), spec, spec, spec] + [_ANY] * 4,
                  out_specs=[spec] * 4, sem=("parallel",),
                  input_output_aliases={4 + j: j for j in range(4)})(recv, w, m, v, *prev)


def _as3d(a):
    if a.ndim == 1:
        return a.reshape(1, 1, -1)
    if a.ndim == 2:
        return a.reshape(1, a.shape[0], a.shape[1])
    return a.reshape(a.shape[0], -1, a.shape[-1])


def kernel(x, mem, positions, norm_mix, w_in, dn_conv, dn_a_log, dn_dt_bias, dn_out_norm, mla_q_norm, mla_w_qb, mla_kv_norm, mla_w_kvb, w_out, mem_norm, norm_xattn, xa_wq, xa_wk, xa_wv, xa_wo, norm_ffn, ffn_w_up, ffn_conv, ffn_conv_bias, ffn_w_down, norm_final, loss_target, m_norm_mix, m_w_in, m_dn_conv, m_dn_a_log, m_dn_dt_bias, m_dn_out_norm, m_mla_q_norm, m_mla_w_qb, m_mla_kv_norm, m_mla_w_kvb, m_w_out, m_mem_norm, m_norm_xattn, m_xa_wq, m_xa_wk, m_xa_wv, m_xa_wo, m_norm_ffn, m_ffn_w_up, m_ffn_conv, m_ffn_conv_bias, m_ffn_w_down, m_norm_final, v_norm_mix, v_w_in, v_dn_conv, v_dn_a_log, v_dn_dt_bias, v_dn_out_norm, v_mla_q_norm, v_mla_w_qb, v_mla_kv_norm, v_mla_w_kvb, v_w_out, v_mem_norm, v_norm_xattn, v_xa_wq, v_xa_wk, v_xa_wv, v_xa_wo, v_norm_ffn, v_ffn_w_up, v_ffn_conv, v_ffn_conv_bias, v_ffn_w_down, v_norm_final):
    w = dict(norm_mix=norm_mix, w_in=w_in, dn_conv=dn_conv, dn_a_log=dn_a_log, dn_dt_bias=dn_dt_bias,
             dn_out_norm=dn_out_norm, mla_q_norm=mla_q_norm, mla_w_qb=mla_w_qb, mla_kv_norm=mla_kv_norm,
             mla_w_kvb=mla_w_kvb, w_out=w_out, mem_norm=mem_norm, norm_xattn=norm_xattn, xa_wq=xa_wq, xa_wk=xa_wk,
             xa_wv=xa_wv, xa_wo=xa_wo, norm_ffn=norm_ffn, ffn_w_up=ffn_w_up, ffn_conv=ffn_conv,
             ffn_conv_bias=ffn_conv_bias, ffn_w_down=ffn_w_down, norm_final=norm_final)
    mom = dict(norm_mix=m_norm_mix, w_in=m_w_in, dn_conv=m_dn_conv, dn_a_log=m_dn_a_log, dn_dt_bias=m_dn_dt_bias,
               dn_out_norm=m_dn_out_norm, mla_q_norm=m_mla_q_norm, mla_w_qb=m_mla_w_qb, mla_kv_norm=m_mla_kv_norm,
               mla_w_kvb=m_mla_w_kvb, w_out=m_w_out, mem_norm=m_mem_norm, norm_xattn=m_norm_xattn, xa_wq=m_xa_wq,
               xa_wk=m_xa_wk, xa_wv=m_xa_wv, xa_wo=m_xa_wo, norm_ffn=m_norm_ffn, ffn_w_up=m_ffn_w_up,
               ffn_conv=m_ffn_conv, ffn_conv_bias=m_ffn_conv_bias, ffn_w_down=m_ffn_w_down, norm_final=m_norm_final)
    var = dict(norm_mix=v_norm_mix, w_in=v_w_in, dn_conv=v_dn_conv, dn_a_log=v_dn_a_log, dn_dt_bias=v_dn_dt_bias,
               dn_out_norm=v_dn_out_norm, mla_q_norm=v_mla_q_norm, mla_w_qb=v_mla_w_qb, mla_kv_norm=v_mla_kv_norm,
               mla_w_kvb=v_mla_w_kvb, w_out=v_w_out, mem_norm=v_mem_norm, norm_xattn=v_norm_xattn, xa_wq=v_xa_wq,
               xa_wk=v_xa_wk, xa_wv=v_xa_wv, xa_wo=v_xa_wo, norm_ffn=v_norm_ffn, ffn_w_up=v_ffn_w_up,
               ffn_conv=v_ffn_conv, ffn_conv_bias=v_ffn_conv_bias, ffn_w_down=v_ffn_w_down, norm_final=v_norm_final)
    depth = w_in.shape[0]
    me = _my_index()
    rep = {k: w[k] for k in _REPLICATED}
    x0, mem0, pos0, target0 = x[0], mem[0], positions[0], loss_target[0]

    def with_own_block(block):
        return lax.dynamic_update_index_in_dim(lax.empty((N_DEV,) + block.shape, block.dtype), block, me, 0)

    def gather_start(l, keys, tag, after):
        shards = [w[k][l].astype(BF16) for k in keys]
        plan = _plan_gather_first(len(keys))
        flight, token = exchange_start(shards + [with_own_block(sh) for sh in shards], plan,
                                       f"gather_l{l}{tag}_first_start", after)
        return (flight, plan, keys, f"gather_l{l}{tag}"), token

    def gather_pass(state, after):
        flight, plan, keys, name = state
        lands = exchange_wait(flight, plan, name + "_first_wait", after)[len(keys):]
        plan = _plan_gather_pass(len(keys))
        flight, token = exchange_start(lands, plan, name + "_pass_start")
        return (flight, plan, keys, name), token

    def gather_finish(state, after):
        flight, plan, keys, name = state
        return dict(zip(keys, exchange_wait(flight, plan, name + "_pass_wait", after)))

    small = dict(zip(_SMALL_SHARDED, all_gather([w[k] for k in _SMALL_SHARDED], "gather_small")))
    mem_n, vjp_mem = jax.vjp(lambda g: rms_norm(mem0, g, "mem_norm", BF16), rep["mem_norm"])

    groups = [(l, keys, tag) for l in range(depth) for keys, tag in ((_MIX_KEYS, "_mix"), (_TAIL_KEYS, "_tail"))]
    n_stage = len(groups)
    states, ready = {}, {}
    token = small["dn_conv"]
    for s in range(min(3, n_stage)):
        states[s], token = gather_start(*groups[s], token)
    states[0], token = gather_pass(states[0], token)
    ready[0] = gather_finish(states[0], token)

    h = x0
    tapes = []
    for s in range(n_stage):
        l = s // 2
        tokens = []
        if s >= 2 and s + 1 < n_stage:
            states[s + 1], token = gather_pass(states[s + 1], h)
            tokens.append(token)
        if s + 3 < n_stage:
            states[s + 3], token = gather_start(*groups[s + 3], h)
            tokens.append(token)
        if s % 2 == 0:
            rp = {k: rep[k][l] for k in _REP_MIX}
            for token in tokens:
                rp["norm_mix"] = rp["norm_mix"] + token[0, 0]
            h_new, tape = jax.vjp(lambda hh, gw, cv, rr: _mix(hh, gw, cv, rr, l, pos0), h, ready.pop(s),
                                  small["dn_conv"][:, l], rp)
        else:
            rp = {k: rep[k][l] for k in _REP_TAIL}
            for token in tokens:
                rp["norm_xattn"] = rp["norm_xattn"] + token[0, 0]
            h_new, tape = jax.vjp(lambda hh, gw, cv, rr, mn: _tail(hh, gw, cv, rr, l, mn), h, ready.pop(s),
                                  small["ffn_conv"][:, l], rp, mem_n)
        if s + 1 < n_stage:
            if s < 2:
                states[s + 1], token = gather_pass(states[s + 1], h_new)
                ready[s + 1] = gather_finish(states[s + 1], token)
            else:
                ready[s + 1] = gather_finish(states[s + 1], h_new)
        tapes.append(tape)
        h = h_new

    rows, vjp_loss = jax.vjp(lambda hh, g: loss_rows(hh, g, target0, "loss"), h, rep["norm_final"])
    loss = lax.psum(jnp.sum(rows), ("x", "y", "c"))
    dh, d_norm_final = vjp_loss(jnp.ones_like(rows))

    d_rep = {k: [None] * depth for k in _REP_MIX + _REP_TAIL}
    d_conv = {k: [None] * depth for k in _SMALL_SHARDED}
    recv_big = [dict() for _ in range(depth)]
    d_mem_n = None

    def scatter_start(l, grads, keys, tag):
        parts = [grads[k] for k in keys]
        own = [with_own_block(lax.dynamic_index_in_dim(p, me, 0, keepdims=False)) for p in parts]
        plan = _plan_scatter(len(keys))
        flight, token = exchange_start(parts + own, plan, f"scatter_l{l}{tag}_start")
        return (flight, plan, keys, l, f"scatter_l{l}{tag}_wait"), token

    def scatter_finish(state, after):
        flight, plan, keys, l, name = state
        recv_big[l].update(zip(keys, exchange_wait(flight, plan, name, after)[len(keys):]))

    pending = {}
    token = None
    for s in reversed(range(n_stage)):
        l, keys, tag = groups[s]
        if token is not None:
            dh = dh + token[0, 0]
        if s % 2:
            dh, dg, d_conv["ffn_conv"][l], d_rp, d_mn = tapes[s](dh)
            d_mem_n = d_mn if d_mem_n is None else d_mem_n + d_mn
        else:
            dh, dg, d_conv["dn_conv"][l], d_rp = tapes[s](dh)
        for k, v in d_rp.items():
            d_rep[k][l] = v
        if s + 2 in pending:
            scatter_finish(pending.pop(s + 2), dh)
        pending[s], token = scatter_start(l, dg, keys, tag)

    (d_mem_norm,) = vjp_mem(d_mem_n)
    d_rep_all = {k: jnp.stack(v) for k, v in d_rep.items()}
    d_rep_all["mem_norm"] = d_mem_norm
    d_rep_all["norm_final"] = d_norm_final + token[0, 0]
    parts = [jnp.stack(d_conv[k], axis=1) for k in _SMALL_SHARDED]
    own = [with_own_block(lax.dynamic_index_in_dim(p, me, 0, keepdims=False)) for p in parts]
    plan_small = _plan_scatter(len(parts))
    flight_small, token = exchange_start(parts + own, plan_small, "scatter_small_start")
    blocks = [_as3d(d_rep_all[k])[0] for k in _REPLICATED]
    blocks[0] = blocks[0] + token[0, 0]
    plan_rep = _plan_gather_direct(len(blocks))
    flight_rep, token = exchange_start(blocks + [with_own_block(b) for b in blocks], plan_rep, "gather_rep_start")

    big = {k: None for k in _BIG}

    def update_big(l):
        for k in _BIG:
            recv = recv_big[l][k]
            big[k] = adam_layer(recv.reshape((N_DEV, -1, recv.shape[-1])), _as3d(w[k]), _as3d(mom[k]),
                                _as3d(var[k]), big[k], l, f"adam_{k}_l{l}")

    for l in reversed(range(1, depth)):
        update_big(l)
    after = token if depth == 1 else big[_BIG[-1]][0]
    out = {}
    recv_small = exchange_wait(flight_small, plan_small, "scatter_small_wait", after)[len(parts):]
    for k, recv in zip(_SMALL_SHARDED, recv_small):
        r3 = _as3d(w[k])
        out[k] = adam_update(recv.reshape((1, N_DEV, -1, recv.shape[-1])), r3.reshape((1, -1, r3.shape[-1])),
                             _as3d(mom[k]).reshape((1, -1, r3.shape[-1])), _as3d(var[k]).reshape((1, -1, r3.shape[-1])),
                             "adam_" + k)
    recv_rep = exchange_wait(flight_rep, plan_rep, "gather_rep_wait", after)[len(blocks):]
    for k, recv in zip(_REPLICATED, recv_rep):
        r3 = _as3d(w[k])
        flat = lambda a, r3=r3: a.reshape((1, -1, r3.shape[-1]))
        out[k] = adam_update(recv[None], flat(w[k]), flat(mom[k]), flat(var[k]), "adam_" + k)
    for s in sorted(pending, reverse=True):
        scatter_finish(pending[s], out["norm_final"][0])
    update_big(0)
    out.update(big)

    res = [loss, dh[None]]
    for j in range(4):
        res += [out[k][j].reshape(w[k].shape) for k in _WEIGHTS]
    return tuple(res)
```

```python
import functools

import numpy as np
import jax
import jax.numpy as jnp
from jax import lax
from jax.experimental import pallas as pl
from jax.experimental.pallas import tpu as pltpu

F32 = jnp.float32
BF16 = jnp.bfloat16
_MXU_DTYPE = BF16
_INTERPRET = False
_VMEM_LIMIT_BYTES = 48 * 1024 * 1024
_MM_VMEM_BUDGET_BYTES = 36 * 1024 * 1024
_MM_TILE_CAP = 2048

N_DEV = 8
CHUNK = 64
_CHUNK_SHIFT = 6
DN_HEAD_DIM = 128
DN_CONV = 4
MLA_NOPE = 128
MLA_ROPE = 64
MLA_V = 128
MLA_Q_RANK = 512
MLA_KV_RANK = 256
ROPE_BASE = 10000.0
XA_HEADS = 4
FFN_CONV = 3
EPS = 1e-6
LANE = 128

ADAM_LR = 0.001
ADAM_B1 = 0.9
ADAM_B2 = 0.999
ADAM_EPS = 1e-08
ADAM_WD = 0.01
ADAM_STEP = 10
_ADAM_TILE_ELEMS = 128 * 1024

MESH = pl.DeviceIdType.MESH


def _pcall(body, *, name, out_shape, grid=None, in_specs=None, out_specs=None, scratch=(), sem=None, **kw):
    params = pltpu.CompilerParams(dimension_semantics=sem, vmem_limit_bytes=_VMEM_LIMIT_BYTES)
    args = dict(name=name, out_shape=out_shape, scratch_shapes=list(scratch), compiler_params=params,
                interpret=_INTERPRET, **kw)
    if grid is not None:
        args.update(grid=grid)
    if in_specs is not None:
        args.update(in_specs=in_specs)
    if out_specs is not None:
        args.update(out_specs=out_specs)
    return pl.pallas_call(body, **args)


def _tile(n, pref):
    if n <= pref:
        return n
    t = pref
    while t >= 8:
        if n % t == 0:
            return t
        t //= 2
    return n


def _ein_raw(spec, a, b, hi):
    one = lambda x, y: jnp.einsum(spec, x, y, preferred_element_type=F32)
    a_hi, b_hi = a.astype(_MXU_DTYPE), b.astype(_MXU_DTYPE)
    if not hi:
        return one(a_hi, b_hi)
    a_lo = (a.astype(F32) - a_hi.astype(F32)).astype(_MXU_DTYPE)
    b_lo = (b.astype(F32) - b_hi.astype(F32)).astype(_MXU_DTYPE)
    return one(a_hi, b_hi) + (one(a_hi, b_lo) + one(a_lo, b_hi))


def _make_ein(spec, hi=False, diff_b=True):
    a_s, rest = spec.split(",")
    b_s, o_s = rest.split("->")

    @jax.custom_vjp
    def f(a, b):
        return _ein_raw(spec, a, b, hi)

    def fwd(a, b):
        return f(a, b), (a, b)

    def bwd(res, g):
        a, b = res
        da = _ein_raw(f"{o_s},{b_s}->{a_s}", g, b, hi)
        if not diff_b:
            return da.astype(a.dtype), None
        db = _ein_raw(f"{a_s},{o_s}->{b_s}", a, g, hi)
        return da.astype(a.dtype), db.astype(b.dtype)

    f.defvjp(fwd, bwd)
    return f


_nt = _make_ein("qd,kd->qk")
_nn = _make_ein("qk,kd->qd")
_nn_hi_const = _make_ein("qk,kd->qd", hi=True, diff_b=False)
_bnt = _make_ein("hik,hjk->hij")
_bnt_hi = _make_ein("hik,hjk->hij", hi=True)
_bnn = _make_ein("hij,hjv->hiv")
_bnn_hi = _make_ein("hij,hjv->hiv", hi=True)
_btn = _make_ein("hck,hcv->hkv")


def _divisor_tiles(n, cap):
    out = [n] if n <= cap else []
    t = cap
    while t >= LANE:
        if t < n and n % t == 0:
            out.append(t)
        t //= 2
    return out or [n]


def _mm_tiles(m, n, k, a_bytes, b_bytes, o_bytes, r_bytes):
    best = None
    for tk in sorted({k, *_divisor_tiles(k, _MM_TILE_CAP)}):
        for tm in _divisor_tiles(m, _MM_TILE_CAP):
            for tn in _divisor_tiles(n, _MM_TILE_CAP):
                need = 2 * (tm * tk * a_bytes + tk * tn * b_bytes + tm * tn * (o_bytes + r_bytes))
                need += tm * tn * 4 * (1 if tk == k else 2)
                if need > _MM_VMEM_BUDGET_BYTES:
                    continue
                score = (tm * tn * tk, tk, tm)
                if best is None or score > best[0]:
                    best = (score, (tm, tn, tk))
    assert best is not None, (m, n, k)
    return best[1]


def _mm(a, b, *, ta=False, tb=False, res=None, out_dtype=F32, name="mm"):
    m = a.shape[1] if ta else a.shape[0]
    k = a.shape[0] if ta else a.shape[1]
    n = b.shape[0] if tb else b.shape[1]
    assert (b.shape[1] if tb else b.shape[0]) == k, (a.shape, b.shape, ta, tb)
    has_res = res is not None
    tm, tn, tk = _mm_tiles(m, n, k, a.dtype.itemsize, b.dtype.itemsize, jnp.dtype(out_dtype).itemsize,
                           res.dtype.itemsize if has_res else 0)
    nk = k // tk
    dims = (((0 if ta else 1,), (1 if tb else 0,)), ((), ()))

    def body(*refs):
        a_ref, b_ref = refs[0], refs[1]
        r_ref = refs[2] if has_res else None
        o_ref = refs[2 + has_res]

        def finish(r):
            if has_res:
                r = r + r_ref[...].astype(F32)
            o_ref[...] = r.astype(out_dtype)

        part = lax.dot_general(a_ref[...].astype(_MXU_DTYPE), b_ref[...].astype(_MXU_DTYPE), dims,
                               preferred_element_type=F32)
        if nk == 1:
            finish(part)
            return
        acc = refs[-1]
        kk = pl.program_id(2)

        @pl.when(kk == 0)
        def _():
            acc[...] = part

        @pl.when(kk > 0)
        def _():
            acc[...] += part

        @pl.when(kk == nk - 1)
        def _():
            finish(acc[...])

    a_spec = pl.BlockSpec((tk, tm), lambda i, j, kk: (kk, i)) if ta else pl.BlockSpec((tm, tk), lambda i, j, kk: (i, kk))
    b_spec = pl.BlockSpec((tn, tk), lambda i, j, kk: (j, kk)) if tb else pl.BlockSpec((tk, tn), lambda i, j, kk: (kk, j))
    o_spec = pl.BlockSpec((tm, tn), lambda i, j, kk: (i, j))
    in_specs = [a_spec, b_spec] + ([o_spec] if has_res else [])
    args = (a, b) + ((res,) if has_res else ())
    return _pcall(body, name=name, out_shape=jax.ShapeDtypeStruct((m, n), out_dtype),
                  grid=(m // tm, n // tn, nk), in_specs=in_specs, out_specs=o_spec,
                  scratch=[pltpu.VMEM((tm, tn), F32)] if nk > 1 else [],
                  sem=("parallel", "parallel", "arbitrary"))(*args)


@functools.partial(jax.custom_vjp, nondiff_argnums=(3,))
def _linear_res(a, w, res, name):
    return _mm(a, w, res=res, name=name + "_fwd")


def _linear_res_fwd(a, w, res, name):
    return _mm(a, w, res=res, name=name + "_fwd"), (a, w)


def _linear_res_bwd(name, saved, g):
    a, w = saved
    gm = g.astype(_MXU_DTYPE)
    da = _mm(gm, w, tb=True, out_dtype=a.dtype, name=name + "_da")
    dw = _mm(a, gm, ta=True, out_dtype=w.dtype, name=name + "_dw")
    return da, dw, g


_linear_res.defvjp(_linear_res_fwd, _linear_res_bwd)


@functools.partial(jax.custom_vjp, nondiff_argnums=(2,))
def _linear(a, w, name):
    return _mm(a, w, name=name + "_fwd")


def _linear_fwd(a, w, name):
    return _mm(a, w, name=name + "_fwd"), (a, w)


def _linear_bwd(name, saved, g):
    a, w = saved
    gm = g.astype(_MXU_DTYPE)
    da = _mm(gm, w, tb=True, out_dtype=a.dtype, name=name + "_da")
    dw = _mm(a, gm, ta=True, out_dtype=w.dtype, name=name + "_dw")
    return da, dw


_linear.defvjp(_linear_fwd, _linear_bwd)


def linear(a, w, name, res=None):
    return _linear(a, w, name) if res is None else _linear_res(a, w, res, name)


def _tiled_specs(arrs, kinds, t, axis):
    specs = []
    for a, kind in zip(arrs, kinds):
        if kind == "whole":
            specs.append(pl.BlockSpec(a.shape, lambda i, nd=a.ndim: (0,) * nd))
        elif axis == 0:
            specs.append(pl.BlockSpec((t, a.shape[1]), lambda i: (i, 0)))
        else:
            specs.append(pl.BlockSpec((a.shape[0], t), lambda i: (0, i)))
    return specs


def tilewise(fn, name, args, kinds, outs, *, axis, t, diff):
    n_in = len(args)
    length = next(a.shape[axis] for a, kd in zip(args, kinds) if kd == "tile")
    steps = length // t
    assert steps * t == length, (name, length, t)

    def out_sds(other, dtype):
        return jax.ShapeDtypeStruct((length, other) if axis == 0 else (other, length), dtype)

    def out_spec(other):
        return pl.BlockSpec((t, other), lambda i: (i, 0)) if axis == 0 else pl.BlockSpec((other, t), lambda i: (0, i))

    def run_fwd(*xs):
        def body(*refs):
            vals = fn(*[r[...] for r in refs[:n_in]])
            for o_ref, v in zip(refs[n_in:], vals):
                o_ref[...] = v.astype(o_ref.dtype)

        return _pcall(body, name=name + "_fwd", out_shape=[out_sds(o, d) for o, d in outs], grid=(steps,),
                      in_specs=_tiled_specs(xs, kinds, t, axis), out_specs=[out_spec(o) for o, _ in outs],
                      sem=("parallel",))(*xs)

    didx = [i for i in range(n_in) if diff[i]]

    def run_bwd(xs, cts):
        def body(*refs):
            x_refs, c_refs, g_refs = refs[:n_in], refs[n_in:n_in + len(outs)], refs[n_in + len(outs):]
            vals = [r[...] for r in x_refs]

            def g(*dvals):
                full = list(vals)
                for i, v in zip(didx, dvals):
                    full[i] = v
                return tuple(fn(*full))

            prim_out, vjp = jax.vjp(g, *[vals[i] for i in didx])
            grads = vjp(tuple(c[...].astype(o.dtype) for c, o in zip(c_refs, prim_out)))
            step = pl.program_id(0)
            for i, g_ref, gr in zip(didx, g_refs, grads):
                if kinds[i] == "whole":
                    @pl.when(step == 0)
                    def _(g_ref=g_ref):
                        g_ref[...] = jnp.zeros_like(g_ref)

                    g_ref[...] += gr.astype(g_ref.dtype)
                else:
                    g_ref[...] = gr.astype(g_ref.dtype)

        g_shapes = [jax.ShapeDtypeStruct(xs[i].shape, xs[i].dtype) for i in didx]
        g_specs = _tiled_specs([xs[i] for i in didx], [kinds[i] for i in didx], t, axis)
        ct_specs = [out_spec(o) for o, _ in outs]
        any_whole = any(kinds[i] == "whole" for i in didx)
        return _pcall(body, name=name + "_bwd", out_shape=g_shapes, grid=(steps,),
                      in_specs=_tiled_specs(xs, kinds, t, axis) + ct_specs, out_specs=g_specs,
                      sem=("arbitrary" if any_whole else "parallel",))(*xs, *cts)

    @jax.custom_vjp
    def op(*xs):
        return tuple(run_fwd(*xs))

    def op_fwd(*xs):
        return tuple(run_fwd(*xs)), xs

    def op_bwd(xs, cts):
        grads = run_bwd(xs, cts)
        full = [None] * n_in
        for i, gr in zip(didx, grads):
            full[i] = gr
        return tuple(full)

    op.defvjp(op_fwd, op_bwd)
    return op(*args)


def _silu(x):
    return x * (1.0 / (1.0 + jnp.exp(-x)))


def _softplus(x):
    return jnp.maximum(x, 0.0) + jnp.log(1.0 + jnp.exp(-jnp.abs(x)))


def _rms_tile(x, gain, out_dtype):
    xf = x.astype(F32)
    y = xf * lax.rsqrt(jnp.mean(xf * xf, axis=-1, keepdims=True) + EPS)
    return (y * gain).astype(out_dtype)


def rms_norm(x, gain, name, out_dtype, tr=256):
    def fn(xt, gt):
        return (_rms_tile(xt, gt, out_dtype),)

    return tilewise(fn, name, [x, gain.reshape(1, -1)], ["tile", "whole"], [(x.shape[1], out_dtype)],
                    axis=0, t=_tile(x.shape[0], tr), diff=[True, True])[0]


def _shift_down_raw(x, s):
    rows = lax.broadcasted_iota(jnp.int32, x.shape, 0)
    return jnp.where(rows >= s, pltpu.roll(x, s, 0), 0.0)


def _shift_up_raw(x, s):
    n = x.shape[0]
    rows = lax.broadcasted_iota(jnp.int32, x.shape, 0)
    return jnp.where(rows < n - s, pltpu.roll(x, n - s, 0), 0.0)


@functools.partial(jax.custom_vjp, nondiff_argnums=(1,))
def _shift_down(x, s):
    return _shift_down_raw(x, s)


def _shift_down_fwd(x, s):
    return _shift_down_raw(x, s), None


def _shift_down_bwd(s, _, g):
    return (_shift_up_raw(g, s),)


_shift_down.defvjp(_shift_down_fwd, _shift_down_bwd)


def _causal_dwconv_tile(x, w):
    kk = w.shape[0]
    y = x * w[kk - 1:kk, :]
    for j in range(kk - 1):
        y = y + _shift_down(x, kk - 1 - j) * w[j:j + 1, :]
    return y


def dn_qkv(raw, conv_w, name):
    width = raw.shape[1] // 3
    is_qk = (jnp.arange(raw.shape[1]) < 2 * width).astype(F32).reshape(1, -1)

    def fn(x, w, flag):
        y = _silu(_causal_dwconv_tile(x, w))
        yn = y * lax.rsqrt(jnp.sum(y * y, axis=-1, keepdims=True) + EPS)
        return (jnp.where(flag > 0.5, yn, y),)

    return tilewise(fn, name, [raw, conv_w, is_qk], ["tile", "tile", "tile"], [(raw.shape[0], F32)],
                    axis=1, t=DN_HEAD_DIM, diff=[True, True, False])[0]


def dn_gates(b, a, a_log, dt_bias, name):
    hh = b.shape[1]
    tr = _tile(b.shape[0], 256)

    def fn(bt, at, al, db):
        beta = 1.0 / (1.0 + jnp.exp(-bt))
        g = -jnp.exp(al) * _softplus(at + db)
        pos = lax.broadcasted_iota(jnp.int32, g.shape, 0) & (CHUNK - 1)
        step = 1
        while step < CHUNK:
            g = g + jnp.where(pos >= step, _shift_down(g, step), 0.0)
            step *= 2
        return g, beta

    return tilewise(fn, name, [b, a, a_log.reshape(1, -1), dt_bias.reshape(1, -1)],
                    ["tile", "tile", "whole", "whole"], [(hh, F32), (hh, F32)], axis=0, t=tr, diff=[True] * 4)


def _unit_lower_inv_raw(low):
    c = low.shape[-1]
    ii = lax.broadcasted_iota(jnp.int32, (c, c), 0)
    jj = lax.broadcasted_iota(jnp.int32, (c, c), 1)
    inv = (ii == jj).astype(F32)[None] - low
    p = low
    n = 1
    while 2 * n < c:
        p = _ein_raw("hij,hjk->hik", p, p, True)
        inv = inv + _ein_raw("hij,hjk->hik", inv, p, True)
        n *= 2
    return inv


@jax.custom_vjp
def _unit_lower_inv(low):
    return _unit_lower_inv_raw(low)


def _unit_lower_inv_fwd(low):
    inv = _unit_lower_inv_raw(low)
    return inv, inv


def _unit_lower_inv_bwd(inv, g):
    t = _ein_raw("hji,hjk->hik", inv, g, True)
    return (-_ein_raw("hik,hlk->hil", t, inv, True),)


_unit_lower_inv.defvjp(_unit_lower_inv_fwd, _unit_lower_inv_bwd)


def _dn_chunk(state, q, k, v, gc, gr, gl, bc):
    c = q.shape[1]
    q = q * (q.shape[-1] ** -0.5)
    ii = lax.broadcasted_iota(jnp.int32, (c, c), 0)
    jj = lax.broadcasted_iota(jnp.int32, (c, c), 1)
    incl = (jj <= ii)[None]
    strict = (jj < ii)[None]
    decay = jnp.where(incl, jnp.exp(jnp.where(incl, gc - gr, 0.0)), 0.0)
    kb = k * bc
    low = jnp.where(strict, _bnt_hi(kb, k) * decay, 0.0)
    ainv = _unit_lower_inv(low)
    eg = jnp.exp(gc)
    u = _bnn_hi(ainv, v * bc)
    w = _bnn_hi(ainv, kb * eg)
    attn = _bnt(q, k) * decay
    q_dec = q * eg
    k_dec = k * jnp.exp(gl - gc)
    v_new = u - _bnn(w, state)
    o = _bnn(q_dec, state) + _bnn(attn, v_new)
    new_state = state * jnp.exp(gl) + _btn(k_dec, v_new)
    return new_state, o


def _dn_heads(ref, base, hh):
    return jnp.stack([ref[:, base + h * DN_HEAD_DIM: base + (h + 1) * DN_HEAD_DIM] for h in range(hh)])


def _dn_fwd_call(qkv, gc, gr, gl, bc, name):
    s, w3 = qkv.shape
    width = w3 // 3
    hh = width // DN_HEAD_DIM
    d = DN_HEAD_DIM
    n = s // CHUNK

    def body(qkv_ref, gc_ref, gr_ref, gl_ref, bc_ref, o_ref, st_ref, state):
        @pl.when(pl.program_id(0) == 0)
        def _():
            state[...] = jnp.zeros_like(state)

        s_in = state[...]
        st_ref[0] = s_in
        new_s, o = _dn_chunk(s_in, _dn_heads(qkv_ref, 0, hh), _dn_heads(qkv_ref, width, hh),
                             _dn_heads(qkv_ref, 2 * width, hh), gc_ref[0], gr_ref[0], gl_ref[0], bc_ref[0])
        state[...] = new_s
        for h in range(hh):
            o_ref[:, h * d:(h + 1) * d] = o[h]

    g4 = lambda i: (i, 0, 0, 0)
    return _pcall(
        body, name=name + "_fwd",
        out_shape=[jax.ShapeDtypeStruct((s, width), F32), jax.ShapeDtypeStruct((n, hh, d, d), F32)],
        grid=(n,),
        in_specs=[pl.BlockSpec((CHUNK, w3), lambda i: (i, 0)), pl.BlockSpec((1, hh, CHUNK, 1), g4),
                  pl.BlockSpec((1, hh, 1, CHUNK), g4), pl.BlockSpec((1, hh, 1, 1), g4),
                  pl.BlockSpec((1, hh, CHUNK, 1), g4)],
        out_specs=[pl.BlockSpec((CHUNK, width), lambda i: (i, 0)), pl.BlockSpec((1, hh, d, d), g4)],
        scratch=[pltpu.VMEM((hh, d, d), F32)], sem=("arbitrary",))(qkv, gc, gr, gl, bc)


def _dn_bwd_call(qkv, gc, gr, gl, bc, states, do, name):
    s, w3 = qkv.shape
    width = w3 // 3
    hh = width // DN_HEAD_DIM
    d = DN_HEAD_DIM
    n = s // CHUNK

    def body(qkv_ref, gc_ref, gr_ref, gl_ref, bc_ref, st_ref, do_ref,
             dqkv_ref, dgc_ref, dgr_ref, dgl_ref, dbc_ref, dstate):
        @pl.when(pl.program_id(0) == 0)
        def _():
            dstate[...] = jnp.zeros_like(dstate)

        prim = (st_ref[0], _dn_heads(qkv_ref, 0, hh), _dn_heads(qkv_ref, width, hh),
                _dn_heads(qkv_ref, 2 * width, hh), gc_ref[0], gr_ref[0], gl_ref[0], bc_ref[0])
        _, vjp = jax.vjp(_dn_chunk, *prim)
        ds, dq, dk, dv, dgc, dgr, dgl, dbc = vjp((dstate[...], _dn_heads(do_ref, 0, hh)))
        dstate[...] = ds
        for h in range(hh):
            dqkv_ref[:, h * d:(h + 1) * d] = dq[h]
            dqkv_ref[:, width + h * d: width + (h + 1) * d] = dk[h]
            dqkv_ref[:, 2 * width + h * d: 2 * width + (h + 1) * d] = dv[h]
        dgc_ref[0] = dgc
        dgr_ref[0] = dgr
        dgl_ref[0] = dgl
        dbc_ref[0] = dbc

    r2 = lambda i: (n - 1 - i, 0)
    r4 = lambda i: (n - 1 - i, 0, 0, 0)
    spec_c = pl.BlockSpec((1, hh, CHUNK, 1), r4)
    spec_r = pl.BlockSpec((1, hh, 1, CHUNK), r4)
    spec_l = pl.BlockSpec((1, hh, 1, 1), r4)
    return _pcall(
        body, name=name + "_bwd",
        out_shape=[jax.ShapeDtypeStruct(qkv.shape, F32), jax.ShapeDtypeStruct(gc.shape, F32),
                   jax.ShapeDtypeStruct(gr.shape, F32), jax.ShapeDtypeStruct(gl.shape, F32),
                   jax.ShapeDtypeStruct(bc.shape, F32)],
        grid=(n,),
        in_specs=[pl.BlockSpec((CHUNK, w3), r2), spec_c, spec_r, spec_l, spec_c,
                  pl.BlockSpec((1, hh, d, d), r4), pl.BlockSpec((CHUNK, width), r2)],
        out_specs=[pl.BlockSpec((CHUNK, w3), r2), spec_c, spec_r, spec_l, spec_c],
        scratch=[pltpu.VMEM((hh, d, d), F32)], sem=("arbitrary",))(qkv, gc, gr, gl, bc, states, do)


@functools.partial(jax.custom_vjp, nondiff_argnums=(5,))
def dn_core(qkv, gc, gr, gl, bc, name):
    return _dn_fwd_call(qkv, gc, gr, gl, bc, name)[0]


def _dn_core_fwd(qkv, gc, gr, gl, bc, name):
    o, states = _dn_fwd_call(qkv, gc, gr, gl, bc, name)
    return o, (qkv, gc, gr, gl, bc, states)


def _dn_core_bwd(name, saved, do):
    return tuple(_dn_bwd_call(*saved, do, name))


dn_core.defvjp(_dn_core_fwd, _dn_core_bwd)


def dn_out_gate(o, z, out_norm, name):
    hh = o.shape[1] // DN_HEAD_DIM
    gain = jnp.tile(out_norm.reshape(1, -1), (1, hh))

    def fn(ot, zt, gt):
        y = ot * lax.rsqrt(jnp.mean(ot * ot, axis=-1, keepdims=True) + EPS) * gt
        return (y * _silu(zt),)

    return tilewise(fn, name, [o, z, gain], ["tile", "tile", "tile"], [(o.shape[0], BF16)],
                    axis=1, t=DN_HEAD_DIM, diff=[True, True, True])[0]


def _attn_tile(qn, kn, v, qp, kp, q0, scale, causal):
    s = _nt(qn, kn)
    if qp is not None:
        s = s + _nt(qp, kp)
    s = s * scale
    if causal:
        qpos = q0 + lax.broadcasted_iota(jnp.int32, s.shape, 0)
        kpos = lax.broadcasted_iota(jnp.int32, s.shape, 1)
        s = jnp.where((kpos >> _CHUNK_SHIFT) <= (qpos >> _CHUNK_SHIFT), s, -1e30)
    e = jnp.exp(s - jnp.max(s, axis=-1, keepdims=True))
    p = e / jnp.sum(e, axis=-1, keepdims=True)
    return _nn(p, v)


def _attn_specs(sq, sk, dh, dp, tb):
    q_spec = pl.BlockSpec((tb, dh), lambda h, i: (i, h))
    kv_spec = pl.BlockSpec((sk, dh), lambda h, i: (0, h))
    qp_spec = pl.BlockSpec((1, tb, dp), lambda h, i: (h, i, 0)) if dp else None
    kp_spec = pl.BlockSpec((sk, dp), lambda h, i: (0, 0)) if dp else None
    return q_spec, kv_spec, qp_spec, kp_spec


def _attn_plan(sq, sk, tq, causal):
    tb = sq if causal else tq
    subs = [(slice(i * tq, (i + 1) * tq), (i + 1) * tq if causal else sk) for i in range(tb // tq)]
    return tb, subs


def _attn_fwd_call(q, k, v, qp, kp, *, dh, scale, causal, tq, name):
    sq, sk = q.shape[0], k.shape[0]
    heads = q.shape[1] // dh
    dp = qp.shape[-1] if qp is not None else 0
    tb, subs = _attn_plan(sq, sk, tq, causal)
    q_spec, kv_spec, qp_spec, kp_spec = _attn_specs(sq, sk, dh, dp, tb)

    def body(*refs):
        q_ref, k_ref, v_ref = refs[:3]
        qp_ref, kp_ref = (refs[3], refs[4]) if dp else (None, None)
        o_ref = refs[-1]
        row0 = pl.program_id(1) * tb
        for rows, ke in subs:
            o = _attn_tile(q_ref[rows, :], k_ref[:ke, :], v_ref[:ke, :], qp_ref[0, rows, :] if dp else None,
                           kp_ref[:ke, :] if dp else None, row0 + rows.start, scale, causal)
            o_ref[rows, :] = o.astype(o_ref.dtype)

    in_specs = [q_spec, kv_spec, kv_spec] + ([qp_spec, kp_spec] if dp else [])
    args = (q, k, v) + ((qp, kp) if dp else ())
    return _pcall(body, name=name + "_fwd", out_shape=jax.ShapeDtypeStruct((sq, heads * dh), BF16),
                  grid=(heads, sq // tb), in_specs=in_specs, out_specs=q_spec,
                  sem=("parallel", "parallel"))(*args)


def _attn_bwd_call(q, k, v, qp, kp, do, *, dh, scale, causal, tq, name):
    sq, sk = q.shape[0], k.shape[0]
    heads = q.shape[1] // dh
    dp = qp.shape[-1] if qp is not None else 0
    tb, subs = _attn_plan(sq, sk, tq, causal)
    q_spec, kv_spec, qp_spec, kp_spec = _attn_specs(sq, sk, dh, dp, tb)

    def body(*refs):
        h, i = pl.program_id(0), pl.program_id(1)
        if dp:
            q_ref, k_ref, v_ref, qp_ref, kp_ref, do_ref, dq_ref, dk_ref, dv_ref, dqp_ref, dkp_ref = refs
        else:
            q_ref, k_ref, v_ref, do_ref, dq_ref, dk_ref, dv_ref = refs

        @pl.when(i == 0)
        def _():
            dk_ref[...] = jnp.zeros_like(dk_ref)
            dv_ref[...] = jnp.zeros_like(dv_ref)

        if dp:
            @pl.when(jnp.logical_and(h == 0, i == 0))
            def _():
                dkp_ref[...] = jnp.zeros_like(dkp_ref)

        for rows, ke in subs:
            q0 = i * tb + rows.start
            if dp:
                prim = (q_ref[rows, :], k_ref[:ke, :], v_ref[:ke, :], qp_ref[0, rows, :], kp_ref[:ke, :])
                f = lambda a, b, c, d, e, q0=q0: _attn_tile(a, b, c, d, e, q0, scale, causal)
            else:
                prim = (q_ref[rows, :], k_ref[:ke, :], v_ref[:ke, :])
                f = lambda a, b, c, q0=q0: _attn_tile(a, b, c, None, None, q0, scale, causal)
            _, vjp = jax.vjp(f, *prim)
            grads = vjp(do_ref[rows, :].astype(F32))
            dq_ref[rows, :] = grads[0]
            dk_ref[:ke, :] += grads[1]
            dv_ref[:ke, :] += grads[2]
            if dp:
                dqp_ref[0, rows, :] = grads[3]
                dkp_ref[:ke, :] += grads[4]

    in_specs = [q_spec, kv_spec, kv_spec] + ([qp_spec, kp_spec] if dp else []) + [q_spec]
    out_shape = [jax.ShapeDtypeStruct(q.shape, F32), jax.ShapeDtypeStruct(k.shape, F32),
                 jax.ShapeDtypeStruct(v.shape, F32)]
    out_specs = [q_spec, kv_spec, kv_spec]
    if dp:
        out_shape += [jax.ShapeDtypeStruct(qp.shape, F32), jax.ShapeDtypeStruct(kp.shape, F32)]
        out_specs += [qp_spec, kp_spec]
    args = (q, k, v) + ((qp, kp) if dp else ()) + (do,)
    return _pcall(body, name=name + "_bwd", out_shape=out_shape, grid=(heads, sq // tb), in_specs=in_specs,
                  out_specs=out_specs, sem=("arbitrary", "arbitrary"))(*args)


def attention(q, k, v, qp=None, kp=None, *, dh, scale, causal, name, tq=256):
    tq = _tile(q.shape[0], tq)
    kw = dict(dh=dh, scale=scale, causal=causal, tq=tq, name=name)
    has_pe = qp is not None

    @jax.custom_vjp
    def op(*xs):
        return _attn_fwd_call(*xs, **kw) if has_pe else _attn_fwd_call(*xs, None, None, **kw)

    def op_fwd(*xs):
        return (_attn_fwd_call(*xs, **kw) if has_pe else _attn_fwd_call(*xs, None, None, **kw)), xs

    def op_bwd(xs, do):
        full = xs if has_pe else xs + (None, None)
        return tuple(_attn_bwd_call(*full, do, **kw))

    op.defvjp(op_fwd, op_bwd)
    return op(q, k, v, qp, kp) if has_pe else op(q, k, v)


def _rope_tables(positions, reps):
    half = MLA_ROPE // 2
    inv = ROPE_BASE ** (-jnp.arange(0, MLA_ROPE, 2, dtype=F32) / MLA_ROPE)
    ang = positions.astype(F32)[:, None] * inv
    cos, sin = jnp.cos(ang), jnp.sin(ang)
    c = jnp.tile(jnp.concatenate([cos, cos], axis=-1), (1, reps))
    s = jnp.tile(jnp.concatenate([sin, sin], axis=-1), (1, reps))
    rot = np.zeros((MLA_ROPE, MLA_ROPE), np.float32)
    for i in range(half):
        rot[i + half, i] = -1.0
        rot[i, i + half] = 1.0
    return c, s, jnp.asarray(np.kron(np.eye(reps, dtype=np.float32), rot))


def rope(x, positions, name):
    c, s, rot = _rope_tables(positions, x.shape[1] // MLA_ROPE)

    def fn(xt, ct, st, rt):
        return (xt * ct + _nn_hi_const(xt, rt) * st,)

    return tilewise(fn, name, [x, c, s, rot], ["tile", "tile", "tile", "whole"], [(x.shape[1], F32)],
                    axis=0, t=_tile(x.shape[0], 256), diff=[True, False, False, False])[0]


def ffn_gate(pre_g, pre_u, conv_g, conv_u, bias_g, bias_u, name):
    def fn(g, u, wg, wu, bg, bu):
        return (_silu(_causal_dwconv_tile(g, wg) + bg) * (_causal_dwconv_tile(u, wu) + bu),)

    return tilewise(fn, name, [pre_g, pre_u, conv_g, conv_u, bias_g, bias_u], ["tile"] * 6,
                    [(pre_g.shape[0], BF16)], axis=1, t=_tile(pre_g.shape[1], 256), diff=[True] * 6)[0]


def loss_rows(h, gain, target, name):
    def fn(ht, tt, gt):
        err = _rms_tile(ht, gt, F32) - tt
        return (0.5 * jnp.mean(err * err, axis=-1, keepdims=True),)

    return tilewise(fn, name, [h, target, gain.reshape(1, -1)], ["tile", "tile", "whole"], [(1, F32)],
                    axis=0, t=_tile(h.shape[0], 256), diff=[True, False, True])[0]


def _col_groups(w, per_head, lo, hi):
    r = w.shape[0]
    return w.reshape(r, -1, per_head)[:, :, lo:hi].reshape(r, -1)


def _layer_mix(h, lw, rp, l, positions):
    s, d = h.shape
    width = d // 2
    dn_heads = width // DN_HEAD_DIM
    mla_heads = (d - width) // MLA_V
    nm = f"l{l}_"
    w_in = lw["w_in"]
    c0 = 4 * width
    c1 = c0 + 2 * dn_heads
    c2 = c1 + MLA_Q_RANK
    c3 = c2 + MLA_KV_RANK + MLA_ROPE
    rest_pad = (-(c3 - c0)) % LANE
    w_rest = jnp.pad(w_in[:, c0:c3], ((0, 0), (0, rest_pad)))

    u = rms_norm(h, rp["norm_mix"], nm + "norm_mix", BF16)
    qkv_raw = linear(u, w_in[:, :3 * width], nm + "in_qkv")
    z = linear(u, w_in[:, 3 * width:c0], nm + "in_z")
    rest = linear(u, w_rest, nm + "in_rest")

    qkv = dn_qkv(qkv_raw, lw["dn_conv"], nm + "dn_qkv")
    csum, beta = dn_gates(rest[:, :dn_heads], rest[:, dn_heads:2 * dn_heads], rp["dn_a_log"],
                          rp["dn_dt_bias"], nm + "dn_gates")
    n = s // CHUNK
    g3 = csum.reshape(n, CHUNK, dn_heads).transpose(0, 2, 1)
    b3 = beta.reshape(n, CHUNK, dn_heads).transpose(0, 2, 1)
    o_dn = dn_core(qkv, g3[..., None], g3[:, :, None, :], g3[:, :, CHUNK - 1][..., None, None], b3[..., None],
                   nm + "dn_core")
    o_dn = dn_out_gate(o_dn, z, rp["dn_out_norm"], nm + "dn_gate")

    mq = rest[:, c1 - c0:c2 - c0]
    mkv = rest[:, c2 - c0:c3 - c0]
    qn = rms_norm(mq, rp["mla_q_norm"], nm + "mla_qnorm", BF16)
    per_q = MLA_NOPE + MLA_ROPE
    q_nope = linear(qn, _col_groups(lw["mla_w_qb"], per_q, 0, MLA_NOPE), nm + "mla_qn")
    q_pe = linear(qn, _col_groups(lw["mla_w_qb"], per_q, MLA_NOPE, per_q), nm + "mla_qp")
    kvn = rms_norm(mkv[:, :MLA_KV_RANK], rp["mla_kv_norm"], nm + "mla_kvnorm", BF16)
    per_kv = MLA_NOPE + MLA_V
    k_nope = linear(kvn, _col_groups(lw["mla_w_kvb"], per_kv, 0, MLA_NOPE), nm + "mla_kn")
    v_mla = linear(kvn, _col_groups(lw["mla_w_kvb"], per_kv, MLA_NOPE, per_kv), nm + "mla_v")
    q_pe = rope(q_pe, positions, nm + "rope_q")
    k_pe = rope(mkv[:, MLA_KV_RANK:], positions, nm + "rope_k")
    q_pe = q_pe.reshape(s, mla_heads, MLA_ROPE).transpose(1, 0, 2)
    o_mla = attention(q_nope, k_nope, v_mla, q_pe, k_pe, dh=MLA_NOPE, scale=per_q ** -0.5, causal=True,
                      name=nm + "mla_attn")

    h = linear(o_dn, lw["w_out"][:width], nm + "out_dn", res=h)
    return linear(o_mla, lw["w_out"][width:], nm + "out_mla", res=h)


def _layer_tail(h, lw, rp, l, mem_n):
    d = h.shape[1]
    nm = f"l{l}_"
    hn = rms_norm(h, rp["norm_xattn"], nm + "norm_xattn", BF16)
    xq = linear(hn, lw["xa_wq"], nm + "xa_q")
    xk = linear(mem_n, lw["xa_wk"], nm + "xa_k")
    xv = linear(mem_n, lw["xa_wv"], nm + "xa_v")
    xdh = d // XA_HEADS
    xo = attention(xq, xk, xv, dh=xdh, scale=xdh ** -0.5, causal=False, name=nm + "xattn")
    h = linear(xo, lw["xa_wo"], nm + "xa_o", res=h)

    hn = rms_norm(h, rp["norm_ffn"], nm + "norm_ffn", BF16)
    w_up = lw["ffn_w_up"]
    d_ff = w_up.shape[1] // 2
    pre_g = linear(hn, w_up[:, :d_ff], nm + "ffn_upg")
    pre_u = linear(hn, w_up[:, d_ff:], nm + "ffn_upu")
    bias = rp["ffn_conv_bias"].reshape(1, -1)
    conv = lw["ffn_conv"]
    act = ffn_gate(pre_g, pre_u, conv[:, :d_ff], conv[:, d_ff:], bias[:, :d_ff], bias[:, d_ff:], nm + "ffn_gate")
    return linear(act, lw["ffn_w_down"], nm + "ffn_down", res=h)


_COL_SHARDED = ("w_in", "mla_w_qb", "mla_w_kvb", "ffn_w_up", "dn_conv", "ffn_conv")
_ROW_SHARDED = ("w_out", "xa_wq", "xa_wk", "xa_wv", "xa_wo", "ffn_w_down")
_BIG = ("w_in", "mla_w_qb", "mla_w_kvb", "w_out", "xa_wq", "xa_wk", "xa_wv", "xa_wo", "ffn_w_up", "ffn_w_down")
_MIX_KEYS = ("w_in", "mla_w_qb", "mla_w_kvb", "w_out")
_TAIL_KEYS = ("xa_wq", "xa_wk", "xa_wv", "xa_wo", "ffn_w_up", "ffn_w_down")
_SMALL_SHARDED = ("dn_conv", "ffn_conv")
_REP_MIX = ("norm_mix", "dn_a_log", "dn_dt_bias", "dn_out_norm", "mla_q_norm", "mla_kv_norm")
_REP_TAIL = ("norm_xattn", "norm_ffn", "ffn_conv_bias")
_REPLICATED = ("norm_mix", "dn_a_log", "dn_dt_bias", "dn_out_norm", "mla_q_norm", "mla_kv_norm", "mem_norm",
               "norm_xattn", "norm_ffn", "ffn_conv_bias", "norm_final")
_WEIGHTS = ("norm_mix", "w_in", "dn_conv", "dn_a_log", "dn_dt_bias", "dn_out_norm", "mla_q_norm", "mla_w_qb",
            "mla_kv_norm", "mla_w_kvb", "w_out", "mem_norm", "norm_xattn", "xa_wq", "xa_wk", "xa_wv", "xa_wo",
            "norm_ffn", "ffn_w_up", "ffn_conv", "ffn_conv_bias", "ffn_w_down", "norm_final")


def _assemble(name, g):
    if name in _ROW_SHARDED:
        return g.reshape((-1,) + g.shape[2:])
    return jnp.moveaxis(g, 0, -2).reshape(g.shape[1:-1] + (-1,))


def _mix(h, gw, conv, rp, l, positions):
    lw = {k: _assemble(k, v) for k, v in gw.items()}
    lw["dn_conv"] = _assemble("dn_conv", conv)
    return _layer_mix(h, lw, rp, l, positions)


def _tail(h, gw, conv, rp, l, mem_n):
    lw = {k: _assemble(k, v) for k, v in gw.items()}
    lw["ffn_conv"] = _assemble("ffn_conv", conv)
    return _layer_tail(h, lw, rp, l, mem_n)


def _my_index():
    return 4 * lax.axis_index("x") + 2 * lax.axis_index("y") + lax.axis_index("c")


def _peer(k):
    x, y, c = lax.axis_index("x"), lax.axis_index("y"), lax.axis_index("c")
    px = (1 - x) if k & 4 else x
    py = (1 - y) if k & 2 else y
    pc = (1 - c) if k & 1 else c
    return (px, py, pc), 4 * px + 2 * py + pc


_ANY = pl.BlockSpec(memory_space=pl.ANY)


def all_gather(shards, name):
    n = len(shards)

    def body(*refs):
        x_refs, o_refs = refs[:n], refs[n:2 * n]
        send_sems, recv_sems, local_sems = refs[2 * n:]
        me = _my_index()
        sib_id, sib = _peer(1)
        chips = [_peer(k) for k in (4, 2, 6)]

        def copy(a, k, block, to, src=None):
            return pltpu.make_async_remote_copy(
                src_ref=o_refs[a].at[block] if src is None else src, dst_ref=o_refs[a].at[block],
                send_sem=send_sems.at[a * 7 + k], recv_sem=recv_sems.at[a * 7 + k],
                device_id=to, device_id_type=MESH)

        mine = [pltpu.make_async_copy(x_refs[a], o_refs[a].at[me], local_sems.at[a]) for a in range(n)]
        for cp in mine:
            cp.start()
        first = []
        for a in range(n):
            first.append(copy(a, 0, me, sib_id, src=x_refs[a]))
            first += [copy(a, 1 + j, me, cid, src=x_refs[a]) for j, (cid, _) in enumerate(chips)]
        for cp in first:
            cp.start()
        passed = []
        for a in range(n):
            for j, (cid, cidx) in enumerate(chips):
                copy(a, 1 + j, cidx, cid).wait_recv()
                fwd = copy(a, 4 + j, cidx, sib_id)
                fwd.start()
                passed.append(fwd)
        for a in range(n):
            copy(a, 0, sib, sib_id).wait_recv()
            for j, (_, cidx) in enumerate(chips):
                copy(a, 4 + j, cidx ^ 1, sib_id).wait_recv()
        for cp in first + passed:
            cp.wait_send()
        for cp in mine:
            cp.wait()

    out_shape = [jax.ShapeDtypeStruct((N_DEV,) + s.shape, s.dtype) for s in shards]
    return _pcall(body, name=name, out_shape=out_shape, in_specs=[_ANY] * n, out_specs=[_ANY] * n,
                  scratch=[pltpu.SemaphoreType.DMA((7 * n,)), pltpu.SemaphoreType.DMA((7 * n,)),
                           pltpu.SemaphoreType.DMA((n,))])(*shards)


_HBM = pl.BlockSpec(memory_space=pltpu.HBM)
_SEM = pl.BlockSpec(memory_space=pltpu.SEMAPHORE)
_EFFECT = pltpu.SideEffectType.DATAFLOW_SIDE_EFFECTING


def _plan_gather_first(n):
    def plan(refs):
        me = _my_index()
        sib_id, sib = _peer(1)
        chips = [_peer(k) for k in (4, 2, 6)]
        out = []
        for a in range(n):
            x, land = refs[a], refs[n + a]
            out.append((x, land.at[me], sib_id, land.at[sib]))
            out += [(x, land.at[me], cid, land.at[cidx]) for cid, cidx in chips]
        return out

    return plan, 4 * n


def _plan_gather_pass(n):
    def plan(refs):
        sib_id, _ = _peer(1)
        chips = [_peer(k) for k in (4, 2, 6)]
        return [(refs[a].at[cidx], refs[a].at[cidx], sib_id, refs[a].at[cidx ^ 1])
                for a in range(n) for _, cidx in chips]

    return plan, 3 * n


def _plan_gather_direct(n):
    def plan(refs):
        me = _my_index()
        out = []
        for a in range(n):
            for k in range(1, N_DEV):
                pid, pidx = _peer(k)
                out.append((refs[a], refs[n + a].at[me], pid, refs[n + a].at[pidx]))
        return out

    return plan, 7 * n


def _plan_scatter(n):
    def plan(refs):
        me = _my_index()
        out = []
        for a in range(n):
            for k in range(1, N_DEV):
                pid, pidx = _peer(k)
                out.append((refs[a].at[pidx], refs[n + a].at[me], pid, refs[n + a].at[pidx]))
        return out

    return plan, 7 * n


def _remote_copy(src, dst, send_sems, recv_sems, i, dev):
    return pltpu.make_async_remote_copy(src_ref=src, dst_ref=dst, send_sem=send_sems.at[i], recv_sem=recv_sems.at[i],
                                        device_id=dev, device_id_type=MESH)


def _after(x, tokens):
    for token in tokens:
        x = x + token[0, 0]
    return x


def exchange_start(bufs, plan_n, name, after=None):
    plan, n = plan_n
    nb = len(bufs)
    n_in = nb + (after is not None)

    def body(*refs):
        send_sems, recv_sems = refs[n_in], refs[n_in + 1]
        for i, (src, dst, dev, _) in enumerate(plan(refs[:nb])):
            _remote_copy(src, dst, send_sems, recv_sems, i, dev).start()
        refs[-1][...] = jnp.zeros_like(refs[-1])

    out_shape = (pltpu.SemaphoreType.DMA((n,)), pltpu.SemaphoreType.DMA((n,)),
                 *[pltpu.HBM(b.shape, b.dtype) for b in bufs], jax.ShapeDtypeStruct((8, LANE), F32))
    args = [pltpu.with_memory_space_constraint(b, pltpu.HBM) for b in bufs] + ([after] if after is not None else [])
    res = pl.pallas_call(
        body, name=name, out_shape=out_shape,
        in_specs=[_HBM] * nb + ([_ANY] if after is not None else []),
        out_specs=(_SEM, _SEM, *[_HBM] * nb, pl.BlockSpec(memory_space=pltpu.VMEM)),
        input_output_aliases={i: 2 + i for i in range(nb)},
        compiler_params=pltpu.CompilerParams(has_side_effects=_EFFECT), interpret=_INTERPRET)(*args)
    return (res[0], res[1], list(res[2:2 + nb])), res[-1]


def exchange_wait(flight, plan_n, name, after):
    plan, _ = plan_n
    send_sems, recv_sems, bufs = flight
    nb = len(bufs)

    def body(*refs):
        s_sems, r_sems = refs[nb], refs[nb + 1]
        for i, (src, _, dev, arrival) in enumerate(plan(refs[:nb])):
            cp = _remote_copy(src, arrival, s_sems, r_sems, i, dev)
            cp.wait_send()
            cp.wait_recv()

    return list(pl.pallas_call(
        body, name=name, out_shape=tuple(pltpu.HBM(b.shape, b.dtype) for b in bufs),
        in_specs=[_HBM] * nb + [_SEM, _SEM, _ANY], out_specs=tuple([_HBM] * nb),
        input_output_aliases={i: i for i in range(nb)},
        compiler_params=pltpu.CompilerParams(has_side_effects=_EFFECT), interpret=_INTERPRET)(
            *bufs, send_sems, recv_sems, after))


def _adam_tile(contribs, w, m, v):
    g = contribs[0].astype(F32)
    for part in contribs[1:]:
        g = g + part.astype(F32)
    m_new = ADAM_B1 * m + (1.0 - ADAM_B1) * g
    v_new = ADAM_B2 * v + (1.0 - ADAM_B2) * (g * g)
    m_hat = m_new / (1.0 - ADAM_B1 ** ADAM_STEP)
    v_hat = v_new / (1.0 - ADAM_B2 ** ADAM_STEP)
    return g, -ADAM_LR * (m_hat / (jnp.sqrt(v_hat) + ADAM_EPS) + ADAM_WD * w), m_new, v_new


def adam_update(recv, w, m, v, name):
    ll, _, r, c = recv.shape
    tr = _tile(r, max(8, 1 << ((_ADAM_TILE_ELEMS // c).bit_length() - 1)))

    def body(g_ref, w_ref, m_ref, v_ref, go_ref, d_ref, mo_ref, vo_ref):
        go_ref[0], d_ref[0], mo_ref[0], vo_ref[0] = _adam_tile([g_ref[0, j] for j in range(N_DEV)], w_ref[0],
                                                               m_ref[0], v_ref[0])

    spec = pl.BlockSpec((1, tr, c), lambda l, i: (l, i, 0))
    sds = jax.ShapeDtypeStruct((ll, r, c), F32)
    return _pcall(body, name=name, out_shape=[sds] * 4, grid=(ll, r // tr),
                  in_specs=[pl.BlockSpec((1, N_DEV, tr, c), lambda l, i: (l, 0, i, 0)), spec, spec, spec],
                  out_specs=[spec] * 4, sem=("parallel", "parallel"))(recv, w, m, v)


def adam_layer(recv, w, m, v, prev, l, name):
    _, r, c = recv.shape
    tr = _tile(r, max(8, 1 << ((_ADAM_TILE_ELEMS // c).bit_length() - 1)))
    if prev is None:
        prev = [lax.empty(w.shape, F32) for _ in range(4)]

    def body(g_ref, w_ref, m_ref, v_ref, *rest):
        go_ref, d_ref, mo_ref, vo_ref = rest[4:]
        go_ref[0], d_ref[0], mo_ref[0], vo_ref[0] = _adam_tile([g_ref[j] for j in range(N_DEV)], w_ref[0], m_ref[0],
                                                               v_ref[0])

    spec = pl.BlockSpec((1, tr, c), lambda i: (l, i, 0))
    return _pcall(body, name=name, out_shape=[jax.ShapeDtypeStruct(w.shape, F32)] * 4, grid=(r // tr,),
                  in_specs=[pl.BlockSpec((N_DEV, tr, c), lambda i: (0, i, 0)), spec, spec, spec] + [_ANY] * 4,
                  out_specs=[spec] * 4, sem=("parallel",),
                  input_output_aliases={4 + j: j for j in range(4)})(recv, w, m, v, *prev)


def _as3d(a):
    if a.ndim == 1:
        return a.reshape(1, 1, -1)
    if a.ndim == 2:
        return a.reshape(1, a.shape[0], a.shape[1])
    return a.reshape(a.shape[0], -1, a.shape[-1])


def kernel(x, mem, positions, norm_mix, w_in, dn_conv, dn_a_log, dn_dt_bias, dn_out_norm, mla_q_norm, mla_w_qb, mla_kv_norm, mla_w_kvb, w_out, mem_norm, norm_xattn, xa_wq, xa_wk, xa_wv, xa_wo, norm_ffn, ffn_w_up, ffn_conv, ffn_conv_bias, ffn_w_down, norm_final, loss_target, m_norm_mix, m_w_in, m_dn_conv, m_dn_a_log, m_dn_dt_bias, m_dn_out_norm, m_mla_q_norm, m_mla_w_qb, m_mla_kv_norm, m_mla_w_kvb, m_w_out, m_mem_norm, m_norm_xattn, m_xa_wq, m_xa_wk, m_xa_wv, m_xa_wo, m_norm_ffn, m_ffn_w_up, m_ffn_conv, m_ffn_conv_bias, m_ffn_w_down, m_norm_final, v_norm_mix, v_w_in, v_dn_conv, v_dn_a_log, v_dn_dt_bias, v_dn_out_norm, v_mla_q_norm, v_mla_w_qb, v_mla_kv_norm, v_mla_w_kvb, v_w_out, v_mem_norm, v_norm_xattn, v_xa_wq, v_xa_wk, v_xa_wv, v_xa_wo, v_norm_ffn, v_ffn_w_up, v_ffn_conv, v_ffn_conv_bias, v_ffn_w_down, v_norm_final):
    w = dict(norm_mix=norm_mix, w_in=w_in, dn_conv=dn_conv, dn_a_log=dn_a_log, dn_dt_bias=dn_dt_bias,
             dn_out_norm=dn_out_norm, mla_q_norm=mla_q_norm, mla_w_qb=mla_w_qb, mla_kv_norm=mla_kv_norm,
             mla_w_kvb=mla_w_kvb, w_out=w_out, mem_norm=mem_norm, norm_xattn=norm_xattn, xa_wq=xa_wq, xa_wk=xa_wk,
             xa_wv=xa_wv, xa_wo=xa_wo, norm_ffn=norm_ffn, ffn_w_up=ffn_w_up, ffn_conv=ffn_conv,
             ffn_conv_bias=ffn_conv_bias, ffn_w_down=ffn_w_down, norm_final=norm_final)
    mom = dict(norm_mix=m_norm_mix, w_in=m_w_in, dn_conv=m_dn_conv, dn_a_log=m_dn_a_log, dn_dt_bias=m_dn_dt_bias,
               dn_out_norm=m_dn_out_norm, mla_q_norm=m_mla_q_norm, mla_w_qb=m_mla_w_qb, mla_kv_norm=m_mla_kv_norm,
               mla_w_kvb=m_mla_w_kvb, w_out=m_w_out, mem_norm=m_mem_norm, norm_xattn=m_norm_xattn, xa_wq=m_xa_wq,
               xa_wk=m_xa_wk, xa_wv=m_xa_wv, xa_wo=m_xa_wo, norm_ffn=m_norm_ffn, ffn_w_up=m_ffn_w_up,
               ffn_conv=m_ffn_conv, ffn_conv_bias=m_ffn_conv_bias, ffn_w_down=m_ffn_w_down, norm_final=m_norm_final)
    var = dict(norm_mix=v_norm_mix, w_in=v_w_in, dn_conv=v_dn_conv, dn_a_log=v_dn_a_log, dn_dt_bias=v_dn_dt_bias,
               dn_out_norm=v_dn_out_norm, mla_q_norm=v_mla_q_norm, mla_w_qb=v_mla_w_qb, mla_kv_norm=v_mla_kv_norm,
               mla_w_kvb=v_mla_w_kvb, w_out=v_w_out, mem_norm=v_mem_norm, norm_xattn=v_norm_xattn, xa_wq=v_xa_wq,
               xa_wk=v_xa_wk, xa_wv=v_xa_wv, xa_wo=v_xa_wo, norm_ffn=v_norm_ffn, ffn_w_up=v_ffn_w_up,
               ffn_conv=v_ffn_conv, ffn_conv_bias=v_ffn_conv_bias, ffn_w_down=v_ffn_w_down, norm_final=v_norm_final)
    depth = w_in.shape[0]
    me = _my_index()
    rep = {k: w[k] for k in _REPLICATED}
    x0, mem0, pos0, target0 = x[0], mem[0], positions[0], loss_target[0]

    def with_own_block(block):
        return lax.dynamic_update_index_in_dim(lax.empty((N_DEV,) + block.shape, block.dtype), block, me, 0)

    def gather_start(l, keys, tag, after):
        shards = [w[k][l].astype(BF16) for k in keys]
        plan = _plan_gather_first(len(keys))
        flight, token = exchange_start(shards + [with_own_block(sh) for sh in shards], plan,
                                       f"gather_l{l}{tag}_first_start", after)
        return (flight, plan, keys, f"gather_l{l}{tag}"), token

    def gather_pass(state, after):
        flight, plan, keys, name = state
        lands = exchange_wait(flight, plan, name + "_first_wait", after)[len(keys):]
        plan = _plan_gather_pass(len(keys))
        flight, token = exchange_start(lands, plan, name + "_pass_start")
        return (flight, plan, keys, name), token

    def gather_finish(state, after):
        flight, plan, keys, name = state
        return dict(zip(keys, exchange_wait(flight, plan, name + "_pass_wait", after)))

    small = dict(zip(_SMALL_SHARDED, all_gather([w[k] for k in _SMALL_SHARDED], "gather_small")))
    mem_n, vjp_mem = jax.vjp(lambda g: rms_norm(mem0, g, "mem_norm", BF16), rep["mem_norm"])

    groups = [(l, keys, tag) for l in range(depth) for keys, tag in ((_MIX_KEYS, "_mix"), (_TAIL_KEYS, "_tail"))]
    n_stage = len(groups)
    states, ready = {}, {}
    token = small["dn_conv"]
    for s in range(min(3, n_stage)):
        states[s], token = gather_start(*groups[s], token)
    states[0], token = gather_pass(states[0], token)
    ready[0] = gather_finish(states[0], token)

    h = x0
    tapes = []
    for s in range(n_stage):
        l = s // 2
        tokens = []
        if s >= 2 and s + 1 < n_stage:
            states[s + 1], token = gather_pass(states[s + 1], h)
            tokens.append(token)
        if s + 3 < n_stage:
            states[s + 3], token = gather_start(*groups[s + 3], h)
            tokens.append(token)
        if s % 2 == 0:
            rp = {k: rep[k][l] for k in _REP_MIX}
            rp["norm_mix"] = _after(rp["norm_mix"], tokens)
            h_new, tape = jax.vjp(lambda hh, gw, cv, rr: _mix(hh, gw, cv, rr, l, pos0), h, ready.pop(s),
                                  small["dn_conv"][:, l], rp)
        else:
            rp = {k: rep[k][l] for k in _REP_TAIL}
            rp["norm_xattn"] = _after(rp["norm_xattn"], tokens)
            h_new, tape = jax.vjp(lambda hh, gw, cv, rr, mn: _tail(hh, gw, cv, rr, l, mn), h, ready.pop(s),
                                  small["ffn_conv"][:, l], rp, mem_n)
        if s + 1 < n_stage:
            if s < 2:
                states[s + 1], token = gather_pass(states[s + 1], h_new)
                ready[s + 1] = gather_finish(states[s + 1], token)
            else:
                ready[s + 1] = gather_finish(states[s + 1], h_new)
        tapes.append(tape)
        h = h_new

    rows, vjp_loss = jax.vjp(lambda hh, g: loss_rows(hh, g, target0, "loss"), h, rep["norm_final"])
    loss = lax.psum(jnp.sum(rows), ("x", "y", "c"))
    dh, d_norm_final = vjp_loss(jnp.ones_like(rows))

    d_rep = {k: [None] * depth for k in _REP_MIX + _REP_TAIL}
    d_conv = {k: [None] * depth for k in _SMALL_SHARDED}
    recv_big = [dict() for _ in range(depth)]
    d_mem_n = None

    def scatter_start(l, grads, keys, tag):
        parts = [grads[k] for k in keys]
        own = [with_own_block(lax.dynamic_index_in_dim(p, me, 0, keepdims=False)) for p in parts]
        plan = _plan_scatter(len(keys))
        flight, token = exchange_start(parts + own, plan, f"scatter_l{l}{tag}_start")
        return (flight, plan, keys, l, f"scatter_l{l}{tag}_wait"), token

    def scatter_finish(state, after):
        flight, plan, keys, l, name = state
        recv_big[l].update(zip(keys, exchange_wait(flight, plan, name, after)[len(keys):]))

    pending = {}
    token = None
    for s in reversed(range(n_stage)):
        l, keys, tag = groups[s]
        if token is not None:
            dh = _after(dh, [token])
        if s % 2:
            dh, dg, d_conv["ffn_conv"][l], d_rp, d_mn = tapes[s](dh)
            d_mem_n = d_mn if d_mem_n is None else d_mem_n + d_mn
        else:
            dh, dg, d_conv["dn_conv"][l], d_rp = tapes[s](dh)
        for k, v in d_rp.items():
            d_rep[k][l] = v
        if s + 2 in pending:
            scatter_finish(pending.pop(s + 2), dh)
        pending[s], token = scatter_start(l, dg, keys, tag)

    (d_mem_norm,) = vjp_mem(d_mem_n)
    d_rep_all = {k: jnp.stack(v) for k, v in d_rep.items()}
    d_rep_all["mem_norm"] = d_mem_norm
    d_rep_all["norm_final"] = _after(d_norm_final, [token])
    parts = [jnp.stack(d_conv[k], axis=1) for k in _SMALL_SHARDED]
    own = [with_own_block(lax.dynamic_index_in_dim(p, me, 0, keepdims=False)) for p in parts]
    plan_small = _plan_scatter(len(parts))
    flight_small, token = exchange_start(parts + own, plan_small, "scatter_small_start")
    blocks = [_as3d(d_rep_all[k])[0] for k in _REPLICATED]
    blocks[0] = _after(blocks[0], [token])
    plan_rep = _plan_gather_direct(len(blocks))
    flight_rep, token = exchange_start(blocks + [with_own_block(b) for b in blocks], plan_rep, "gather_rep_start")

    big = {k: None for k in _BIG}

    def update_big(l):
        for k in _BIG:
            recv = recv_big[l][k]
            big[k] = adam_layer(recv.reshape((N_DEV, -1, recv.shape[-1])), _as3d(w[k]), _as3d(mom[k]),
                                _as3d(var[k]), big[k], l, f"adam_{k}_l{l}")

    for l in reversed(range(1, depth)):
        update_big(l)
    after = token if depth == 1 else big[_BIG[-1]][0]
    out = {}
    recv_small = exchange_wait(flight_small, plan_small, "scatter_small_wait", after)[len(parts):]
    for k, recv in zip(_SMALL_SHARDED, recv_small):
        r3 = _as3d(w[k])
        out[k] = adam_update(recv.reshape((1, N_DEV, -1, recv.shape[-1])), r3.reshape((1, -1, r3.shape[-1])),
                             _as3d(mom[k]).reshape((1, -1, r3.shape[-1])), _as3d(var[k]).reshape((1, -1, r3.shape[-1])),
                             "adam_" + k)
    recv_rep = exchange_wait(flight_rep, plan_rep, "gather_rep_wait", after)[len(blocks):]
    for k, recv in zip(_REPLICATED, recv_rep):
        r3 = _as3d(w[k])
        flat = lambda a, r3=r3: a.reshape((1, -1, r3.shape[-1]))
        out[k] = adam_update(recv[None], flat(w[k]), flat(mom[k]), flat(var[k]), "adam_" + k)
    for s in sorted(pending, reverse=True):
        scatter_finish(pending[s], out["norm_final"][0])
    update_big(0)
    out.update(big)

    res = [loss, dh[None]]
    for j in range(4):
        res += [out[k][j].reshape(w[k].shape) for k in _WEIGHTS]
    return tuple(res)
```

```python
import functools

import numpy as np
import jax
import jax.numpy as jnp
from jax import lax
from jax.experimental import pallas as pl
from jax.experimental.pallas import tpu as pltpu

F32 = jnp.float32
BF16 = jnp.bfloat16
_MXU_DTYPE = BF16
_INTERPRET = False
_VMEM_LIMIT_BYTES = 48 * 1024 * 1024
_MM_VMEM_BUDGET_BYTES = 36 * 1024 * 1024
_MM_TILE_CAP = 2048

N_DEV = 8
CHUNK = 64
_CHUNK_SHIFT = 6
DN_HEAD_DIM = 128
DN_CONV = 4
MLA_NOPE = 128
MLA_ROPE = 64
MLA_V = 128
MLA_Q_RANK = 512
MLA_KV_RANK = 256
ROPE_BASE = 10000.0
XA_HEADS = 4
FFN_CONV = 3
EPS = 1e-6
LANE = 128

ADAM_LR = 0.001
ADAM_B1 = 0.9
ADAM_B2 = 0.999
ADAM_EPS = 1e-08
ADAM_WD = 0.01
ADAM_STEP = 10
_ADAM_TILE_ELEMS = 128 * 1024

MESH = pl.DeviceIdType.MESH


def _pcall(body, *, name, out_shape, grid=None, in_specs=None, out_specs=None, scratch=(), sem=None, **kw):
    params = pltpu.CompilerParams(dimension_semantics=sem, vmem_limit_bytes=_VMEM_LIMIT_BYTES)
    args = dict(name=name, out_shape=out_shape, scratch_shapes=list(scratch), compiler_params=params,
                interpret=_INTERPRET, **kw)
    if grid is not None:
        args.update(grid=grid)
    if in_specs is not None:
        args.update(in_specs=in_specs)
    if out_specs is not None:
        args.update(out_specs=out_specs)
    return pl.pallas_call(body, **args)


def _tile(n, pref):
    if n <= pref:
        return n
    t = pref
    while t >= 8:
        if n % t == 0:
            return t
        t //= 2
    return n


def _ein_raw(spec, a, b, hi):
    one = lambda x, y: jnp.einsum(spec, x, y, preferred_element_type=F32)
    a_hi, b_hi = a.astype(_MXU_DTYPE), b.astype(_MXU_DTYPE)
    if not hi:
        return one(a_hi, b_hi)
    a_lo = (a.astype(F32) - a_hi.astype(F32)).astype(_MXU_DTYPE)
    b_lo = (b.astype(F32) - b_hi.astype(F32)).astype(_MXU_DTYPE)
    return one(a_hi, b_hi) + (one(a_hi, b_lo) + one(a_lo, b_hi))


def _make_ein(spec, hi=False, diff_b=True):
    a_s, rest = spec.split(",")
    b_s, o_s = rest.split("->")

    @jax.custom_vjp
    def f(a, b):
        return _ein_raw(spec, a, b, hi)

    def fwd(a, b):
        return f(a, b), (a, b)

    def bwd(res, g):
        a, b = res
        da = _ein_raw(f"{o_s},{b_s}->{a_s}", g, b, hi)
        if not diff_b:
            return da.astype(a.dtype), None
        db = _ein_raw(f"{a_s},{o_s}->{b_s}", a, g, hi)
        return da.astype(a.dtype), db.astype(b.dtype)

    f.defvjp(fwd, bwd)
    return f


_nt = _make_ein("qd,kd->qk")
_nn = _make_ein("qk,kd->qd")
_nn_hi_const = _make_ein("qk,kd->qd", hi=True, diff_b=False)
_bnt = _make_ein("hik,hjk->hij")
_bnt_hi = _make_ein("hik,hjk->hij", hi=True)
_bnn = _make_ein("hij,hjv->hiv")
_bnn_hi = _make_ein("hij,hjv->hiv", hi=True)
_btn = _make_ein("hck,hcv->hkv")


def _divisor_tiles(n, cap):
    out = [n] if n <= cap else []
    t = cap
    while t >= LANE:
        if t < n and n % t == 0:
            out.append(t)
        t //= 2
    return out or [n]


def _mm_tiles(m, n, k, a_bytes, b_bytes, o_bytes, r_bytes, accumulate=False):
    best = None
    for tk in sorted({k, *_divisor_tiles(k, _MM_TILE_CAP)}):
        for tm in _divisor_tiles(m, _MM_TILE_CAP):
            for tn in _divisor_tiles(n, _MM_TILE_CAP):
                need = 2 * (tm * tk * a_bytes + tk * tn * b_bytes + tm * tn * (o_bytes + r_bytes))
                need += tm * tn * 4 * (1 if tk == k and not accumulate else 2)
                if need > _MM_VMEM_BUDGET_BYTES:
                    continue
                score = (tm * tn * tk, tk, tm)
                if best is None or score > best[0]:
                    best = (score, (tm, tn, tk))
    assert best is not None, (m, n, k)
    return best[1]


def _mm(a, b, *, ta=False, tb=False, res=None, out_dtype=F32, name="mm"):
    m = a.shape[1] if ta else a.shape[0]
    k = a.shape[0] if ta else a.shape[1]
    n = b.shape[0] if tb else b.shape[1]
    assert (b.shape[1] if tb else b.shape[0]) == k, (a.shape, b.shape, ta, tb)
    has_res = res is not None
    tm, tn, tk = _mm_tiles(m, n, k, a.dtype.itemsize, b.dtype.itemsize, jnp.dtype(out_dtype).itemsize,
                           res.dtype.itemsize if has_res else 0)
    nk = k // tk
    body = _mm_body(nk, (((0 if ta else 1,), (1 if tb else 0,)), ((), ())), has_res, out_dtype)
    a_spec = pl.BlockSpec((tk, tm), lambda i, j, kk: (kk, i)) if ta else pl.BlockSpec((tm, tk), lambda i, j, kk: (i, kk))
    b_spec = pl.BlockSpec((tn, tk), lambda i, j, kk: (j, kk)) if tb else pl.BlockSpec((tk, tn), lambda i, j, kk: (kk, j))
    o_spec = pl.BlockSpec((tm, tn), lambda i, j, kk: (i, j))
    in_specs = [a_spec, b_spec] + ([o_spec] if has_res else [])
    args = (a, b) + ((res,) if has_res else ())
    return _pcall(body, name=name, out_shape=jax.ShapeDtypeStruct((m, n), out_dtype),
                  grid=(m // tm, n // tn, nk), in_specs=in_specs, out_specs=o_spec,
                  scratch=[pltpu.VMEM((tm, tn), F32)] if nk > 1 else [],
                  sem=("parallel", "parallel", "arbitrary"))(*args)


def _mm_body(nk, dims, has_res, out_dtype):
    def body(*refs):
        a_ref, b_ref = refs[0], refs[1]
        r_ref = refs[2] if has_res else None
        o_ref = refs[2 + has_res]

        def finish(r):
            if has_res:
                r = r + r_ref[...].astype(F32)
            o_ref[...] = r.astype(out_dtype)

        part = lax.dot_general(a_ref[...].astype(_MXU_DTYPE), b_ref[...].astype(_MXU_DTYPE), dims,
                               preferred_element_type=F32)
        if nk == 1:
            finish(part)
            return
        acc = refs[-1]
        kk = pl.program_id(2)

        @pl.when(kk == 0)
        def _():
            acc[...] = part

        @pl.when(kk > 0)
        def _():
            acc[...] += part

        @pl.when(kk == nk - 1)
        def _():
            finish(acc[...])

    return body


def _mm_groups_out(a, b3, *, ta=False, out_dtype=F32, name="mm_groups"):
    groups, k, nb = b3.shape
    m = a.shape[1] if ta else a.shape[0]
    assert (a.shape[0] if ta else a.shape[1]) == k, (a.shape, b3.shape, ta)
    tm, tn, tk = _mm_tiles(m, nb, k, a.dtype.itemsize, b3.dtype.itemsize, jnp.dtype(out_dtype).itemsize, 0)
    nk, per = k // tk, nb // tn
    body = _mm_body(nk, (((0 if ta else 1,), (0,)), ((), ())), False, out_dtype)
    a_spec = pl.BlockSpec((tk, tm), lambda j, i, kk: (kk, i)) if ta else pl.BlockSpec((tm, tk), lambda j, i, kk: (i, kk))
    b_spec = pl.BlockSpec((None, tk, tn), lambda j, i, kk: (j // per, kk, j % per))
    o_spec = pl.BlockSpec((None, tm, tn), lambda j, i, kk: (j // per, i, j % per))
    return _pcall(body, name=name, out_shape=jax.ShapeDtypeStruct((groups, m, nb), out_dtype),
                  grid=(groups * per, m // tm, nk), in_specs=[a_spec, b_spec], out_specs=o_spec,
                  scratch=[pltpu.VMEM((tm, tn), F32)] if nk > 1 else [],
                  sem=("parallel", "parallel", "arbitrary"))(a, b3)


def _mm_groups_contract(a3, b3, *, out_dtype=F32, name="mm_contract"):
    groups, m, nb = a3.shape
    n = b3.shape[1]
    assert b3.shape[0] == groups and b3.shape[2] == nb, (a3.shape, b3.shape)
    tm, tn, tk = _mm_tiles(m, n, nb, a3.dtype.itemsize, b3.dtype.itemsize, jnp.dtype(out_dtype).itemsize, 0,
                           accumulate=True)
    per = nb // tk
    nk = groups * per
    body = _mm_body(nk, (((1,), (1,)), ((), ())), False, out_dtype)
    a_spec = pl.BlockSpec((None, tm, tk), lambda i, j, kk: (kk // per, i, kk % per))
    b_spec = pl.BlockSpec((None, tn, tk), lambda i, j, kk: (kk // per, j, kk % per))
    return _pcall(body, name=name, out_shape=jax.ShapeDtypeStruct((m, n), out_dtype),
                  grid=(m // tm, n // tn, nk), in_specs=[a_spec, b_spec],
                  out_specs=pl.BlockSpec((tm, tn), lambda i, j, kk: (i, j)),
                  scratch=[pltpu.VMEM((tm, tn), F32)], sem=("parallel", "parallel", "arbitrary"))(a3, b3)


@functools.partial(jax.custom_vjp, nondiff_argnums=(3,))
def _linear_res(a, w, res, name):
    return _mm(a, w, res=res, name=name + "_fwd")


def _linear_res_fwd(a, w, res, name):
    return _mm(a, w, res=res, name=name + "_fwd"), (a, w)


def _linear_res_bwd(name, saved, g):
    a, w = saved
    gm = g.astype(_MXU_DTYPE)
    da = _mm(gm, w, tb=True, out_dtype=a.dtype, name=name + "_da")
    dw = _mm(a, gm, ta=True, out_dtype=w.dtype, name=name + "_dw")
    return da, dw, g


_linear_res.defvjp(_linear_res_fwd, _linear_res_bwd)


@functools.partial(jax.custom_vjp, nondiff_argnums=(2,))
def _linear(a, w, name):
    return _mm(a, w, name=name + "_fwd")


def _linear_fwd(a, w, name):
    return _mm(a, w, name=name + "_fwd"), (a, w)


def _linear_bwd(name, saved, g):
    a, w = saved
    gm = g.astype(_MXU_DTYPE)
    da = _mm(gm, w, tb=True, out_dtype=a.dtype, name=name + "_da")
    dw = _mm(a, gm, ta=True, out_dtype=w.dtype, name=name + "_dw")
    return da, dw


_linear.defvjp(_linear_fwd, _linear_bwd)


def linear(a, w, name, res=None):
    return _linear(a, w, name) if res is None else _linear_res(a, w, res, name)


def _tiled_specs(arrs, kinds, t, axis):
    specs = []
    for a, kind in zip(arrs, kinds):
        if kind == "whole":
            specs.append(pl.BlockSpec(a.shape, lambda i, nd=a.ndim: (0,) * nd))
        elif axis == 0:
            specs.append(pl.BlockSpec((t, a.shape[1]), lambda i: (i, 0)))
        else:
            specs.append(pl.BlockSpec((a.shape[0], t), lambda i: (0, i)))
    return specs


def tilewise(fn, name, args, kinds, outs, *, axis, t, diff):
    n_in = len(args)
    length = next(a.shape[axis] for a, kd in zip(args, kinds) if kd == "tile")
    steps = length // t
    assert steps * t == length, (name, length, t)

    def out_sds(other, dtype):
        return jax.ShapeDtypeStruct((length, other) if axis == 0 else (other, length), dtype)

    def out_spec(other):
        return pl.BlockSpec((t, other), lambda i: (i, 0)) if axis == 0 else pl.BlockSpec((other, t), lambda i: (0, i))

    def run_fwd(*xs):
        def body(*refs):
            vals = fn(*[r[...] for r in refs[:n_in]])
            for o_ref, v in zip(refs[n_in:], vals):
                o_ref[...] = v.astype(o_ref.dtype)

        return _pcall(body, name=name + "_fwd", out_shape=[out_sds(o, d) for o, d in outs], grid=(steps,),
                      in_specs=_tiled_specs(xs, kinds, t, axis), out_specs=[out_spec(o) for o, _ in outs],
                      sem=("parallel",))(*xs)

    didx = [i for i in range(n_in) if diff[i]]

    def run_bwd(xs, cts):
        def body(*refs):
            x_refs, c_refs, g_refs = refs[:n_in], refs[n_in:n_in + len(outs)], refs[n_in + len(outs):]
            vals = [r[...] for r in x_refs]

            def g(*dvals):
                full = list(vals)
                for i, v in zip(didx, dvals):
                    full[i] = v
                return tuple(fn(*full))

            prim_out, vjp = jax.vjp(g, *[vals[i] for i in didx])
            grads = vjp(tuple(c[...].astype(o.dtype) for c, o in zip(c_refs, prim_out)))
            step = pl.program_id(0)
            for i, g_ref, gr in zip(didx, g_refs, grads):
                if kinds[i] == "whole":
                    @pl.when(step == 0)
                    def _(g_ref=g_ref):
                        g_ref[...] = jnp.zeros_like(g_ref)

                    g_ref[...] += gr.astype(g_ref.dtype)
                else:
                    g_ref[...] = gr.astype(g_ref.dtype)

        g_shapes = [jax.ShapeDtypeStruct(xs[i].shape, xs[i].dtype) for i in didx]
        g_specs = _tiled_specs([xs[i] for i in didx], [kinds[i] for i in didx], t, axis)
        ct_specs = [out_spec(o) for o, _ in outs]
        any_whole = any(kinds[i] == "whole" for i in didx)
        return _pcall(body, name=name + "_bwd", out_shape=g_shapes, grid=(steps,),
                      in_specs=_tiled_specs(xs, kinds, t, axis) + ct_specs, out_specs=g_specs,
                      sem=("arbitrary" if any_whole else "parallel",))(*xs, *cts)

    @jax.custom_vjp
    def op(*xs):
        return tuple(run_fwd(*xs))

    def op_fwd(*xs):
        return tuple(run_fwd(*xs)), xs

    def op_bwd(xs, cts):
        grads = run_bwd(xs, cts)
        full = [None] * n_in
        for i, gr in zip(didx, grads):
            full[i] = gr
        return tuple(full)

    op.defvjp(op_fwd, op_bwd)
    return op(*args)


def _silu(x):
    return x * (1.0 / (1.0 + jnp.exp(-x)))


def _softplus(x):
    return jnp.maximum(x, 0.0) + jnp.log(1.0 + jnp.exp(-jnp.abs(x)))


def _rms_tile(x, gain, out_dtype):
    xf = x.astype(F32)
    y = xf * lax.rsqrt(jnp.mean(xf * xf, axis=-1, keepdims=True) + EPS)
    return (y * gain).astype(out_dtype)


def rms_norm(x, gain, name, out_dtype, tr=256):
    def fn(xt, gt):
        return (_rms_tile(xt, gt, out_dtype),)

    return tilewise(fn, name, [x, gain.reshape(1, -1)], ["tile", "whole"], [(x.shape[1], out_dtype)],
                    axis=0, t=_tile(x.shape[0], tr), diff=[True, True])[0]


def _shift_down_raw(x, s):
    rows = lax.broadcasted_iota(jnp.int32, x.shape, 0)
    return jnp.where(rows >= s, pltpu.roll(x, s, 0), 0.0)


def _shift_up_raw(x, s):
    n = x.shape[0]
    rows = lax.broadcasted_iota(jnp.int32, x.shape, 0)
    return jnp.where(rows < n - s, pltpu.roll(x, n - s, 0), 0.0)


@functools.partial(jax.custom_vjp, nondiff_argnums=(1,))
def _shift_down(x, s):
    return _shift_down_raw(x, s)


def _shift_down_fwd(x, s):
    return _shift_down_raw(x, s), None


def _shift_down_bwd(s, _, g):
    return (_shift_up_raw(g, s),)


_shift_down.defvjp(_shift_down_fwd, _shift_down_bwd)


def _causal_dwconv_tile(x, w):
    kk = w.shape[0]
    y = x * w[kk - 1:kk, :]
    for j in range(kk - 1):
        y = y + _shift_down(x, kk - 1 - j) * w[j:j + 1, :]
    return y


def dn_qkv(raw, conv_w, name):
    width = raw.shape[1] // 3
    is_qk = (jnp.arange(raw.shape[1]) < 2 * width).astype(F32).reshape(1, -1)

    def fn(x, w, flag):
        y = _silu(_causal_dwconv_tile(x, w))
        yn = y * lax.rsqrt(jnp.sum(y * y, axis=-1, keepdims=True) + EPS)
        return (jnp.where(flag > 0.5, yn, y),)

    return tilewise(fn, name, [raw, conv_w, is_qk], ["tile", "tile", "tile"], [(raw.shape[0], F32)],
                    axis=1, t=DN_HEAD_DIM, diff=[True, True, False])[0]


def dn_gates(b, a, a_log, dt_bias, name):
    hh = b.shape[1]
    tr = _tile(b.shape[0], 256)

    def fn(bt, at, al, db):
        beta = 1.0 / (1.0 + jnp.exp(-bt))
        g = -jnp.exp(al) * _softplus(at + db)
        pos = lax.broadcasted_iota(jnp.int32, g.shape, 0) & (CHUNK - 1)
        step = 1
        while step < CHUNK:
            g = g + jnp.where(pos >= step, _shift_down(g, step), 0.0)
            step *= 2
        return g, beta

    return tilewise(fn, name, [b, a, a_log.reshape(1, -1), dt_bias.reshape(1, -1)],
                    ["tile", "tile", "whole", "whole"], [(hh, F32), (hh, F32)], axis=0, t=tr, diff=[True] * 4)


def _unit_lower_inv_raw(low):
    c = low.shape[-1]
    ii = lax.broadcasted_iota(jnp.int32, (c, c), 0)
    jj = lax.broadcasted_iota(jnp.int32, (c, c), 1)
    inv = (ii == jj).astype(F32)[None] - low
    p = low
    n = 1
    while 2 * n < c:
        p = _ein_raw("hij,hjk->hik", p, p, True)
        inv = inv + _ein_raw("hij,hjk->hik", inv, p, True)
        n *= 2
    return inv


@jax.custom_vjp
def _unit_lower_inv(low):
    return _unit_lower_inv_raw(low)


def _unit_lower_inv_fwd(low):
    inv = _unit_lower_inv_raw(low)
    return inv, inv


def _unit_lower_inv_bwd(inv, g):
    t = _ein_raw("hji,hjk->hik", inv, g, True)
    return (-_ein_raw("hik,hlk->hil", t, inv, True),)


_unit_lower_inv.defvjp(_unit_lower_inv_fwd, _unit_lower_inv_bwd)


def _dn_chunk(state, q, k, v, gc, gr, gl, bc):
    c = q.shape[1]
    q = q * (q.shape[-1] ** -0.5)
    ii = lax.broadcasted_iota(jnp.int32, (c, c), 0)
    jj = lax.broadcasted_iota(jnp.int32, (c, c), 1)
    incl = (jj <= ii)[None]
    strict = (jj < ii)[None]
    decay = jnp.where(incl, jnp.exp(jnp.where(incl, gc - gr, 0.0)), 0.0)
    kb = k * bc
    low = jnp.where(strict, _bnt_hi(kb, k) * decay, 0.0)
    ainv = _unit_lower_inv(low)
    eg = jnp.exp(gc)
    u = _bnn_hi(ainv, v * bc)
    w = _bnn_hi(ainv, kb * eg)
    attn = _bnt(q, k) * decay
    q_dec = q * eg
    k_dec = k * jnp.exp(gl - gc)
    v_new = u - _bnn(w, state)
    o = _bnn(q_dec, state) + _bnn(attn, v_new)
    new_state = state * jnp.exp(gl) + _btn(k_dec, v_new)
    return new_state, o


def _dn_heads(ref, base, hh):
    return jnp.stack([ref[:, base + h * DN_HEAD_DIM: base + (h + 1) * DN_HEAD_DIM] for h in range(hh)])


def _dn_fwd_call(qkv, gc, gr, gl, bc, name):
    s, w3 = qkv.shape
    width = w3 // 3
    hh = width // DN_HEAD_DIM
    d = DN_HEAD_DIM
    n = s // CHUNK

    def body(qkv_ref, gc_ref, gr_ref, gl_ref, bc_ref, o_ref, st_ref, state):
        @pl.when(pl.program_id(0) == 0)
        def _():
            state[...] = jnp.zeros_like(state)

        s_in = state[...]
        st_ref[0] = s_in
        new_s, o = _dn_chunk(s_in, _dn_heads(qkv_ref, 0, hh), _dn_heads(qkv_ref, width, hh),
                             _dn_heads(qkv_ref, 2 * width, hh), gc_ref[0], gr_ref[0], gl_ref[0], bc_ref[0])
        state[...] = new_s
        for h in range(hh):
            o_ref[:, h * d:(h + 1) * d] = o[h]

    g4 = lambda i: (i, 0, 0, 0)
    return _pcall(
        body, name=name + "_fwd",
        out_shape=[jax.ShapeDtypeStruct((s, width), F32), jax.ShapeDtypeStruct((n, hh, d, d), F32)],
        grid=(n,),
        in_specs=[pl.BlockSpec((CHUNK, w3), lambda i: (i, 0)), pl.BlockSpec((1, hh, CHUNK, 1), g4),
                  pl.BlockSpec((1, hh, 1, CHUNK), g4), pl.BlockSpec((1, hh, 1, 1), g4),
                  pl.BlockSpec((1, hh, CHUNK, 1), g4)],
        out_specs=[pl.BlockSpec((CHUNK, width), lambda i: (i, 0)), pl.BlockSpec((1, hh, d, d), g4)],
        scratch=[pltpu.VMEM((hh, d, d), F32)], sem=("arbitrary",))(qkv, gc, gr, gl, bc)


def _dn_bwd_call(qkv, gc, gr, gl, bc, states, do, name):
    s, w3 = qkv.shape
    width = w3 // 3
    hh = width // DN_HEAD_DIM
    d = DN_HEAD_DIM
    n = s // CHUNK

    def body(qkv_ref, gc_ref, gr_ref, gl_ref, bc_ref, st_ref, do_ref,
             dqkv_ref, dgc_ref, dgr_ref, dgl_ref, dbc_ref, dstate):
        @pl.when(pl.program_id(0) == 0)
        def _():
            dstate[...] = jnp.zeros_like(dstate)

        prim = (st_ref[0], _dn_heads(qkv_ref, 0, hh), _dn_heads(qkv_ref, width, hh),
                _dn_heads(qkv_ref, 2 * width, hh), gc_ref[0], gr_ref[0], gl_ref[0], bc_ref[0])
        _, vjp = jax.vjp(_dn_chunk, *prim)
        ds, dq, dk, dv, dgc, dgr, dgl, dbc = vjp((dstate[...], _dn_heads(do_ref, 0, hh)))
        dstate[...] = ds
        for h in range(hh):
            dqkv_ref[:, h * d:(h + 1) * d] = dq[h]
            dqkv_ref[:, width + h * d: width + (h + 1) * d] = dk[h]
            dqkv_ref[:, 2 * width + h * d: 2 * width + (h + 1) * d] = dv[h]
        dgc_ref[0] = dgc
        dgr_ref[0] = dgr
        dgl_ref[0] = dgl
        dbc_ref[0] = dbc

    r2 = lambda i: (n - 1 - i, 0)
    r4 = lambda i: (n - 1 - i, 0, 0, 0)
    spec_c = pl.BlockSpec((1, hh, CHUNK, 1), r4)
    spec_r = pl.BlockSpec((1, hh, 1, CHUNK), r4)
    spec_l = pl.BlockSpec((1, hh, 1, 1), r4)
    return _pcall(
        body, name=name + "_bwd",
        out_shape=[jax.ShapeDtypeStruct(qkv.shape, F32), jax.ShapeDtypeStruct(gc.shape, F32),
                   jax.ShapeDtypeStruct(gr.shape, F32), jax.ShapeDtypeStruct(gl.shape, F32),
                   jax.ShapeDtypeStruct(bc.shape, F32)],
        grid=(n,),
        in_specs=[pl.BlockSpec((CHUNK, w3), r2), spec_c, spec_r, spec_l, spec_c,
                  pl.BlockSpec((1, hh, d, d), r4), pl.BlockSpec((CHUNK, width), r2)],
        out_specs=[pl.BlockSpec((CHUNK, w3), r2), spec_c, spec_r, spec_l, spec_c],
        scratch=[pltpu.VMEM((hh, d, d), F32)], sem=("arbitrary",))(qkv, gc, gr, gl, bc, states, do)


@functools.partial(jax.custom_vjp, nondiff_argnums=(5,))
def dn_core(qkv, gc, gr, gl, bc, name):
    return _dn_fwd_call(qkv, gc, gr, gl, bc, name)[0]


def _dn_core_fwd(qkv, gc, gr, gl, bc, name):
    o, states = _dn_fwd_call(qkv, gc, gr, gl, bc, name)
    return o, (qkv, gc, gr, gl, bc, states)


def _dn_core_bwd(name, saved, do):
    return tuple(_dn_bwd_call(*saved, do, name))


dn_core.defvjp(_dn_core_fwd, _dn_core_bwd)


def dn_out_gate(o, z, out_norm, name):
    hh = o.shape[1] // DN_HEAD_DIM
    gain = jnp.tile(out_norm.reshape(1, -1), (1, hh))

    def fn(ot, zt, gt):
        y = ot * lax.rsqrt(jnp.mean(ot * ot, axis=-1, keepdims=True) + EPS) * gt
        return (y * _silu(zt),)

    return tilewise(fn, name, [o, z, gain], ["tile", "tile", "tile"], [(o.shape[0], BF16)],
                    axis=1, t=DN_HEAD_DIM, diff=[True, True, True])[0]


def _attn_tile(qn, kn, v, qp, kp, q0, scale, causal):
    s = _nt(qn, kn)
    if qp is not None:
        s = s + _nt(qp, kp)
    s = s * scale
    if causal:
        qpos = q0 + lax.broadcasted_iota(jnp.int32, s.shape, 0)
        kpos = lax.broadcasted_iota(jnp.int32, s.shape, 1)
        s = jnp.where((kpos >> _CHUNK_SHIFT) <= (qpos >> _CHUNK_SHIFT), s, -1e30)
    e = jnp.exp(s - jnp.max(s, axis=-1, keepdims=True))
    p = e / jnp.sum(e, axis=-1, keepdims=True)
    return _nn(p, v)


def _attn_specs(sq, sk, dh, dp, tb):
    q_spec = pl.BlockSpec((tb, dh), lambda h, i: (i, h))
    kv_spec = pl.BlockSpec((sk, dh), lambda h, i: (0, h))
    qp_spec = pl.BlockSpec((1, tb, dp), lambda h, i: (h, i, 0)) if dp else None
    kp_spec = pl.BlockSpec((sk, dp), lambda h, i: (0, 0)) if dp else None
    return q_spec, kv_spec, qp_spec, kp_spec


def _attn_plan(sq, sk, tq, causal):
    tb = sq if causal else tq
    subs = [(slice(i * tq, (i + 1) * tq), (i + 1) * tq if causal else sk) for i in range(tb // tq)]
    return tb, subs


def _attn_fwd_call(q, k, v, qp, kp, *, dh, scale, causal, tq, name):
    sq, sk = q.shape[0], k.shape[0]
    heads = q.shape[1] // dh
    dp = qp.shape[-1] if qp is not None else 0
    tb, subs = _attn_plan(sq, sk, tq, causal)
    q_spec, kv_spec, qp_spec, kp_spec = _attn_specs(sq, sk, dh, dp, tb)

    def body(*refs):
        q_ref, k_ref, v_ref = refs[:3]
        qp_ref, kp_ref = (refs[3], refs[4]) if dp else (None, None)
        o_ref = refs[-1]
        row0 = pl.program_id(1) * tb
        for rows, ke in subs:
            o = _attn_tile(q_ref[rows, :], k_ref[:ke, :], v_ref[:ke, :], qp_ref[0, rows, :] if dp else None,
                           kp_ref[:ke, :] if dp else None, row0 + rows.start, scale, causal)
            o_ref[rows, :] = o.astype(o_ref.dtype)

    in_specs = [q_spec, kv_spec, kv_spec] + ([qp_spec, kp_spec] if dp else [])
    args = (q, k, v) + ((qp, kp) if dp else ())
    return _pcall(body, name=name + "_fwd", out_shape=jax.ShapeDtypeStruct((sq, heads * dh), BF16),
                  grid=(heads, sq // tb), in_specs=in_specs, out_specs=q_spec,
                  sem=("parallel", "parallel"))(*args)


def _attn_bwd_call(q, k, v, qp, kp, do, *, dh, scale, causal, tq, name):
    sq, sk = q.shape[0], k.shape[0]
    heads = q.shape[1] // dh
    dp = qp.shape[-1] if qp is not None else 0
    tb, subs = _attn_plan(sq, sk, tq, causal)
    q_spec, kv_spec, qp_spec, kp_spec = _attn_specs(sq, sk, dh, dp, tb)

    def body(*refs):
        h, i = pl.program_id(0), pl.program_id(1)
        if dp:
            q_ref, k_ref, v_ref, qp_ref, kp_ref, do_ref, dq_ref, dk_ref, dv_ref, dqp_ref, dkp_ref = refs
        else:
            q_ref, k_ref, v_ref, do_ref, dq_ref, dk_ref, dv_ref = refs

        @pl.when(i == 0)
        def _():
            dk_ref[...] = jnp.zeros_like(dk_ref)
            dv_ref[...] = jnp.zeros_like(dv_ref)

        if dp:
            @pl.when(jnp.logical_and(h == 0, i == 0))
            def _():
                dkp_ref[...] = jnp.zeros_like(dkp_ref)

        for rows, ke in subs:
            q0 = i * tb + rows.start
            if dp:
                prim = (q_ref[rows, :], k_ref[:ke, :], v_ref[:ke, :], qp_ref[0, rows, :], kp_ref[:ke, :])
                f = lambda a, b, c, d, e, q0=q0: _attn_tile(a, b, c, d, e, q0, scale, causal)
            else:
                prim = (q_ref[rows, :], k_ref[:ke, :], v_ref[:ke, :])
                f = lambda a, b, c, q0=q0: _attn_tile(a, b, c, None, None, q0, scale, causal)
            _, vjp = jax.vjp(f, *prim)
            grads = vjp(do_ref[rows, :].astype(F32))
            dq_ref[rows, :] = grads[0]
            dk_ref[:ke, :] += grads[1]
            dv_ref[:ke, :] += grads[2]
            if dp:
                dqp_ref[0, rows, :] = grads[3]
                dkp_ref[:ke, :] += grads[4]

    in_specs = [q_spec, kv_spec, kv_spec] + ([qp_spec, kp_spec] if dp else []) + [q_spec]
    out_shape = [jax.ShapeDtypeStruct(q.shape, F32), jax.ShapeDtypeStruct(k.shape, F32),
                 jax.ShapeDtypeStruct(v.shape, F32)]
    out_specs = [q_spec, kv_spec, kv_spec]
    if dp:
        out_shape += [jax.ShapeDtypeStruct(qp.shape, F32), jax.ShapeDtypeStruct(kp.shape, F32)]
        out_specs += [qp_spec, kp_spec]
    args = (q, k, v) + ((qp, kp) if dp else ()) + (do,)
    return _pcall(body, name=name + "_bwd", out_shape=out_shape, grid=(heads, sq // tb), in_specs=in_specs,
                  out_specs=out_specs, sem=("arbitrary", "arbitrary"))(*args)


def attention(q, k, v, qp=None, kp=None, *, dh, scale, causal, name, tq=256):
    tq = _tile(q.shape[0], tq)
    kw = dict(dh=dh, scale=scale, causal=causal, tq=tq, name=name)
    has_pe = qp is not None

    @jax.custom_vjp
    def op(*xs):
        return _attn_fwd_call(*xs, **kw) if has_pe else _attn_fwd_call(*xs, None, None, **kw)

    def op_fwd(*xs):
        return (_attn_fwd_call(*xs, **kw) if has_pe else _attn_fwd_call(*xs, None, None, **kw)), xs

    def op_bwd(xs, do):
        full = xs if has_pe else xs + (None, None)
        return tuple(_attn_bwd_call(*full, do, **kw))

    op.defvjp(op_fwd, op_bwd)
    return op(q, k, v, qp, kp) if has_pe else op(q, k, v)


def _rope_tables(positions, reps):
    half = MLA_ROPE // 2
    inv = ROPE_BASE ** (-jnp.arange(0, MLA_ROPE, 2, dtype=F32) / MLA_ROPE)
    ang = positions.astype(F32)[:, None] * inv
    cos, sin = jnp.cos(ang), jnp.sin(ang)
    c = jnp.tile(jnp.concatenate([cos, cos], axis=-1), (1, reps))
    s = jnp.tile(jnp.concatenate([sin, sin], axis=-1), (1, reps))
    rot = np.zeros((MLA_ROPE, MLA_ROPE), np.float32)
    for i in range(half):
        rot[i + half, i] = -1.0
        rot[i, i + half] = 1.0
    return c, s, jnp.asarray(np.kron(np.eye(reps, dtype=np.float32), rot))


def rope(x, positions, name):
    c, s, rot = _rope_tables(positions, x.shape[1] // MLA_ROPE)

    def fn(xt, ct, st, rt):
        return (xt * ct + _nn_hi_const(xt, rt) * st,)

    return tilewise(fn, name, [x, c, s, rot], ["tile", "tile", "tile", "whole"], [(x.shape[1], F32)],
                    axis=0, t=_tile(x.shape[0], 256), diff=[True, False, False, False])[0]


def _ffn_gate_tile(g, u, wg, wu, bg, bu):
    return _silu(_causal_dwconv_tile(g, wg) + bg) * (_causal_dwconv_tile(u, wu) + bu)


def _ffn_gate_specs(pre4, conv4):
    _, groups, s, nb = pre4.shape
    tc = LANE if nb % LANE == 0 else nb
    per = nb // tc
    at = lambda g, j: (0, g, 0, j)
    specs = dict(pre=pl.BlockSpec((2, None, s, tc), at), conv=pl.BlockSpec((2, None, conv4.shape[2], tc), at),
                 bias=pl.BlockSpec((2, None, 1, tc), at), act=pl.BlockSpec((s, tc), lambda g, j: (0, g * per + j)))
    return specs, (groups, per)


def _ffn_gate_fwd_call(pre4, conv4, bias4, name):
    specs, grid = _ffn_gate_specs(pre4, conv4)
    _, groups, s, nb = pre4.shape

    def body(pre_ref, conv_ref, bias_ref, act_ref):
        act_ref[...] = _ffn_gate_tile(pre_ref[0], pre_ref[1], conv_ref[0], conv_ref[1], bias_ref[0],
                                      bias_ref[1]).astype(act_ref.dtype)

    return _pcall(body, name=name + "_fwd", out_shape=jax.ShapeDtypeStruct((s, groups * nb), BF16), grid=grid,
                  in_specs=[specs["pre"], specs["conv"], specs["bias"]], out_specs=specs["act"],
                  sem=("parallel", "parallel"))(pre4, conv4, bias4)


def _ffn_gate_bwd_call(pre4, conv4, bias4, dact, name):
    specs, grid = _ffn_gate_specs(pre4, conv4)

    def body(pre_ref, conv_ref, bias_ref, dact_ref, dpre_ref, dconv_ref, dbias_ref):
        prim = (pre_ref[0], pre_ref[1], conv_ref[0], conv_ref[1], bias_ref[0], bias_ref[1])
        _, vjp = jax.vjp(_ffn_gate_tile, *prim)
        dg, du, dwg, dwu, dbg, dbu = vjp(dact_ref[...].astype(F32))
        dpre_ref[0] = dg.astype(dpre_ref.dtype)
        dpre_ref[1] = du.astype(dpre_ref.dtype)
        dconv_ref[0] = dwg
        dconv_ref[1] = dwu
        dbias_ref[0] = dbg
        dbias_ref[1] = dbu

    return _pcall(body, name=name + "_bwd",
                  out_shape=[jax.ShapeDtypeStruct(pre4.shape, _MXU_DTYPE), jax.ShapeDtypeStruct(conv4.shape, F32),
                             jax.ShapeDtypeStruct(bias4.shape, F32)],
                  grid=grid, in_specs=[specs["pre"], specs["conv"], specs["bias"], specs["act"]],
                  out_specs=[specs["pre"], specs["conv"], specs["bias"]], sem=("parallel", "parallel"))(
                      pre4, conv4, bias4, dact)


@functools.partial(jax.custom_vjp, nondiff_argnums=(4,))
def ffn_up_gate(hn, w3, conv3, bias, name):
    return _ffn_up_gate_fwd(hn, w3, conv3, bias, name)[0]


def _ffn_up_gate_fwd(hn, w3, conv3, bias, name):
    groups, _, nb = w3.shape
    half = groups // 2
    pre4 = _mm_groups_out(hn, w3, name=name + "_up_fwd").reshape(2, half, hn.shape[0], nb)
    conv4 = conv3.reshape(2, half, conv3.shape[1], nb)
    bias4 = bias.reshape(2, half, 1, nb)
    return _ffn_gate_fwd_call(pre4, conv4, bias4, name + "_gate"), (hn, w3, pre4, conv4, bias4)


def _ffn_up_gate_bwd(name, saved, dact):
    hn, w3, pre4, conv4, bias4 = saved
    dpre4, dconv4, dbias4 = _ffn_gate_bwd_call(pre4, conv4, bias4, dact, name + "_gate")
    dpre3 = dpre4.reshape((w3.shape[0],) + dpre4.shape[2:])
    dw3 = _mm_groups_out(hn, dpre3, ta=True, out_dtype=w3.dtype, name=name + "_up_dw")
    dhn = _mm_groups_contract(dpre3, w3, out_dtype=hn.dtype, name=name + "_up_da")
    return dhn, dw3, dconv4.reshape((w3.shape[0],) + dconv4.shape[2:]), dbias4.reshape(-1)


ffn_up_gate.defvjp(_ffn_up_gate_fwd, _ffn_up_gate_bwd)


def loss_rows(h, gain, target, name):
    def fn(ht, tt, gt):
        err = _rms_tile(ht, gt, F32) - tt
        return (0.5 * jnp.mean(err * err, axis=-1, keepdims=True),)

    return tilewise(fn, name, [h, target, gain.reshape(1, -1)], ["tile", "tile", "whole"], [(1, F32)],
                    axis=0, t=_tile(h.shape[0], 256), diff=[True, False, True])[0]


def _col_groups(w, per_head, lo, hi):
    r = w.shape[0]
    return w.reshape(r, -1, per_head)[:, :, lo:hi].reshape(r, -1)


def _layer_mix(h, lw, rp, l, positions):
    s, d = h.shape
    width = d // 2
    dn_heads = width // DN_HEAD_DIM
    mla_heads = (d - width) // MLA_V
    nm = f"l{l}_"
    w_in = lw["w_in"]
    c0 = 4 * width
    c1 = c0 + 2 * dn_heads
    c2 = c1 + MLA_Q_RANK
    c3 = c2 + MLA_KV_RANK + MLA_ROPE
    rest_pad = (-(c3 - c0)) % LANE
    w_rest = jnp.pad(w_in[:, c0:c3], ((0, 0), (0, rest_pad)))

    u = rms_norm(h, rp["norm_mix"], nm + "norm_mix", BF16)
    qkv_raw = linear(u, w_in[:, :3 * width], nm + "in_qkv")
    z = linear(u, w_in[:, 3 * width:c0], nm + "in_z")
    rest = linear(u, w_rest, nm + "in_rest")

    qkv = dn_qkv(qkv_raw, lw["dn_conv"], nm + "dn_qkv")
    csum, beta = dn_gates(rest[:, :dn_heads], rest[:, dn_heads:2 * dn_heads], rp["dn_a_log"],
                          rp["dn_dt_bias"], nm + "dn_gates")
    n = s // CHUNK
    g3 = csum.reshape(n, CHUNK, dn_heads).transpose(0, 2, 1)
    b3 = beta.reshape(n, CHUNK, dn_heads).transpose(0, 2, 1)
    o_dn = dn_core(qkv, g3[..., None], g3[:, :, None, :], g3[:, :, CHUNK - 1][..., None, None], b3[..., None],
                   nm + "dn_core")
    o_dn = dn_out_gate(o_dn, z, rp["dn_out_norm"], nm + "dn_gate")

    mq = rest[:, c1 - c0:c2 - c0]
    mkv = rest[:, c2 - c0:c3 - c0]
    qn = rms_norm(mq, rp["mla_q_norm"], nm + "mla_qnorm", BF16)
    per_q = MLA_NOPE + MLA_ROPE
    q_nope = linear(qn, _col_groups(lw["mla_w_qb"], per_q, 0, MLA_NOPE), nm + "mla_qn")
    q_pe = linear(qn, _col_groups(lw["mla_w_qb"], per_q, MLA_NOPE, per_q), nm + "mla_qp")
    kvn = rms_norm(mkv[:, :MLA_KV_RANK], rp["mla_kv_norm"], nm + "mla_kvnorm", BF16)
    per_kv = MLA_NOPE + MLA_V
    k_nope = linear(kvn, _col_groups(lw["mla_w_kvb"], per_kv, 0, MLA_NOPE), nm + "mla_kn")
    v_mla = linear(kvn, _col_groups(lw["mla_w_kvb"], per_kv, MLA_NOPE, per_kv), nm + "mla_v")
    q_pe = rope(q_pe, positions, nm + "rope_q")
    k_pe = rope(mkv[:, MLA_KV_RANK:], positions, nm + "rope_k")
    q_pe = q_pe.reshape(s, mla_heads, MLA_ROPE).transpose(1, 0, 2)
    o_mla = attention(q_nope, k_nope, v_mla, q_pe, k_pe, dh=MLA_NOPE, scale=per_q ** -0.5, causal=True,
                      name=nm + "mla_attn")

    h = linear(o_dn, lw["w_out"][:width], nm + "out_dn", res=h)
    return linear(o_mla, lw["w_out"][width:], nm + "out_mla", res=h)


def _layer_tail(h, lw, rp, l, mem_n):
    d = h.shape[1]
    nm = f"l{l}_"
    hn = rms_norm(h, rp["norm_xattn"], nm + "norm_xattn", BF16)
    xq = linear(hn, lw["xa_wq"], nm + "xa_q")
    xk = linear(mem_n, lw["xa_wk"], nm + "xa_k")
    xv = linear(mem_n, lw["xa_wv"], nm + "xa_v")
    xdh = d // XA_HEADS
    xo = attention(xq, xk, xv, dh=xdh, scale=xdh ** -0.5, causal=False, name=nm + "xattn")
    h = linear(xo, lw["xa_wo"], nm + "xa_o", res=h)

    hn = rms_norm(h, rp["norm_ffn"], nm + "norm_ffn", BF16)
    act = ffn_up_gate(hn, lw["ffn_w_up"], lw["ffn_conv"], rp["ffn_conv_bias"], nm + "ffn")
    return linear(act, lw["ffn_w_down"], nm + "ffn_down", res=h)


_COL_SHARDED = ("w_in", "mla_w_qb", "mla_w_kvb", "ffn_w_up", "dn_conv", "ffn_conv")
_ROW_SHARDED = ("w_out", "xa_wq", "xa_wk", "xa_wv", "xa_wo", "ffn_w_down")
_BIG = ("w_in", "mla_w_qb", "mla_w_kvb", "w_out", "xa_wq", "xa_wk", "xa_wv", "xa_wo", "ffn_w_up", "ffn_w_down")
_MIX_KEYS = ("w_in", "mla_w_qb", "mla_w_kvb", "w_out")
_TAIL_KEYS = ("xa_wq", "xa_wk", "xa_wv", "xa_wo", "ffn_w_up", "ffn_w_down")
_SMALL_SHARDED = ("dn_conv", "ffn_conv")
_REP_MIX = ("norm_mix", "dn_a_log", "dn_dt_bias", "dn_out_norm", "mla_q_norm", "mla_kv_norm")
_REP_TAIL = ("norm_xattn", "norm_ffn", "ffn_conv_bias")
_REPLICATED = ("norm_mix", "dn_a_log", "dn_dt_bias", "dn_out_norm", "mla_q_norm", "mla_kv_norm", "mem_norm",
               "norm_xattn", "norm_ffn", "ffn_conv_bias", "norm_final")
_WEIGHTS = ("norm_mix", "w_in", "dn_conv", "dn_a_log", "dn_dt_bias", "dn_out_norm", "mla_q_norm", "mla_w_qb",
            "mla_kv_norm", "mla_w_kvb", "w_out", "mem_norm", "norm_xattn", "xa_wq", "xa_wk", "xa_wv", "xa_wo",
            "norm_ffn", "ffn_w_up", "ffn_conv", "ffn_conv_bias", "ffn_w_down", "norm_final")


def _assemble(name, g):
    if name in _ROW_SHARDED:
        return g.reshape((-1,) + g.shape[2:])
    return jnp.moveaxis(g, 0, -2).reshape(g.shape[1:-1] + (-1,))


def _mix(h, gw, conv, rp, l, positions):
    lw = {k: _assemble(k, v) for k, v in gw.items()}
    lw["dn_conv"] = _assemble("dn_conv", conv)
    return _layer_mix(h, lw, rp, l, positions)


def _tail(h, gw, conv, rp, l, mem_n):
    lw = {k: v if k == "ffn_w_up" else _assemble(k, v) for k, v in gw.items()}
    lw["ffn_conv"] = conv
    return _layer_tail(h, lw, rp, l, mem_n)


def _my_index():
    return 4 * lax.axis_index("x") + 2 * lax.axis_index("y") + lax.axis_index("c")


def _peer(k):
    x, y, c = lax.axis_index("x"), lax.axis_index("y"), lax.axis_index("c")
    px = (1 - x) if k & 4 else x
    py = (1 - y) if k & 2 else y
    pc = (1 - c) if k & 1 else c
    return (px, py, pc), 4 * px + 2 * py + pc


_ANY = pl.BlockSpec(memory_space=pl.ANY)


def all_gather(shards, name):
    n = len(shards)

    def body(*refs):
        x_refs, o_refs = refs[:n], refs[n:2 * n]
        send_sems, recv_sems, local_sems = refs[2 * n:]
        me = _my_index()
        sib_id, sib = _peer(1)
        chips = [_peer(k) for k in (4, 2, 6)]

        def copy(a, k, block, to, src=None):
            return pltpu.make_async_remote_copy(
                src_ref=o_refs[a].at[block] if src is None else src, dst_ref=o_refs[a].at[block],
                send_sem=send_sems.at[a * 7 + k], recv_sem=recv_sems.at[a * 7 + k],
                device_id=to, device_id_type=MESH)

        mine = [pltpu.make_async_copy(x_refs[a], o_refs[a].at[me], local_sems.at[a]) for a in range(n)]
        for cp in mine:
            cp.start()
        first = []
        for a in range(n):
            first.append(copy(a, 0, me, sib_id, src=x_refs[a]))
            first += [copy(a, 1 + j, me, cid, src=x_refs[a]) for j, (cid, _) in enumerate(chips)]
        for cp in first:
            cp.start()
        passed = []
        for a in range(n):
            for j, (cid, cidx) in enumerate(chips):
                copy(a, 1 + j, cidx, cid).wait_recv()
                fwd = copy(a, 4 + j, cidx, sib_id)
                fwd.start()
                passed.append(fwd)
        for a in range(n):
            copy(a, 0, sib, sib_id).wait_recv()
            for j, (_, cidx) in enumerate(chips):
                copy(a, 4 + j, cidx ^ 1, sib_id).wait_recv()
        for cp in first + passed:
            cp.wait_send()
        for cp in mine:
            cp.wait()

    out_shape = [jax.ShapeDtypeStruct((N_DEV,) + s.shape, s.dtype) for s in shards]
    return _pcall(body, name=name, out_shape=out_shape, in_specs=[_ANY] * n, out_specs=[_ANY] * n,
                  scratch=[pltpu.SemaphoreType.DMA((7 * n,)), pltpu.SemaphoreType.DMA((7 * n,)),
                           pltpu.SemaphoreType.DMA((n,))])(*shards)


_HBM = pl.BlockSpec(memory_space=pltpu.HBM)
_SEM = pl.BlockSpec(memory_space=pltpu.SEMAPHORE)
_EFFECT = pltpu.SideEffectType.DATAFLOW_SIDE_EFFECTING


def _plan_gather_first(n):
    def plan(refs):
        me = _my_index()
        sib_id, sib = _peer(1)
        chips = [_peer(k) for k in (4, 2, 6)]
        out = []
        for a in range(n):
            x, land = refs[a], refs[n + a]
            out.append((x, land.at[me], sib_id, land.at[sib]))
            out += [(x, land.at[me], cid, land.at[cidx]) for cid, cidx in chips]
        return out

    return plan, 4 * n


def _plan_gather_pass(n):
    def plan(refs):
        sib_id, _ = _peer(1)
        chips = [_peer(k) for k in (4, 2, 6)]
        return [(refs[a].at[cidx], refs[a].at[cidx], sib_id, refs[a].at[cidx ^ 1])
                for a in range(n) for _, cidx in chips]

    return plan, 3 * n


def _plan_gather_direct(n):
    def plan(refs):
        me = _my_index()
        out = []
        for a in range(n):
            for k in range(1, N_DEV):
                pid, pidx = _peer(k)
                out.append((refs[a], refs[n + a].at[me], pid, refs[n + a].at[pidx]))
        return out

    return plan, 7 * n


def _plan_scatter(n):
    def plan(refs):
        me = _my_index()
        out = []
        for a in range(n):
            for k in range(1, N_DEV):
                pid, pidx = _peer(k)
                out.append((refs[a].at[pidx], refs[n + a].at[me], pid, refs[n + a].at[pidx]))
        return out

    return plan, 7 * n


def _remote_copy(src, dst, send_sems, recv_sems, i, dev):
    return pltpu.make_async_remote_copy(src_ref=src, dst_ref=dst, send_sem=send_sems.at[i], recv_sem=recv_sems.at[i],
                                        device_id=dev, device_id_type=MESH)


def _after(x, tokens):
    for token in tokens:
        x = x + token[0, 0]
    return x


def exchange_start(bufs, plan_n, name, after=None):
    plan, n = plan_n
    nb = len(bufs)
    n_in = nb + (after is not None)

    def body(*refs):
        send_sems, recv_sems = refs[n_in], refs[n_in + 1]
        for i, (src, dst, dev, _) in enumerate(plan(refs[:nb])):
            _remote_copy(src, dst, send_sems, recv_sems, i, dev).start()
        refs[-1][...] = jnp.zeros_like(refs[-1])

    out_shape = (pltpu.SemaphoreType.DMA((n,)), pltpu.SemaphoreType.DMA((n,)),
                 *[pltpu.HBM(b.shape, b.dtype) for b in bufs], jax.ShapeDtypeStruct((8, LANE), F32))
    args = [pltpu.with_memory_space_constraint(b, pltpu.HBM) for b in bufs] + ([after] if after is not None else [])
    res = pl.pallas_call(
        body, name=name, out_shape=out_shape,
        in_specs=[_HBM] * nb + ([_ANY] if after is not None else []),
        out_specs=(_SEM, _SEM, *[_HBM] * nb, pl.BlockSpec(memory_space=pltpu.VMEM)),
        input_output_aliases={i: 2 + i for i in range(nb)},
        compiler_params=pltpu.CompilerParams(has_side_effects=_EFFECT), interpret=_INTERPRET)(*args)
    return (res[0], res[1], list(res[2:2 + nb])), res[-1]


def exchange_wait(flight, plan_n, name, after):
    plan, _ = plan_n
    send_sems, recv_sems, bufs = flight
    nb = len(bufs)

    def body(*refs):
        s_sems, r_sems = refs[nb], refs[nb + 1]
        for i, (src, _, dev, arrival) in enumerate(plan(refs[:nb])):
            cp = _remote_copy(src, arrival, s_sems, r_sems, i, dev)
            cp.wait_send()
            cp.wait_recv()

    return list(pl.pallas_call(
        body, name=name, out_shape=tuple(pltpu.HBM(b.shape, b.dtype) for b in bufs),
        in_specs=[_HBM] * nb + [_SEM, _SEM, _ANY], out_specs=tuple([_HBM] * nb),
        input_output_aliases={i: i for i in range(nb)},
        compiler_params=pltpu.CompilerParams(has_side_effects=_EFFECT), interpret=_INTERPRET)(
            *bufs, send_sems, recv_sems, after))


def _adam_tile(contribs, w, m, v):
    g = contribs[0].astype(F32)
    for part in contribs[1:]:
        g = g + part.astype(F32)
    m_new = ADAM_B1 * m + (1.0 - ADAM_B1) * g
    v_new = ADAM_B2 * v + (1.0 - ADAM_B2) * (g * g)
    m_hat = m_new / (1.0 - ADAM_B1 ** ADAM_STEP)
    v_hat = v_new / (1.0 - ADAM_B2 ** ADAM_STEP)
    return g, -ADAM_LR * (m_hat / (jnp.sqrt(v_hat) + ADAM_EPS) + ADAM_WD * w), m_new, v_new


def adam_update(recv, w, m, v, name):
    ll, _, r, c = recv.shape
    tr = _tile(r, max(8, 1 << ((_ADAM_TILE_ELEMS // c).bit_length() - 1)))

    def body(g_ref, w_ref, m_ref, v_ref, go_ref, d_ref, mo_ref, vo_ref):
        go_ref[0], d_ref[0], mo_ref[0], vo_ref[0] = _adam_tile([g_ref[0, j] for j in range(N_DEV)], w_ref[0],
                                                               m_ref[0], v_ref[0])

    spec = pl.BlockSpec((1, tr, c), lambda l, i: (l, i, 0))
    sds = jax.ShapeDtypeStruct((ll, r, c), F32)
    return _pcall(body, name=name, out_shape=[sds] * 4, grid=(ll, r // tr),
                  in_specs=[pl.BlockSpec((1, N_DEV, tr, c), lambda l, i: (l, 0, i, 0)), spec, spec, spec],
                  out_specs=[spec] * 4, sem=("parallel", "parallel"))(recv, w, m, v)


def adam_layer(recv, w, m, v, prev, l, name):
    _, r, c = recv.shape
    tr = _tile(r, max(8, 1 << ((_ADAM_TILE_ELEMS // c).bit_length() - 1)))
    if prev is None:
        prev = [lax.empty(w.shape, F32) for _ in range(4)]

    def body(g_ref, w_ref, m_ref, v_ref, *rest):
        go_ref, d_ref, mo_ref, vo_ref = rest[4:]
        go_ref[0], d_ref[0], mo_ref[0], vo_ref[0] = _adam_tile([g_ref[j] for j in range(N_DEV)], w_ref[0], m_ref[0],
                                                               v_ref[0])

    spec = pl.BlockSpec((1, tr, c), lambda i: (l, i, 0))
    return _pcall(body, name=name, out_shape=[jax.ShapeDtypeStruct(w.shape, F32)] * 4, grid=(r // tr,),
                  in_specs=[pl.BlockSpec((N_DEV, tr, c), lambda i: (0, i, 0)), spec, spec, spec] + [_ANY] * 4,
                  out_specs=[spec] * 4, sem=("parallel",),
                  input_output_aliases={4 + j: j for j in range(4)})(recv, w, m, v, *prev)


def _as3d(a):
    if a.ndim == 1:
        return a.reshape(1, 1, -1)
    if a.ndim == 2:
        return a.reshape(1, a.shape[0], a.shape[1])
    return a.reshape(a.shape[0], -1, a.shape[-1])


def kernel(x, mem, positions, norm_mix, w_in, dn_conv, dn_a_log, dn_dt_bias, dn_out_norm, mla_q_norm, mla_w_qb, mla_kv_norm, mla_w_kvb, w_out, mem_norm, norm_xattn, xa_wq, xa_wk, xa_wv, xa_wo, norm_ffn, ffn_w_up, ffn_conv, ffn_conv_bias, ffn_w_down, norm_final, loss_target, m_norm_mix, m_w_in, m_dn_conv, m_dn_a_log, m_dn_dt_bias, m_dn_out_norm, m_mla_q_norm, m_mla_w_qb, m_mla_kv_norm, m_mla_w_kvb, m_w_out, m_mem_norm, m_norm_xattn, m_xa_wq, m_xa_wk, m_xa_wv, m_xa_wo, m_norm_ffn, m_ffn_w_up, m_ffn_conv, m_ffn_conv_bias, m_ffn_w_down, m_norm_final, v_norm_mix, v_w_in, v_dn_conv, v_dn_a_log, v_dn_dt_bias, v_dn_out_norm, v_mla_q_norm, v_mla_w_qb, v_mla_kv_norm, v_mla_w_kvb, v_w_out, v_mem_norm, v_norm_xattn, v_xa_wq, v_xa_wk, v_xa_wv, v_xa_wo, v_norm_ffn, v_ffn_w_up, v_ffn_conv, v_ffn_conv_bias, v_ffn_w_down, v_norm_final):
    w = dict(norm_mix=norm_mix, w_in=w_in, dn_conv=dn_conv, dn_a_log=dn_a_log, dn_dt_bias=dn_dt_bias,
             dn_out_norm=dn_out_norm, mla_q_norm=mla_q_norm, mla_w_qb=mla_w_qb, mla_kv_norm=mla_kv_norm,
             mla_w_kvb=mla_w_kvb, w_out=w_out, mem_norm=mem_norm, norm_xattn=norm_xattn, xa_wq=xa_wq, xa_wk=xa_wk,
             xa_wv=xa_wv, xa_wo=xa_wo, norm_ffn=norm_ffn, ffn_w_up=ffn_w_up, ffn_conv=ffn_conv,
             ffn_conv_bias=ffn_conv_bias, ffn_w_down=ffn_w_down, norm_final=norm_final)
    mom = dict(norm_mix=m_norm_mix, w_in=m_w_in, dn_conv=m_dn_conv, dn_a_log=m_dn_a_log, dn_dt_bias=m_dn_dt_bias,
               dn_out_norm=m_dn_out_norm, mla_q_norm=m_mla_q_norm, mla_w_qb=m_mla_w_qb, mla_kv_norm=m_mla_kv_norm,
               mla_w_kvb=m_mla_w_kvb, w_out=m_w_out, mem_norm=m_mem_norm, norm_xattn=m_norm_xattn, xa_wq=m_xa_wq,
               xa_wk=m_xa_wk, xa_wv=m_xa_wv, xa_wo=m_xa_wo, norm_ffn=m_norm_ffn, ffn_w_up=m_ffn_w_up,
               ffn_conv=m_ffn_conv, ffn_conv_bias=m_ffn_conv_bias, ffn_w_down=m_ffn_w_down, norm_final=m_norm_final)
    var = dict(norm_mix=v_norm_mix, w_in=v_w_in, dn_conv=v_dn_conv, dn_a_log=v_dn_a_log, dn_dt_bias=v_dn_dt_bias,
               dn_out_norm=v_dn_out_norm, mla_q_norm=v_mla_q_norm, mla_w_qb=v_mla_w_qb, mla_kv_norm=v_mla_kv_norm,
               mla_w_kvb=v_mla_w_kvb, w_out=v_w_out, mem_norm=v_mem_norm, norm_xattn=v_norm_xattn, xa_wq=v_xa_wq,
               xa_wk=v_xa_wk, xa_wv=v_xa_wv, xa_wo=v_xa_wo, norm_ffn=v_norm_ffn, ffn_w_up=v_ffn_w_up,
               ffn_conv=v_ffn_conv, ffn_conv_bias=v_ffn_conv_bias, ffn_w_down=v_ffn_w_down, norm_final=v_norm_final)
    depth = w_in.shape[0]
    me = _my_index()
    rep = {k: w[k] for k in _REPLICATED}
    x0, mem0, pos0, target0 = x[0], mem[0], positions[0], loss_target[0]

    def with_own_block(block):
        return lax.dynamic_update_index_in_dim(lax.empty((N_DEV,) + block.shape, block.dtype), block, me, 0)

    def gather_start(l, keys, tag, after):
        shards = [w[k][l].astype(BF16) for k in keys]
        plan = _plan_gather_first(len(keys))
        flight, token = exchange_start(shards + [with_own_block(sh) for sh in shards], plan,
                                       f"gather_l{l}{tag}_first_start", after)
        return (flight, plan, keys, f"gather_l{l}{tag}"), token

    def gather_pass(state, after):
        flight, plan, keys, name = state
        lands = exchange_wait(flight, plan, name + "_first_wait", after)[len(keys):]
        plan = _plan_gather_pass(len(keys))
        flight, token = exchange_start(lands, plan, name + "_pass_start")
        return (flight, plan, keys, name), token

    def gather_finish(state, after):
        flight, plan, keys, name = state
        return dict(zip(keys, exchange_wait(flight, plan, name + "_pass_wait", after)))

    small = dict(zip(_SMALL_SHARDED, all_gather([w[k] for k in _SMALL_SHARDED], "gather_small")))
    mem_n, vjp_mem = jax.vjp(lambda g: rms_norm(mem0, g, "mem_norm", BF16), rep["mem_norm"])

    groups = [(l, keys, tag) for l in range(depth) for keys, tag in ((_MIX_KEYS, "_mix"), (_TAIL_KEYS, "_tail"))]
    n_stage = len(groups)
    states, ready = {}, {}
    token = small["dn_conv"]
    for s in range(min(3, n_stage)):
        states[s], token = gather_start(*groups[s], token)
    states[0], token = gather_pass(states[0], token)
    ready[0] = gather_finish(states[0], token)

    h = x0
    tapes = []
    for s in range(n_stage):
        l = s // 2
        tokens = []
        if s >= 2 and s + 1 < n_stage:
            states[s + 1], token = gather_pass(states[s + 1], h)
            tokens.append(token)
        if s + 3 < n_stage:
            states[s + 3], token = gather_start(*groups[s + 3], h)
            tokens.append(token)
        if s % 2 == 0:
            rp = {k: rep[k][l] for k in _REP_MIX}
            rp["norm_mix"] = _after(rp["norm_mix"], tokens)
            h_new, tape = jax.vjp(lambda hh, gw, cv, rr: _mix(hh, gw, cv, rr, l, pos0), h, ready.pop(s),
                                  small["dn_conv"][:, l], rp)
        else:
            rp = {k: rep[k][l] for k in _REP_TAIL}
            rp["norm_xattn"] = _after(rp["norm_xattn"], tokens)
            h_new, tape = jax.vjp(lambda hh, gw, cv, rr, mn: _tail(hh, gw, cv, rr, l, mn), h, ready.pop(s),
                                  small["ffn_conv"][:, l], rp, mem_n)
        if s + 1 < n_stage:
            if s < 2:
                states[s + 1], token = gather_pass(states[s + 1], h_new)
                ready[s + 1] = gather_finish(states[s + 1], token)
            else:
                ready[s + 1] = gather_finish(states[s + 1], h_new)
        tapes.append(tape)
        h = h_new

    rows, vjp_loss = jax.vjp(lambda hh, g: loss_rows(hh, g, target0, "loss"), h, rep["norm_final"])
    loss = lax.psum(jnp.sum(rows), ("x", "y", "c"))
    dh, d_norm_final = vjp_loss(jnp.ones_like(rows))

    d_rep = {k: [None] * depth for k in _REP_MIX + _REP_TAIL}
    d_conv = {k: [None] * depth for k in _SMALL_SHARDED}
    recv_big = [dict() for _ in range(depth)]
    d_mem_n = None

    def scatter_start(l, grads, keys, tag):
        parts = [grads[k] for k in keys]
        own = [with_own_block(lax.dynamic_index_in_dim(p, me, 0, keepdims=False)) for p in parts]
        plan = _plan_scatter(len(keys))
        flight, token = exchange_start(parts + own, plan, f"scatter_l{l}{tag}_start")
        return (flight, plan, keys, l, f"scatter_l{l}{tag}_wait"), token

    def scatter_finish(state, after):
        flight, plan, keys, l, name = state
        recv_big[l].update(zip(keys, exchange_wait(flight, plan, name, after)[len(keys):]))

    pending = {}
    token = None
    for s in reversed(range(n_stage)):
        l, keys, tag = groups[s]
        if token is not None:
            dh = _after(dh, [token])
        if s % 2:
            dh, dg, d_conv["ffn_conv"][l], d_rp, d_mn = tapes[s](dh)
            d_mem_n = d_mn if d_mem_n is None else d_mem_n + d_mn
        else:
            dh, dg, d_conv["dn_conv"][l], d_rp = tapes[s](dh)
        for k, v in d_rp.items():
            d_rep[k][l] = v
        if s + 2 in pending:
            scatter_finish(pending.pop(s + 2), dh)
        pending[s], token = scatter_start(l, dg, keys, tag)

    (d_mem_norm,) = vjp_mem(d_mem_n)
    d_rep_all = {k: jnp.stack(v) for k, v in d_rep.items()}
    d_rep_all["mem_norm"] = d_mem_norm
    d_rep_all["norm_final"] = _after(d_norm_final, [token])
    parts = [jnp.stack(d_conv[k], axis=1) for k in _SMALL_SHARDED]
    own = [with_own_block(lax.dynamic_index_in_dim(p, me, 0, keepdims=False)) for p in parts]
    plan_small = _plan_scatter(len(parts))
    flight_small, token = exchange_start(parts + own, plan_small, "scatter_small_start")
    blocks = [_as3d(d_rep_all[k])[0] for k in _REPLICATED]
    blocks[0] = _after(blocks[0], [token])
    plan_rep = _plan_gather_direct(len(blocks))
    flight_rep, token = exchange_start(blocks + [with_own_block(b) for b in blocks], plan_rep, "gather_rep_start")

    big = {k: None for k in _BIG}

    def update_big(l):
        for k in _BIG:
            recv = recv_big[l][k]
            big[k] = adam_layer(recv.reshape((N_DEV, -1, recv.shape[-1])), _as3d(w[k]), _as3d(mom[k]),
                                _as3d(var[k]), big[k], l, f"adam_{k}_l{l}")

    for l in reversed(range(1, depth)):
        update_big(l)
    after = token if depth == 1 else big[_BIG[-1]][0]
    out = {}
    recv_small = exchange_wait(flight_small, plan_small, "scatter_small_wait", after)[len(parts):]
    for k, recv in zip(_SMALL_SHARDED, recv_small):
        r3 = _as3d(w[k])
        out[k] = adam_update(recv.reshape((1, N_DEV, -1, recv.shape[-1])), r3.reshape((1, -1, r3.shape[-1])),
                             _as3d(mom[k]).reshape((1, -1, r3.shape[-1])), _as3d(var[k]).reshape((1, -1, r3.shape[-1])),
                             "adam_" + k)
    recv_rep = exchange_wait(flight_rep, plan_rep, "gather_rep_wait", after)[len(blocks):]
    for k, recv in zip(_REPLICATED, recv_rep):
        r3 = _as3d(w[k])
        flat = lambda a, r3=r3: a.reshape((1, -1, r3.shape[-1]))
        out[k] = adam_update(recv[None], flat(w[k]), flat(mom[k]), flat(var[k]), "adam_" + k)
    for s in sorted(pending, reverse=True):
        scatter_finish(pending[s], out["norm_final"][0])
    update_big(0)
    out.update(big)

    res = [loss, dh[None]]
    for j in range(4):
        res += [out[k][j].reshape(w[k].shape) for k in _WEIGHTS]
    return tuple(res)
```

```python
import functools

import numpy as np
import jax
import jax.numpy as jnp
from jax import lax
from jax.experimental import pallas as pl
from jax.experimental.pallas import tpu as pltpu

F32 = jnp.float32
BF16 = jnp.bfloat16
_MXU_DTYPE = BF16
_INTERPRET = False
_VMEM_LIMIT_BYTES = 48 * 1024 * 1024
_MM_VMEM_BUDGET_BYTES = 36 * 1024 * 1024
_MM_TILE_CAP = 2048

N_DEV = 8
CHUNK = 64
_CHUNK_SHIFT = 6
DN_HEAD_DIM = 128
DN_CONV = 4
MLA_NOPE = 128
MLA_ROPE = 64
MLA_V = 128
MLA_Q_RANK = 512
MLA_KV_RANK = 256
ROPE_BASE = 10000.0
XA_HEADS = 4
FFN_CONV = 3
EPS = 1e-6
LANE = 128

ADAM_LR = 0.001
ADAM_B1 = 0.9
ADAM_B2 = 0.999
ADAM_EPS = 1e-08
ADAM_WD = 0.01
ADAM_STEP = 10
_ADAM_TILE_ELEMS = 128 * 1024

MESH = pl.DeviceIdType.MESH


def _pcall(body, *, name, out_shape, grid=None, in_specs=None, out_specs=None, scratch=(), sem=None, **kw):
    params = pltpu.CompilerParams(dimension_semantics=sem, vmem_limit_bytes=_VMEM_LIMIT_BYTES)
    args = dict(name=name, out_shape=out_shape, scratch_shapes=list(scratch), compiler_params=params,
                interpret=_INTERPRET, **kw)
    if grid is not None:
        args.update(grid=grid)
    if in_specs is not None:
        args.update(in_specs=in_specs)
    if out_specs is not None:
        args.update(out_specs=out_specs)
    return pl.pallas_call(body, **args)


def _tile(n, pref):
    if n <= pref:
        return n
    t = pref
    while t >= 8:
        if n % t == 0:
            return t
        t //= 2
    return n


def _ein_raw(spec, a, b, hi):
    one = lambda x, y: jnp.einsum(spec, x, y, preferred_element_type=F32)
    a_hi, b_hi = a.astype(_MXU_DTYPE), b.astype(_MXU_DTYPE)
    if not hi:
        return one(a_hi, b_hi)
    a_lo = (a.astype(F32) - a_hi.astype(F32)).astype(_MXU_DTYPE)
    b_lo = (b.astype(F32) - b_hi.astype(F32)).astype(_MXU_DTYPE)
    return one(a_hi, b_hi) + (one(a_hi, b_lo) + one(a_lo, b_hi))


def _make_ein(spec, hi=False, diff_b=True):
    a_s, rest = spec.split(",")
    b_s, o_s = rest.split("->")

    @jax.custom_vjp
    def f(a, b):
        return _ein_raw(spec, a, b, hi)

    def fwd(a, b):
        return f(a, b), (a, b)

    def bwd(res, g):
        a, b = res
        da = _ein_raw(f"{o_s},{b_s}->{a_s}", g, b, hi)
        if not diff_b:
            return da.astype(a.dtype), None
        db = _ein_raw(f"{a_s},{o_s}->{b_s}", a, g, hi)
        return da.astype(a.dtype), db.astype(b.dtype)

    f.defvjp(fwd, bwd)
    return f


_nt = _make_ein("qd,kd->qk")
_nn = _make_ein("qk,kd->qd")
_nn_hi_const = _make_ein("qk,kd->qd", hi=True, diff_b=False)
_bnt = _make_ein("hik,hjk->hij")
_bnn = _make_ein("hij,hjv->hiv")
_btn = _make_ein("hck,hcv->hkv")


def _divisor_tiles(n, cap):
    out = [n] if n <= cap else []
    t = cap
    while t >= LANE:
        if t < n and n % t == 0:
            out.append(t)
        t //= 2
    return out or [n]


def _mm_tiles(m, n, k, a_bytes, b_bytes, o_bytes, r_bytes, accumulate=False):
    best = None
    for tk in sorted({k, *_divisor_tiles(k, _MM_TILE_CAP)}):
        for tm in _divisor_tiles(m, _MM_TILE_CAP):
            for tn in _divisor_tiles(n, _MM_TILE_CAP):
                need = 2 * (tm * tk * a_bytes + tk * tn * b_bytes + tm * tn * (o_bytes + r_bytes))
                need += tm * tn * 4 * (1 if tk == k and not accumulate else 2)
                if need > _MM_VMEM_BUDGET_BYTES:
                    continue
                score = (tm * tn * tk, tk, tm)
                if best is None or score > best[0]:
                    best = (score, (tm, tn, tk))
    assert best is not None, (m, n, k)
    return best[1]


def _mm(a, b, *, ta=False, tb=False, res=None, out_dtype=F32, name="mm"):
    m = a.shape[1] if ta else a.shape[0]
    k = a.shape[0] if ta else a.shape[1]
    n = b.shape[0] if tb else b.shape[1]
    assert (b.shape[1] if tb else b.shape[0]) == k, (a.shape, b.shape, ta, tb)
    has_res = res is not None
    tm, tn, tk = _mm_tiles(m, n, k, a.dtype.itemsize, b.dtype.itemsize, jnp.dtype(out_dtype).itemsize,
                           res.dtype.itemsize if has_res else 0)
    nk = k // tk
    body = _mm_body(nk, (((0 if ta else 1,), (1 if tb else 0,)), ((), ())), has_res, out_dtype)
    a_spec = pl.BlockSpec((tk, tm), lambda i, j, kk: (kk, i)) if ta else pl.BlockSpec((tm, tk), lambda i, j, kk: (i, kk))
    b_spec = pl.BlockSpec((tn, tk), lambda i, j, kk: (j, kk)) if tb else pl.BlockSpec((tk, tn), lambda i, j, kk: (kk, j))
    o_spec = pl.BlockSpec((tm, tn), lambda i, j, kk: (i, j))
    in_specs = [a_spec, b_spec] + ([o_spec] if has_res else [])
    args = (a, b) + ((res,) if has_res else ())
    return _pcall(body, name=name, out_shape=jax.ShapeDtypeStruct((m, n), out_dtype),
                  grid=(m // tm, n // tn, nk), in_specs=in_specs, out_specs=o_spec,
                  scratch=[pltpu.VMEM((tm, tn), F32)] if nk > 1 else [],
                  sem=("parallel", "parallel", "arbitrary"))(*args)


def _mm_body(nk, dims, has_res, out_dtype):
    def body(*refs):
        a_ref, b_ref = refs[0], refs[1]
        r_ref = refs[2] if has_res else None
        o_ref = refs[2 + has_res]

        def finish(r):
            if has_res:
                r = r + r_ref[...].astype(F32)
            o_ref[...] = r.astype(out_dtype)

        part = lax.dot_general(a_ref[...].astype(_MXU_DTYPE), b_ref[...].astype(_MXU_DTYPE), dims,
                               preferred_element_type=F32)
        if nk == 1:
            finish(part)
            return
        acc = refs[-1]
        kk = pl.program_id(2)

        @pl.when(kk == 0)
        def _():
            acc[...] = part

        @pl.when(kk > 0)
        def _():
            acc[...] += part

        @pl.when(kk == nk - 1)
        def _():
            finish(acc[...])

    return body


def _mm_groups_out(a, b3, *, ta=False, out_dtype=F32, name="mm_groups"):
    groups, k, nb = b3.shape
    m = a.shape[1] if ta else a.shape[0]
    assert (a.shape[0] if ta else a.shape[1]) == k, (a.shape, b3.shape, ta)
    tm, tn, tk = _mm_tiles(m, nb, k, a.dtype.itemsize, b3.dtype.itemsize, jnp.dtype(out_dtype).itemsize, 0)
    nk, per = k // tk, nb // tn
    body = _mm_body(nk, (((0 if ta else 1,), (0,)), ((), ())), False, out_dtype)
    a_spec = pl.BlockSpec((tk, tm), lambda j, i, kk: (kk, i)) if ta else pl.BlockSpec((tm, tk), lambda j, i, kk: (i, kk))
    b_spec = pl.BlockSpec((None, tk, tn), lambda j, i, kk: (j // per, kk, j % per))
    o_spec = pl.BlockSpec((None, tm, tn), lambda j, i, kk: (j // per, i, j % per))
    return _pcall(body, name=name, out_shape=jax.ShapeDtypeStruct((groups, m, nb), out_dtype),
                  grid=(groups * per, m // tm, nk), in_specs=[a_spec, b_spec], out_specs=o_spec,
                  scratch=[pltpu.VMEM((tm, tn), F32)] if nk > 1 else [],
                  sem=("parallel", "parallel", "arbitrary"))(a, b3)


def _mm_groups_contract(a3, b3, *, out_dtype=F32, name="mm_contract"):
    groups, m, nb = a3.shape
    n = b3.shape[1]
    assert b3.shape[0] == groups and b3.shape[2] == nb, (a3.shape, b3.shape)
    tm, tn, tk = _mm_tiles(m, n, nb, a3.dtype.itemsize, b3.dtype.itemsize, jnp.dtype(out_dtype).itemsize, 0,
                           accumulate=True)
    per = nb // tk
    nk = groups * per
    body = _mm_body(nk, (((1,), (1,)), ((), ())), False, out_dtype)
    a_spec = pl.BlockSpec((None, tm, tk), lambda i, j, kk: (kk // per, i, kk % per))
    b_spec = pl.BlockSpec((None, tn, tk), lambda i, j, kk: (kk // per, j, kk % per))
    return _pcall(body, name=name, out_shape=jax.ShapeDtypeStruct((m, n), out_dtype),
                  grid=(m // tm, n // tn, nk), in_specs=[a_spec, b_spec],
                  out_specs=pl.BlockSpec((tm, tn), lambda i, j, kk: (i, j)),
                  scratch=[pltpu.VMEM((tm, tn), F32)], sem=("parallel", "parallel", "arbitrary"))(a3, b3)


@functools.partial(jax.custom_vjp, nondiff_argnums=(3,))
def _linear_res(a, w, res, name):
    return _mm(a, w, res=res, name=name + "_fwd")


def _linear_res_fwd(a, w, res, name):
    return _mm(a, w, res=res, name=name + "_fwd"), (a, w)


def _linear_res_bwd(name, saved, g):
    a, w = saved
    gm = g.astype(_MXU_DTYPE)
    da = _mm(gm, w, tb=True, out_dtype=a.dtype, name=name + "_da")
    dw = _mm(a, gm, ta=True, out_dtype=w.dtype, name=name + "_dw")
    return da, dw, g


_linear_res.defvjp(_linear_res_fwd, _linear_res_bwd)


@functools.partial(jax.custom_vjp, nondiff_argnums=(2,))
def _linear(a, w, name):
    return _mm(a, w, name=name + "_fwd")


def _linear_fwd(a, w, name):
    return _mm(a, w, name=name + "_fwd"), (a, w)


def _linear_bwd(name, saved, g):
    a, w = saved
    gm = g.astype(_MXU_DTYPE)
    da = _mm(gm, w, tb=True, out_dtype=a.dtype, name=name + "_da")
    dw = _mm(a, gm, ta=True, out_dtype=w.dtype, name=name + "_dw")
    return da, dw


_linear.defvjp(_linear_fwd, _linear_bwd)


def linear(a, w, name, res=None):
    return _linear(a, w, name) if res is None else _linear_res(a, w, res, name)


def _tiled_specs(arrs, kinds, t, axis):
    specs = []
    for a, kind in zip(arrs, kinds):
        if kind == "whole":
            specs.append(pl.BlockSpec(a.shape, lambda i, nd=a.ndim: (0,) * nd))
        elif axis == 0:
            specs.append(pl.BlockSpec((t, a.shape[1]), lambda i: (i, 0)))
        else:
            specs.append(pl.BlockSpec((a.shape[0], t), lambda i: (0, i)))
    return specs


def tilewise(fn, name, args, kinds, outs, *, axis, t, diff):
    n_in = len(args)
    length = next(a.shape[axis] for a, kd in zip(args, kinds) if kd == "tile")
    steps = length // t
    assert steps * t == length, (name, length, t)

    def out_sds(other, dtype):
        return jax.ShapeDtypeStruct((length, other) if axis == 0 else (other, length), dtype)

    def out_spec(other):
        return pl.BlockSpec((t, other), lambda i: (i, 0)) if axis == 0 else pl.BlockSpec((other, t), lambda i: (0, i))

    def run_fwd(*xs):
        def body(*refs):
            vals = fn(*[r[...] for r in refs[:n_in]])
            for o_ref, v in zip(refs[n_in:], vals):
                o_ref[...] = v.astype(o_ref.dtype)

        return _pcall(body, name=name + "_fwd", out_shape=[out_sds(o, d) for o, d in outs], grid=(steps,),
                      in_specs=_tiled_specs(xs, kinds, t, axis), out_specs=[out_spec(o) for o, _ in outs],
                      sem=("parallel",))(*xs)

    didx = [i for i in range(n_in) if diff[i]]

    def run_bwd(xs, cts):
        def body(*refs):
            x_refs, c_refs, g_refs = refs[:n_in], refs[n_in:n_in + len(outs)], refs[n_in + len(outs):]
            vals = [r[...] for r in x_refs]

            def g(*dvals):
                full = list(vals)
                for i, v in zip(didx, dvals):
                    full[i] = v
                return tuple(fn(*full))

            prim_out, vjp = jax.vjp(g, *[vals[i] for i in didx])
            grads = vjp(tuple(c[...].astype(o.dtype) for c, o in zip(c_refs, prim_out)))
            step = pl.program_id(0)
            for i, g_ref, gr in zip(didx, g_refs, grads):
                if kinds[i] == "whole":
                    @pl.when(step == 0)
                    def _(g_ref=g_ref):
                        g_ref[...] = jnp.zeros_like(g_ref)

                    g_ref[...] += gr.astype(g_ref.dtype)
                else:
                    g_ref[...] = gr.astype(g_ref.dtype)

        g_shapes = [jax.ShapeDtypeStruct(xs[i].shape, xs[i].dtype) for i in didx]
        g_specs = _tiled_specs([xs[i] for i in didx], [kinds[i] for i in didx], t, axis)
        ct_specs = [out_spec(o) for o, _ in outs]
        any_whole = any(kinds[i] == "whole" for i in didx)
        return _pcall(body, name=name + "_bwd", out_shape=g_shapes, grid=(steps,),
                      in_specs=_tiled_specs(xs, kinds, t, axis) + ct_specs, out_specs=g_specs,
                      sem=("arbitrary" if any_whole else "parallel",))(*xs, *cts)

    @jax.custom_vjp
    def op(*xs):
        return tuple(run_fwd(*xs))

    def op_fwd(*xs):
        return tuple(run_fwd(*xs)), xs

    def op_bwd(xs, cts):
        grads = run_bwd(xs, cts)
        full = [None] * n_in
        for i, gr in zip(didx, grads):
            full[i] = gr
        return tuple(full)

    op.defvjp(op_fwd, op_bwd)
    return op(*args)


def _silu(x):
    return x * (1.0 / (1.0 + jnp.exp(-x)))


def _softplus(x):
    return jnp.maximum(x, 0.0) + jnp.log(1.0 + jnp.exp(-jnp.abs(x)))


def _rms_tile(x, gain, out_dtype):
    xf = x.astype(F32)
    y = xf * lax.rsqrt(jnp.mean(xf * xf, axis=-1, keepdims=True) + EPS)
    return (y * gain).astype(out_dtype)


def rms_norm(x, gain, name, out_dtype, tr=256):
    def fn(xt, gt):
        return (_rms_tile(xt, gt, out_dtype),)

    return tilewise(fn, name, [x, gain.reshape(1, -1)], ["tile", "whole"], [(x.shape[1], out_dtype)],
                    axis=0, t=_tile(x.shape[0], tr), diff=[True, True])[0]


def _shift_down_raw(x, s):
    rows = lax.broadcasted_iota(jnp.int32, x.shape, 0)
    return jnp.where(rows >= s, pltpu.roll(x, s, 0), 0.0)


def _shift_up_raw(x, s):
    n = x.shape[0]
    rows = lax.broadcasted_iota(jnp.int32, x.shape, 0)
    return jnp.where(rows < n - s, pltpu.roll(x, n - s, 0), 0.0)


@functools.partial(jax.custom_vjp, nondiff_argnums=(1,))
def _shift_down(x, s):
    return _shift_down_raw(x, s)


def _shift_down_fwd(x, s):
    return _shift_down_raw(x, s), None


def _shift_down_bwd(s, _, g):
    return (_shift_up_raw(g, s),)


_shift_down.defvjp(_shift_down_fwd, _shift_down_bwd)


def _causal_dwconv_tile(x, w):
    kk = w.shape[0]
    y = x * w[kk - 1:kk, :]
    for j in range(kk - 1):
        y = y + _shift_down(x, kk - 1 - j) * w[j:j + 1, :]
    return y


def dn_qkv(raw, conv_w, name):
    width = raw.shape[1] // 3
    is_qk = (jnp.arange(raw.shape[1]) < 2 * width).astype(F32).reshape(1, -1)

    def fn(x, w, flag):
        y = _silu(_causal_dwconv_tile(x, w))
        yn = y * lax.rsqrt(jnp.sum(y * y, axis=-1, keepdims=True) + EPS)
        return (jnp.where(flag > 0.5, yn, y),)

    return tilewise(fn, name, [raw, conv_w, is_qk], ["tile", "tile", "tile"], [(raw.shape[0], F32)],
                    axis=1, t=DN_HEAD_DIM, diff=[True, True, False])[0]


def dn_gates(b, a, a_log, dt_bias, name):
    hh = b.shape[1]
    tr = _tile(b.shape[0], 256)

    def fn(bt, at, al, db):
        beta = 1.0 / (1.0 + jnp.exp(-bt))
        g = -jnp.exp(al) * _softplus(at + db)
        pos = lax.broadcasted_iota(jnp.int32, g.shape, 0) & (CHUNK - 1)
        step = 1
        while step < CHUNK:
            g = g + jnp.where(pos >= step, _shift_down(g, step), 0.0)
            step *= 2
        return g, beta

    return tilewise(fn, name, [b, a, a_log.reshape(1, -1), dt_bias.reshape(1, -1)],
                    ["tile", "tile", "whole", "whole"], [(hh, F32), (hh, F32)], axis=0, t=tr, diff=[True] * 4)


def _unit_lower_inv_raw(low):
    c = low.shape[-1]
    ii = lax.broadcasted_iota(jnp.int32, (c, c), 0)
    jj = lax.broadcasted_iota(jnp.int32, (c, c), 1)
    inv = (ii == jj).astype(F32)[None] - low
    p = low
    n = 1
    while 2 * n < c:
        p = _ein_raw("hij,hjk->hik", p, p, True)
        inv = inv + _ein_raw("hij,hjk->hik", inv, p, True)
        n *= 2
    return inv


@jax.custom_vjp
def _unit_lower_inv(low):
    return _unit_lower_inv_raw(low)


def _unit_lower_inv_fwd(low):
    inv = _unit_lower_inv_raw(low)
    return inv, inv


def _unit_lower_inv_bwd(inv, g):
    t = _ein_raw("hji,hjk->hik", inv, g, True)
    return (-_ein_raw("hik,hlk->hil", t, inv, True),)


_unit_lower_inv.defvjp(_unit_lower_inv_fwd, _unit_lower_inv_bwd)


def _dn_chunk(state, q, k, v, gc, gr, gl, bc):
    c = q.shape[1]
    q = q * (q.shape[-1] ** -0.5)
    ii = lax.broadcasted_iota(jnp.int32, (c, c), 0)
    jj = lax.broadcasted_iota(jnp.int32, (c, c), 1)
    incl = (jj <= ii)[None]
    strict = (jj < ii)[None]
    decay = jnp.where(incl, jnp.exp(jnp.where(incl, gc - gr, 0.0)), 0.0)
    kb = k * bc
    low = jnp.where(strict, _bnt(kb, k) * decay, 0.0)
    ainv = _unit_lower_inv(low)
    eg = jnp.exp(gc)
    u = _bnn(ainv, v * bc)
    w = _bnn(ainv, kb * eg)
    attn = _bnt(q, k) * decay
    q_dec = q * eg
    k_dec = k * jnp.exp(gl - gc)
    v_new = u - _bnn(w, state)
    o = _bnn(q_dec, state) + _bnn(attn, v_new)
    new_state = state * jnp.exp(gl) + _btn(k_dec, v_new)
    return new_state, o


def _dn_heads(ref, base, hh):
    return jnp.stack([ref[:, base + h * DN_HEAD_DIM: base + (h + 1) * DN_HEAD_DIM] for h in range(hh)])


def _dn_fwd_call(qkv, gc, gr, gl, bc, name):
    s, w3 = qkv.shape
    width = w3 // 3
    hh = width // DN_HEAD_DIM
    d = DN_HEAD_DIM
    n = s // CHUNK

    def body(qkv_ref, gc_ref, gr_ref, gl_ref, bc_ref, o_ref, st_ref, state):
        @pl.when(pl.program_id(0) == 0)
        def _():
            state[...] = jnp.zeros_like(state)

        s_in = state[...]
        st_ref[0] = s_in
        new_s, o = _dn_chunk(s_in, _dn_heads(qkv_ref, 0, hh), _dn_heads(qkv_ref, width, hh),
                             _dn_heads(qkv_ref, 2 * width, hh), gc_ref[0], gr_ref[0], gl_ref[0], bc_ref[0])
        state[...] = new_s
        for h in range(hh):
            o_ref[:, h * d:(h + 1) * d] = o[h]

    g4 = lambda i: (i, 0, 0, 0)
    return _pcall(
        body, name=name + "_fwd",
        out_shape=[jax.ShapeDtypeStruct((s, width), F32), jax.ShapeDtypeStruct((n, hh, d, d), F32)],
        grid=(n,),
        in_specs=[pl.BlockSpec((CHUNK, w3), lambda i: (i, 0)), pl.BlockSpec((1, hh, CHUNK, 1), g4),
                  pl.BlockSpec((1, hh, 1, CHUNK), g4), pl.BlockSpec((1, hh, 1, 1), g4),
                  pl.BlockSpec((1, hh, CHUNK, 1), g4)],
        out_specs=[pl.BlockSpec((CHUNK, width), lambda i: (i, 0)), pl.BlockSpec((1, hh, d, d), g4)],
        scratch=[pltpu.VMEM((hh, d, d), F32)], sem=("arbitrary",))(qkv, gc, gr, gl, bc)


def _dn_bwd_call(qkv, gc, gr, gl, bc, states, do, name):
    s, w3 = qkv.shape
    width = w3 // 3
    hh = width // DN_HEAD_DIM
    d = DN_HEAD_DIM
    n = s // CHUNK

    def body(qkv_ref, gc_ref, gr_ref, gl_ref, bc_ref, st_ref, do_ref,
             dqkv_ref, dgc_ref, dgr_ref, dgl_ref, dbc_ref, dstate):
        @pl.when(pl.program_id(0) == 0)
        def _():
            dstate[...] = jnp.zeros_like(dstate)

        prim = (st_ref[0], _dn_heads(qkv_ref, 0, hh), _dn_heads(qkv_ref, width, hh),
                _dn_heads(qkv_ref, 2 * width, hh), gc_ref[0], gr_ref[0], gl_ref[0], bc_ref[0])
        _, vjp = jax.vjp(_dn_chunk, *prim)
        ds, dq, dk, dv, dgc, dgr, dgl, dbc = vjp((dstate[...], _dn_heads(do_ref, 0, hh)))
        dstate[...] = ds
        for h in range(hh):
            dqkv_ref[:, h * d:(h + 1) * d] = dq[h]
            dqkv_ref[:, width + h * d: width + (h + 1) * d] = dk[h]
            dqkv_ref[:, 2 * width + h * d: 2 * width + (h + 1) * d] = dv[h]
        dgc_ref[0] = dgc
        dgr_ref[0] = dgr
        dgl_ref[0] = dgl
        dbc_ref[0] = dbc

    r2 = lambda i: (n - 1 - i, 0)
    r4 = lambda i: (n - 1 - i, 0, 0, 0)
    spec_c = pl.BlockSpec((1, hh, CHUNK, 1), r4)
    spec_r = pl.BlockSpec((1, hh, 1, CHUNK), r4)
    spec_l = pl.BlockSpec((1, hh, 1, 1), r4)
    return _pcall(
        body, name=name + "_bwd",
        out_shape=[jax.ShapeDtypeStruct(qkv.shape, F32), jax.ShapeDtypeStruct(gc.shape, F32),
                   jax.ShapeDtypeStruct(gr.shape, F32), jax.ShapeDtypeStruct(gl.shape, F32),
                   jax.ShapeDtypeStruct(bc.shape, F32)],
        grid=(n,),
        in_specs=[pl.BlockSpec((CHUNK, w3), r2), spec_c, spec_r, spec_l, spec_c,
                  pl.BlockSpec((1, hh, d, d), r4), pl.BlockSpec((CHUNK, width), r2)],
        out_specs=[pl.BlockSpec((CHUNK, w3), r2), spec_c, spec_r, spec_l, spec_c],
        scratch=[pltpu.VMEM((hh, d, d), F32)], sem=("arbitrary",))(qkv, gc, gr, gl, bc, states, do)


@functools.partial(jax.custom_vjp, nondiff_argnums=(5,))
def dn_core(qkv, gc, gr, gl, bc, name):
    return _dn_fwd_call(qkv, gc, gr, gl, bc, name)[0]


def _dn_core_fwd(qkv, gc, gr, gl, bc, name):
    o, states = _dn_fwd_call(qkv, gc, gr, gl, bc, name)
    return o, (qkv, gc, gr, gl, bc, states)


def _dn_core_bwd(name, saved, do):
    return tuple(_dn_bwd_call(*saved, do, name))


dn_core.defvjp(_dn_core_fwd, _dn_core_bwd)


def dn_out_gate(o, z, out_norm, name):
    hh = o.shape[1] // DN_HEAD_DIM
    gain = jnp.tile(out_norm.reshape(1, -1), (1, hh))

    def fn(ot, zt, gt):
        y = ot * lax.rsqrt(jnp.mean(ot * ot, axis=-1, keepdims=True) + EPS) * gt
        return (y * _silu(zt),)

    return tilewise(fn, name, [o, z, gain], ["tile", "tile", "tile"], [(o.shape[0], BF16)],
                    axis=1, t=DN_HEAD_DIM, diff=[True, True, True])[0]


def _attn_tile(qn, kn, v, qp, kp, q0, scale, causal):
    s = _nt(qn, kn)
    if qp is not None:
        s = s + _nt(qp, kp)
    s = s * scale
    if causal:
        qpos = q0 + lax.broadcasted_iota(jnp.int32, s.shape, 0)
        kpos = lax.broadcasted_iota(jnp.int32, s.shape, 1)
        s = jnp.where((kpos >> _CHUNK_SHIFT) <= (qpos >> _CHUNK_SHIFT), s, -1e30)
    e = jnp.exp(s - jnp.max(s, axis=-1, keepdims=True))
    p = e / jnp.sum(e, axis=-1, keepdims=True)
    return _nn(p, v)


def _attn_specs(sq, sk, dh, dp, tb):
    q_spec = pl.BlockSpec((tb, dh), lambda h, i: (i, h))
    kv_spec = pl.BlockSpec((sk, dh), lambda h, i: (0, h))
    qp_spec = pl.BlockSpec((1, tb, dp), lambda h, i: (h, i, 0)) if dp else None
    kp_spec = pl.BlockSpec((sk, dp), lambda h, i: (0, 0)) if dp else None
    return q_spec, kv_spec, qp_spec, kp_spec


def _attn_plan(sq, sk, tq, causal):
    tb = sq if causal else tq
    subs = [(slice(i * tq, (i + 1) * tq), (i + 1) * tq if causal else sk) for i in range(tb // tq)]
    return tb, subs


def _attn_fwd_call(q, k, v, qp, kp, *, dh, scale, causal, tq, name):
    sq, sk = q.shape[0], k.shape[0]
    heads = q.shape[1] // dh
    dp = qp.shape[-1] if qp is not None else 0
    tb, subs = _attn_plan(sq, sk, tq, causal)
    q_spec, kv_spec, qp_spec, kp_spec = _attn_specs(sq, sk, dh, dp, tb)

    def body(*refs):
        q_ref, k_ref, v_ref = refs[:3]
        qp_ref, kp_ref = (refs[3], refs[4]) if dp else (None, None)
        o_ref = refs[-1]
        row0 = pl.program_id(1) * tb
        for rows, ke in subs:
            o = _attn_tile(q_ref[rows, :], k_ref[:ke, :], v_ref[:ke, :], qp_ref[0, rows, :] if dp else None,
                           kp_ref[:ke, :] if dp else None, row0 + rows.start, scale, causal)
            o_ref[rows, :] = o.astype(o_ref.dtype)

    in_specs = [q_spec, kv_spec, kv_spec] + ([qp_spec, kp_spec] if dp else [])
    args = (q, k, v) + ((qp, kp) if dp else ())
    return _pcall(body, name=name + "_fwd", out_shape=jax.ShapeDtypeStruct((sq, heads * dh), BF16),
                  grid=(heads, sq // tb), in_specs=in_specs, out_specs=q_spec,
                  sem=("parallel", "parallel"))(*args)


def _attn_bwd_call(q, k, v, qp, kp, do, *, dh, scale, causal, tq, name):
    sq, sk = q.shape[0], k.shape[0]
    heads = q.shape[1] // dh
    dp = qp.shape[-1] if qp is not None else 0
    tb, subs = _attn_plan(sq, sk, tq, causal)
    q_spec, kv_spec, qp_spec, kp_spec = _attn_specs(sq, sk, dh, dp, tb)

    def body(*refs):
        h, i = pl.program_id(0), pl.program_id(1)
        if dp:
            q_ref, k_ref, v_ref, qp_ref, kp_ref, do_ref, dq_ref, dk_ref, dv_ref, dqp_ref, dkp_ref = refs
        else:
            q_ref, k_ref, v_ref, do_ref, dq_ref, dk_ref, dv_ref = refs

        @pl.when(i == 0)
        def _():
            dk_ref[...] = jnp.zeros_like(dk_ref)
            dv_ref[...] = jnp.zeros_like(dv_ref)

        if dp:
            @pl.when(jnp.logical_and(h == 0, i == 0))
            def _():
                dkp_ref[...] = jnp.zeros_like(dkp_ref)

        for rows, ke in subs:
            q0 = i * tb + rows.start
            if dp:
                prim = (q_ref[rows, :], k_ref[:ke, :], v_ref[:ke, :], qp_ref[0, rows, :], kp_ref[:ke, :])
                f = lambda a, b, c, d, e, q0=q0: _attn_tile(a, b, c, d, e, q0, scale, causal)
            else:
                prim = (q_ref[rows, :], k_ref[:ke, :], v_ref[:ke, :])
                f = lambda a, b, c, q0=q0: _attn_tile(a, b, c, None, None, q0, scale, causal)
            _, vjp = jax.vjp(f, *prim)
            grads = vjp(do_ref[rows, :].astype(F32))
            dq_ref[rows, :] = grads[0]
            dk_ref[:ke, :] += grads[1]
            dv_ref[:ke, :] += grads[2]
            if dp:
                dqp_ref[0, rows, :] = grads[3]
                dkp_ref[:ke, :] += grads[4]

    in_specs = [q_spec, kv_spec, kv_spec] + ([qp_spec, kp_spec] if dp else []) + [q_spec]
    out_shape = [jax.ShapeDtypeStruct(q.shape, F32), jax.ShapeDtypeStruct(k.shape, F32),
                 jax.ShapeDtypeStruct(v.shape, F32)]
    out_specs = [q_spec, kv_spec, kv_spec]
    if dp:
        out_shape += [jax.ShapeDtypeStruct(qp.shape, F32), jax.ShapeDtypeStruct(kp.shape, F32)]
        out_specs += [qp_spec, kp_spec]
    args = (q, k, v) + ((qp, kp) if dp else ()) + (do,)
    return _pcall(body, name=name + "_bwd", out_shape=out_shape, grid=(heads, sq // tb), in_specs=in_specs,
                  out_specs=out_specs, sem=("arbitrary", "arbitrary"))(*args)


def attention(q, k, v, qp=None, kp=None, *, dh, scale, causal, name, tq=256):
    tq = _tile(q.shape[0], tq)
    kw = dict(dh=dh, scale=scale, causal=causal, tq=tq, name=name)
    has_pe = qp is not None

    @jax.custom_vjp
    def op(*xs):
        return _attn_fwd_call(*xs, **kw) if has_pe else _attn_fwd_call(*xs, None, None, **kw)

    def op_fwd(*xs):
        return (_attn_fwd_call(*xs, **kw) if has_pe else _attn_fwd_call(*xs, None, None, **kw)), xs

    def op_bwd(xs, do):
        full = xs if has_pe else xs + (None, None)
        return tuple(_attn_bwd_call(*full, do, **kw))

    op.defvjp(op_fwd, op_bwd)
    return op(q, k, v, qp, kp) if has_pe else op(q, k, v)


def _rope_tables(positions, reps):
    half = MLA_ROPE // 2
    inv = ROPE_BASE ** (-jnp.arange(0, MLA_ROPE, 2, dtype=F32) / MLA_ROPE)
    ang = positions.astype(F32)[:, None] * inv
    cos, sin = jnp.cos(ang), jnp.sin(ang)
    c = jnp.tile(jnp.concatenate([cos, cos], axis=-1), (1, reps))
    s = jnp.tile(jnp.concatenate([sin, sin], axis=-1), (1, reps))
    rot = np.zeros((MLA_ROPE, MLA_ROPE), np.float32)
    for i in range(half):
        rot[i + half, i] = -1.0
        rot[i, i + half] = 1.0
    return c, s, jnp.asarray(np.kron(np.eye(reps, dtype=np.float32), rot))


def rope(x, positions, name):
    c, s, rot = _rope_tables(positions, x.shape[1] // MLA_ROPE)

    def fn(xt, ct, st, rt):
        return (xt * ct + _nn_hi_const(xt, rt) * st,)

    return tilewise(fn, name, [x, c, s, rot], ["tile", "tile", "tile", "whole"], [(x.shape[1], F32)],
                    axis=0, t=_tile(x.shape[0], 256), diff=[True, False, False, False])[0]


def _ffn_gate_tile(g, u, wg, wu, bg, bu):
    return _silu(_causal_dwconv_tile(g, wg) + bg) * (_causal_dwconv_tile(u, wu) + bu)


def _ffn_gate_specs(pre4, conv4):
    _, groups, s, nb = pre4.shape
    tc = LANE if nb % LANE == 0 else nb
    per = nb // tc
    at = lambda g, j: (0, g, 0, j)
    specs = dict(pre=pl.BlockSpec((2, None, s, tc), at), conv=pl.BlockSpec((2, None, conv4.shape[2], tc), at),
                 bias=pl.BlockSpec((2, None, 1, tc), at), act=pl.BlockSpec((s, tc), lambda g, j: (0, g * per + j)))
    return specs, (groups, per)


def _ffn_gate_fwd_call(pre4, conv4, bias4, name):
    specs, grid = _ffn_gate_specs(pre4, conv4)
    _, groups, s, nb = pre4.shape

    def body(pre_ref, conv_ref, bias_ref, act_ref):
        act_ref[...] = _ffn_gate_tile(pre_ref[0], pre_ref[1], conv_ref[0], conv_ref[1], bias_ref[0],
                                      bias_ref[1]).astype(act_ref.dtype)

    return _pcall(body, name=name + "_fwd", out_shape=jax.ShapeDtypeStruct((s, groups * nb), BF16), grid=grid,
                  in_specs=[specs["pre"], specs["conv"], specs["bias"]], out_specs=specs["act"],
                  sem=("parallel", "parallel"))(pre4, conv4, bias4)


def _ffn_gate_bwd_call(pre4, conv4, bias4, dact, name):
    specs, grid = _ffn_gate_specs(pre4, conv4)

    def body(pre_ref, conv_ref, bias_ref, dact_ref, dpre_ref, dconv_ref, dbias_ref):
        prim = (pre_ref[0], pre_ref[1], conv_ref[0], conv_ref[1], bias_ref[0], bias_ref[1])
        _, vjp = jax.vjp(_ffn_gate_tile, *prim)
        dg, du, dwg, dwu, dbg, dbu = vjp(dact_ref[...].astype(F32))
        dpre_ref[0] = dg.astype(dpre_ref.dtype)
        dpre_ref[1] = du.astype(dpre_ref.dtype)
        dconv_ref[0] = dwg
        dconv_ref[1] = dwu
        dbias_ref[0] = dbg
        dbias_ref[1] = dbu

    return _pcall(body, name=name + "_bwd",
                  out_shape=[jax.ShapeDtypeStruct(pre4.shape, _MXU_DTYPE), jax.ShapeDtypeStruct(conv4.shape, F32),
                             jax.ShapeDtypeStruct(bias4.shape, F32)],
                  grid=grid, in_specs=[specs["pre"], specs["conv"], specs["bias"], specs["act"]],
                  out_specs=[specs["pre"], specs["conv"], specs["bias"]], sem=("parallel", "parallel"))(
                      pre4, conv4, bias4, dact)


@functools.partial(jax.custom_vjp, nondiff_argnums=(4,))
def ffn_up_gate(hn, w3, conv3, bias, name):
    return _ffn_up_gate_fwd(hn, w3, conv3, bias, name)[0]


def _ffn_up_gate_fwd(hn, w3, conv3, bias, name):
    groups, _, nb = w3.shape
    half = groups // 2
    pre4 = _mm_groups_out(hn, w3, name=name + "_up_fwd").reshape(2, half, hn.shape[0], nb)
    conv4 = conv3.reshape(2, half, conv3.shape[1], nb)
    bias4 = bias.reshape(2, half, 1, nb)
    return _ffn_gate_fwd_call(pre4, conv4, bias4, name + "_gate"), (hn, w3, pre4, conv4, bias4)


def _ffn_up_gate_bwd(name, saved, dact):
    hn, w3, pre4, conv4, bias4 = saved
    dpre4, dconv4, dbias4 = _ffn_gate_bwd_call(pre4, conv4, bias4, dact, name + "_gate")
    dpre3 = dpre4.reshape((w3.shape[0],) + dpre4.shape[2:])
    dw3 = _mm_groups_out(hn, dpre3, ta=True, out_dtype=w3.dtype, name=name + "_up_dw")
    dhn = _mm_groups_contract(dpre3, w3, out_dtype=hn.dtype, name=name + "_up_da")
    return dhn, dw3, dconv4.reshape((w3.shape[0],) + dconv4.shape[2:]), dbias4.reshape(-1)


ffn_up_gate.defvjp(_ffn_up_gate_fwd, _ffn_up_gate_bwd)


def loss_rows(h, gain, target, name):
    def fn(ht, tt, gt):
        err = _rms_tile(ht, gt, F32) - tt
        return (0.5 * jnp.mean(err * err, axis=-1, keepdims=True),)

    return tilewise(fn, name, [h, target, gain.reshape(1, -1)], ["tile", "tile", "whole"], [(1, F32)],
                    axis=0, t=_tile(h.shape[0], 256), diff=[True, False, True])[0]


def _col_groups(w, per_head, lo, hi):
    r = w.shape[0]
    return w.reshape(r, -1, per_head)[:, :, lo:hi].reshape(r, -1)


def _layer_mix(h, lw, rp, l, positions):
    s, d = h.shape
    width = d // 2
    dn_heads = width // DN_HEAD_DIM
    mla_heads = (d - width) // MLA_V
    nm = f"l{l}_"
    w_in = lw["w_in"]
    c0 = 4 * width
    c1 = c0 + 2 * dn_heads
    c2 = c1 + MLA_Q_RANK
    c3 = c2 + MLA_KV_RANK + MLA_ROPE
    rest_pad = (-(c3 - c0)) % LANE
    w_rest = jnp.pad(w_in[:, c0:c3], ((0, 0), (0, rest_pad)))

    u = rms_norm(h, rp["norm_mix"], nm + "norm_mix", BF16)
    qkv_raw = linear(u, w_in[:, :3 * width], nm + "in_qkv")
    z = linear(u, w_in[:, 3 * width:c0], nm + "in_z")
    rest = linear(u, w_rest, nm + "in_rest")

    qkv = dn_qkv(qkv_raw, lw["dn_conv"], nm + "dn_qkv")
    csum, beta = dn_gates(rest[:, :dn_heads], rest[:, dn_heads:2 * dn_heads], rp["dn_a_log"],
                          rp["dn_dt_bias"], nm + "dn_gates")
    n = s // CHUNK
    g3 = csum.reshape(n, CHUNK, dn_heads).transpose(0, 2, 1)
    b3 = beta.reshape(n, CHUNK, dn_heads).transpose(0, 2, 1)
    o_dn = dn_core(qkv, g3[..., None], g3[:, :, None, :], g3[:, :, CHUNK - 1][..., None, None], b3[..., None],
                   nm + "dn_core")
    o_dn = dn_out_gate(o_dn, z, rp["dn_out_norm"], nm + "dn_gate")

    mq = rest[:, c1 - c0:c2 - c0]
    mkv = rest[:, c2 - c0:c3 - c0]
    qn = rms_norm(mq, rp["mla_q_norm"], nm + "mla_qnorm", BF16)
    per_q = MLA_NOPE + MLA_ROPE
    q_nope = linear(qn, _col_groups(lw["mla_w_qb"], per_q, 0, MLA_NOPE), nm + "mla_qn")
    q_pe = linear(qn, _col_groups(lw["mla_w_qb"], per_q, MLA_NOPE, per_q), nm + "mla_qp")
    kvn = rms_norm(mkv[:, :MLA_KV_RANK], rp["mla_kv_norm"], nm + "mla_kvnorm", BF16)
    per_kv = MLA_NOPE + MLA_V
    k_nope = linear(kvn, _col_groups(lw["mla_w_kvb"], per_kv, 0, MLA_NOPE), nm + "mla_kn")
    v_mla = linear(kvn, _col_groups(lw["mla_w_kvb"], per_kv, MLA_NOPE, per_kv), nm + "mla_v")
    q_pe = rope(q_pe, positions, nm + "rope_q")
    k_pe = rope(mkv[:, MLA_KV_RANK:], positions, nm + "rope_k")
    q_pe = q_pe.reshape(s, mla_heads, MLA_ROPE).transpose(1, 0, 2)
    o_mla = attention(q_nope, k_nope, v_mla, q_pe, k_pe, dh=MLA_NOPE, scale=per_q ** -0.5, causal=True,
                      name=nm + "mla_attn")

    h = linear(o_dn, lw["w_out"][:width], nm + "out_dn", res=h)
    return linear(o_mla, lw["w_out"][width:], nm + "out_mla", res=h)


def _layer_tail(h, lw, rp, l, mem_n):
    d = h.shape[1]
    nm = f"l{l}_"
    hn = rms_norm(h, rp["norm_xattn"], nm + "norm_xattn", BF16)
    xq = linear(hn, lw["xa_wq"], nm + "xa_q")
    xk = linear(mem_n, lw["xa_wk"], nm + "xa_k")
    xv = linear(mem_n, lw["xa_wv"], nm + "xa_v")
    xdh = d // XA_HEADS
    xo = attention(xq, xk, xv, dh=xdh, scale=xdh ** -0.5, causal=False, name=nm + "xattn")
    h = linear(xo, lw["xa_wo"], nm + "xa_o", res=h)

    hn = rms_norm(h, rp["norm_ffn"], nm + "norm_ffn", BF16)
    act = ffn_up_gate(hn, lw["ffn_w_up"], lw["ffn_conv"], rp["ffn_conv_bias"], nm + "ffn")
    return linear(act, lw["ffn_w_down"], nm + "ffn_down", res=h)


_COL_SHARDED = ("w_in", "mla_w_qb", "mla_w_kvb", "ffn_w_up", "dn_conv", "ffn_conv")
_ROW_SHARDED = ("w_out", "xa_wq", "xa_wk", "xa_wv", "xa_wo", "ffn_w_down")
_BIG = ("w_in", "mla_w_qb", "mla_w_kvb", "w_out", "xa_wq", "xa_wk", "xa_wv", "xa_wo", "ffn_w_up", "ffn_w_down")
_MIX_KEYS = ("w_in", "mla_w_qb", "mla_w_kvb", "w_out")
_TAIL_KEYS = ("xa_wq", "xa_wk", "xa_wv", "xa_wo", "ffn_w_up", "ffn_w_down")
_SMALL_SHARDED = ("dn_conv", "ffn_conv")
_REP_MIX = ("norm_mix", "dn_a_log", "dn_dt_bias", "dn_out_norm", "mla_q_norm", "mla_kv_norm")
_REP_TAIL = ("norm_xattn", "norm_ffn", "ffn_conv_bias")
_REPLICATED = ("norm_mix", "dn_a_log", "dn_dt_bias", "dn_out_norm", "mla_q_norm", "mla_kv_norm", "mem_norm",
               "norm_xattn", "norm_ffn", "ffn_conv_bias", "norm_final")
_WEIGHTS = ("norm_mix", "w_in", "dn_conv", "dn_a_log", "dn_dt_bias", "dn_out_norm", "mla_q_norm", "mla_w_qb",
            "mla_kv_norm", "mla_w_kvb", "w_out", "mem_norm", "norm_xattn", "xa_wq", "xa_wk", "xa_wv", "xa_wo",
            "norm_ffn", "ffn_w_up", "ffn_conv", "ffn_conv_bias", "ffn_w_down", "norm_final")


def _assemble(name, g):
    if name in _ROW_SHARDED:
        return g.reshape((-1,) + g.shape[2:])
    return jnp.moveaxis(g, 0, -2).reshape(g.shape[1:-1] + (-1,))


def _mix(h, gw, conv, rp, l, positions):
    lw = {k: _assemble(k, v) for k, v in gw.items()}
    lw["dn_conv"] = _assemble("dn_conv", conv)
    return _layer_mix(h, lw, rp, l, positions)


def _tail(h, gw, conv, rp, l, mem_n):
    lw = {k: v if k == "ffn_w_up" else _assemble(k, v) for k, v in gw.items()}
    lw["ffn_conv"] = conv
    return _layer_tail(h, lw, rp, l, mem_n)


def _my_index():
    return 4 * lax.axis_index("x") + 2 * lax.axis_index("y") + lax.axis_index("c")


def _peer(k):
    x, y, c = lax.axis_index("x"), lax.axis_index("y"), lax.axis_index("c")
    px = (1 - x) if k & 4 else x
    py = (1 - y) if k & 2 else y
    pc = (1 - c) if k & 1 else c
    return (px, py, pc), 4 * px + 2 * py + pc


_ANY = pl.BlockSpec(memory_space=pl.ANY)


def all_gather(shards, name):
    n = len(shards)

    def body(*refs):
        x_refs, o_refs = refs[:n], refs[n:2 * n]
        send_sems, recv_sems, local_sems = refs[2 * n:]
        me = _my_index()
        sib_id, sib = _peer(1)
        chips = [_peer(k) for k in (4, 2, 6)]

        def copy(a, k, block, to, src=None):
            return pltpu.make_async_remote_copy(
                src_ref=o_refs[a].at[block] if src is None else src, dst_ref=o_refs[a].at[block],
                send_sem=send_sems.at[a * 7 + k], recv_sem=recv_sems.at[a * 7 + k],
                device_id=to, device_id_type=MESH)

        mine = [pltpu.make_async_copy(x_refs[a], o_refs[a].at[me], local_sems.at[a]) for a in range(n)]
        for cp in mine:
            cp.start()
        first = []
        for a in range(n):
            first.append(copy(a, 0, me, sib_id, src=x_refs[a]))
            first += [copy(a, 1 + j, me, cid, src=x_refs[a]) for j, (cid, _) in enumerate(chips)]
        for cp in first:
            cp.start()
        passed = []
        for a in range(n):
            for j, (cid, cidx) in enumerate(chips):
                copy(a, 1 + j, cidx, cid).wait_recv()
                fwd = copy(a, 4 + j, cidx, sib_id)
                fwd.start()
                passed.append(fwd)
        for a in range(n):
            copy(a, 0, sib, sib_id).wait_recv()
            for j, (_, cidx) in enumerate(chips):
                copy(a, 4 + j, cidx ^ 1, sib_id).wait_recv()
        for cp in first + passed:
            cp.wait_send()
        for cp in mine:
            cp.wait()

    out_shape = [jax.ShapeDtypeStruct((N_DEV,) + s.shape, s.dtype) for s in shards]
    return _pcall(body, name=name, out_shape=out_shape, in_specs=[_ANY] * n, out_specs=[_ANY] * n,
                  scratch=[pltpu.SemaphoreType.DMA((7 * n,)), pltpu.SemaphoreType.DMA((7 * n,)),
                           pltpu.SemaphoreType.DMA((n,))])(*shards)


_HBM = pl.BlockSpec(memory_space=pltpu.HBM)
_SEM = pl.BlockSpec(memory_space=pltpu.SEMAPHORE)
_EFFECT = pltpu.SideEffectType.DATAFLOW_SIDE_EFFECTING


def _plan_gather_first(n):
    def plan(refs):
        me = _my_index()
        sib_id, sib = _peer(1)
        chips = [_peer(k) for k in (4, 2, 6)]
        out = []
        for a in range(n):
            x, land = refs[a], refs[n + a]
            out.append((x, land.at[me], sib_id, land.at[sib]))
            out += [(x, land.at[me], cid, land.at[cidx]) for cid, cidx in chips]
        return out

    return plan, 4 * n


def _plan_gather_pass(n):
    def plan(refs):
        sib_id, _ = _peer(1)
        chips = [_peer(k) for k in (4, 2, 6)]
        return [(refs[a].at[cidx], refs[a].at[cidx], sib_id, refs[a].at[cidx ^ 1])
                for a in range(n) for _, cidx in chips]

    return plan, 3 * n


def _plan_gather_direct(n):
    def plan(refs):
        me = _my_index()
        out = []
        for a in range(n):
            for k in range(1, N_DEV):
                pid, pidx = _peer(k)
                out.append((refs[a], refs[n + a].at[me], pid, refs[n + a].at[pidx]))
        return out

    return plan, 7 * n


def _plan_scatter(n):
    def plan(refs):
        me = _my_index()
        out = []
        for a in range(n):
            for k in range(1, N_DEV):
                pid, pidx = _peer(k)
                out.append((refs[a].at[pidx], refs[n + a].at[me], pid, refs[n + a].at[pidx]))
        return out

    return plan, 7 * n


def _remote_copy(src, dst, send_sems, recv_sems, i, dev):
    return pltpu.make_async_remote_copy(src_ref=src, dst_ref=dst, send_sem=send_sems.at[i], recv_sem=recv_sems.at[i],
                                        device_id=dev, device_id_type=MESH)


def _after(x, tokens):
    for token in tokens:
        x = x + token[0, 0]
    return x


def exchange_start(bufs, plan_n, name, after=None):
    plan, n = plan_n
    nb = len(bufs)
    n_in = nb + (after is not None)

    def body(*refs):
        send_sems, recv_sems = refs[n_in], refs[n_in + 1]
        for i, (src, dst, dev, _) in enumerate(plan(refs[:nb])):
            _remote_copy(src, dst, send_sems, recv_sems, i, dev).start()
        refs[-1][...] = jnp.zeros_like(refs[-1])

    out_shape = (pltpu.SemaphoreType.DMA((n,)), pltpu.SemaphoreType.DMA((n,)),
                 *[pltpu.HBM(b.shape, b.dtype) for b in bufs], jax.ShapeDtypeStruct((8, LANE), F32))
    args = [pltpu.with_memory_space_constraint(b, pltpu.HBM) for b in bufs] + ([after] if after is not None else [])
    res = pl.pallas_call(
        body, name=name, out_shape=out_shape,
        in_specs=[_HBM] * nb + ([_ANY] if after is not None else []),
        out_specs=(_SEM, _SEM, *[_HBM] * nb, pl.BlockSpec(memory_space=pltpu.VMEM)),
        input_output_aliases={i: 2 + i for i in range(nb)},
        compiler_params=pltpu.CompilerParams(has_side_effects=_EFFECT), interpret=_INTERPRET)(*args)
    return (res[0], res[1], list(res[2:2 + nb])), res[-1]


def exchange_wait(flight, plan_n, name, after):
    plan, _ = plan_n
    send_sems, recv_sems, bufs = flight
    nb = len(bufs)

    def body(*refs):
        s_sems, r_sems = refs[nb], refs[nb + 1]
        for i, (src, _, dev, arrival) in enumerate(plan(refs[:nb])):
            cp = _remote_copy(src, arrival, s_sems, r_sems, i, dev)
            cp.wait_send()
            cp.wait_recv()

    return list(pl.pallas_call(
        body, name=name, out_shape=tuple(pltpu.HBM(b.shape, b.dtype) for b in bufs),
        in_specs=[_HBM] * nb + [_SEM, _SEM, _ANY], out_specs=tuple([_HBM] * nb),
        input_output_aliases={i: i for i in range(nb)},
        compiler_params=pltpu.CompilerParams(has_side_effects=_EFFECT), interpret=_INTERPRET)(
            *bufs, send_sems, recv_sems, after))


def _adam_tile(contribs, w, m, v):
    g = contribs[0].astype(F32)
    for part in contribs[1:]:
        g = g + part.astype(F32)
    m_new = ADAM_B1 * m + (1.0 - ADAM_B1) * g
    v_new = ADAM_B2 * v + (1.0 - ADAM_B2) * (g * g)
    m_hat = m_new / (1.0 - ADAM_B1 ** ADAM_STEP)
    v_hat = v_new / (1.0 - ADAM_B2 ** ADAM_STEP)
    return g, -ADAM_LR * (m_hat / (jnp.sqrt(v_hat) + ADAM_EPS) + ADAM_WD * w), m_new, v_new


def adam_update(recv, w, m, v, name):
    ll, _, r, c = recv.shape
    tr = _tile(r, max(8, 1 << ((_ADAM_TILE_ELEMS // c).bit_length() - 1)))

    def body(g_ref, w_ref, m_ref, v_ref, go_ref, d_ref, mo_ref, vo_ref):
        go_ref[0], d_ref[0], mo_ref[0], vo_ref[0] = _adam_tile([g_ref[0, j] for j in range(N_DEV)], w_ref[0],
                                                               m_ref[0], v_ref[0])

    spec = pl.BlockSpec((1, tr, c), lambda l, i: (l, i, 0))
    sds = jax.ShapeDtypeStruct((ll, r, c), F32)
    return _pcall(body, name=name, out_shape=[sds] * 4, grid=(ll, r // tr),
                  in_specs=[pl.BlockSpec((1, N_DEV, tr, c), lambda l, i: (l, 0, i, 0)), spec, spec, spec],
                  out_specs=[spec] * 4, sem=("parallel", "parallel"))(recv, w, m, v)


def adam_layer(recv, w, m, v, prev, l, name):
    _, r, c = recv.shape
    tr = _tile(r, max(8, 1 << ((_ADAM_TILE_ELEMS // c).bit_length() - 1)))
    if prev is None:
        prev = [lax.empty(w.shape, F32) for _ in range(4)]

    def body(g_ref, w_ref, m_ref, v_ref, *rest):
        go_ref, d_ref, mo_ref, vo_ref = rest[4:]
        go_ref[0], d_ref[0], mo_ref[0], vo_ref[0] = _adam_tile([g_ref[j] for j in range(N_DEV)], w_ref[0], m_ref[0],
                                                               v_ref[0])

    spec = pl.BlockSpec((1, tr, c), lambda i: (l, i, 0))
    return _pcall(body, name=name, out_shape=[jax.ShapeDtypeStruct(w.shape, F32)] * 4, grid=(r // tr,),
                  in_specs=[pl.BlockSpec((N_DEV, tr, c), lambda i: (0, i, 0)), spec, spec, spec] + [_ANY] * 4,
                  out_specs=[spec] * 4, sem=("parallel",),
                  input_output_aliases={4 + j: j for j in range(4)})(recv, w, m, v, *prev)


def _as3d(a):
    if a.ndim == 1:
        return a.reshape(1, 1, -1)
    if a.ndim == 2:
        return a.reshape(1, a.shape[0], a.shape[1])
    return a.reshape(a.shape[0], -1, a.shape[-1])


def kernel(x, mem, positions, norm_mix, w_in, dn_conv, dn_a_log, dn_dt_bias, dn_out_norm, mla_q_norm, mla_w_qb, mla_kv_norm, mla_w_kvb, w_out, mem_norm, norm_xattn, xa_wq, xa_wk, xa_wv, xa_wo, norm_ffn, ffn_w_up, ffn_conv, ffn_conv_bias, ffn_w_down, norm_final, loss_target, m_norm_mix, m_w_in, m_dn_conv, m_dn_a_log, m_dn_dt_bias, m_dn_out_norm, m_mla_q_norm, m_mla_w_qb, m_mla_kv_norm, m_mla_w_kvb, m_w_out, m_mem_norm, m_norm_xattn, m_xa_wq, m_xa_wk, m_xa_wv, m_xa_wo, m_norm_ffn, m_ffn_w_up, m_ffn_conv, m_ffn_conv_bias, m_ffn_w_down, m_norm_final, v_norm_mix, v_w_in, v_dn_conv, v_dn_a_log, v_dn_dt_bias, v_dn_out_norm, v_mla_q_norm, v_mla_w_qb, v_mla_kv_norm, v_mla_w_kvb, v_w_out, v_mem_norm, v_norm_xattn, v_xa_wq, v_xa_wk, v_xa_wv, v_xa_wo, v_norm_ffn, v_ffn_w_up, v_ffn_conv, v_ffn_conv_bias, v_ffn_w_down, v_norm_final):
    w = dict(norm_mix=norm_mix, w_in=w_in, dn_conv=dn_conv, dn_a_log=dn_a_log, dn_dt_bias=dn_dt_bias,
             dn_out_norm=dn_out_norm, mla_q_norm=mla_q_norm, mla_w_qb=mla_w_qb, mla_kv_norm=mla_kv_norm,
             mla_w_kvb=mla_w_kvb, w_out=w_out, mem_norm=mem_norm, norm_xattn=norm_xattn, xa_wq=xa_wq, xa_wk=xa_wk,
             xa_wv=xa_wv, xa_wo=xa_wo, norm_ffn=norm_ffn, ffn_w_up=ffn_w_up, ffn_conv=ffn_conv,
             ffn_conv_bias=ffn_conv_bias, ffn_w_down=ffn_w_down, norm_final=norm_final)
    mom = dict(norm_mix=m_norm_mix, w_in=m_w_in, dn_conv=m_dn_conv, dn_a_log=m_dn_a_log, dn_dt_bias=m_dn_dt_bias,
               dn_out_norm=m_dn_out_norm, mla_q_norm=m_mla_q_norm, mla_w_qb=m_mla_w_qb, mla_kv_norm=m_mla_kv_norm,
               mla_w_kvb=m_mla_w_kvb, w_out=m_w_out, mem_norm=m_mem_norm, norm_xattn=m_norm_xattn, xa_wq=m_xa_wq,
               xa_wk=m_xa_wk, xa_wv=m_xa_wv, xa_wo=m_xa_wo, norm_ffn=m_norm_ffn, ffn_w_up=m_ffn_w_up,
               ffn_conv=m_ffn_conv, ffn_conv_bias=m_ffn_conv_bias, ffn_w_down=m_ffn_w_down, norm_final=m_norm_final)
    var = dict(norm_mix=v_norm_mix, w_in=v_w_in, dn_conv=v_dn_conv, dn_a_log=v_dn_a_log, dn_dt_bias=v_dn_dt_bias,
               dn_out_norm=v_dn_out_norm, mla_q_norm=v_mla_q_norm, mla_w_qb=v_mla_w_qb, mla_kv_norm=v_mla_kv_norm,
               mla_w_kvb=v_mla_w_kvb, w_out=v_w_out, mem_norm=v_mem_norm, norm_xattn=v_norm_xattn, xa_wq=v_xa_wq,
               xa_wk=v_xa_wk, xa_wv=v_xa_wv, xa_wo=v_xa_wo, norm_ffn=v_norm_ffn, ffn_w_up=v_ffn_w_up,
               ffn_conv=v_ffn_conv, ffn_conv_bias=v_ffn_conv_bias, ffn_w_down=v_ffn_w_down, norm_final=v_norm_final)
    depth = w_in.shape[0]
    me = _my_index()
    rep = {k: w[k] for k in _REPLICATED}
    x0, mem0, pos0, target0 = x[0], mem[0], positions[0], loss_target[0]

    def with_own_block(block):
        return lax.dynamic_update_index_in_dim(lax.empty((N_DEV,) + block.shape, block.dtype), block, me, 0)

    def gather_start(l, keys, tag, after):
        shards = [w[k][l].astype(BF16) for k in keys]
        plan = _plan_gather_first(len(keys))
        flight, token = exchange_start(shards + [with_own_block(sh) for sh in shards], plan,
                                       f"gather_l{l}{tag}_first_start", after)
        return (flight, plan, keys, f"gather_l{l}{tag}"), token

    def gather_pass(state, after):
        flight, plan, keys, name = state
        lands = exchange_wait(flight, plan, name + "_first_wait", after)[len(keys):]
        plan = _plan_gather_pass(len(keys))
        flight, token = exchange_start(lands, plan, name + "_pass_start")
        return (flight, plan, keys, name), token

    def gather_finish(state, after):
        flight, plan, keys, name = state
        return dict(zip(keys, exchange_wait(flight, plan, name + "_pass_wait", after)))

    small = dict(zip(_SMALL_SHARDED, all_gather([w[k] for k in _SMALL_SHARDED], "gather_small")))
    mem_n, vjp_mem = jax.vjp(lambda g: rms_norm(mem0, g, "mem_norm", BF16), rep["mem_norm"])

    groups = [(l, keys, tag) for l in range(depth) for keys, tag in ((_MIX_KEYS, "_mix"), (_TAIL_KEYS, "_tail"))]
    n_stage = len(groups)
    states, ready = {}, {}
    token = small["dn_conv"]
    for s in range(min(3, n_stage)):
        states[s], token = gather_start(*groups[s], token)
    states[0], token = gather_pass(states[0], token)
    ready[0] = gather_finish(states[0], token)

    h = x0
    tapes = []
    for s in range(n_stage):
        l = s // 2
        tokens = []
        if s >= 2 and s + 1 < n_stage:
            states[s + 1], token = gather_pass(states[s + 1], h)
            tokens.append(token)
        if s + 3 < n_stage:
            states[s + 3], token = gather_start(*groups[s + 3], h)
            tokens.append(token)
        if s % 2 == 0:
            rp = {k: rep[k][l] for k in _REP_MIX}
            rp["norm_mix"] = _after(rp["norm_mix"], tokens)
            h_new, tape = jax.vjp(lambda hh, gw, cv, rr: _mix(hh, gw, cv, rr, l, pos0), h, ready.pop(s),
                                  small["dn_conv"][:, l], rp)
        else:
            rp = {k: rep[k][l] for k in _REP_TAIL}
            rp["norm_xattn"] = _after(rp["norm_xattn"], tokens)
            h_new, tape = jax.vjp(lambda hh, gw, cv, rr, mn: _tail(hh, gw, cv, rr, l, mn), h, ready.pop(s),
                                  small["ffn_conv"][:, l], rp, mem_n)
        if s + 1 < n_stage:
            if s < 2:
                states[s + 1], token = gather_pass(states[s + 1], h_new)
                ready[s + 1] = gather_finish(states[s + 1], token)
            else:
                ready[s + 1] = gather_finish(states[s + 1], h_new)
        tapes.append(tape)
        h = h_new

    rows, vjp_loss = jax.vjp(lambda hh, g: loss_rows(hh, g, target0, "loss"), h, rep["norm_final"])
    loss = lax.psum(jnp.sum(rows), ("x", "y", "c"))
    dh, d_norm_final = vjp_loss(jnp.ones_like(rows))

    d_rep = {k: [None] * depth for k in _REP_MIX + _REP_TAIL}
    d_conv = {k: [None] * depth for k in _SMALL_SHARDED}
    recv_big = [dict() for _ in range(depth)]
    d_mem_n = None

    def scatter_start(l, grads, keys, tag):
        parts = [grads[k] for k in keys]
        own = [with_own_block(lax.dynamic_index_in_dim(p, me, 0, keepdims=False)) for p in parts]
        plan = _plan_scatter(len(keys))
        flight, token = exchange_start(parts + own, plan, f"scatter_l{l}{tag}_start")
        return (flight, plan, keys, l, f"scatter_l{l}{tag}_wait"), token

    def scatter_finish(state, after):
        flight, plan, keys, l, name = state
        recv_big[l].update(zip(keys, exchange_wait(flight, plan, name, after)[len(keys):]))

    pending = {}
    token = None
    for s in reversed(range(n_stage)):
        l, keys, tag = groups[s]
        if token is not None:
            dh = _after(dh, [token])
        if s % 2:
            dh, dg, d_conv["ffn_conv"][l], d_rp, d_mn = tapes[s](dh)
            d_mem_n = d_mn if d_mem_n is None else d_mem_n + d_mn
        else:
            dh, dg, d_conv["dn_conv"][l], d_rp = tapes[s](dh)
        for k, v in d_rp.items():
            d_rep[k][l] = v
        if s + 2 in pending:
            scatter_finish(pending.pop(s + 2), dh)
        pending[s], token = scatter_start(l, dg, keys, tag)

    (d_mem_norm,) = vjp_mem(d_mem_n)
    d_rep_all = {k: jnp.stack(v) for k, v in d_rep.items()}
    d_rep_all["mem_norm"] = d_mem_norm
    d_rep_all["norm_final"] = _after(d_norm_final, [token])
    parts = [jnp.stack(d_conv[k], axis=1) for k in _SMALL_SHARDED]
    own = [with_own_block(lax.dynamic_index_in_dim(p, me, 0, keepdims=False)) for p in parts]
    plan_small = _plan_scatter(len(parts))
    flight_small, token = exchange_start(parts + own, plan_small, "scatter_small_start")
    blocks = [_as3d(d_rep_all[k])[0] for k in _REPLICATED]
    blocks[0] = _after(blocks[0], [token])
    plan_rep = _plan_gather_direct(len(blocks))
    flight_rep, token = exchange_start(blocks + [with_own_block(b) for b in blocks], plan_rep, "gather_rep_start")

    big = {k: None for k in _BIG}

    def update_big(l):
        for k in _BIG:
            recv = recv_big[l][k]
            big[k] = adam_layer(recv.reshape((N_DEV, -1, recv.shape[-1])), _as3d(w[k]), _as3d(mom[k]),
                                _as3d(var[k]), big[k], l, f"adam_{k}_l{l}")

    for l in reversed(range(1, depth)):
        update_big(l)
    after = token if depth == 1 else big[_BIG[-1]][0]
    out = {}
    recv_small = exchange_wait(flight_small, plan_small, "scatter_small_wait", after)[len(parts):]
    for k, recv in zip(_SMALL_SHARDED, recv_small):
        r3 = _as3d(w[k])
        out[k] = adam_update(recv.reshape((1, N_DEV, -1, recv.shape[-1])), r3.reshape((1, -1, r3.shape[-1])),
                             _as3d(mom[k]).reshape((1, -1, r3.shape[-1])), _as3d(var[k]).reshape((1, -1, r3.shape[-1])),
                             "adam_" + k)
    recv_rep = exchange_wait(flight_rep, plan_rep, "gather_rep_wait", after)[len(blocks):]
    for k, recv in zip(_REPLICATED, recv_rep):
        r3 = _as3d(w[k])
        flat = lambda a, r3=r3: a.reshape((1, -1, r3.shape[-1]))
        out[k] = adam_update(recv[None], flat(w[k]), flat(mom[k]), flat(var[k]), "adam_" + k)
    for s in sorted(pending, reverse=True):
        scatter_finish(pending[s], out["norm_final"][0])
    update_big(0)
    out.update(big)

    res = [loss, dh[None]]
    for j in range(4):
        res += [out[k][j].reshape(w[k].shape) for k in _WEIGHTS]
    return tuple(res)
```

```python
import functools

import numpy as np
import jax
import jax.numpy as jnp
from jax import lax
from jax.experimental import pallas as pl
from jax.experimental.pallas import tpu as pltpu

F32 = jnp.float32
BF16 = jnp.bfloat16
_MXU_DTYPE = BF16
_INTERPRET = False
_VMEM_LIMIT_BYTES = 48 * 1024 * 1024
_MM_VMEM_BUDGET_BYTES = 36 * 1024 * 1024
_MM_TILE_CAP = 2048

N_DEV = 8
CHUNK = 64
_CHUNK_SHIFT = 6
DN_HEAD_DIM = 128
DN_CONV = 4
MLA_NOPE = 128
MLA_ROPE = 64
MLA_V = 128
MLA_Q_RANK = 512
MLA_KV_RANK = 256
ROPE_BASE = 10000.0
XA_HEADS = 4
FFN_CONV = 3
EPS = 1e-6
LANE = 128

ADAM_LR = 0.001
ADAM_B1 = 0.9
ADAM_B2 = 0.999
ADAM_EPS = 1e-08
ADAM_WD = 0.01
ADAM_STEP = 10
_ADAM_TILE_ELEMS = 128 * 1024

MESH = pl.DeviceIdType.MESH


def _pcall(body, *, name, out_shape, grid=None, in_specs=None, out_specs=None, scratch=(), sem=None, **kw):
    params = pltpu.CompilerParams(dimension_semantics=sem, vmem_limit_bytes=_VMEM_LIMIT_BYTES)
    args = dict(name=name, out_shape=out_shape, scratch_shapes=list(scratch), compiler_params=params,
                interpret=_INTERPRET, **kw)
    if grid is not None:
        args.update(grid=grid)
    if in_specs is not None:
        args.update(in_specs=in_specs)
    if out_specs is not None:
        args.update(out_specs=out_specs)
    return pl.pallas_call(body, **args)


def _tile(n, pref):
    if n <= pref:
        return n
    t = pref
    while t >= 8:
        if n % t == 0:
            return t
        t //= 2
    return n


def _ein_raw(spec, a, b, hi):
    one = lambda x, y: jnp.einsum(spec, x, y, preferred_element_type=F32)
    a_hi, b_hi = a.astype(_MXU_DTYPE), b.astype(_MXU_DTYPE)
    if not hi:
        return one(a_hi, b_hi)
    a_lo = (a.astype(F32) - a_hi.astype(F32)).astype(_MXU_DTYPE)
    b_lo = (b.astype(F32) - b_hi.astype(F32)).astype(_MXU_DTYPE)
    return one(a_hi, b_hi) + (one(a_hi, b_lo) + one(a_lo, b_hi))


def _make_ein(spec, hi=False, diff_b=True):
    a_s, rest = spec.split(",")
    b_s, o_s = rest.split("->")

    @jax.custom_vjp
    def f(a, b):
        return _ein_raw(spec, a, b, hi)

    def fwd(a, b):
        return f(a, b), (a, b)

    def bwd(res, g):
        a, b = res
        da = _ein_raw(f"{o_s},{b_s}->{a_s}", g, b, hi)
        if not diff_b:
            return da.astype(a.dtype), None
        db = _ein_raw(f"{a_s},{o_s}->{b_s}", a, g, hi)
        return da.astype(a.dtype), db.astype(b.dtype)

    f.defvjp(fwd, bwd)
    return f


_nt = _make_ein("qd,kd->qk")
_nn = _make_ein("qk,kd->qd")
_nn_hi_const = _make_ein("qk,kd->qd", hi=True, diff_b=False)
_bnt = _make_ein("hik,hjk->hij")
_bnn = _make_ein("hij,hjv->hiv")
_btn = _make_ein("hck,hcv->hkv")


def _divisor_tiles(n, cap):
    out = [n] if n <= cap else []
    t = cap
    while t >= LANE:
        if t < n and n % t == 0:
            out.append(t)
        t //= 2
    return out or [n]


def _mm_tiles(m, n, k, a_bytes, b_bytes, o_bytes, r_bytes, accumulate=False):
    best = None
    for tk in sorted({k, *_divisor_tiles(k, _MM_TILE_CAP)}):
        for tm in _divisor_tiles(m, _MM_TILE_CAP):
            for tn in _divisor_tiles(n, _MM_TILE_CAP):
                need = 2 * (tm * tk * a_bytes + tk * tn * b_bytes + tm * tn * (o_bytes + r_bytes))
                need += tm * tn * 4 * (1 if tk == k and not accumulate else 2)
                if need > _MM_VMEM_BUDGET_BYTES:
                    continue
                score = (tm * tn * tk, tk, tm)
                if best is None or score > best[0]:
                    best = (score, (tm, tn, tk))
    assert best is not None, (m, n, k)
    return best[1]


def _mm(a, b, *, ta=False, tb=False, res=None, out_dtype=F32, name="mm"):
    m = a.shape[1] if ta else a.shape[0]
    k = a.shape[0] if ta else a.shape[1]
    n = b.shape[0] if tb else b.shape[1]
    assert (b.shape[1] if tb else b.shape[0]) == k, (a.shape, b.shape, ta, tb)
    has_res = res is not None
    tm, tn, tk = _mm_tiles(m, n, k, a.dtype.itemsize, b.dtype.itemsize, jnp.dtype(out_dtype).itemsize,
                           res.dtype.itemsize if has_res else 0)
    nk = k // tk
    body = _mm_body(nk, (((0 if ta else 1,), (1 if tb else 0,)), ((), ())), has_res, out_dtype)
    a_spec = pl.BlockSpec((tk, tm), lambda i, j, kk: (kk, i)) if ta else pl.BlockSpec((tm, tk), lambda i, j, kk: (i, kk))
    b_spec = pl.BlockSpec((tn, tk), lambda i, j, kk: (j, kk)) if tb else pl.BlockSpec((tk, tn), lambda i, j, kk: (kk, j))
    o_spec = pl.BlockSpec((tm, tn), lambda i, j, kk: (i, j))
    in_specs = [a_spec, b_spec] + ([o_spec] if has_res else [])
    args = (a, b) + ((res,) if has_res else ())
    return _pcall(body, name=name, out_shape=jax.ShapeDtypeStruct((m, n), out_dtype),
                  grid=(m // tm, n // tn, nk), in_specs=in_specs, out_specs=o_spec,
                  scratch=[pltpu.VMEM((tm, tn), F32)] if nk > 1 else [],
                  sem=("parallel", "parallel", "arbitrary"))(*args)


def _mm_body(nk, dims, has_res, out_dtype):
    def body(*refs):
        a_ref, b_ref = refs[0], refs[1]
        r_ref = refs[2] if has_res else None
        o_ref = refs[2 + has_res]

        def finish(r):
            if has_res:
                r = r + r_ref[...].astype(F32)
            o_ref[...] = r.astype(out_dtype)

        part = lax.dot_general(a_ref[...].astype(_MXU_DTYPE), b_ref[...].astype(_MXU_DTYPE), dims,
                               preferred_element_type=F32)
        if nk == 1:
            finish(part)
            return
        acc = refs[-1]
        kk = pl.program_id(2)

        @pl.when(kk == 0)
        def _():
            acc[...] = part

        @pl.when(kk > 0)
        def _():
            acc[...] += part

        @pl.when(kk == nk - 1)
        def _():
            finish(acc[...])

    return body


def _mm_groups_out(a, b3, *, ta=False, out_dtype=F32, name="mm_groups"):
    groups, k, nb = b3.shape
    m = a.shape[1] if ta else a.shape[0]
    assert (a.shape[0] if ta else a.shape[1]) == k, (a.shape, b3.shape, ta)
    tm, tn, tk = _mm_tiles(m, nb, k, a.dtype.itemsize, b3.dtype.itemsize, jnp.dtype(out_dtype).itemsize, 0)
    nk, per = k // tk, nb // tn
    body = _mm_body(nk, (((0 if ta else 1,), (0,)), ((), ())), False, out_dtype)
    a_spec = pl.BlockSpec((tk, tm), lambda j, i, kk: (kk, i)) if ta else pl.BlockSpec((tm, tk), lambda j, i, kk: (i, kk))
    b_spec = pl.BlockSpec((None, tk, tn), lambda j, i, kk: (j // per, kk, j % per))
    o_spec = pl.BlockSpec((None, tm, tn), lambda j, i, kk: (j // per, i, j % per))
    return _pcall(body, name=name, out_shape=jax.ShapeDtypeStruct((groups, m, nb), out_dtype),
                  grid=(groups * per, m // tm, nk), in_specs=[a_spec, b_spec], out_specs=o_spec,
                  scratch=[pltpu.VMEM((tm, tn), F32)] if nk > 1 else [],
                  sem=("parallel", "parallel", "arbitrary"))(a, b3)


def _mm_groups_contract(a3, b3, *, out_dtype=F32, name="mm_contract"):
    groups, m, nb = a3.shape
    n = b3.shape[1]
    assert b3.shape[0] == groups and b3.shape[2] == nb, (a3.shape, b3.shape)
    tm, tn, tk = _mm_tiles(m, n, nb, a3.dtype.itemsize, b3.dtype.itemsize, jnp.dtype(out_dtype).itemsize, 0,
                           accumulate=True)
    per = nb // tk
    nk = groups * per
    body = _mm_body(nk, (((1,), (1,)), ((), ())), False, out_dtype)
    a_spec = pl.BlockSpec((None, tm, tk), lambda i, j, kk: (kk // per, i, kk % per))
    b_spec = pl.BlockSpec((None, tn, tk), lambda i, j, kk: (kk // per, j, kk % per))
    return _pcall(body, name=name, out_shape=jax.ShapeDtypeStruct((m, n), out_dtype),
                  grid=(m // tm, n // tn, nk), in_specs=[a_spec, b_spec],
                  out_specs=pl.BlockSpec((tm, tn), lambda i, j, kk: (i, j)),
                  scratch=[pltpu.VMEM((tm, tn), F32)], sem=("parallel", "parallel", "arbitrary"))(a3, b3)


@functools.partial(jax.custom_vjp, nondiff_argnums=(3,))
def _linear_res(a, w, res, name):
    return _mm(a, w, res=res, name=name + "_fwd")


def _linear_res_fwd(a, w, res, name):
    return _mm(a, w, res=res, name=name + "_fwd"), (a, w)


def _linear_res_bwd(name, saved, g):
    a, w = saved
    gm = g.astype(_MXU_DTYPE)
    da = _mm(gm, w, tb=True, out_dtype=a.dtype, name=name + "_da")
    dw = _mm(a, gm, ta=True, out_dtype=w.dtype, name=name + "_dw")
    return da, dw, g


_linear_res.defvjp(_linear_res_fwd, _linear_res_bwd)


@functools.partial(jax.custom_vjp, nondiff_argnums=(2,))
def _linear(a, w, name):
    return _mm(a, w, name=name + "_fwd")


def _linear_fwd(a, w, name):
    return _mm(a, w, name=name + "_fwd"), (a, w)


def _linear_bwd(name, saved, g):
    a, w = saved
    gm = g.astype(_MXU_DTYPE)
    da = _mm(gm, w, tb=True, out_dtype=a.dtype, name=name + "_da")
    dw = _mm(a, gm, ta=True, out_dtype=w.dtype, name=name + "_dw")
    return da, dw


_linear.defvjp(_linear_fwd, _linear_bwd)


def linear(a, w, name, res=None):
    return _linear(a, w, name) if res is None else _linear_res(a, w, res, name)


def _tiled_specs(arrs, kinds, t, axis):
    specs = []
    for a, kind in zip(arrs, kinds):
        if kind == "whole":
            specs.append(pl.BlockSpec(a.shape, lambda i, nd=a.ndim: (0,) * nd))
        elif axis == 0:
            specs.append(pl.BlockSpec((t, a.shape[1]), lambda i: (i, 0)))
        else:
            specs.append(pl.BlockSpec((a.shape[0], t), lambda i: (0, i)))
    return specs


def tilewise(fn, name, args, kinds, outs, *, axis, t, diff):
    n_in = len(args)
    length = next(a.shape[axis] for a, kd in zip(args, kinds) if kd == "tile")
    steps = length // t
    assert steps * t == length, (name, length, t)

    def out_sds(other, dtype):
        return jax.ShapeDtypeStruct((length, other) if axis == 0 else (other, length), dtype)

    def out_spec(other):
        return pl.BlockSpec((t, other), lambda i: (i, 0)) if axis == 0 else pl.BlockSpec((other, t), lambda i: (0, i))

    def run_fwd(*xs):
        def body(*refs):
            vals = fn(*[r[...] for r in refs[:n_in]])
            for o_ref, v in zip(refs[n_in:], vals):
                o_ref[...] = v.astype(o_ref.dtype)

        return _pcall(body, name=name + "_fwd", out_shape=[out_sds(o, d) for o, d in outs], grid=(steps,),
                      in_specs=_tiled_specs(xs, kinds, t, axis), out_specs=[out_spec(o) for o, _ in outs],
                      sem=("parallel",))(*xs)

    didx = [i for i in range(n_in) if diff[i]]

    def run_bwd(xs, cts):
        def body(*refs):
            x_refs, c_refs, g_refs = refs[:n_in], refs[n_in:n_in + len(outs)], refs[n_in + len(outs):]
            vals = [r[...] for r in x_refs]

            def g(*dvals):
                full = list(vals)
                for i, v in zip(didx, dvals):
                    full[i] = v
                return tuple(fn(*full))

            prim_out, vjp = jax.vjp(g, *[vals[i] for i in didx])
            grads = vjp(tuple(c[...].astype(o.dtype) for c, o in zip(c_refs, prim_out)))
            step = pl.program_id(0)
            for i, g_ref, gr in zip(didx, g_refs, grads):
                if kinds[i] == "whole":
                    @pl.when(step == 0)
                    def _(g_ref=g_ref):
                        g_ref[...] = jnp.zeros_like(g_ref)

                    g_ref[...] += gr.astype(g_ref.dtype)
                else:
                    g_ref[...] = gr.astype(g_ref.dtype)

        g_shapes = [jax.ShapeDtypeStruct(xs[i].shape, xs[i].dtype) for i in didx]
        g_specs = _tiled_specs([xs[i] for i in didx], [kinds[i] for i in didx], t, axis)
        ct_specs = [out_spec(o) for o, _ in outs]
        any_whole = any(kinds[i] == "whole" for i in didx)
        return _pcall(body, name=name + "_bwd", out_shape=g_shapes, grid=(steps,),
                      in_specs=_tiled_specs(xs, kinds, t, axis) + ct_specs, out_specs=g_specs,
                      sem=("arbitrary" if any_whole else "parallel",))(*xs, *cts)

    @jax.custom_vjp
    def op(*xs):
        return tuple(run_fwd(*xs))

    def op_fwd(*xs):
        return tuple(run_fwd(*xs)), xs

    def op_bwd(xs, cts):
        grads = run_bwd(xs, cts)
        full = [None] * n_in
        for i, gr in zip(didx, grads):
            full[i] = gr
        return tuple(full)

    op.defvjp(op_fwd, op_bwd)
    return op(*args)


def _silu(x):
    return x * (1.0 / (1.0 + jnp.exp(-x)))


def _softplus(x):
    return jnp.maximum(x, 0.0) + jnp.log(1.0 + jnp.exp(-jnp.abs(x)))


def _rms_tile(x, gain, out_dtype):
    xf = x.astype(F32)
    y = xf * lax.rsqrt(jnp.mean(xf * xf, axis=-1, keepdims=True) + EPS)
    return (y * gain).astype(out_dtype)


def rms_norm(x, gain, name, out_dtype, tr=256):
    def fn(xt, gt):
        return (_rms_tile(xt, gt, out_dtype),)

    return tilewise(fn, name, [x, gain.reshape(1, -1)], ["tile", "whole"], [(x.shape[1], out_dtype)],
                    axis=0, t=_tile(x.shape[0], tr), diff=[True, True])[0]


def _shift_down_raw(x, s):
    rows = lax.broadcasted_iota(jnp.int32, x.shape, 0)
    return jnp.where(rows >= s, pltpu.roll(x, s, 0), 0.0)


def _shift_up_raw(x, s):
    n = x.shape[0]
    rows = lax.broadcasted_iota(jnp.int32, x.shape, 0)
    return jnp.where(rows < n - s, pltpu.roll(x, n - s, 0), 0.0)


@functools.partial(jax.custom_vjp, nondiff_argnums=(1,))
def _shift_down(x, s):
    return _shift_down_raw(x, s)


def _shift_down_fwd(x, s):
    return _shift_down_raw(x, s), None


def _shift_down_bwd(s, _, g):
    return (_shift_up_raw(g, s),)


_shift_down.defvjp(_shift_down_fwd, _shift_down_bwd)


def _causal_dwconv_tile(x, w):
    kk = w.shape[0]
    y = x * w[kk - 1:kk, :]
    for j in range(kk - 1):
        y = y + _shift_down(x, kk - 1 - j) * w[j:j + 1, :]
    return y


def dn_qkv(raw, conv_w, name):
    width = raw.shape[1] // 3
    is_qk = (jnp.arange(raw.shape[1]) < 2 * width).astype(F32).reshape(1, -1)

    def fn(x, w, flag):
        y = _silu(_causal_dwconv_tile(x, w))
        yn = y * lax.rsqrt(jnp.sum(y * y, axis=-1, keepdims=True) + EPS)
        return (jnp.where(flag > 0.5, yn, y),)

    return tilewise(fn, name, [raw, conv_w, is_qk], ["tile", "tile", "tile"], [(raw.shape[0], F32)],
                    axis=1, t=DN_HEAD_DIM, diff=[True, True, False])[0]


def dn_gates(b, a, a_log, dt_bias, name):
    hh = b.shape[1]
    tr = _tile(b.shape[0], 256)

    def fn(bt, at, al, db):
        beta = 1.0 / (1.0 + jnp.exp(-bt))
        g = -jnp.exp(al) * _softplus(at + db)
        pos = lax.broadcasted_iota(jnp.int32, g.shape, 0) & (CHUNK - 1)
        step = 1
        while step < CHUNK:
            g = g + jnp.where(pos >= step, _shift_down(g, step), 0.0)
            step *= 2
        return g, beta

    return tilewise(fn, name, [b, a, a_log.reshape(1, -1), dt_bias.reshape(1, -1)],
                    ["tile", "tile", "whole", "whole"], [(hh, F32), (hh, F32)], axis=0, t=tr, diff=[True] * 4)


def _unit_lower_inv_raw(low):
    c = low.shape[-1]
    ii = lax.broadcasted_iota(jnp.int32, (c, c), 0)
    jj = lax.broadcasted_iota(jnp.int32, (c, c), 1)
    inv = (ii == jj).astype(F32)[None] - low
    p = low
    n = 1
    while 2 * n < c:
        p = _ein_raw("hij,hjk->hik", p, p, True)
        inv = inv + _ein_raw("hij,hjk->hik", inv, p, True)
        n *= 2
    return inv


@jax.custom_vjp
def _unit_lower_inv(low):
    return _unit_lower_inv_raw(low)


def _unit_lower_inv_fwd(low):
    inv = _unit_lower_inv_raw(low)
    return inv, inv


def _unit_lower_inv_bwd(inv, g):
    t = _ein_raw("hji,hjk->hik", inv, g, True)
    return (-_ein_raw("hik,hlk->hil", t, inv, True),)


_unit_lower_inv.defvjp(_unit_lower_inv_fwd, _unit_lower_inv_bwd)


def _dn_chunk(state, q, k, v, gc, gr, gl, bc):
    c = q.shape[1]
    q = q * (q.shape[-1] ** -0.5)
    ii = lax.broadcasted_iota(jnp.int32, (c, c), 0)
    jj = lax.broadcasted_iota(jnp.int32, (c, c), 1)
    incl = (jj <= ii)[None]
    strict = (jj < ii)[None]
    decay = jnp.where(incl, jnp.exp(jnp.where(incl, gc - gr, 0.0)), 0.0)
    kb = k * bc
    low = jnp.where(strict, _bnt(kb, k) * decay, 0.0)
    ainv = _unit_lower_inv(low)
    eg = jnp.exp(gc)
    u = _bnn(ainv, v * bc)
    w = _bnn(ainv, kb * eg)
    attn = _bnt(q, k) * decay
    q_dec = q * eg
    k_dec = k * jnp.exp(gl - gc)
    v_new = u - _bnn(w, state)
    o = _bnn(q_dec, state) + _bnn(attn, v_new)
    new_state = state * jnp.exp(gl) + _btn(k_dec, v_new)
    return new_state, o


def _dn_heads(ref, base, hh):
    return jnp.stack([ref[:, base + h * DN_HEAD_DIM: base + (h + 1) * DN_HEAD_DIM] for h in range(hh)])


def _dn_fwd_call(qkv, gc, gr, gl, bc, name):
    s, w3 = qkv.shape
    width = w3 // 3
    hh = width // DN_HEAD_DIM
    d = DN_HEAD_DIM
    n = s // CHUNK

    def body(qkv_ref, gc_ref, gr_ref, gl_ref, bc_ref, o_ref, st_ref, state):
        @pl.when(pl.program_id(0) == 0)
        def _():
            state[...] = jnp.zeros_like(state)

        s_in = state[...]
        st_ref[0] = s_in
        new_s, o = _dn_chunk(s_in, _dn_heads(qkv_ref, 0, hh), _dn_heads(qkv_ref, width, hh),
                             _dn_heads(qkv_ref, 2 * width, hh), gc_ref[0], gr_ref[0], gl_ref[0], bc_ref[0])
        state[...] = new_s
        for h in range(hh):
            o_ref[:, h * d:(h + 1) * d] = o[h]

    g4 = lambda i: (i, 0, 0, 0)
    return _pcall(
        body, name=name + "_fwd",
        out_shape=[jax.ShapeDtypeStruct((s, width), F32), jax.ShapeDtypeStruct((n, hh, d, d), F32)],
        grid=(n,),
        in_specs=[pl.BlockSpec((CHUNK, w3), lambda i: (i, 0)), pl.BlockSpec((1, hh, CHUNK, 1), g4),
                  pl.BlockSpec((1, hh, 1, CHUNK), g4), pl.BlockSpec((1, hh, 1, 1), g4),
                  pl.BlockSpec((1, hh, CHUNK, 1), g4)],
        out_specs=[pl.BlockSpec((CHUNK, width), lambda i: (i, 0)), pl.BlockSpec((1, hh, d, d), g4)],
        scratch=[pltpu.VMEM((hh, d, d), F32)], sem=("arbitrary",))(qkv, gc, gr, gl, bc)


def _dn_bwd_call(qkv, gc, gr, gl, bc, states, do, name):
    s, w3 = qkv.shape
    width = w3 // 3
    hh = width // DN_HEAD_DIM
    d = DN_HEAD_DIM
    n = s // CHUNK

    def body(qkv_ref, gc_ref, gr_ref, gl_ref, bc_ref, st_ref, do_ref,
             dqkv_ref, dgc_ref, dgr_ref, dgl_ref, dbc_ref, dstate):
        @pl.when(pl.program_id(0) == 0)
        def _():
            dstate[...] = jnp.zeros_like(dstate)

        prim = (st_ref[0], _dn_heads(qkv_ref, 0, hh), _dn_heads(qkv_ref, width, hh),
                _dn_heads(qkv_ref, 2 * width, hh), gc_ref[0], gr_ref[0], gl_ref[0], bc_ref[0])
        _, vjp = jax.vjp(_dn_chunk, *prim)
        ds, dq, dk, dv, dgc, dgr, dgl, dbc = vjp((dstate[...], _dn_heads(do_ref, 0, hh)))
        dstate[...] = ds
        for h in range(hh):
            dqkv_ref[:, h * d:(h + 1) * d] = dq[h]
            dqkv_ref[:, width + h * d: width + (h + 1) * d] = dk[h]
            dqkv_ref[:, 2 * width + h * d: 2 * width + (h + 1) * d] = dv[h]
        dgc_ref[0] = dgc
        dgr_ref[0] = dgr
        dgl_ref[0] = dgl
        dbc_ref[0] = dbc

    r2 = lambda i: (n - 1 - i, 0)
    r4 = lambda i: (n - 1 - i, 0, 0, 0)
    spec_c = pl.BlockSpec((1, hh, CHUNK, 1), r4)
    spec_r = pl.BlockSpec((1, hh, 1, CHUNK), r4)
    spec_l = pl.BlockSpec((1, hh, 1, 1), r4)
    return _pcall(
        body, name=name + "_bwd",
        out_shape=[jax.ShapeDtypeStruct(qkv.shape, F32), jax.ShapeDtypeStruct(gc.shape, F32),
                   jax.ShapeDtypeStruct(gr.shape, F32), jax.ShapeDtypeStruct(gl.shape, F32),
                   jax.ShapeDtypeStruct(bc.shape, F32)],
        grid=(n,),
        in_specs=[pl.BlockSpec((CHUNK, w3), r2), spec_c, spec_r, spec_l, spec_c,
                  pl.BlockSpec((1, hh, d, d), r4), pl.BlockSpec((CHUNK, width), r2)],
        out_specs=[pl.BlockSpec((CHUNK, w3), r2), spec_c, spec_r, spec_l, spec_c],
        scratch=[pltpu.VMEM((hh, d, d), F32)], sem=("arbitrary",))(qkv, gc, gr, gl, bc, states, do)


@functools.partial(jax.custom_vjp, nondiff_argnums=(5,))
def dn_core(qkv, gc, gr, gl, bc, name):
    return _dn_fwd_call(qkv, gc, gr, gl, bc, name)[0]


def _dn_core_fwd(qkv, gc, gr, gl, bc, name):
    o, states = _dn_fwd_call(qkv, gc, gr, gl, bc, name)
    return o, (qkv, gc, gr, gl, bc, states)


def _dn_core_bwd(name, saved, do):
    return tuple(_dn_bwd_call(*saved, do, name))


dn_core.defvjp(_dn_core_fwd, _dn_core_bwd)


def dn_out_gate(o, z, out_norm, name):
    hh = o.shape[1] // DN_HEAD_DIM
    gain = jnp.tile(out_norm.reshape(1, -1), (1, hh))

    def fn(ot, zt, gt):
        y = ot * lax.rsqrt(jnp.mean(ot * ot, axis=-1, keepdims=True) + EPS) * gt
        return (y * _silu(zt),)

    return tilewise(fn, name, [o, z, gain], ["tile", "tile", "tile"], [(o.shape[0], BF16)],
                    axis=1, t=DN_HEAD_DIM, diff=[True, True, True])[0]


def _attn_tile(qn, kn, v, qp, kp, q0, scale, causal):
    s = _nt(qn, kn)
    if qp is not None:
        s = s + _nt(qp, kp)
    s = s * scale
    if causal:
        qpos = q0 + lax.broadcasted_iota(jnp.int32, s.shape, 0)
        kpos = lax.broadcasted_iota(jnp.int32, s.shape, 1)
        s = jnp.where((kpos >> _CHUNK_SHIFT) <= (qpos >> _CHUNK_SHIFT), s, -1e30)
    e = jnp.exp(s - jnp.max(s, axis=-1, keepdims=True))
    p = e / jnp.sum(e, axis=-1, keepdims=True)
    return _nn(p, v)


def _attn_specs(sq, sk, dh, dp, tb):
    q_spec = pl.BlockSpec((tb, dh), lambda h, i: (i, h))
    kv_spec = pl.BlockSpec((sk, dh), lambda h, i: (0, h))
    qp_spec = pl.BlockSpec((1, tb, dp), lambda h, i: (h, i, 0)) if dp else None
    kp_spec = pl.BlockSpec((sk, dp), lambda h, i: (0, 0)) if dp else None
    return q_spec, kv_spec, qp_spec, kp_spec


def _attn_plan(sq, sk, tq, causal):
    tb = sq if causal else tq
    subs = [(slice(i * tq, (i + 1) * tq), (i + 1) * tq if causal else sk) for i in range(tb // tq)]
    return tb, subs


def _attn_fwd_call(q, k, v, qp, kp, *, dh, scale, causal, tq, name):
    sq, sk = q.shape[0], k.shape[0]
    heads = q.shape[1] // dh
    dp = qp.shape[-1] if qp is not None else 0
    tb, subs = _attn_plan(sq, sk, tq, causal)
    q_spec, kv_spec, qp_spec, kp_spec = _attn_specs(sq, sk, dh, dp, tb)

    def body(*refs):
        q_ref, k_ref, v_ref = refs[:3]
        qp_ref, kp_ref = (refs[3], refs[4]) if dp else (None, None)
        o_ref = refs[-1]
        row0 = pl.program_id(1) * tb
        for rows, ke in subs:
            o = _attn_tile(q_ref[rows, :], k_ref[:ke, :], v_ref[:ke, :], qp_ref[0, rows, :] if dp else None,
                           kp_ref[:ke, :] if dp else None, row0 + rows.start, scale, causal)
            o_ref[rows, :] = o.astype(o_ref.dtype)

    in_specs = [q_spec, kv_spec, kv_spec] + ([qp_spec, kp_spec] if dp else [])
    args = (q, k, v) + ((qp, kp) if dp else ())
    return _pcall(body, name=name + "_fwd", out_shape=jax.ShapeDtypeStruct((sq, heads * dh), BF16),
                  grid=(heads, sq // tb), in_specs=in_specs, out_specs=q_spec,
                  sem=("parallel", "parallel"))(*args)


def _attn_bwd_call(q, k, v, qp, kp, do, *, dh, scale, causal, tq, name):
    sq, sk = q.shape[0], k.shape[0]
    heads = q.shape[1] // dh
    dp = qp.shape[-1] if qp is not None else 0
    tb, subs = _attn_plan(sq, sk, tq, causal)
    q_spec, kv_spec, qp_spec, kp_spec = _attn_specs(sq, sk, dh, dp, tb)

    def body(*refs):
        h, i = pl.program_id(0), pl.program_id(1)
        if dp:
            q_ref, k_ref, v_ref, qp_ref, kp_ref, do_ref, dq_ref, dk_ref, dv_ref, dqp_ref, dkp_ref = refs
        else:
            q_ref, k_ref, v_ref, do_ref, dq_ref, dk_ref, dv_ref = refs

        @pl.when(i == 0)
        def _():
            dk_ref[...] = jnp.zeros_like(dk_ref)
            dv_ref[...] = jnp.zeros_like(dv_ref)

        if dp:
            @pl.when(jnp.logical_and(h == 0, i == 0))
            def _():
                dkp_ref[...] = jnp.zeros_like(dkp_ref)

        for rows, ke in subs:
            q0 = i * tb + rows.start
            if dp:
                prim = (q_ref[rows, :], k_ref[:ke, :], v_ref[:ke, :], qp_ref[0, rows, :], kp_ref[:ke, :])
                f = lambda a, b, c, d, e, q0=q0: _attn_tile(a, b, c, d, e, q0, scale, causal)
            else:
                prim = (q_ref[rows, :], k_ref[:ke, :], v_ref[:ke, :])
                f = lambda a, b, c, q0=q0: _attn_tile(a, b, c, None, None, q0, scale, causal)
            _, vjp = jax.vjp(f, *prim)
            grads = vjp(do_ref[rows, :].astype(F32))
            dq_ref[rows, :] = grads[0]
            dk_ref[:ke, :] += grads[1]
            dv_ref[:ke, :] += grads[2]
            if dp:
                dqp_ref[0, rows, :] = grads[3]
                dkp_ref[:ke, :] += grads[4]

    in_specs = [q_spec, kv_spec, kv_spec] + ([qp_spec, kp_spec] if dp else []) + [q_spec]
    out_shape = [jax.ShapeDtypeStruct(q.shape, F32), jax.ShapeDtypeStruct(k.shape, F32),
                 jax.ShapeDtypeStruct(v.shape, F32)]
    out_specs = [q_spec, kv_spec, kv_spec]
    if dp:
        out_shape += [jax.ShapeDtypeStruct(qp.shape, F32), jax.ShapeDtypeStruct(kp.shape, F32)]
        out_specs += [qp_spec, kp_spec]
    args = (q, k, v) + ((qp, kp) if dp else ()) + (do,)
    return _pcall(body, name=name + "_bwd", out_shape=out_shape, grid=(heads, sq // tb), in_specs=in_specs,
                  out_specs=out_specs, sem=("arbitrary", "arbitrary"))(*args)


def attention(q, k, v, qp=None, kp=None, *, dh, scale, causal, name, tq=256):
    tq = _tile(q.shape[0], tq)
    kw = dict(dh=dh, scale=scale, causal=causal, tq=tq, name=name)
    has_pe = qp is not None

    @jax.custom_vjp
    def op(*xs):
        return _attn_fwd_call(*xs, **kw) if has_pe else _attn_fwd_call(*xs, None, None, **kw)

    def op_fwd(*xs):
        return (_attn_fwd_call(*xs, **kw) if has_pe else _attn_fwd_call(*xs, None, None, **kw)), xs

    def op_bwd(xs, do):
        full = xs if has_pe else xs + (None, None)
        return tuple(_attn_bwd_call(*full, do, **kw))

    op.defvjp(op_fwd, op_bwd)
    return op(q, k, v, qp, kp) if has_pe else op(q, k, v)


def _rope_tables(positions, reps):
    half = MLA_ROPE // 2
    inv = ROPE_BASE ** (-jnp.arange(0, MLA_ROPE, 2, dtype=F32) / MLA_ROPE)
    ang = positions.astype(F32)[:, None] * inv
    cos, sin = jnp.cos(ang), jnp.sin(ang)
    c = jnp.tile(jnp.concatenate([cos, cos], axis=-1), (1, reps))
    s = jnp.tile(jnp.concatenate([sin, sin], axis=-1), (1, reps))
    rot = np.zeros((MLA_ROPE, MLA_ROPE), np.float32)
    for i in range(half):
        rot[i + half, i] = -1.0
        rot[i, i + half] = 1.0
    return c, s, jnp.asarray(np.kron(np.eye(reps, dtype=np.float32), rot))


def rope(x, positions, name):
    c, s, rot = _rope_tables(positions, x.shape[1] // MLA_ROPE)

    def fn(xt, ct, st, rt):
        return (xt * ct + _nn_hi_const(xt, rt) * st,)

    return tilewise(fn, name, [x, c, s, rot], ["tile", "tile", "tile", "whole"], [(x.shape[1], F32)],
                    axis=0, t=_tile(x.shape[0], 256), diff=[True, False, False, False])[0]


def _ffn_gate_tile(g, u, wg, wu, bg, bu):
    return _silu(_causal_dwconv_tile(g, wg) + bg) * (_causal_dwconv_tile(u, wu) + bu)


def _ffn_gate_specs(pre4, conv4):
    _, groups, s, nb = pre4.shape
    tc = LANE if nb % LANE == 0 else nb
    per = nb // tc
    at = lambda g, j: (0, g, 0, j)
    specs = dict(pre=pl.BlockSpec((2, None, s, tc), at), conv=pl.BlockSpec((2, None, conv4.shape[2], tc), at),
                 bias=pl.BlockSpec((2, None, 1, tc), at), act=pl.BlockSpec((s, tc), lambda g, j: (0, g * per + j)))
    return specs, (groups, per)


def _ffn_gate_fwd_call(pre4, conv4, bias4, name):
    specs, grid = _ffn_gate_specs(pre4, conv4)
    _, groups, s, nb = pre4.shape

    def body(pre_ref, conv_ref, bias_ref, act_ref):
        act_ref[...] = _ffn_gate_tile(pre_ref[0], pre_ref[1], conv_ref[0], conv_ref[1], bias_ref[0],
                                      bias_ref[1]).astype(act_ref.dtype)

    return _pcall(body, name=name + "_fwd", out_shape=jax.ShapeDtypeStruct((s, groups * nb), BF16), grid=grid,
                  in_specs=[specs["pre"], specs["conv"], specs["bias"]], out_specs=specs["act"],
                  sem=("parallel", "parallel"))(pre4, conv4, bias4)


def _ffn_gate_bwd_call(pre4, conv4, bias4, dact, name):
    specs, grid = _ffn_gate_specs(pre4, conv4)

    def body(pre_ref, conv_ref, bias_ref, dact_ref, dpre_ref, dconv_ref, dbias_ref):
        prim = (pre_ref[0], pre_ref[1], conv_ref[0], conv_ref[1], bias_ref[0], bias_ref[1])
        _, vjp = jax.vjp(_ffn_gate_tile, *prim)
        dg, du, dwg, dwu, dbg, dbu = vjp(dact_ref[...].astype(F32))
        dpre_ref[0] = dg.astype(dpre_ref.dtype)
        dpre_ref[1] = du.astype(dpre_ref.dtype)
        dconv_ref[0] = dwg
        dconv_ref[1] = dwu
        dbias_ref[0] = dbg
        dbias_ref[1] = dbu

    return _pcall(body, name=name + "_bwd",
                  out_shape=[jax.ShapeDtypeStruct(pre4.shape, _MXU_DTYPE), jax.ShapeDtypeStruct(conv4.shape, F32),
                             jax.ShapeDtypeStruct(bias4.shape, F32)],
                  grid=grid, in_specs=[specs["pre"], specs["conv"], specs["bias"], specs["act"]],
                  out_specs=[specs["pre"], specs["conv"], specs["bias"]], sem=("parallel", "parallel"))(
                      pre4, conv4, bias4, dact)


@functools.partial(jax.custom_vjp, nondiff_argnums=(4,))
def ffn_up_gate(hn, w3, conv3, bias, name):
    return _ffn_up_gate_fwd(hn, w3, conv3, bias, name)[0]


def _ffn_up_gate_fwd(hn, w3, conv3, bias, name):
    groups, _, nb = w3.shape
    half = groups // 2
    pre4 = _mm_groups_out(hn, w3, name=name + "_up_fwd").reshape(2, half, hn.shape[0], nb)
    conv4 = conv3.reshape(2, half, conv3.shape[1], nb)
    bias4 = bias.reshape(2, half, 1, nb)
    return _ffn_gate_fwd_call(pre4, conv4, bias4, name + "_gate"), (hn, w3, pre4, conv4, bias4)


def _ffn_up_gate_bwd(name, saved, dact):
    hn, w3, pre4, conv4, bias4 = saved
    dpre4, dconv4, dbias4 = _ffn_gate_bwd_call(pre4, conv4, bias4, dact, name + "_gate")
    dpre3 = dpre4.reshape((w3.shape[0],) + dpre4.shape[2:])
    dw3 = _mm_groups_out(hn, dpre3, ta=True, out_dtype=w3.dtype, name=name + "_up_dw")
    dhn = _mm_groups_contract(dpre3, w3, out_dtype=hn.dtype, name=name + "_up_da")
    return dhn, dw3, dconv4.reshape((w3.shape[0],) + dconv4.shape[2:]), dbias4.reshape(-1)


ffn_up_gate.defvjp(_ffn_up_gate_fwd, _ffn_up_gate_bwd)


def loss_rows(h, gain, target, name):
    def fn(ht, tt, gt):
        err = _rms_tile(ht, gt, F32) - tt
        return (0.5 * jnp.mean(err * err, axis=-1, keepdims=True),)

    return tilewise(fn, name, [h, target, gain.reshape(1, -1)], ["tile", "tile", "whole"], [(1, F32)],
                    axis=0, t=_tile(h.shape[0], 256), diff=[True, False, True])[0]


def _col_groups(w, per_head, lo, hi):
    r = w.shape[0]
    return w.reshape(r, -1, per_head)[:, :, lo:hi].reshape(r, -1)


def _layer_mix(h, lw, rp, l, positions):
    s, d = h.shape
    width = d // 2
    dn_heads = width // DN_HEAD_DIM
    mla_heads = (d - width) // MLA_V
    nm = f"l{l}_"
    w_in = lw["w_in"]
    c0 = 4 * width
    c1 = c0 + 2 * dn_heads
    c2 = c1 + MLA_Q_RANK
    c3 = c2 + MLA_KV_RANK + MLA_ROPE
    rest_pad = (-(c3 - c0)) % LANE
    w_rest = jnp.pad(w_in[:, c0:c3], ((0, 0), (0, rest_pad)))

    u = rms_norm(h, rp["norm_mix"], nm + "norm_mix", BF16)
    qkv_raw = linear(u, w_in[:, :3 * width], nm + "in_qkv")
    z = linear(u, w_in[:, 3 * width:c0], nm + "in_z")
    rest = linear(u, w_rest, nm + "in_rest")

    qkv = dn_qkv(qkv_raw, lw["dn_conv"], nm + "dn_qkv")
    csum, beta = dn_gates(rest[:, :dn_heads], rest[:, dn_heads:2 * dn_heads], rp["dn_a_log"],
                          rp["dn_dt_bias"], nm + "dn_gates")
    n = s // CHUNK
    g3 = csum.reshape(n, CHUNK, dn_heads).transpose(0, 2, 1)
    b3 = beta.reshape(n, CHUNK, dn_heads).transpose(0, 2, 1)
    o_dn = dn_core(qkv, g3[..., None], g3[:, :, None, :], g3[:, :, CHUNK - 1][..., None, None], b3[..., None],
                   nm + "dn_core")
    o_dn = dn_out_gate(o_dn, z, rp["dn_out_norm"], nm + "dn_gate")

    mq = rest[:, c1 - c0:c2 - c0]
    mkv = rest[:, c2 - c0:c3 - c0]
    qn = rms_norm(mq, rp["mla_q_norm"], nm + "mla_qnorm", BF16)
    per_q = MLA_NOPE + MLA_ROPE
    q_nope = linear(qn, _col_groups(lw["mla_w_qb"], per_q, 0, MLA_NOPE), nm + "mla_qn")
    q_pe = linear(qn, _col_groups(lw["mla_w_qb"], per_q, MLA_NOPE, per_q), nm + "mla_qp")
    kvn = rms_norm(mkv[:, :MLA_KV_RANK], rp["mla_kv_norm"], nm + "mla_kvnorm", BF16)
    per_kv = MLA_NOPE + MLA_V
    k_nope = linear(kvn, _col_groups(lw["mla_w_kvb"], per_kv, 0, MLA_NOPE), nm + "mla_kn")
    v_mla = linear(kvn, _col_groups(lw["mla_w_kvb"], per_kv, MLA_NOPE, per_kv), nm + "mla_v")
    q_pe = rope(q_pe, positions, nm + "rope_q")
    k_pe = rope(mkv[:, MLA_KV_RANK:], positions, nm + "rope_k")
    q_pe = q_pe.reshape(s, mla_heads, MLA_ROPE).transpose(1, 0, 2)
    o_mla = attention(q_nope, k_nope, v_mla, q_pe, k_pe, dh=MLA_NOPE, scale=per_q ** -0.5, causal=True,
                      name=nm + "mla_attn")

    h = linear(o_dn, lw["w_out"][:width], nm + "out_dn", res=h)
    return linear(o_mla, lw["w_out"][width:], nm + "out_mla", res=h)


def _layer_xattn(h, lw, rp, l, mem_n):
    d = h.shape[1]
    nm = f"l{l}_"
    hn = rms_norm(h, rp["norm_xattn"], nm + "norm_xattn", BF16)
    xq = linear(hn, lw["xa_wq"], nm + "xa_q")
    xk = linear(mem_n, lw["xa_wk"], nm + "xa_k")
    xv = linear(mem_n, lw["xa_wv"], nm + "xa_v")
    xdh = d // XA_HEADS
    xo = attention(xq, xk, xv, dh=xdh, scale=xdh ** -0.5, causal=False, name=nm + "xattn")
    return linear(xo, lw["xa_wo"], nm + "xa_o", res=h)


def _layer_ffn(h, lw, rp, l):
    nm = f"l{l}_"
    hn = rms_norm(h, rp["norm_ffn"], nm + "norm_ffn", BF16)
    act = ffn_up_gate(hn, lw["ffn_w_up"], lw["ffn_conv"], rp["ffn_conv_bias"], nm + "ffn")
    return linear(act, lw["ffn_w_down"], nm + "ffn_down", res=h)


_COL_SHARDED = ("w_in", "mla_w_qb", "mla_w_kvb", "ffn_w_up", "dn_conv", "ffn_conv")
_ROW_SHARDED = ("w_out", "xa_wq", "xa_wk", "xa_wv", "xa_wo", "ffn_w_down")
_BIG = ("w_in", "mla_w_qb", "mla_w_kvb", "w_out", "xa_wq", "xa_wk", "xa_wv", "xa_wo", "ffn_w_up", "ffn_w_down")
_SMALL_SHARDED = ("dn_conv", "ffn_conv")
_STAGES = (("_mix", ("w_in", "mla_w_qb", "mla_w_kvb", "w_out"),
            ("norm_mix", "dn_a_log", "dn_dt_bias", "dn_out_norm", "mla_q_norm", "mla_kv_norm"), "norm_mix"),
           ("_xattn", ("xa_wq", "xa_wk", "xa_wv", "xa_wo"), ("norm_xattn",), "norm_xattn"),
           ("_ffn", ("ffn_w_up", "ffn_w_down"), ("norm_ffn", "ffn_conv_bias"), "norm_ffn"))
_REPLICATED = ("norm_mix", "dn_a_log", "dn_dt_bias", "dn_out_norm", "mla_q_norm", "mla_kv_norm", "mem_norm",
               "norm_xattn", "norm_ffn", "ffn_conv_bias", "norm_final")
_WEIGHTS = ("norm_mix", "w_in", "dn_conv", "dn_a_log", "dn_dt_bias", "dn_out_norm", "mla_q_norm", "mla_w_qb",
            "mla_kv_norm", "mla_w_kvb", "w_out", "mem_norm", "norm_xattn", "xa_wq", "xa_wk", "xa_wv", "xa_wo",
            "norm_ffn", "ffn_w_up", "ffn_conv", "ffn_conv_bias", "ffn_w_down", "norm_final")


def _assemble(name, g):
    if name in _ROW_SHARDED:
        return g.reshape((-1,) + g.shape[2:])
    return jnp.moveaxis(g, 0, -2).reshape(g.shape[1:-1] + (-1,))


def _mix(h, gw, conv, rp, l, positions):
    lw = {k: _assemble(k, v) for k, v in gw.items()}
    lw["dn_conv"] = _assemble("dn_conv", conv)
    return _layer_mix(h, lw, rp, l, positions)


def _xattn(h, gw, rp, l, mem_n):
    return _layer_xattn(h, {k: _assemble(k, v) for k, v in gw.items()}, rp, l, mem_n)


def _ffn(h, gw, conv, rp, l):
    lw = {"ffn_w_up": gw["ffn_w_up"], "ffn_w_down": _assemble("ffn_w_down", gw["ffn_w_down"]), "ffn_conv": conv}
    return _layer_ffn(h, lw, rp, l)


def _my_index():
    return 4 * lax.axis_index("x") + 2 * lax.axis_index("y") + lax.axis_index("c")


def _peer(k):
    x, y, c = lax.axis_index("x"), lax.axis_index("y"), lax.axis_index("c")
    px = (1 - x) if k & 4 else x
    py = (1 - y) if k & 2 else y
    pc = (1 - c) if k & 1 else c
    return (px, py, pc), 4 * px + 2 * py + pc


_ANY = pl.BlockSpec(memory_space=pl.ANY)


def all_gather(shards, name):
    n = len(shards)

    def body(*refs):
        x_refs, o_refs = refs[:n], refs[n:2 * n]
        send_sems, recv_sems, local_sems = refs[2 * n:]
        me = _my_index()
        sib_id, sib = _peer(1)
        chips = [_peer(k) for k in (4, 2, 6)]

        def copy(a, k, block, to, src=None):
            return pltpu.make_async_remote_copy(
                src_ref=o_refs[a].at[block] if src is None else src, dst_ref=o_refs[a].at[block],
                send_sem=send_sems.at[a * 7 + k], recv_sem=recv_sems.at[a * 7 + k],
                device_id=to, device_id_type=MESH)

        mine = [pltpu.make_async_copy(x_refs[a], o_refs[a].at[me], local_sems.at[a]) for a in range(n)]
        for cp in mine:
            cp.start()
        first = []
        for a in range(n):
            first.append(copy(a, 0, me, sib_id, src=x_refs[a]))
            first += [copy(a, 1 + j, me, cid, src=x_refs[a]) for j, (cid, _) in enumerate(chips)]
        for cp in first:
            cp.start()
        passed = []
        for a in range(n):
            for j, (cid, cidx) in enumerate(chips):
                copy(a, 1 + j, cidx, cid).wait_recv()
                fwd = copy(a, 4 + j, cidx, sib_id)
                fwd.start()
                passed.append(fwd)
        for a in range(n):
            copy(a, 0, sib, sib_id).wait_recv()
            for j, (_, cidx) in enumerate(chips):
                copy(a, 4 + j, cidx ^ 1, sib_id).wait_recv()
        for cp in first + passed:
            cp.wait_send()
        for cp in mine:
            cp.wait()

    out_shape = [jax.ShapeDtypeStruct((N_DEV,) + s.shape, s.dtype) for s in shards]
    return _pcall(body, name=name, out_shape=out_shape, in_specs=[_ANY] * n, out_specs=[_ANY] * n,
                  scratch=[pltpu.SemaphoreType.DMA((7 * n,)), pltpu.SemaphoreType.DMA((7 * n,)),
                           pltpu.SemaphoreType.DMA((n,))])(*shards)


_HBM = pl.BlockSpec(memory_space=pltpu.HBM)
_SEM = pl.BlockSpec(memory_space=pltpu.SEMAPHORE)
_EFFECT = pltpu.SideEffectType.DATAFLOW_SIDE_EFFECTING


def _plan_gather_first(n):
    def plan(refs):
        me = _my_index()
        sib_id, sib = _peer(1)
        chips = [_peer(k) for k in (4, 2, 6)]
        out = []
        for a in range(n):
            x, land = refs[a], refs[n + a]
            out.append((x, land.at[me], sib_id, land.at[sib]))
            out += [(x, land.at[me], cid, land.at[cidx]) for cid, cidx in chips]
        return out

    return plan, 4 * n


def _plan_gather_pass(n):
    def plan(refs):
        sib_id, _ = _peer(1)
        chips = [_peer(k) for k in (4, 2, 6)]
        return [(refs[a].at[cidx], refs[a].at[cidx], sib_id, refs[a].at[cidx ^ 1])
                for a in range(n) for _, cidx in chips]

    return plan, 3 * n


def _plan_gather_direct(n):
    def plan(refs):
        me = _my_index()
        out = []
        for a in range(n):
            for k in range(1, N_DEV):
                pid, pidx = _peer(k)
                out.append((refs[a], refs[n + a].at[me], pid, refs[n + a].at[pidx]))
        return out

    return plan, 7 * n


def _plan_scatter(n):
    def plan(refs):
        me = _my_index()
        out = []
        for a in range(n):
            for k in range(1, N_DEV):
                pid, pidx = _peer(k)
                out.append((refs[a].at[pidx], refs[n + a].at[me], pid, refs[n + a].at[pidx]))
        return out

    return plan, 7 * n


def _remote_copy(src, dst, send_sems, recv_sems, i, dev):
    return pltpu.make_async_remote_copy(src_ref=src, dst_ref=dst, send_sem=send_sems.at[i], recv_sem=recv_sems.at[i],
                                        device_id=dev, device_id_type=MESH)


def _after(x, tokens):
    for token in tokens:
        x = x + token[0, 0]
    return x


def exchange_start(bufs, plan_n, name, after=None):
    plan, n = plan_n
    nb = len(bufs)
    n_in = nb + (after is not None)

    def body(*refs):
        send_sems, recv_sems = refs[n_in], refs[n_in + 1]
        for i, (src, dst, dev, _) in enumerate(plan(refs[:nb])):
            _remote_copy(src, dst, send_sems, recv_sems, i, dev).start()
        refs[-1][...] = jnp.zeros_like(refs[-1])

    out_shape = (pltpu.SemaphoreType.DMA((n,)), pltpu.SemaphoreType.DMA((n,)),
                 *[pltpu.HBM(b.shape, b.dtype) for b in bufs], jax.ShapeDtypeStruct((8, LANE), F32))
    args = [pltpu.with_memory_space_constraint(b, pltpu.HBM) for b in bufs] + ([after] if after is not None else [])
    res = pl.pallas_call(
        body, name=name, out_shape=out_shape,
        in_specs=[_HBM] * nb + ([_ANY] if after is not None else []),
        out_specs=(_SEM, _SEM, *[_HBM] * nb, pl.BlockSpec(memory_space=pltpu.VMEM)),
        input_output_aliases={i: 2 + i for i in range(nb)},
        compiler_params=pltpu.CompilerParams(has_side_effects=_EFFECT), interpret=_INTERPRET)(*args)
    return (res[0], res[1], list(res[2:2 + nb])), res[-1]


def exchange_wait(flight, plan_n, name, after):
    plan, _ = plan_n
    send_sems, recv_sems, bufs = flight
    nb = len(bufs)

    def body(*refs):
        s_sems, r_sems = refs[nb], refs[nb + 1]
        for i, (src, _, dev, arrival) in enumerate(plan(refs[:nb])):
            cp = _remote_copy(src, arrival, s_sems, r_sems, i, dev)
            cp.wait_send()
            cp.wait_recv()

    return list(pl.pallas_call(
        body, name=name, out_shape=tuple(pltpu.HBM(b.shape, b.dtype) for b in bufs),
        in_specs=[_HBM] * nb + [_SEM, _SEM, _ANY], out_specs=tuple([_HBM] * nb),
        input_output_aliases={i: i for i in range(nb)},
        compiler_params=pltpu.CompilerParams(has_side_effects=_EFFECT), interpret=_INTERPRET)(
            *bufs, send_sems, recv_sems, after))


def _adam_tile(contribs, w, m, v):
    g = contribs[0].astype(F32)
    for part in contribs[1:]:
        g = g + part.astype(F32)
    m_new = ADAM_B1 * m + (1.0 - ADAM_B1) * g
    v_new = ADAM_B2 * v + (1.0 - ADAM_B2) * (g * g)
    m_hat = m_new / (1.0 - ADAM_B1 ** ADAM_STEP)
    v_hat = v_new / (1.0 - ADAM_B2 ** ADAM_STEP)
    return g, -ADAM_LR * (m_hat / (jnp.sqrt(v_hat) + ADAM_EPS) + ADAM_WD * w), m_new, v_new


def adam_update(recv, w, m, v, name):
    ll, _, r, c = recv.shape
    tr = _tile(r, max(8, 1 << ((_ADAM_TILE_ELEMS // c).bit_length() - 1)))

    def body(g_ref, w_ref, m_ref, v_ref, go_ref, d_ref, mo_ref, vo_ref):
        go_ref[0], d_ref[0], mo_ref[0], vo_ref[0] = _adam_tile([g_ref[0, j] for j in range(N_DEV)], w_ref[0],
                                                               m_ref[0], v_ref[0])

    spec = pl.BlockSpec((1, tr, c), lambda l, i: (l, i, 0))
    sds = jax.ShapeDtypeStruct((ll, r, c), F32)
    return _pcall(body, name=name, out_shape=[sds] * 4, grid=(ll, r // tr),
                  in_specs=[pl.BlockSpec((1, N_DEV, tr, c), lambda l, i: (l, 0, i, 0)), spec, spec, spec],
                  out_specs=[spec] * 4, sem=("parallel", "parallel"))(recv, w, m, v)


def adam_layer(recv, w, m, v, prev, l, name):
    _, r, c = recv.shape
    tr = _tile(r, max(8, 1 << ((_ADAM_TILE_ELEMS // c).bit_length() - 1)))
    if prev is None:
        prev = [lax.empty(w.shape, F32) for _ in range(4)]

    def body(g_ref, w_ref, m_ref, v_ref, *rest):
        go_ref, d_ref, mo_ref, vo_ref = rest[4:]
        go_ref[0], d_ref[0], mo_ref[0], vo_ref[0] = _adam_tile([g_ref[j] for j in range(N_DEV)], w_ref[0], m_ref[0],
                                                               v_ref[0])

    spec = pl.BlockSpec((1, tr, c), lambda i: (l, i, 0))
    return _pcall(body, name=name, out_shape=[jax.ShapeDtypeStruct(w.shape, F32)] * 4, grid=(r // tr,),
                  in_specs=[pl.BlockSpec((N_DEV, tr, c), lambda i: (0, i, 0)), spec, spec, spec] + [_ANY] * 4,
                  out_specs=[spec] * 4, sem=("parallel",),
                  input_output_aliases={4 + j: j for j in range(4)})(recv, w, m, v, *prev)


def _as3d(a):
    if a.ndim == 1:
        return a.reshape(1, 1, -1)
    if a.ndim == 2:
        return a.reshape(1, a.shape[0], a.shape[1])
    return a.reshape(a.shape[0], -1, a.shape[-1])


def kernel(x, mem, positions, norm_mix, w_in, dn_conv, dn_a_log, dn_dt_bias, dn_out_norm, mla_q_norm, mla_w_qb, mla_kv_norm, mla_w_kvb, w_out, mem_norm, norm_xattn, xa_wq, xa_wk, xa_wv, xa_wo, norm_ffn, ffn_w_up, ffn_conv, ffn_conv_bias, ffn_w_down, norm_final, loss_target, m_norm_mix, m_w_in, m_dn_conv, m_dn_a_log, m_dn_dt_bias, m_dn_out_norm, m_mla_q_norm, m_mla_w_qb, m_mla_kv_norm, m_mla_w_kvb, m_w_out, m_mem_norm, m_norm_xattn, m_xa_wq, m_xa_wk, m_xa_wv, m_xa_wo, m_norm_ffn, m_ffn_w_up, m_ffn_conv, m_ffn_conv_bias, m_ffn_w_down, m_norm_final, v_norm_mix, v_w_in, v_dn_conv, v_dn_a_log, v_dn_dt_bias, v_dn_out_norm, v_mla_q_norm, v_mla_w_qb, v_mla_kv_norm, v_mla_w_kvb, v_w_out, v_mem_norm, v_norm_xattn, v_xa_wq, v_xa_wk, v_xa_wv, v_xa_wo, v_norm_ffn, v_ffn_w_up, v_ffn_conv, v_ffn_conv_bias, v_ffn_w_down, v_norm_final):
    w = dict(norm_mix=norm_mix, w_in=w_in, dn_conv=dn_conv, dn_a_log=dn_a_log, dn_dt_bias=dn_dt_bias,
             dn_out_norm=dn_out_norm, mla_q_norm=mla_q_norm, mla_w_qb=mla_w_qb, mla_kv_norm=mla_kv_norm,
             mla_w_kvb=mla_w_kvb, w_out=w_out, mem_norm=mem_norm, norm_xattn=norm_xattn, xa_wq=xa_wq, xa_wk=xa_wk,
             xa_wv=xa_wv, xa_wo=xa_wo, norm_ffn=norm_ffn, ffn_w_up=ffn_w_up, ffn_conv=ffn_conv,
             ffn_conv_bias=ffn_conv_bias, ffn_w_down=ffn_w_down, norm_final=norm_final)
    mom = dict(norm_mix=m_norm_mix, w_in=m_w_in, dn_conv=m_dn_conv, dn_a_log=m_dn_a_log, dn_dt_bias=m_dn_dt_bias,
               dn_out_norm=m_dn_out_norm, mla_q_norm=m_mla_q_norm, mla_w_qb=m_mla_w_qb, mla_kv_norm=m_mla_kv_norm,
               mla_w_kvb=m_mla_w_kvb, w_out=m_w_out, mem_norm=m_mem_norm, norm_xattn=m_norm_xattn, xa_wq=m_xa_wq,
               xa_wk=m_xa_wk, xa_wv=m_xa_wv, xa_wo=m_xa_wo, norm_ffn=m_norm_ffn, ffn_w_up=m_ffn_w_up,
               ffn_conv=m_ffn_conv, ffn_conv_bias=m_ffn_conv_bias, ffn_w_down=m_ffn_w_down, norm_final=m_norm_final)
    var = dict(norm_mix=v_norm_mix, w_in=v_w_in, dn_conv=v_dn_conv, dn_a_log=v_dn_a_log, dn_dt_bias=v_dn_dt_bias,
               dn_out_norm=v_dn_out_norm, mla_q_norm=v_mla_q_norm, mla_w_qb=v_mla_w_qb, mla_kv_norm=v_mla_kv_norm,
               mla_w_kvb=v_mla_w_kvb, w_out=v_w_out, mem_norm=v_mem_norm, norm_xattn=v_norm_xattn, xa_wq=v_xa_wq,
               xa_wk=v_xa_wk, xa_wv=v_xa_wv, xa_wo=v_xa_wo, norm_ffn=v_norm_ffn, ffn_w_up=v_ffn_w_up,
               ffn_conv=v_ffn_conv, ffn_conv_bias=v_ffn_conv_bias, ffn_w_down=v_ffn_w_down, norm_final=v_norm_final)
    depth = w_in.shape[0]
    me = _my_index()
    rep = {k: w[k] for k in _REPLICATED}
    x0, mem0, pos0, target0 = x[0], mem[0], positions[0], loss_target[0]

    def with_own_block(block):
        return lax.dynamic_update_index_in_dim(lax.empty((N_DEV,) + block.shape, block.dtype), block, me, 0)

    def gather_start(l, keys, tag, after):
        shards = [w[k][l].astype(BF16) for k in keys]
        plan = _plan_gather_first(len(keys))
        flight, token = exchange_start(shards + [with_own_block(sh) for sh in shards], plan,
                                       f"gather_l{l}{tag}_first_start", after)
        return (flight, plan, keys, f"gather_l{l}{tag}"), token

    def gather_pass(state, after):
        flight, plan, keys, name = state
        lands = exchange_wait(flight, plan, name + "_first_wait", after)[len(keys):]
        plan = _plan_gather_pass(len(keys))
        flight, token = exchange_start(lands, plan, name + "_pass_start")
        return (flight, plan, keys, name), token

    def gather_finish(state, after):
        flight, plan, keys, name = state
        return dict(zip(keys, exchange_wait(flight, plan, name + "_pass_wait", after)))

    small = dict(zip(_SMALL_SHARDED, all_gather([w[k] for k in _SMALL_SHARDED], "gather_small")))
    mem_n, vjp_mem = jax.vjp(lambda g: rms_norm(mem0, g, "mem_norm", BF16), rep["mem_norm"])

    n_kind = len(_STAGES)
    groups = [(l, keys, tag) for l in range(depth) for tag, keys, _, _ in _STAGES]
    n_stage = len(groups)
    states, ready = {}, {}
    token = small["dn_conv"]
    for s in range(min(3, n_stage)):
        states[s], token = gather_start(*groups[s], token)
    states[0], token = gather_pass(states[0], token)
    ready[0] = gather_finish(states[0], token)

    h = x0
    tapes = []
    for s in range(n_stage):
        l, kind = divmod(s, n_kind)
        tokens = []
        if s >= 2 and s + 1 < n_stage:
            states[s + 1], token = gather_pass(states[s + 1], h)
            tokens.append(token)
        if s + 3 < n_stage:
            states[s + 3], token = gather_start(*groups[s + 3], h)
            tokens.append(token)
        _, _, rep_keys, gain = _STAGES[kind]
        rp = {k: rep[k][l] for k in rep_keys}
        rp[gain] = _after(rp[gain], tokens)
        if kind == 0:
            h_new, tape = jax.vjp(lambda hh, gw, cv, rr: _mix(hh, gw, cv, rr, l, pos0), h, ready.pop(s),
                                  small["dn_conv"][:, l], rp)
        elif kind == 1:
            h_new, tape = jax.vjp(lambda hh, gw, rr, mn: _xattn(hh, gw, rr, l, mn), h, ready.pop(s), rp, mem_n)
        else:
            h_new, tape = jax.vjp(lambda hh, gw, cv, rr: _ffn(hh, gw, cv, rr, l), h, ready.pop(s),
                                  small["ffn_conv"][:, l], rp)
        if s + 1 < n_stage:
            if s < 2:
                states[s + 1], token = gather_pass(states[s + 1], h_new)
                ready[s + 1] = gather_finish(states[s + 1], token)
            else:
                ready[s + 1] = gather_finish(states[s + 1], h_new)
        tapes.append(tape)
        h = h_new

    rows, vjp_loss = jax.vjp(lambda hh, g: loss_rows(hh, g, target0, "loss"), h, rep["norm_final"])
    loss = lax.psum(jnp.sum(rows), ("x", "y", "c"))
    dh, d_norm_final = vjp_loss(jnp.ones_like(rows))

    d_rep = {k: [None] * depth for _, _, rep_keys, _ in _STAGES for k in rep_keys}
    d_conv = {k: [None] * depth for k in _SMALL_SHARDED}
    recv_big = [dict() for _ in range(depth)]
    d_mem_n = None

    def scatter_start(l, grads, keys, tag):
        parts = [grads[k] for k in keys]
        own = [with_own_block(lax.dynamic_index_in_dim(p, me, 0, keepdims=False)) for p in parts]
        plan = _plan_scatter(len(keys))
        flight, token = exchange_start(parts + own, plan, f"scatter_l{l}{tag}_start")
        return (flight, plan, keys, l, f"scatter_l{l}{tag}_wait"), token

    def scatter_finish(state, after):
        flight, plan, keys, l, name = state
        recv_big[l].update(zip(keys, exchange_wait(flight, plan, name, after)[len(keys):]))

    pending = {}
    token = None
    for s in reversed(range(n_stage)):
        l, keys, tag = groups[s]
        if token is not None:
            dh = _after(dh, [token])
        if s % n_kind == 0:
            dh, dg, d_conv["dn_conv"][l], d_rp = tapes[s](dh)
        elif s % n_kind == 1:
            dh, dg, d_rp, d_mn = tapes[s](dh)
            d_mem_n = d_mn if d_mem_n is None else d_mem_n + d_mn
        else:
            dh, dg, d_conv["ffn_conv"][l], d_rp = tapes[s](dh)
        for k, v in d_rp.items():
            d_rep[k][l] = v
        if s + n_kind in pending:
            scatter_finish(pending.pop(s + n_kind), dh)
        pending[s], token = scatter_start(l, dg, keys, tag)

    (d_mem_norm,) = vjp_mem(d_mem_n)
    d_rep_all = {k: jnp.stack(v) for k, v in d_rep.items()}
    d_rep_all["mem_norm"] = d_mem_norm
    d_rep_all["norm_final"] = _after(d_norm_final, [token])
    parts = [jnp.stack(d_conv[k], axis=1) for k in _SMALL_SHARDED]
    own = [with_own_block(lax.dynamic_index_in_dim(p, me, 0, keepdims=False)) for p in parts]
    plan_small = _plan_scatter(len(parts))
    flight_small, token = exchange_start(parts + own, plan_small, "scatter_small_start")
    blocks = [_as3d(d_rep_all[k])[0] for k in _REPLICATED]
    blocks[0] = _after(blocks[0], [token])
    plan_rep = _plan_gather_direct(len(blocks))
    flight_rep, token = exchange_start(blocks + [with_own_block(b) for b in blocks], plan_rep, "gather_rep_start")

    big = {k: None for k in _BIG}

    def update_big(l):
        for k in _BIG:
            recv = recv_big[l][k]
            big[k] = adam_layer(recv.reshape((N_DEV, -1, recv.shape[-1])), _as3d(w[k]), _as3d(mom[k]),
                                _as3d(var[k]), big[k], l, f"adam_{k}_l{l}")

    for l in reversed(range(1, depth)):
        update_big(l)
    after = token if depth == 1 else big[_BIG[-1]][0]
    out = {}
    recv_small = exchange_wait(flight_small, plan_small, "scatter_small_wait", after)[len(parts):]
    for k, recv in zip(_SMALL_SHARDED, recv_small):
        r3 = _as3d(w[k])
        out[k] = adam_update(recv.reshape((1, N_DEV, -1, recv.shape[-1])), r3.reshape((1, -1, r3.shape[-1])),
                             _as3d(mom[k]).reshape((1, -1, r3.shape[-1])), _as3d(var[k]).reshape((1, -1, r3.shape[-1])),
                             "adam_" + k)
    recv_rep = exchange_wait(flight_rep, plan_rep, "gather_rep_wait", after)[len(blocks):]
    for k, recv in zip(_REPLICATED, recv_rep):
        r3 = _as3d(w[k])
        flat = lambda a, r3=r3: a.reshape((1, -1, r3.shape[-1]))
        out[k] = adam_update(recv[None], flat(w[k]), flat(mom[k]), flat(var[k]), "adam_" + k)
    for s in sorted(pending, reverse=True):
        scatter_finish(pending[s], out["norm_final"][0])
    update_big(0)
    out.update(big)

    res = [loss, dh[None]]
    for j in range(4):
        res += [out[k][j].reshape(w[k].shape) for k in _WEIGHTS]
    return tuple(res)
```

```python
import functools

import numpy as np
import jax
import jax.numpy as jnp
from jax import lax
from jax.experimental import pallas as pl
from jax.experimental.pallas import tpu as pltpu

F32 = jnp.float32
BF16 = jnp.bfloat16
_MXU_DTYPE = BF16
_INTERPRET = False
_VMEM_LIMIT_BYTES = 48 * 1024 * 1024
_MM_VMEM_BUDGET_BYTES = 36 * 1024 * 1024
_MM_TILE_CAP = 2048

N_DEV = 8
CHUNK = 64
_CHUNK_SHIFT = 6
DN_HEAD_DIM = 128
DN_CONV = 4
MLA_NOPE = 128
MLA_ROPE = 64
MLA_V = 128
MLA_Q_RANK = 512
MLA_KV_RANK = 256
ROPE_BASE = 10000.0
XA_HEADS = 4
FFN_CONV = 3
EPS = 1e-6
LANE = 128

ADAM_LR = 0.001
ADAM_B1 = 0.9
ADAM_B2 = 0.999
ADAM_EPS = 1e-08
ADAM_WD = 0.01
ADAM_STEP = 10
_ADAM_TILE_ELEMS = 256 * 1024

MESH = pl.DeviceIdType.MESH


def _pcall(body, *, name, out_shape, grid=None, in_specs=None, out_specs=None, scratch=(), sem=None, **kw):
    params = pltpu.CompilerParams(dimension_semantics=sem, vmem_limit_bytes=_VMEM_LIMIT_BYTES)
    args = dict(name=name, out_shape=out_shape, scratch_shapes=list(scratch), compiler_params=params,
                interpret=_INTERPRET, **kw)
    if grid is not None:
        args.update(grid=grid)
    if in_specs is not None:
        args.update(in_specs=in_specs)
    if out_specs is not None:
        args.update(out_specs=out_specs)
    return pl.pallas_call(body, **args)


def _tile(n, pref):
    if n <= pref:
        return n
    t = pref
    while t >= 8:
        if n % t == 0:
            return t
        t //= 2
    return n


def _ein_raw(spec, a, b, hi):
    one = lambda x, y: jnp.einsum(spec, x, y, preferred_element_type=F32)
    a_hi, b_hi = a.astype(_MXU_DTYPE), b.astype(_MXU_DTYPE)
    if not hi:
        return one(a_hi, b_hi)
    a_lo = (a.astype(F32) - a_hi.astype(F32)).astype(_MXU_DTYPE)
    b_lo = (b.astype(F32) - b_hi.astype(F32)).astype(_MXU_DTYPE)
    return one(a_hi, b_hi) + (one(a_hi, b_lo) + one(a_lo, b_hi))


def _make_ein(spec, hi=False, diff_b=True):
    a_s, rest = spec.split(",")
    b_s, o_s = rest.split("->")

    @jax.custom_vjp
    def f(a, b):
        return _ein_raw(spec, a, b, hi)

    def fwd(a, b):
        return f(a, b), (a, b)

    def bwd(res, g):
        a, b = res
        da = _ein_raw(f"{o_s},{b_s}->{a_s}", g, b, hi)
        if not diff_b:
            return da.astype(a.dtype), None
        db = _ein_raw(f"{a_s},{o_s}->{b_s}", a, g, hi)
        return da.astype(a.dtype), db.astype(b.dtype)

    f.defvjp(fwd, bwd)
    return f


_nt = _make_ein("qd,kd->qk")
_nn = _make_ein("qk,kd->qd")
_nn_hi_const = _make_ein("qk,kd->qd", hi=True, diff_b=False)
_bnt = _make_ein("hik,hjk->hij")
_bnn = _make_ein("hij,hjv->hiv")
_btn = _make_ein("hck,hcv->hkv")


def _divisor_tiles(n, cap):
    out = [n] if n <= cap else []
    t = cap
    while t >= LANE:
        if t < n and n % t == 0:
            out.append(t)
        t //= 2
    return out or [n]


def _mm_tiles(m, n, k, a_bytes, b_bytes, o_bytes, r_bytes, accumulate=False):
    best = None
    for tk in sorted({k, *_divisor_tiles(k, _MM_TILE_CAP)}):
        for tm in _divisor_tiles(m, _MM_TILE_CAP):
            for tn in _divisor_tiles(n, _MM_TILE_CAP):
                need = 2 * (tm * tk * a_bytes + tk * tn * b_bytes + tm * tn * (o_bytes + r_bytes))
                need += tm * tn * 4 * (1 if tk == k and not accumulate else 2)
                if need > _MM_VMEM_BUDGET_BYTES:
                    continue
                score = (tm * tn * tk, tk, tm)
                if best is None or score > best[0]:
                    best = (score, (tm, tn, tk))
    assert best is not None, (m, n, k)
    return best[1]


def _mm(a, b, *, ta=False, tb=False, res=None, out_dtype=F32, name="mm"):
    m = a.shape[1] if ta else a.shape[0]
    k = a.shape[0] if ta else a.shape[1]
    n = b.shape[0] if tb else b.shape[1]
    assert (b.shape[1] if tb else b.shape[0]) == k, (a.shape, b.shape, ta, tb)
    has_res = res is not None
    tm, tn, tk = _mm_tiles(m, n, k, a.dtype.itemsize, b.dtype.itemsize, jnp.dtype(out_dtype).itemsize,
                           res.dtype.itemsize if has_res else 0)
    nk = k // tk
    body = _mm_body(nk, (((0 if ta else 1,), (1 if tb else 0,)), ((), ())), has_res, out_dtype)
    a_spec = pl.BlockSpec((tk, tm), lambda i, j, kk: (kk, i)) if ta else pl.BlockSpec((tm, tk), lambda i, j, kk: (i, kk))
    b_spec = pl.BlockSpec((tn, tk), lambda i, j, kk: (j, kk)) if tb else pl.BlockSpec((tk, tn), lambda i, j, kk: (kk, j))
    o_spec = pl.BlockSpec((tm, tn), lambda i, j, kk: (i, j))
    in_specs = [a_spec, b_spec] + ([o_spec] if has_res else [])
    args = (a, b) + ((res,) if has_res else ())
    return _pcall(body, name=name, out_shape=jax.ShapeDtypeStruct((m, n), out_dtype),
                  grid=(m // tm, n // tn, nk), in_specs=in_specs, out_specs=o_spec,
                  scratch=[pltpu.VMEM((tm, tn), F32)] if nk > 1 else [],
                  sem=("parallel", "parallel", "arbitrary"))(*args)


def _mm_body(nk, dims, has_res, out_dtype):
    def body(*refs):
        a_ref, b_ref = refs[0], refs[1]
        r_ref = refs[2] if has_res else None
        o_ref = refs[2 + has_res]

        def finish(r):
            if has_res:
                r = r + r_ref[...].astype(F32)
            o_ref[...] = r.astype(out_dtype)

        part = lax.dot_general(a_ref[...].astype(_MXU_DTYPE), b_ref[...].astype(_MXU_DTYPE), dims,
                               preferred_element_type=F32)
        if nk == 1:
            finish(part)
            return
        acc = refs[-1]
        kk = pl.program_id(2)

        @pl.when(kk == 0)
        def _():
            acc[...] = part

        @pl.when(kk > 0)
        def _():
            acc[...] += part

        @pl.when(kk == nk - 1)
        def _():
            finish(acc[...])

    return body


def _mm_groups_out(a, b3, *, ta=False, out_dtype=F32, name="mm_groups"):
    groups, k, nb = b3.shape
    m = a.shape[1] if ta else a.shape[0]
    assert (a.shape[0] if ta else a.shape[1]) == k, (a.shape, b3.shape, ta)
    tm, tn, tk = _mm_tiles(m, nb, k, a.dtype.itemsize, b3.dtype.itemsize, jnp.dtype(out_dtype).itemsize, 0)
    nk, per = k // tk, nb // tn
    body = _mm_body(nk, (((0 if ta else 1,), (0,)), ((), ())), False, out_dtype)
    a_spec = pl.BlockSpec((tk, tm), lambda j, i, kk: (kk, i)) if ta else pl.BlockSpec((tm, tk), lambda j, i, kk: (i, kk))
    b_spec = pl.BlockSpec((None, tk, tn), lambda j, i, kk: (j // per, kk, j % per))
    o_spec = pl.BlockSpec((None, tm, tn), lambda j, i, kk: (j // per, i, j % per))
    return _pcall(body, name=name, out_shape=jax.ShapeDtypeStruct((groups, m, nb), out_dtype),
                  grid=(groups * per, m // tm, nk), in_specs=[a_spec, b_spec], out_specs=o_spec,
                  scratch=[pltpu.VMEM((tm, tn), F32)] if nk > 1 else [],
                  sem=("parallel", "parallel", "arbitrary"))(a, b3)


def _mm_groups_contract(a3, b3, *, out_dtype=F32, name="mm_contract"):
    groups, m, nb = a3.shape
    n = b3.shape[1]
    assert b3.shape[0] == groups and b3.shape[2] == nb, (a3.shape, b3.shape)
    tm, tn, tk = _mm_tiles(m, n, nb, a3.dtype.itemsize, b3.dtype.itemsize, jnp.dtype(out_dtype).itemsize, 0,
                           accumulate=True)
    per = nb // tk
    nk = groups * per
    body = _mm_body(nk, (((1,), (1,)), ((), ())), False, out_dtype)
    a_spec = pl.BlockSpec((None, tm, tk), lambda i, j, kk: (kk // per, i, kk % per))
    b_spec = pl.BlockSpec((None, tn, tk), lambda i, j, kk: (kk // per, j, kk % per))
    return _pcall(body, name=name, out_shape=jax.ShapeDtypeStruct((m, n), out_dtype),
                  grid=(m // tm, n // tn, nk), in_specs=[a_spec, b_spec],
                  out_specs=pl.BlockSpec((tm, tn), lambda i, j, kk: (i, j)),
                  scratch=[pltpu.VMEM((tm, tn), F32)], sem=("parallel", "parallel", "arbitrary"))(a3, b3)


@functools.partial(jax.custom_vjp, nondiff_argnums=(3,))
def _linear_res(a, w, res, name):
    return _mm(a, w, res=res, name=name + "_fwd")


def _linear_res_fwd(a, w, res, name):
    return _mm(a, w, res=res, name=name + "_fwd"), (a, w)


def _linear_res_bwd(name, saved, g):
    a, w = saved
    gm = g.astype(_MXU_DTYPE)
    da = _mm(gm, w, tb=True, out_dtype=a.dtype, name=name + "_da")
    dw = _mm(a, gm, ta=True, out_dtype=w.dtype, name=name + "_dw")
    return da, dw, g


_linear_res.defvjp(_linear_res_fwd, _linear_res_bwd)


@functools.partial(jax.custom_vjp, nondiff_argnums=(2,))
def _linear(a, w, name):
    return _mm(a, w, name=name + "_fwd")


def _linear_fwd(a, w, name):
    return _mm(a, w, name=name + "_fwd"), (a, w)


def _linear_bwd(name, saved, g):
    a, w = saved
    gm = g.astype(_MXU_DTYPE)
    da = _mm(gm, w, tb=True, out_dtype=a.dtype, name=name + "_da")
    dw = _mm(a, gm, ta=True, out_dtype=w.dtype, name=name + "_dw")
    return da, dw


_linear.defvjp(_linear_fwd, _linear_bwd)


def linear(a, w, name, res=None):
    return _linear(a, w, name) if res is None else _linear_res(a, w, res, name)


def _tiled_specs(arrs, kinds, t, axis):
    specs = []
    for a, kind in zip(arrs, kinds):
        if kind == "whole":
            specs.append(pl.BlockSpec(a.shape, lambda i, nd=a.ndim: (0,) * nd))
        elif axis == 0:
            specs.append(pl.BlockSpec((t, a.shape[1]), lambda i: (i, 0)))
        else:
            specs.append(pl.BlockSpec((a.shape[0], t), lambda i: (0, i)))
    return specs


def tilewise(fn, name, args, kinds, outs, *, axis, t, diff):
    n_in = len(args)
    length = next(a.shape[axis] for a, kd in zip(args, kinds) if kd == "tile")
    steps = length // t
    assert steps * t == length, (name, length, t)

    def out_sds(other, dtype):
        return jax.ShapeDtypeStruct((length, other) if axis == 0 else (other, length), dtype)

    def out_spec(other):
        return pl.BlockSpec((t, other), lambda i: (i, 0)) if axis == 0 else pl.BlockSpec((other, t), lambda i: (0, i))

    def run_fwd(*xs):
        def body(*refs):
            vals = fn(*[r[...] for r in refs[:n_in]])
            for o_ref, v in zip(refs[n_in:], vals):
                o_ref[...] = v.astype(o_ref.dtype)

        return _pcall(body, name=name + "_fwd", out_shape=[out_sds(o, d) for o, d in outs], grid=(steps,),
                      in_specs=_tiled_specs(xs, kinds, t, axis), out_specs=[out_spec(o) for o, _ in outs],
                      sem=("parallel",))(*xs)

    didx = [i for i in range(n_in) if diff[i]]

    def run_bwd(xs, cts):
        def body(*refs):
            x_refs, c_refs, g_refs = refs[:n_in], refs[n_in:n_in + len(outs)], refs[n_in + len(outs):]
            vals = [r[...] for r in x_refs]

            def g(*dvals):
                full = list(vals)
                for i, v in zip(didx, dvals):
                    full[i] = v
                return tuple(fn(*full))

            prim_out, vjp = jax.vjp(g, *[vals[i] for i in didx])
            grads = vjp(tuple(c[...].astype(o.dtype) for c, o in zip(c_refs, prim_out)))
            step = pl.program_id(0)
            for i, g_ref, gr in zip(didx, g_refs, grads):
                if kinds[i] == "whole":
                    @pl.when(step == 0)
                    def _(g_ref=g_ref):
                        g_ref[...] = jnp.zeros_like(g_ref)

                    g_ref[...] += gr.astype(g_ref.dtype)
                else:
                    g_ref[...] = gr.astype(g_ref.dtype)

        g_shapes = [jax.ShapeDtypeStruct(xs[i].shape, xs[i].dtype) for i in didx]
        g_specs = _tiled_specs([xs[i] for i in didx], [kinds[i] for i in didx], t, axis)
        ct_specs = [out_spec(o) for o, _ in outs]
        any_whole = any(kinds[i] == "whole" for i in didx)
        return _pcall(body, name=name + "_bwd", out_shape=g_shapes, grid=(steps,),
                      in_specs=_tiled_specs(xs, kinds, t, axis) + ct_specs, out_specs=g_specs,
                      sem=("arbitrary" if any_whole else "parallel",))(*xs, *cts)

    @jax.custom_vjp
    def op(*xs):
        return tuple(run_fwd(*xs))

    def op_fwd(*xs):
        return tuple(run_fwd(*xs)), xs

    def op_bwd(xs, cts):
        grads = run_bwd(xs, cts)
        full = [None] * n_in
        for i, gr in zip(didx, grads):
            full[i] = gr
        return tuple(full)

    op.defvjp(op_fwd, op_bwd)
    return op(*args)


def _silu(x):
    return x * (1.0 / (1.0 + jnp.exp(-x)))


def _softplus(x):
    return jnp.maximum(x, 0.0) + jnp.log(1.0 + jnp.exp(-jnp.abs(x)))


def _rms_tile(x, gain, out_dtype):
    xf = x.astype(F32)
    y = xf * lax.rsqrt(jnp.mean(xf * xf, axis=-1, keepdims=True) + EPS)
    return (y * gain).astype(out_dtype)


def rms_norm(x, gain, name, out_dtype, tr=256):
    def fn(xt, gt):
        return (_rms_tile(xt, gt, out_dtype),)

    return tilewise(fn, name, [x, gain.reshape(1, -1)], ["tile", "whole"], [(x.shape[1], out_dtype)],
                    axis=0, t=_tile(x.shape[0], tr), diff=[True, True])[0]


def _shift_down_raw(x, s):
    rows = lax.broadcasted_iota(jnp.int32, x.shape, 0)
    return jnp.where(rows >= s, pltpu.roll(x, s, 0), 0.0)


def _shift_up_raw(x, s):
    n = x.shape[0]
    rows = lax.broadcasted_iota(jnp.int32, x.shape, 0)
    return jnp.where(rows < n - s, pltpu.roll(x, n - s, 0), 0.0)


@functools.partial(jax.custom_vjp, nondiff_argnums=(1,))
def _shift_down(x, s):
    return _shift_down_raw(x, s)


def _shift_down_fwd(x, s):
    return _shift_down_raw(x, s), None


def _shift_down_bwd(s, _, g):
    return (_shift_up_raw(g, s),)


_shift_down.defvjp(_shift_down_fwd, _shift_down_bwd)


def _causal_dwconv_tile(x, w):
    kk = w.shape[0]
    y = x * w[kk - 1:kk, :]
    for j in range(kk - 1):
        y = y + _shift_down(x, kk - 1 - j) * w[j:j + 1, :]
    return y


def dn_qkv(raw, conv_w, name):
    width = raw.shape[1] // 3
    is_qk = (jnp.arange(raw.shape[1]) < 2 * width).astype(F32).reshape(1, -1)

    def fn(x, w, flag):
        y = _silu(_causal_dwconv_tile(x, w))
        yn = y * lax.rsqrt(jnp.sum(y * y, axis=-1, keepdims=True) + EPS)
        return (jnp.where(flag > 0.5, yn, y),)

    return tilewise(fn, name, [raw, conv_w, is_qk], ["tile", "tile", "tile"], [(raw.shape[0], F32)],
                    axis=1, t=DN_HEAD_DIM, diff=[True, True, False])[0]


def dn_gates(b, a, a_log, dt_bias, name):
    hh = b.shape[1]
    tr = _tile(b.shape[0], 256)

    def fn(bt, at, al, db):
        beta = 1.0 / (1.0 + jnp.exp(-bt))
        g = -jnp.exp(al) * _softplus(at + db)
        pos = lax.broadcasted_iota(jnp.int32, g.shape, 0) & (CHUNK - 1)
        step = 1
        while step < CHUNK:
            g = g + jnp.where(pos >= step, _shift_down(g, step), 0.0)
            step *= 2
        return g, beta

    return tilewise(fn, name, [b, a, a_log.reshape(1, -1), dt_bias.reshape(1, -1)],
                    ["tile", "tile", "whole", "whole"], [(hh, F32), (hh, F32)], axis=0, t=tr, diff=[True] * 4)


def _unit_lower_inv_raw(low):
    c = low.shape[-1]
    ii = lax.broadcasted_iota(jnp.int32, (c, c), 0)
    jj = lax.broadcasted_iota(jnp.int32, (c, c), 1)
    inv = (ii == jj).astype(F32)[None] - low
    p = low
    n = 1
    while 2 * n < c:
        p = _ein_raw("hij,hjk->hik", p, p, True)
        inv = inv + _ein_raw("hij,hjk->hik", inv, p, True)
        n *= 2
    return inv


@jax.custom_vjp
def _unit_lower_inv(low):
    return _unit_lower_inv_raw(low)


def _unit_lower_inv_fwd(low):
    inv = _unit_lower_inv_raw(low)
    return inv, inv


def _unit_lower_inv_bwd(inv, g):
    t = _ein_raw("hji,hjk->hik", inv, g, True)
    return (-_ein_raw("hik,hlk->hil", t, inv, True),)


_unit_lower_inv.defvjp(_unit_lower_inv_fwd, _unit_lower_inv_bwd)


def _dn_chunk(state, q, k, v, gc, gr, gl, bc):
    c = q.shape[1]
    q = q * (q.shape[-1] ** -0.5)
    ii = lax.broadcasted_iota(jnp.int32, (c, c), 0)
    jj = lax.broadcasted_iota(jnp.int32, (c, c), 1)
    incl = (jj <= ii)[None]
    strict = (jj < ii)[None]
    decay = jnp.where(incl, jnp.exp(jnp.where(incl, gc - gr, 0.0)), 0.0)
    kb = k * bc
    low = jnp.where(strict, _bnt(kb, k) * decay, 0.0)
    ainv = _unit_lower_inv(low)
    eg = jnp.exp(gc)
    u = _bnn(ainv, v * bc)
    w = _bnn(ainv, kb * eg)
    attn = _bnt(q, k) * decay
    q_dec = q * eg
    k_dec = k * jnp.exp(gl - gc)
    v_new = u - _bnn(w, state)
    o = _bnn(q_dec, state) + _bnn(attn, v_new)
    new_state = state * jnp.exp(gl) + _btn(k_dec, v_new)
    return new_state, o


def _dn_heads(ref, base, hh):
    return jnp.stack([ref[:, base + h * DN_HEAD_DIM: base + (h + 1) * DN_HEAD_DIM] for h in range(hh)])


def _dn_fwd_call(qkv, gc, gr, gl, bc, name):
    s, w3 = qkv.shape
    width = w3 // 3
    hh = width // DN_HEAD_DIM
    d = DN_HEAD_DIM
    n = s // CHUNK

    def body(qkv_ref, gc_ref, gr_ref, gl_ref, bc_ref, o_ref, st_ref, state):
        @pl.when(pl.program_id(0) == 0)
        def _():
            state[...] = jnp.zeros_like(state)

        s_in = state[...]
        st_ref[0] = s_in
        new_s, o = _dn_chunk(s_in, _dn_heads(qkv_ref, 0, hh), _dn_heads(qkv_ref, width, hh),
                             _dn_heads(qkv_ref, 2 * width, hh), gc_ref[0], gr_ref[0], gl_ref[0], bc_ref[0])
        state[...] = new_s
        for h in range(hh):
            o_ref[:, h * d:(h + 1) * d] = o[h]

    g4 = lambda i: (i, 0, 0, 0)
    return _pcall(
        body, name=name + "_fwd",
        out_shape=[jax.ShapeDtypeStruct((s, width), F32), jax.ShapeDtypeStruct((n, hh, d, d), F32)],
        grid=(n,),
        in_specs=[pl.BlockSpec((CHUNK, w3), lambda i: (i, 0)), pl.BlockSpec((1, hh, CHUNK, 1), g4),
                  pl.BlockSpec((1, hh, 1, CHUNK), g4), pl.BlockSpec((1, hh, 1, 1), g4),
                  pl.BlockSpec((1, hh, CHUNK, 1), g4)],
        out_specs=[pl.BlockSpec((CHUNK, width), lambda i: (i, 0)), pl.BlockSpec((1, hh, d, d), g4)],
        scratch=[pltpu.VMEM((hh, d, d), F32)], sem=("arbitrary",))(qkv, gc, gr, gl, bc)


def _dn_bwd_call(qkv, gc, gr, gl, bc, states, do, name):
    s, w3 = qkv.shape
    width = w3 // 3
    hh = width // DN_HEAD_DIM
    d = DN_HEAD_DIM
    n = s // CHUNK

    def body(qkv_ref, gc_ref, gr_ref, gl_ref, bc_ref, st_ref, do_ref,
             dqkv_ref, dgc_ref, dgr_ref, dgl_ref, dbc_ref, dstate):
        @pl.when(pl.program_id(0) == 0)
        def _():
            dstate[...] = jnp.zeros_like(dstate)

        prim = (st_ref[0], _dn_heads(qkv_ref, 0, hh), _dn_heads(qkv_ref, width, hh),
                _dn_heads(qkv_ref, 2 * width, hh), gc_ref[0], gr_ref[0], gl_ref[0], bc_ref[0])
        _, vjp = jax.vjp(_dn_chunk, *prim)
        ds, dq, dk, dv, dgc, dgr, dgl, dbc = vjp((dstate[...], _dn_heads(do_ref, 0, hh)))
        dstate[...] = ds
        for h in range(hh):
            dqkv_ref[:, h * d:(h + 1) * d] = dq[h]
            dqkv_ref[:, width + h * d: width + (h + 1) * d] = dk[h]
            dqkv_ref[:, 2 * width + h * d: 2 * width + (h + 1) * d] = dv[h]
        dgc_ref[0] = dgc
        dgr_ref[0] = dgr
        dgl_ref[0] = dgl
        dbc_ref[0] = dbc

    r2 = lambda i: (n - 1 - i, 0)
    r4 = lambda i: (n - 1 - i, 0, 0, 0)
    spec_c = pl.BlockSpec((1, hh, CHUNK, 1), r4)
    spec_r = pl.BlockSpec((1, hh, 1, CHUNK), r4)
    spec_l = pl.BlockSpec((1, hh, 1, 1), r4)
    return _pcall(
        body, name=name + "_bwd",
        out_shape=[jax.ShapeDtypeStruct(qkv.shape, F32), jax.ShapeDtypeStruct(gc.shape, F32),
                   jax.ShapeDtypeStruct(gr.shape, F32), jax.ShapeDtypeStruct(gl.shape, F32),
                   jax.ShapeDtypeStruct(bc.shape, F32)],
        grid=(n,),
        in_specs=[pl.BlockSpec((CHUNK, w3), r2), spec_c, spec_r, spec_l, spec_c,
                  pl.BlockSpec((1, hh, d, d), r4), pl.BlockSpec((CHUNK, width), r2)],
        out_specs=[pl.BlockSpec((CHUNK, w3), r2), spec_c, spec_r, spec_l, spec_c],
        scratch=[pltpu.VMEM((hh, d, d), F32)], sem=("arbitrary",))(qkv, gc, gr, gl, bc, states, do)


@functools.partial(jax.custom_vjp, nondiff_argnums=(5,))
def dn_core(qkv, gc, gr, gl, bc, name):
    return _dn_fwd_call(qkv, gc, gr, gl, bc, name)[0]


def _dn_core_fwd(qkv, gc, gr, gl, bc, name):
    o, states = _dn_fwd_call(qkv, gc, gr, gl, bc, name)
    return o, (qkv, gc, gr, gl, bc, states)


def _dn_core_bwd(name, saved, do):
    return tuple(_dn_bwd_call(*saved, do, name))


dn_core.defvjp(_dn_core_fwd, _dn_core_bwd)


def dn_out_gate(o, z, out_norm, name):
    hh = o.shape[1] // DN_HEAD_DIM
    gain = jnp.tile(out_norm.reshape(1, -1), (1, hh))

    def fn(ot, zt, gt):
        y = ot * lax.rsqrt(jnp.mean(ot * ot, axis=-1, keepdims=True) + EPS) * gt
        return (y * _silu(zt),)

    return tilewise(fn, name, [o, z, gain], ["tile", "tile", "tile"], [(o.shape[0], BF16)],
                    axis=1, t=DN_HEAD_DIM, diff=[True, True, True])[0]


def _attn_tile(qn, kn, v, qp, kp, q0, scale, causal):
    s = _nt(qn, kn)
    if qp is not None:
        s = s + _nt(qp, kp)
    s = s * scale
    if causal:
        qpos = q0 + lax.broadcasted_iota(jnp.int32, s.shape, 0)
        kpos = lax.broadcasted_iota(jnp.int32, s.shape, 1)
        s = jnp.where((kpos >> _CHUNK_SHIFT) <= (qpos >> _CHUNK_SHIFT), s, -1e30)
    e = jnp.exp(s - jnp.max(s, axis=-1, keepdims=True))
    p = e / jnp.sum(e, axis=-1, keepdims=True)
    return _nn(p, v)


def _attn_specs(sq, sk, dh, dp, tb):
    q_spec = pl.BlockSpec((tb, dh), lambda h, i: (i, h))
    kv_spec = pl.BlockSpec((sk, dh), lambda h, i: (0, h))
    qp_spec = pl.BlockSpec((1, tb, dp), lambda h, i: (h, i, 0)) if dp else None
    kp_spec = pl.BlockSpec((sk, dp), lambda h, i: (0, 0)) if dp else None
    return q_spec, kv_spec, qp_spec, kp_spec


def _attn_plan(sq, sk, tq, causal):
    tb = sq if causal else tq
    subs = [(slice(i * tq, (i + 1) * tq), (i + 1) * tq if causal else sk) for i in range(tb // tq)]
    return tb, subs


def _attn_fwd_call(q, k, v, qp, kp, *, dh, scale, causal, tq, name):
    sq, sk = q.shape[0], k.shape[0]
    heads = q.shape[1] // dh
    dp = qp.shape[-1] if qp is not None else 0
    tb, subs = _attn_plan(sq, sk, tq, causal)
    q_spec, kv_spec, qp_spec, kp_spec = _attn_specs(sq, sk, dh, dp, tb)

    def body(*refs):
        q_ref, k_ref, v_ref = refs[:3]
        qp_ref, kp_ref = (refs[3], refs[4]) if dp else (None, None)
        o_ref = refs[-1]
        row0 = pl.program_id(1) * tb
        for rows, ke in subs:
            o = _attn_tile(q_ref[rows, :], k_ref[:ke, :], v_ref[:ke, :], qp_ref[0, rows, :] if dp else None,
                           kp_ref[:ke, :] if dp else None, row0 + rows.start, scale, causal)
            o_ref[rows, :] = o.astype(o_ref.dtype)

    in_specs = [q_spec, kv_spec, kv_spec] + ([qp_spec, kp_spec] if dp else [])
    args = (q, k, v) + ((qp, kp) if dp else ())
    return _pcall(body, name=name + "_fwd", out_shape=jax.ShapeDtypeStruct((sq, heads * dh), BF16),
                  grid=(heads, sq // tb), in_specs=in_specs, out_specs=q_spec,
                  sem=("parallel", "parallel"))(*args)


def _attn_bwd_call(q, k, v, qp, kp, do, *, dh, scale, causal, tq, name):
    sq, sk = q.shape[0], k.shape[0]
    heads = q.shape[1] // dh
    dp = qp.shape[-1] if qp is not None else 0
    tb, subs = _attn_plan(sq, sk, tq, causal)
    q_spec, kv_spec, qp_spec, kp_spec = _attn_specs(sq, sk, dh, dp, tb)

    def body(*refs):
        h, i = pl.program_id(0), pl.program_id(1)
        if dp:
            q_ref, k_ref, v_ref, qp_ref, kp_ref, do_ref, dq_ref, dk_ref, dv_ref, dqp_ref, dkp_ref = refs
        else:
            q_ref, k_ref, v_ref, do_ref, dq_ref, dk_ref, dv_ref = refs

        @pl.when(i == 0)
        def _():
            dk_ref[...] = jnp.zeros_like(dk_ref)
            dv_ref[...] = jnp.zeros_like(dv_ref)

        if dp:
            @pl.when(jnp.logical_and(h == 0, i == 0))
            def _():
                dkp_ref[...] = jnp.zeros_like(dkp_ref)

        for rows, ke in subs:
            q0 = i * tb + rows.start
            if dp:
                prim = (q_ref[rows, :], k_ref[:ke, :], v_ref[:ke, :], qp_ref[0, rows, :], kp_ref[:ke, :])
                f = lambda a, b, c, d, e, q0=q0: _attn_tile(a, b, c, d, e, q0, scale, causal)
            else:
                prim = (q_ref[rows, :], k_ref[:ke, :], v_ref[:ke, :])
                f = lambda a, b, c, q0=q0: _attn_tile(a, b, c, None, None, q0, scale, causal)
            _, vjp = jax.vjp(f, *prim)
            grads = vjp(do_ref[rows, :].astype(F32))
            dq_ref[rows, :] = grads[0]
            dk_ref[:ke, :] += grads[1]
            dv_ref[:ke, :] += grads[2]
            if dp:
                dqp_ref[0, rows, :] = grads[3]
                dkp_ref[:ke, :] += grads[4]

    in_specs = [q_spec, kv_spec, kv_spec] + ([qp_spec, kp_spec] if dp else []) + [q_spec]
    out_shape = [jax.ShapeDtypeStruct(q.shape, F32), jax.ShapeDtypeStruct(k.shape, F32),
                 jax.ShapeDtypeStruct(v.shape, F32)]
    out_specs = [q_spec, kv_spec, kv_spec]
    if dp:
        out_shape += [jax.ShapeDtypeStruct(qp.shape, F32), jax.ShapeDtypeStruct(kp.shape, F32)]
        out_specs += [qp_spec, kp_spec]
    args = (q, k, v) + ((qp, kp) if dp else ()) + (do,)
    return _pcall(body, name=name + "_bwd", out_shape=out_shape, grid=(heads, sq // tb), in_specs=in_specs,
                  out_specs=out_specs, sem=("arbitrary", "arbitrary"))(*args)


def attention(q, k, v, qp=None, kp=None, *, dh, scale, causal, name, tq=256):
    tq = _tile(q.shape[0], tq)
    kw = dict(dh=dh, scale=scale, causal=causal, tq=tq, name=name)
    has_pe = qp is not None

    @jax.custom_vjp
    def op(*xs):
        return _attn_fwd_call(*xs, **kw) if has_pe else _attn_fwd_call(*xs, None, None, **kw)

    def op_fwd(*xs):
        return (_attn_fwd_call(*xs, **kw) if has_pe else _attn_fwd_call(*xs, None, None, **kw)), xs

    def op_bwd(xs, do):
        full = xs if has_pe else xs + (None, None)
        return tuple(_attn_bwd_call(*full, do, **kw))

    op.defvjp(op_fwd, op_bwd)
    return op(q, k, v, qp, kp) if has_pe else op(q, k, v)


def _rope_tables(positions, reps):
    half = MLA_ROPE // 2
    inv = ROPE_BASE ** (-jnp.arange(0, MLA_ROPE, 2, dtype=F32) / MLA_ROPE)
    ang = positions.astype(F32)[:, None] * inv
    cos, sin = jnp.cos(ang), jnp.sin(ang)
    c = jnp.tile(jnp.concatenate([cos, cos], axis=-1), (1, reps))
    s = jnp.tile(jnp.concatenate([sin, sin], axis=-1), (1, reps))
    rot = np.zeros((MLA_ROPE, MLA_ROPE), np.float32)
    for i in range(half):
        rot[i + half, i] = -1.0
        rot[i, i + half] = 1.0
    return c, s, jnp.asarray(np.kron(np.eye(reps, dtype=np.float32), rot))


def rope(x, positions, name):
    c, s, rot = _rope_tables(positions, x.shape[1] // MLA_ROPE)

    def fn(xt, ct, st, rt):
        return (xt * ct + _nn_hi_const(xt, rt) * st,)

    return tilewise(fn, name, [x, c, s, rot], ["tile", "tile", "tile", "whole"], [(x.shape[1], F32)],
                    axis=0, t=_tile(x.shape[0], 256), diff=[True, False, False, False])[0]


def _ffn_gate_tile(g, u, wg, wu, bg, bu):
    return _silu(_causal_dwconv_tile(g, wg) + bg) * (_causal_dwconv_tile(u, wu) + bu)


def _ffn_gate_specs(pre4, conv4):
    _, groups, s, nb = pre4.shape
    tc = LANE if nb % LANE == 0 else nb
    per = nb // tc
    at = lambda g, j: (0, g, 0, j)
    specs = dict(pre=pl.BlockSpec((2, None, s, tc), at), conv=pl.BlockSpec((2, None, conv4.shape[2], tc), at),
                 bias=pl.BlockSpec((2, None, 1, tc), at), act=pl.BlockSpec((s, tc), lambda g, j: (0, g * per + j)))
    return specs, (groups, per)


def _ffn_gate_fwd_call(pre4, conv4, bias4, name):
    specs, grid = _ffn_gate_specs(pre4, conv4)
    _, groups, s, nb = pre4.shape

    def body(pre_ref, conv_ref, bias_ref, act_ref):
        act_ref[...] = _ffn_gate_tile(pre_ref[0], pre_ref[1], conv_ref[0], conv_ref[1], bias_ref[0],
                                      bias_ref[1]).astype(act_ref.dtype)

    return _pcall(body, name=name + "_fwd", out_shape=jax.ShapeDtypeStruct((s, groups * nb), BF16), grid=grid,
                  in_specs=[specs["pre"], specs["conv"], specs["bias"]], out_specs=specs["act"],
                  sem=("parallel", "parallel"))(pre4, conv4, bias4)


def _ffn_gate_bwd_call(pre4, conv4, bias4, dact, name):
    specs, grid = _ffn_gate_specs(pre4, conv4)

    def body(pre_ref, conv_ref, bias_ref, dact_ref, dpre_ref, dconv_ref, dbias_ref):
        prim = (pre_ref[0], pre_ref[1], conv_ref[0], conv_ref[1], bias_ref[0], bias_ref[1])
        _, vjp = jax.vjp(_ffn_gate_tile, *prim)
        dg, du, dwg, dwu, dbg, dbu = vjp(dact_ref[...].astype(F32))
        dpre_ref[0] = dg.astype(dpre_ref.dtype)
        dpre_ref[1] = du.astype(dpre_ref.dtype)
        dconv_ref[0] = dwg
        dconv_ref[1] = dwu
        dbias_ref[0] = dbg
        dbias_ref[1] = dbu

    return _pcall(body, name=name + "_bwd",
                  out_shape=[jax.ShapeDtypeStruct(pre4.shape, _MXU_DTYPE), jax.ShapeDtypeStruct(conv4.shape, F32),
                             jax.ShapeDtypeStruct(bias4.shape, F32)],
                  grid=grid, in_specs=[specs["pre"], specs["conv"], specs["bias"], specs["act"]],
                  out_specs=[specs["pre"], specs["conv"], specs["bias"]], sem=("parallel", "parallel"))(
                      pre4, conv4, bias4, dact)


@functools.partial(jax.custom_vjp, nondiff_argnums=(4,))
def ffn_up_gate(hn, w3, conv3, bias, name):
    return _ffn_up_gate_fwd(hn, w3, conv3, bias, name)[0]


def _ffn_up_gate_fwd(hn, w3, conv3, bias, name):
    groups, _, nb = w3.shape
    half = groups // 2
    pre4 = _mm_groups_out(hn, w3, name=name + "_up_fwd").reshape(2, half, hn.shape[0], nb)
    conv4 = conv3.reshape(2, half, conv3.shape[1], nb)
    bias4 = bias.reshape(2, half, 1, nb)
    return _ffn_gate_fwd_call(pre4, conv4, bias4, name + "_gate"), (hn, w3, pre4, conv4, bias4)


def _ffn_up_gate_bwd(name, saved, dact):
    hn, w3, pre4, conv4, bias4 = saved
    dpre4, dconv4, dbias4 = _ffn_gate_bwd_call(pre4, conv4, bias4, dact, name + "_gate")
    dpre3 = dpre4.reshape((w3.shape[0],) + dpre4.shape[2:])
    dw3 = _mm_groups_out(hn, dpre3, ta=True, out_dtype=w3.dtype, name=name + "_up_dw")
    dhn = _mm_groups_contract(dpre3, w3, out_dtype=hn.dtype, name=name + "_up_da")
    return dhn, dw3, dconv4.reshape((w3.shape[0],) + dconv4.shape[2:]), dbias4.reshape(-1)


ffn_up_gate.defvjp(_ffn_up_gate_fwd, _ffn_up_gate_bwd)


def loss_rows(h, gain, target, name):
    def fn(ht, tt, gt):
        err = _rms_tile(ht, gt, F32) - tt
        return (0.5 * jnp.mean(err * err, axis=-1, keepdims=True),)

    return tilewise(fn, name, [h, target, gain.reshape(1, -1)], ["tile", "tile", "whole"], [(1, F32)],
                    axis=0, t=_tile(h.shape[0], 256), diff=[True, False, True])[0]


def _col_groups(w, per_head, lo, hi):
    r = w.shape[0]
    return w.reshape(r, -1, per_head)[:, :, lo:hi].reshape(r, -1)


def _layer_mix(h, lw, rp, l, positions):
    s, d = h.shape
    width = d // 2
    dn_heads = width // DN_HEAD_DIM
    mla_heads = (d - width) // MLA_V
    nm = f"l{l}_"
    w_in = lw["w_in"]
    c0 = 4 * width
    c1 = c0 + 2 * dn_heads
    c2 = c1 + MLA_Q_RANK
    c3 = c2 + MLA_KV_RANK + MLA_ROPE
    rest_pad = (-(c3 - c0)) % LANE
    w_rest = jnp.pad(w_in[:, c0:c3], ((0, 0), (0, rest_pad)))

    u = rms_norm(h, rp["norm_mix"], nm + "norm_mix", BF16)
    qkv_raw = linear(u, w_in[:, :3 * width], nm + "in_qkv")
    z = linear(u, w_in[:, 3 * width:c0], nm + "in_z")
    rest = linear(u, w_rest, nm + "in_rest")

    qkv = dn_qkv(qkv_raw, lw["dn_conv"], nm + "dn_qkv")
    csum, beta = dn_gates(rest[:, :dn_heads], rest[:, dn_heads:2 * dn_heads], rp["dn_a_log"],
                          rp["dn_dt_bias"], nm + "dn_gates")
    n = s // CHUNK
    g3 = csum.reshape(n, CHUNK, dn_heads).transpose(0, 2, 1)
    b3 = beta.reshape(n, CHUNK, dn_heads).transpose(0, 2, 1)
    o_dn = dn_core(qkv, g3[..., None], g3[:, :, None, :], g3[:, :, CHUNK - 1][..., None, None], b3[..., None],
                   nm + "dn_core")
    o_dn = dn_out_gate(o_dn, z, rp["dn_out_norm"], nm + "dn_gate")

    mq = rest[:, c1 - c0:c2 - c0]
    mkv = rest[:, c2 - c0:c3 - c0]
    qn = rms_norm(mq, rp["mla_q_norm"], nm + "mla_qnorm", BF16)
    per_q = MLA_NOPE + MLA_ROPE
    q_nope = linear(qn, _col_groups(lw["mla_w_qb"], per_q, 0, MLA_NOPE), nm + "mla_qn")
    q_pe = linear(qn, _col_groups(lw["mla_w_qb"], per_q, MLA_NOPE, per_q), nm + "mla_qp")
    kvn = rms_norm(mkv[:, :MLA_KV_RANK], rp["mla_kv_norm"], nm + "mla_kvnorm", BF16)
    per_kv = MLA_NOPE + MLA_V
    k_nope = linear(kvn, _col_groups(lw["mla_w_kvb"], per_kv, 0, MLA_NOPE), nm + "mla_kn")
    v_mla = linear(kvn, _col_groups(lw["mla_w_kvb"], per_kv, MLA_NOPE, per_kv), nm + "mla_v")
    q_pe = rope(q_pe, positions, nm + "rope_q")
    k_pe = rope(mkv[:, MLA_KV_RANK:], positions, nm + "rope_k")
    q_pe = q_pe.reshape(s, mla_heads, MLA_ROPE).transpose(1, 0, 2)
    o_mla = attention(q_nope, k_nope, v_mla, q_pe, k_pe, dh=MLA_NOPE, scale=per_q ** -0.5, causal=True,
                      name=nm + "mla_attn")

    h = linear(o_dn, lw["w_out"][:width], nm + "out_dn", res=h)
    return linear(o_mla, lw["w_out"][width:], nm + "out_mla", res=h)


def _layer_xattn(h, lw, rp, l, mem_n):
    d = h.shape[1]
    nm = f"l{l}_"
    hn = rms_norm(h, rp["norm_xattn"], nm + "norm_xattn", BF16)
    xq = linear(hn, lw["xa_wq"], nm + "xa_q")
    xk = linear(mem_n, lw["xa_wk"], nm + "xa_k")
    xv = linear(mem_n, lw["xa_wv"], nm + "xa_v")
    xdh = d // XA_HEADS
    xo = attention(xq, xk, xv, dh=xdh, scale=xdh ** -0.5, causal=False, name=nm + "xattn")
    return linear(xo, lw["xa_wo"], nm + "xa_o", res=h)


def _layer_ffn(h, lw, rp, l):
    nm = f"l{l}_"
    hn = rms_norm(h, rp["norm_ffn"], nm + "norm_ffn", BF16)
    act = ffn_up_gate(hn, lw["ffn_w_up"], lw["ffn_conv"], rp["ffn_conv_bias"], nm + "ffn")
    return linear(act, lw["ffn_w_down"], nm + "ffn_down", res=h)


_COL_SHARDED = ("w_in", "mla_w_qb", "mla_w_kvb", "ffn_w_up", "dn_conv", "ffn_conv")
_ROW_SHARDED = ("w_out", "xa_wq", "xa_wk", "xa_wv", "xa_wo", "ffn_w_down")
_BIG = ("w_in", "mla_w_qb", "mla_w_kvb", "w_out", "xa_wq", "xa_wk", "xa_wv", "xa_wo", "ffn_w_up", "ffn_w_down")
_SMALL_SHARDED = ("dn_conv", "ffn_conv")
_STAGES = (("_mix", ("w_in", "mla_w_qb", "mla_w_kvb", "w_out"),
            ("norm_mix", "dn_a_log", "dn_dt_bias", "dn_out_norm", "mla_q_norm", "mla_kv_norm"), "norm_mix"),
           ("_xattn", ("xa_wq", "xa_wk", "xa_wv", "xa_wo"), ("norm_xattn",), "norm_xattn"),
           ("_ffn", ("ffn_w_up", "ffn_w_down"), ("norm_ffn", "ffn_conv_bias"), "norm_ffn"))
_REPLICATED = ("norm_mix", "dn_a_log", "dn_dt_bias", "dn_out_norm", "mla_q_norm", "mla_kv_norm", "mem_norm",
               "norm_xattn", "norm_ffn", "ffn_conv_bias", "norm_final")
_WEIGHTS = ("norm_mix", "w_in", "dn_conv", "dn_a_log", "dn_dt_bias", "dn_out_norm", "mla_q_norm", "mla_w_qb",
            "mla_kv_norm", "mla_w_kvb", "w_out", "mem_norm", "norm_xattn", "xa_wq", "xa_wk", "xa_wv", "xa_wo",
            "norm_ffn", "ffn_w_up", "ffn_conv", "ffn_conv_bias", "ffn_w_down", "norm_final")


def _assemble(name, g):
    if name in _ROW_SHARDED:
        return g.reshape((-1,) + g.shape[2:])
    return jnp.moveaxis(g, 0, -2).reshape(g.shape[1:-1] + (-1,))


def _mix(h, gw, conv, rp, l, positions):
    lw = {k: _assemble(k, v) for k, v in gw.items()}
    lw["dn_conv"] = _assemble("dn_conv", conv)
    return _layer_mix(h, lw, rp, l, positions)


def _xattn(h, gw, rp, l, mem_n):
    return _layer_xattn(h, {k: _assemble(k, v) for k, v in gw.items()}, rp, l, mem_n)


def _ffn(h, gw, conv, rp, l):
    lw = {"ffn_w_up": gw["ffn_w_up"], "ffn_w_down": _assemble("ffn_w_down", gw["ffn_w_down"]), "ffn_conv": conv}
    return _layer_ffn(h, lw, rp, l)


def _my_index():
    return 4 * lax.axis_index("x") + 2 * lax.axis_index("y") + lax.axis_index("c")


def _peer(k):
    x, y, c = lax.axis_index("x"), lax.axis_index("y"), lax.axis_index("c")
    px = (1 - x) if k & 4 else x
    py = (1 - y) if k & 2 else y
    pc = (1 - c) if k & 1 else c
    return (px, py, pc), 4 * px + 2 * py + pc


_ANY = pl.BlockSpec(memory_space=pl.ANY)


def all_gather(shards, name):
    n = len(shards)

    def body(*refs):
        x_refs, o_refs = refs[:n], refs[n:2 * n]
        send_sems, recv_sems, local_sems = refs[2 * n:]
        me = _my_index()
        sib_id, sib = _peer(1)
        chips = [_peer(k) for k in (4, 2, 6)]

        def copy(a, k, block, to, src=None):
            return pltpu.make_async_remote_copy(
                src_ref=o_refs[a].at[block] if src is None else src, dst_ref=o_refs[a].at[block],
                send_sem=send_sems.at[a * 7 + k], recv_sem=recv_sems.at[a * 7 + k],
                device_id=to, device_id_type=MESH)

        mine = [pltpu.make_async_copy(x_refs[a], o_refs[a].at[me], local_sems.at[a]) for a in range(n)]
        for cp in mine:
            cp.start()
        first = []
        for a in range(n):
            first.append(copy(a, 0, me, sib_id, src=x_refs[a]))
            first += [copy(a, 1 + j, me, cid, src=x_refs[a]) for j, (cid, _) in enumerate(chips)]
        for cp in first:
            cp.start()
        passed = []
        for a in range(n):
            for j, (cid, cidx) in enumerate(chips):
                copy(a, 1 + j, cidx, cid).wait_recv()
                fwd = copy(a, 4 + j, cidx, sib_id)
                fwd.start()
                passed.append(fwd)
        for a in range(n):
            copy(a, 0, sib, sib_id).wait_recv()
            for j, (_, cidx) in enumerate(chips):
                copy(a, 4 + j, cidx ^ 1, sib_id).wait_recv()
        for cp in first + passed:
            cp.wait_send()
        for cp in mine:
            cp.wait()

    out_shape = [jax.ShapeDtypeStruct((N_DEV,) + s.shape, s.dtype) for s in shards]
    return _pcall(body, name=name, out_shape=out_shape, in_specs=[_ANY] * n, out_specs=[_ANY] * n,
                  scratch=[pltpu.SemaphoreType.DMA((7 * n,)), pltpu.SemaphoreType.DMA((7 * n,)),
                           pltpu.SemaphoreType.DMA((n,))])(*shards)


_HBM = pl.BlockSpec(memory_space=pltpu.HBM)
_SEM = pl.BlockSpec(memory_space=pltpu.SEMAPHORE)
_EFFECT = pltpu.SideEffectType.DATAFLOW_SIDE_EFFECTING


def _plan_gather_first(n):
    def plan(refs):
        me = _my_index()
        sib_id, sib = _peer(1)
        chips = [_peer(k) for k in (4, 2, 6)]
        out = []
        for a in range(n):
            x, land = refs[a], refs[n + a]
            out.append((x, land.at[me], sib_id, land.at[sib]))
            out += [(x, land.at[me], cid, land.at[cidx]) for cid, cidx in chips]
        return out

    return plan, 4 * n


def _plan_gather_pass(n):
    def plan(refs):
        sib_id, _ = _peer(1)
        chips = [_peer(k) for k in (4, 2, 6)]
        return [(refs[a].at[cidx], refs[a].at[cidx], sib_id, refs[a].at[cidx ^ 1])
                for a in range(n) for _, cidx in chips]

    return plan, 3 * n


def _plan_gather_direct(n):
    def plan(refs):
        me = _my_index()
        out = []
        for a in range(n):
            for k in range(1, N_DEV):
                pid, pidx = _peer(k)
                out.append((refs[a], refs[n + a].at[me], pid, refs[n + a].at[pidx]))
        return out

    return plan, 7 * n


def _plan_scatter(n):
    def plan(refs):
        me = _my_index()
        out = []
        for a in range(n):
            for k in range(1, N_DEV):
                pid, pidx = _peer(k)
                out.append((refs[a].at[pidx], refs[n + a].at[me], pid, refs[n + a].at[pidx]))
        return out

    return plan, 7 * n


def _remote_copy(src, dst, send_sems, recv_sems, i, dev):
    return pltpu.make_async_remote_copy(src_ref=src, dst_ref=dst, send_sem=send_sems.at[i], recv_sem=recv_sems.at[i],
                                        device_id=dev, device_id_type=MESH)


def _after(x, tokens):
    for token in tokens:
        x = x + token[0, 0]
    return x


def exchange_start(bufs, plan_n, name, after=None):
    plan, n = plan_n
    nb = len(bufs)
    n_in = nb + (after is not None)

    def body(*refs):
        send_sems, recv_sems = refs[n_in], refs[n_in + 1]
        for i, (src, dst, dev, _) in enumerate(plan(refs[:nb])):
            _remote_copy(src, dst, send_sems, recv_sems, i, dev).start()
        refs[-1][...] = jnp.zeros_like(refs[-1])

    out_shape = (pltpu.SemaphoreType.DMA((n,)), pltpu.SemaphoreType.DMA((n,)),
                 *[pltpu.HBM(b.shape, b.dtype) for b in bufs], jax.ShapeDtypeStruct((8, LANE), F32))
    args = [pltpu.with_memory_space_constraint(b, pltpu.HBM) for b in bufs] + ([after] if after is not None else [])
    res = pl.pallas_call(
        body, name=name, out_shape=out_shape,
        in_specs=[_HBM] * nb + ([_ANY] if after is not None else []),
        out_specs=(_SEM, _SEM, *[_HBM] * nb, pl.BlockSpec(memory_space=pltpu.VMEM)),
        input_output_aliases={i: 2 + i for i in range(nb)},
        compiler_params=pltpu.CompilerParams(has_side_effects=_EFFECT), interpret=_INTERPRET)(*args)
    return (res[0], res[1], list(res[2:2 + nb])), res[-1]


def exchange_wait(flight, plan_n, name, after):
    plan, _ = plan_n
    send_sems, recv_sems, bufs = flight
    nb = len(bufs)

    def body(*refs):
        s_sems, r_sems = refs[nb], refs[nb + 1]
        for i, (src, _, dev, arrival) in enumerate(plan(refs[:nb])):
            cp = _remote_copy(src, arrival, s_sems, r_sems, i, dev)
            cp.wait_send()
            cp.wait_recv()

    return list(pl.pallas_call(
        body, name=name, out_shape=tuple(pltpu.HBM(b.shape, b.dtype) for b in bufs),
        in_specs=[_HBM] * nb + [_SEM, _SEM, _ANY], out_specs=tuple([_HBM] * nb),
        input_output_aliases={i: i for i in range(nb)},
        compiler_params=pltpu.CompilerParams(has_side_effects=_EFFECT), interpret=_INTERPRET)(
            *bufs, send_sems, recv_sems, after))


def _adam_tile(contribs, w, m, v):
    g = contribs[0].astype(F32)
    for part in contribs[1:]:
        g = g + part.astype(F32)
    m_new = ADAM_B1 * m + (1.0 - ADAM_B1) * g
    v_new = ADAM_B2 * v + (1.0 - ADAM_B2) * (g * g)
    m_hat = m_new / (1.0 - ADAM_B1 ** ADAM_STEP)
    v_hat = v_new / (1.0 - ADAM_B2 ** ADAM_STEP)
    return g, -ADAM_LR * (m_hat / (jnp.sqrt(v_hat) + ADAM_EPS) + ADAM_WD * w), m_new, v_new


def adam_update(recv, w, m, v, name):
    ll, _, r, c = recv.shape
    tr = _tile(r, max(8, 1 << ((_ADAM_TILE_ELEMS // c).bit_length() - 1)))

    def body(g_ref, w_ref, m_ref, v_ref, go_ref, d_ref, mo_ref, vo_ref):
        go_ref[0], d_ref[0], mo_ref[0], vo_ref[0] = _adam_tile([g_ref[0, j] for j in range(N_DEV)], w_ref[0],
                                                               m_ref[0], v_ref[0])

    spec = pl.BlockSpec((1, tr, c), lambda l, i: (l, i, 0))
    sds = jax.ShapeDtypeStruct((ll, r, c), F32)
    return _pcall(body, name=name, out_shape=[sds] * 4, grid=(ll, r // tr),
                  in_specs=[pl.BlockSpec((1, N_DEV, tr, c), lambda l, i: (l, 0, i, 0)), spec, spec, spec],
                  out_specs=[spec] * 4, sem=("parallel", "parallel"))(recv, w, m, v)


def adam_layer(recv, w, m, v, prev, l, name):
    _, r, c = recv.shape
    tr = _tile(r, max(8, 1 << ((_ADAM_TILE_ELEMS // c).bit_length() - 1)))
    if prev is None:
        prev = [lax.empty(w.shape, F32) for _ in range(4)]

    def body(g_ref, w_ref, m_ref, v_ref, *rest):
        go_ref, d_ref, mo_ref, vo_ref = rest[4:]
        go_ref[0], d_ref[0], mo_ref[0], vo_ref[0] = _adam_tile([g_ref[j] for j in range(N_DEV)], w_ref[0], m_ref[0],
                                                               v_ref[0])

    spec = pl.BlockSpec((1, tr, c), lambda i: (l, i, 0))
    return _pcall(body, name=name, out_shape=[jax.ShapeDtypeStruct(w.shape, F32)] * 4, grid=(r // tr,),
                  in_specs=[pl.BlockSpec((N_DEV, tr, c), lambda i: (0, i, 0)), spec, spec, spec] + [_ANY] * 4,
                  out_specs=[spec] * 4, sem=("parallel",),
                  input_output_aliases={4 + j: j for j in range(4)})(recv, w, m, v, *prev)


def _as3d(a):
    if a.ndim == 1:
        return a.reshape(1, 1, -1)
    if a.ndim == 2:
        return a.reshape(1, a.shape[0], a.shape[1])
    return a.reshape(a.shape[0], -1, a.shape[-1])


def kernel(x, mem, positions, norm_mix, w_in, dn_conv, dn_a_log, dn_dt_bias, dn_out_norm, mla_q_norm, mla_w_qb, mla_kv_norm, mla_w_kvb, w_out, mem_norm, norm_xattn, xa_wq, xa_wk, xa_wv, xa_wo, norm_ffn, ffn_w_up, ffn_conv, ffn_conv_bias, ffn_w_down, norm_final, loss_target, m_norm_mix, m_w_in, m_dn_conv, m_dn_a_log, m_dn_dt_bias, m_dn_out_norm, m_mla_q_norm, m_mla_w_qb, m_mla_kv_norm, m_mla_w_kvb, m_w_out, m_mem_norm, m_norm_xattn, m_xa_wq, m_xa_wk, m_xa_wv, m_xa_wo, m_norm_ffn, m_ffn_w_up, m_ffn_conv, m_ffn_conv_bias, m_ffn_w_down, m_norm_final, v_norm_mix, v_w_in, v_dn_conv, v_dn_a_log, v_dn_dt_bias, v_dn_out_norm, v_mla_q_norm, v_mla_w_qb, v_mla_kv_norm, v_mla_w_kvb, v_w_out, v_mem_norm, v_norm_xattn, v_xa_wq, v_xa_wk, v_xa_wv, v_xa_wo, v_norm_ffn, v_ffn_w_up, v_ffn_conv, v_ffn_conv_bias, v_ffn_w_down, v_norm_final):
    w = dict(norm_mix=norm_mix, w_in=w_in, dn_conv=dn_conv, dn_a_log=dn_a_log, dn_dt_bias=dn_dt_bias,
             dn_out_norm=dn_out_norm, mla_q_norm=mla_q_norm, mla_w_qb=mla_w_qb, mla_kv_norm=mla_kv_norm,
             mla_w_kvb=mla_w_kvb, w_out=w_out, mem_norm=mem_norm, norm_xattn=norm_xattn, xa_wq=xa_wq, xa_wk=xa_wk,
             xa_wv=xa_wv, xa_wo=xa_wo, norm_ffn=norm_ffn, ffn_w_up=ffn_w_up, ffn_conv=ffn_conv,
             ffn_conv_bias=ffn_conv_bias, ffn_w_down=ffn_w_down, norm_final=norm_final)
    mom = dict(norm_mix=m_norm_mix, w_in=m_w_in, dn_conv=m_dn_conv, dn_a_log=m_dn_a_log, dn_dt_bias=m_dn_dt_bias,
               dn_out_norm=m_dn_out_norm, mla_q_norm=m_mla_q_norm, mla_w_qb=m_mla_w_qb, mla_kv_norm=m_mla_kv_norm,
               mla_w_kvb=m_mla_w_kvb, w_out=m_w_out, mem_norm=m_mem_norm, norm_xattn=m_norm_xattn, xa_wq=m_xa_wq,
               xa_wk=m_xa_wk, xa_wv=m_xa_wv, xa_wo=m_xa_wo, norm_ffn=m_norm_ffn, ffn_w_up=m_ffn_w_up,
               ffn_conv=m_ffn_conv, ffn_conv_bias=m_ffn_conv_bias, ffn_w_down=m_ffn_w_down, norm_final=m_norm_final)
    var = dict(norm_mix=v_norm_mix, w_in=v_w_in, dn_conv=v_dn_conv, dn_a_log=v_dn_a_log, dn_dt_bias=v_dn_dt_bias,
               dn_out_norm=v_dn_out_norm, mla_q_norm=v_mla_q_norm, mla_w_qb=v_mla_w_qb, mla_kv_norm=v_mla_kv_norm,
               mla_w_kvb=v_mla_w_kvb, w_out=v_w_out, mem_norm=v_mem_norm, norm_xattn=v_norm_xattn, xa_wq=v_xa_wq,
               xa_wk=v_xa_wk, xa_wv=v_xa_wv, xa_wo=v_xa_wo, norm_ffn=v_norm_ffn, ffn_w_up=v_ffn_w_up,
               ffn_conv=v_ffn_conv, ffn_conv_bias=v_ffn_conv_bias, ffn_w_down=v_ffn_w_down, norm_final=v_norm_final)
    depth = w_in.shape[0]
    me = _my_index()
    rep = {k: w[k] for k in _REPLICATED}
    x0, mem0, pos0, target0 = x[0], mem[0], positions[0], loss_target[0]

    def with_own_block(block):
        return lax.dynamic_update_index_in_dim(lax.empty((N_DEV,) + block.shape, block.dtype), block, me, 0)

    def gather_start(l, keys, tag, after):
        shards = [w[k][l].astype(BF16) for k in keys]
        plan = _plan_gather_first(len(keys))
        flight, token = exchange_start(shards + [with_own_block(sh) for sh in shards], plan,
                                       f"gather_l{l}{tag}_first_start", after)
        return (flight, plan, keys, f"gather_l{l}{tag}"), token

    def gather_pass(state, after):
        flight, plan, keys, name = state
        lands = exchange_wait(flight, plan, name + "_first_wait", after)[len(keys):]
        plan = _plan_gather_pass(len(keys))
        flight, token = exchange_start(lands, plan, name + "_pass_start")
        return (flight, plan, keys, name), token

    def gather_finish(state, after):
        flight, plan, keys, name = state
        return dict(zip(keys, exchange_wait(flight, plan, name + "_pass_wait", after)))

    small = dict(zip(_SMALL_SHARDED, all_gather([w[k] for k in _SMALL_SHARDED], "gather_small")))
    mem_n, vjp_mem = jax.vjp(lambda g: rms_norm(mem0, g, "mem_norm", BF16), rep["mem_norm"])

    n_kind = len(_STAGES)
    groups = [(l, keys, tag) for l in range(depth) for tag, keys, _, _ in _STAGES]
    n_stage = len(groups)
    states, ready = {}, {}
    token = small["dn_conv"]
    for s in range(min(3, n_stage)):
        states[s], token = gather_start(*groups[s], token)
    states[0], token = gather_pass(states[0], token)
    ready[0] = gather_finish(states[0], token)

    h = x0
    tapes = []
    for s in range(n_stage):
        l, kind = divmod(s, n_kind)
        tokens = []
        if s >= 2 and s + 1 < n_stage:
            states[s + 1], token = gather_pass(states[s + 1], h)
            tokens.append(token)
        if s + 3 < n_stage:
            states[s + 3], token = gather_start(*groups[s + 3], h)
            tokens.append(token)
        _, _, rep_keys, gain = _STAGES[kind]
        rp = {k: rep[k][l] for k in rep_keys}
        rp[gain] = _after(rp[gain], tokens)
        if kind == 0:
            h_new, tape = jax.vjp(lambda hh, gw, cv, rr: _mix(hh, gw, cv, rr, l, pos0), h, ready.pop(s),
                                  small["dn_conv"][:, l], rp)
        elif kind == 1:
            h_new, tape = jax.vjp(lambda hh, gw, rr, mn: _xattn(hh, gw, rr, l, mn), h, ready.pop(s), rp, mem_n)
        else:
            h_new, tape = jax.vjp(lambda hh, gw, cv, rr: _ffn(hh, gw, cv, rr, l), h, ready.pop(s),
                                  small["ffn_conv"][:, l], rp)
        if s + 1 < n_stage:
            if s < 2:
                states[s + 1], token = gather_pass(states[s + 1], h_new)
                ready[s + 1] = gather_finish(states[s + 1], token)
            else:
                ready[s + 1] = gather_finish(states[s + 1], h_new)
        tapes.append(tape)
        h = h_new

    rows, vjp_loss = jax.vjp(lambda hh, g: loss_rows(hh, g, target0, "loss"), h, rep["norm_final"])
    loss = lax.psum(jnp.sum(rows), ("x", "y", "c"))
    dh, d_norm_final = vjp_loss(jnp.ones_like(rows))

    d_rep = {k: [None] * depth for _, _, rep_keys, _ in _STAGES for k in rep_keys}
    d_conv = {k: [None] * depth for k in _SMALL_SHARDED}
    recv_big = [dict() for _ in range(depth)]
    d_mem_n = None

    def scatter_start(l, grads, keys, tag):
        parts = [grads[k] for k in keys]
        own = [with_own_block(lax.dynamic_index_in_dim(p, me, 0, keepdims=False)) for p in parts]
        plan = _plan_scatter(len(keys))
        flight, token = exchange_start(parts + own, plan, f"scatter_l{l}{tag}_start")
        return (flight, plan, keys, l, f"scatter_l{l}{tag}_wait"), token

    def scatter_finish(state, after):
        flight, plan, keys, l, name = state
        recv_big[l].update(zip(keys, exchange_wait(flight, plan, name, after)[len(keys):]))

    pending = {}
    token = None
    for s in reversed(range(n_stage)):
        l, keys, tag = groups[s]
        if token is not None:
            dh = _after(dh, [token])
        if s % n_kind == 0:
            dh, dg, d_conv["dn_conv"][l], d_rp = tapes[s](dh)
        elif s % n_kind == 1:
            dh, dg, d_rp, d_mn = tapes[s](dh)
            d_mem_n = d_mn if d_mem_n is None else d_mem_n + d_mn
        else:
            dh, dg, d_conv["ffn_conv"][l], d_rp = tapes[s](dh)
        for k, v in d_rp.items():
            d_rep[k][l] = v
        if s + n_kind in pending:
            scatter_finish(pending.pop(s + n_kind), dh)
        pending[s], token = scatter_start(l, dg, keys, tag)

    (d_mem_norm,) = vjp_mem(d_mem_n)
    d_rep_all = {k: jnp.stack(v) for k, v in d_rep.items()}
    d_rep_all["mem_norm"] = d_mem_norm
    d_rep_all["norm_final"] = _after(d_norm_final, [token])
    parts = [jnp.stack(d_conv[k], axis=1) for k in _SMALL_SHARDED]
    own = [with_own_block(lax.dynamic_index_in_dim(p, me, 0, keepdims=False)) for p in parts]
    plan_small = _plan_scatter(len(parts))
    flight_small, token = exchange_start(parts + own, plan_small, "scatter_small_start")
    blocks = [_as3d(d_rep_all[k])[0] for k in _REPLICATED]
    blocks[0] = _after(blocks[0], [token])
    plan_rep = _plan_gather_direct(len(blocks))
    flight_rep, token = exchange_start(blocks + [with_own_block(b) for b in blocks], plan_rep, "gather_rep_start")

    big = {k: None for k in _BIG}

    def update_big(l, keys):
        for k in keys:
            recv = recv_big[l][k]
            big[k] = adam_layer(recv.reshape((N_DEV, -1, recv.shape[-1])), _as3d(w[k]), _as3d(mom[k]),
                                _as3d(var[k]), big[k], l, f"adam_{k}_l{l}")

    for l in reversed(range(1, depth)):
        update_big(l, _BIG)
    after = token if depth == 1 else big[_BIG[-1]][0]
    for s in sorted(pending, reverse=True):
        scatter_finish(pending[s], after)
        update_big(0, groups[s][1])
        after = big[groups[s][1][-1]][0]
    out = {}
    recv_small = exchange_wait(flight_small, plan_small, "scatter_small_wait", after)[len(parts):]
    for k, recv in zip(_SMALL_SHARDED, recv_small):
        r3 = _as3d(w[k])
        out[k] = adam_update(recv.reshape((1, N_DEV, -1, recv.shape[-1])), r3.reshape((1, -1, r3.shape[-1])),
                             _as3d(mom[k]).reshape((1, -1, r3.shape[-1])), _as3d(var[k]).reshape((1, -1, r3.shape[-1])),
                             "adam_" + k)
    recv_rep = exchange_wait(flight_rep, plan_rep, "gather_rep_wait", after)[len(blocks):]
    for k, recv in zip(_REPLICATED, recv_rep):
        r3 = _as3d(w[k])
        flat = lambda a, r3=r3: a.reshape((1, -1, r3.shape[-1]))
        out[k] = adam_update(recv[None], flat(w[k]), flat(mom[k]), flat(var[k]), "adam_" + k)
    out.update(big)

    res = [loss, dh[None]]
    for j in range(4):
        res += [out[k][j].reshape(w[k].shape) for k in _WEIGHTS]
    return tuple(res)
```

```python
import functools

import numpy as np
import jax
import jax.numpy as jnp
from jax import lax
from jax.experimental import pallas as pl
from jax.experimental.pallas import tpu as pltpu

F32 = jnp.float32
BF16 = jnp.bfloat16
_MXU_DTYPE = BF16
_INTERPRET = False
_VMEM_LIMIT_BYTES = 48 * 1024 * 1024
_MM_VMEM_BUDGET_BYTES = 36 * 1024 * 1024
_MM_TILE_CAP = 2048

N_DEV = 8
CHUNK = 64
_CHUNK_SHIFT = 6
DN_HEAD_DIM = 128
DN_CONV = 4
MLA_NOPE = 128
MLA_ROPE = 64
MLA_V = 128
MLA_Q_RANK = 512
MLA_KV_RANK = 256
ROPE_BASE = 10000.0
XA_HEADS = 4
FFN_CONV = 3
EPS = 1e-6
LANE = 128

ADAM_LR = 0.001
ADAM_B1 = 0.9
ADAM_B2 = 0.999
ADAM_EPS = 1e-08
ADAM_WD = 0.01
ADAM_STEP = 10
_ADAM_TILE_ELEMS = 256 * 1024

MESH = pl.DeviceIdType.MESH


def _pcall(body, *, name, out_shape, grid=None, in_specs=None, out_specs=None, scratch=(), sem=None, **kw):
    params = pltpu.CompilerParams(dimension_semantics=sem, vmem_limit_bytes=_VMEM_LIMIT_BYTES)
    args = dict(name=name, out_shape=out_shape, scratch_shapes=list(scratch), compiler_params=params,
                interpret=_INTERPRET, **kw)
    if grid is not None:
        args.update(grid=grid)
    if in_specs is not None:
        args.update(in_specs=in_specs)
    if out_specs is not None:
        args.update(out_specs=out_specs)
    return pl.pallas_call(body, **args)


def _tile(n, pref):
    if n <= pref:
        return n
    t = pref
    while t >= 8:
        if n % t == 0:
            return t
        t //= 2
    return n


def _ein_raw(spec, a, b, hi):
    one = lambda x, y: jnp.einsum(spec, x, y, preferred_element_type=F32)
    a_hi, b_hi = a.astype(_MXU_DTYPE), b.astype(_MXU_DTYPE)
    if not hi:
        return one(a_hi, b_hi)
    a_lo = (a.astype(F32) - a_hi.astype(F32)).astype(_MXU_DTYPE)
    b_lo = (b.astype(F32) - b_hi.astype(F32)).astype(_MXU_DTYPE)
    return one(a_hi, b_hi) + (one(a_hi, b_lo) + one(a_lo, b_hi))


def _make_ein(spec, hi=False, diff_b=True):
    a_s, rest = spec.split(",")
    b_s, o_s = rest.split("->")

    @jax.custom_vjp
    def f(a, b):
        return _ein_raw(spec, a, b, hi)

    def fwd(a, b):
        return f(a, b), (a, b)

    def bwd(res, g):
        a, b = res
        da = _ein_raw(f"{o_s},{b_s}->{a_s}", g, b, hi)
        if not diff_b:
            return da.astype(a.dtype), None
        db = _ein_raw(f"{a_s},{o_s}->{b_s}", a, g, hi)
        return da.astype(a.dtype), db.astype(b.dtype)

    f.defvjp(fwd, bwd)
    return f


_nt = _make_ein("qd,kd->qk")
_nn = _make_ein("qk,kd->qd")
_nn_hi_const = _make_ein("qk,kd->qd", hi=True, diff_b=False)
_bnt = _make_ein("hik,hjk->hij")
_bnn = _make_ein("hij,hjv->hiv")
_btn = _make_ein("hck,hcv->hkv")


def _divisor_tiles(n, cap):
    out = [n] if n <= cap else []
    t = cap
    while t >= LANE:
        if t < n and n % t == 0:
            out.append(t)
        t //= 2
    return out or [n]


def _mm_tiles(m, n, k, a_bytes, b_bytes, o_bytes, r_bytes, accumulate=False):
    best = None
    for tk in sorted({k, *_divisor_tiles(k, _MM_TILE_CAP)}):
        for tm in _divisor_tiles(m, _MM_TILE_CAP):
            for tn in _divisor_tiles(n, _MM_TILE_CAP):
                need = 2 * (tm * tk * a_bytes + tk * tn * b_bytes + tm * tn * (o_bytes + r_bytes))
                need += tm * tn * 4 * (1 if tk == k and not accumulate else 2)
                if need > _MM_VMEM_BUDGET_BYTES:
                    continue
                score = (tm * tn * tk, tk, tm)
                if best is None or score > best[0]:
                    best = (score, (tm, tn, tk))
    assert best is not None, (m, n, k)
    return best[1]


def _mm(a, b, *, ta=False, tb=False, res=None, out_dtype=F32, name="mm", b_cols=None, into=None):
    m = a.shape[1] if ta else a.shape[0]
    k = a.shape[0] if ta else a.shape[1]
    b_start, b_width = b_cols if b_cols is not None else (0, b.shape[1])
    n = b.shape[0] if tb else b_width
    assert (b_width if tb else b.shape[0]) == k, (a.shape, b.shape, ta, tb, b_cols)
    has_res = res is not None
    tm, tn, tk = _mm_tiles(m, n, k, a.dtype.itemsize, b.dtype.itemsize, jnp.dtype(out_dtype).itemsize,
                           res.dtype.itemsize if has_res else 0)
    nk = k // tk
    b_off = b_start // (tk if tb else tn)
    assert b_off * (tk if tb else tn) == b_start, (b_cols, tn, tk)
    body = _mm_body(nk, (((0 if ta else 1,), (1 if tb else 0,)), ((), ())), has_res, out_dtype, into is not None)
    a_spec = pl.BlockSpec((tk, tm), lambda i, j, kk: (kk, i)) if ta else pl.BlockSpec((tm, tk), lambda i, j, kk: (i, kk))
    b_spec = (pl.BlockSpec((tn, tk), lambda i, j, kk: (j, kk + b_off)) if tb
              else pl.BlockSpec((tk, tn), lambda i, j, kk: (kk, j + b_off)))
    o_spec = pl.BlockSpec((tm, tn), lambda i, j, kk: (i, j))
    in_specs = [a_spec, b_spec] + ([o_spec] if has_res else [])
    args = (a, b) + ((res,) if has_res else ())
    out_shape = jax.ShapeDtypeStruct((m, n), out_dtype)
    extra = {}
    if into is not None:
        buf, o_start, total = into
        o_off = o_start // tn
        assert o_off * tn == o_start, (into[1:], tn)
        out_shape = jax.ShapeDtypeStruct((m, total), out_dtype)
        o_spec = pl.BlockSpec((tm, tn), lambda i, j, kk: (i, j + o_off))
        in_specs = in_specs + [_ANY]
        args = args + (lax.empty((m, total), out_dtype) if buf is None else buf,)
        extra = dict(input_output_aliases={len(args) - 1: 0})
    return _pcall(body, name=name, out_shape=out_shape,
                  grid=(m // tm, n // tn, nk), in_specs=in_specs, out_specs=o_spec,
                  scratch=[pltpu.VMEM((tm, tn), F32)] if nk > 1 else [],
                  sem=("parallel", "parallel", "arbitrary"), **extra)(*args)


def _mm_body(nk, dims, has_res, out_dtype, has_buffer=False):
    def body(*refs):
        a_ref, b_ref = refs[0], refs[1]
        r_ref = refs[2] if has_res else None
        o_ref = refs[2 + has_res + has_buffer]

        def finish(r):
            if has_res:
                r = r + r_ref[...].astype(F32)
            o_ref[...] = r.astype(out_dtype)

        part = lax.dot_general(a_ref[...].astype(_MXU_DTYPE), b_ref[...].astype(_MXU_DTYPE), dims,
                               preferred_element_type=F32)
        if nk == 1:
            finish(part)
            return
        acc = refs[-1]
        kk = pl.program_id(2)

        @pl.when(kk == 0)
        def _():
            acc[...] = part

        @pl.when(kk > 0)
        def _():
            acc[...] += part

        @pl.when(kk == nk - 1)
        def _():
            finish(acc[...])

    return body


def _mm_groups_out(a, b3, *, ta=False, out_dtype=F32, name="mm_groups"):
    groups, k, nb = b3.shape
    m = a.shape[1] if ta else a.shape[0]
    assert (a.shape[0] if ta else a.shape[1]) == k, (a.shape, b3.shape, ta)
    tm, tn, tk = _mm_tiles(m, nb, k, a.dtype.itemsize, b3.dtype.itemsize, jnp.dtype(out_dtype).itemsize, 0)
    nk, per = k // tk, nb // tn
    body = _mm_body(nk, (((0 if ta else 1,), (0,)), ((), ())), False, out_dtype)
    a_spec = pl.BlockSpec((tk, tm), lambda j, i, kk: (kk, i)) if ta else pl.BlockSpec((tm, tk), lambda j, i, kk: (i, kk))
    b_spec = pl.BlockSpec((None, tk, tn), lambda j, i, kk: (j // per, kk, j % per))
    o_spec = pl.BlockSpec((None, tm, tn), lambda j, i, kk: (j // per, i, j % per))
    return _pcall(body, name=name, out_shape=jax.ShapeDtypeStruct((groups, m, nb), out_dtype),
                  grid=(groups * per, m // tm, nk), in_specs=[a_spec, b_spec], out_specs=o_spec,
                  scratch=[pltpu.VMEM((tm, tn), F32)] if nk > 1 else [],
                  sem=("parallel", "parallel", "arbitrary"))(a, b3)


def _mm_groups_contract(a3, b3, *, out_dtype=F32, name="mm_contract"):
    groups, m, nb = a3.shape
    n = b3.shape[1]
    assert b3.shape[0] == groups and b3.shape[2] == nb, (a3.shape, b3.shape)
    tm, tn, tk = _mm_tiles(m, n, nb, a3.dtype.itemsize, b3.dtype.itemsize, jnp.dtype(out_dtype).itemsize, 0,
                           accumulate=True)
    per = nb // tk
    nk = groups * per
    body = _mm_body(nk, (((1,), (1,)), ((), ())), False, out_dtype)
    a_spec = pl.BlockSpec((None, tm, tk), lambda i, j, kk: (kk // per, i, kk % per))
    b_spec = pl.BlockSpec((None, tn, tk), lambda i, j, kk: (kk // per, j, kk % per))
    return _pcall(body, name=name, out_shape=jax.ShapeDtypeStruct((m, n), out_dtype),
                  grid=(m // tm, n // tn, nk), in_specs=[a_spec, b_spec],
                  out_specs=pl.BlockSpec((tm, tn), lambda i, j, kk: (i, j)),
                  scratch=[pltpu.VMEM((tm, tn), F32)], sem=("parallel", "parallel", "arbitrary"))(a3, b3)


@functools.partial(jax.custom_vjp, nondiff_argnums=(3,))
def _linear_res(a, w, res, name):
    return _mm(a, w, res=res, name=name + "_fwd")


def _linear_res_fwd(a, w, res, name):
    return _mm(a, w, res=res, name=name + "_fwd"), (a, w)


def _linear_res_bwd(name, saved, g):
    a, w = saved
    gm = g.astype(_MXU_DTYPE)
    da = _mm(gm, w, tb=True, out_dtype=a.dtype, name=name + "_da")
    dw = _mm(a, gm, ta=True, out_dtype=w.dtype, name=name + "_dw")
    return da, dw, g


_linear_res.defvjp(_linear_res_fwd, _linear_res_bwd)


@functools.partial(jax.custom_vjp, nondiff_argnums=(2,))
def _linear(a, w, name):
    return _mm(a, w, name=name + "_fwd")


def _linear_fwd(a, w, name):
    return _mm(a, w, name=name + "_fwd"), (a, w)


def _linear_bwd(name, saved, g):
    a, w = saved
    gm = g.astype(_MXU_DTYPE)
    da = _mm(gm, w, tb=True, out_dtype=a.dtype, name=name + "_da")
    dw = _mm(a, gm, ta=True, out_dtype=w.dtype, name=name + "_dw")
    return da, dw


_linear.defvjp(_linear_fwd, _linear_bwd)


def linear(a, w, name, res=None):
    return _linear(a, w, name) if res is None else _linear_res(a, w, res, name)


def _rest_block(w_in, widths):
    c0 = widths[0] + widths[1]
    return jnp.pad(w_in[:, c0:c0 + widths[2]], ((0, 0), (0, (-widths[2]) % LANE)))


@functools.partial(jax.custom_vjp, nondiff_argnums=(2, 3))
def in_proj(u, w_in, widths, name):
    return _in_proj_fwd(u, w_in, widths, name)[0]


def _in_proj_fwd(u, w_in, widths, name):
    a, b, _ = widths
    outs = (_mm(u, w_in, b_cols=(0, a), name=name + "_qkv_fwd"), _mm(u, w_in, b_cols=(a, b), name=name + "_z_fwd"),
            _mm(u, _rest_block(w_in, widths), name=name + "_rest_fwd"))
    return outs, (u, w_in)


def _in_proj_bwd(widths, name, saved, cts):
    u, w_in = saved
    a, b, r = widths
    total = w_in.shape[1]
    gq, gz, gr = (c.astype(_MXU_DTYPE) for c in cts)
    da = _mm(gq, w_in, tb=True, b_cols=(0, a), name=name + "_qkv_da")
    da = _mm(gz, w_in, tb=True, b_cols=(a, b), res=da, name=name + "_z_da")
    da = _mm(gr, _rest_block(w_in, widths), tb=True, res=da, out_dtype=u.dtype, name=name + "_rest_da")
    dw = _mm(u, gq, ta=True, out_dtype=w_in.dtype, into=(None, 0, total), name=name + "_qkv_dw")
    dw = _mm(u, gz, ta=True, out_dtype=w_in.dtype, into=(dw, a, total), name=name + "_z_dw")
    dw_rest = _mm(u, gr, ta=True, out_dtype=w_in.dtype, name=name + "_rest_dw")
    return da, lax.dynamic_update_slice(dw, dw_rest[:, :r], (0, a + b))


in_proj.defvjp(_in_proj_fwd, _in_proj_bwd)


def _tiled_specs(arrs, kinds, t, axis):
    specs = []
    for a, kind in zip(arrs, kinds):
        if kind == "whole":
            specs.append(pl.BlockSpec(a.shape, lambda i, nd=a.ndim: (0,) * nd))
        elif axis == 0:
            specs.append(pl.BlockSpec((t, a.shape[1]), lambda i: (i, 0)))
        else:
            specs.append(pl.BlockSpec((a.shape[0], t), lambda i: (0, i)))
    return specs


def tilewise(fn, name, args, kinds, outs, *, axis, t, diff):
    n_in = len(args)
    length = next(a.shape[axis] for a, kd in zip(args, kinds) if kd == "tile")
    steps = length // t
    assert steps * t == length, (name, length, t)

    def out_sds(other, dtype):
        return jax.ShapeDtypeStruct((length, other) if axis == 0 else (other, length), dtype)

    def out_spec(other):
        return pl.BlockSpec((t, other), lambda i: (i, 0)) if axis == 0 else pl.BlockSpec((other, t), lambda i: (0, i))

    def run_fwd(*xs):
        def body(*refs):
            vals = fn(*[r[...] for r in refs[:n_in]])
            for o_ref, v in zip(refs[n_in:], vals):
                o_ref[...] = v.astype(o_ref.dtype)

        return _pcall(body, name=name + "_fwd", out_shape=[out_sds(o, d) for o, d in outs], grid=(steps,),
                      in_specs=_tiled_specs(xs, kinds, t, axis), out_specs=[out_spec(o) for o, _ in outs],
                      sem=("parallel",))(*xs)

    didx = [i for i in range(n_in) if diff[i]]

    def run_bwd(xs, cts):
        def body(*refs):
            x_refs, c_refs, g_refs = refs[:n_in], refs[n_in:n_in + len(outs)], refs[n_in + len(outs):]
            vals = [r[...] for r in x_refs]

            def g(*dvals):
                full = list(vals)
                for i, v in zip(didx, dvals):
                    full[i] = v
                return tuple(fn(*full))

            prim_out, vjp = jax.vjp(g, *[vals[i] for i in didx])
            grads = vjp(tuple(c[...].astype(o.dtype) for c, o in zip(c_refs, prim_out)))
            step = pl.program_id(0)
            for i, g_ref, gr in zip(didx, g_refs, grads):
                if kinds[i] == "whole":
                    @pl.when(step == 0)
                    def _(g_ref=g_ref):
                        g_ref[...] = jnp.zeros_like(g_ref)

                    g_ref[...] += gr.astype(g_ref.dtype)
                else:
                    g_ref[...] = gr.astype(g_ref.dtype)

        g_shapes = [jax.ShapeDtypeStruct(xs[i].shape, xs[i].dtype) for i in didx]
        g_specs = _tiled_specs([xs[i] for i in didx], [kinds[i] for i in didx], t, axis)
        ct_specs = [out_spec(o) for o, _ in outs]
        any_whole = any(kinds[i] == "whole" for i in didx)
        return _pcall(body, name=name + "_bwd", out_shape=g_shapes, grid=(steps,),
                      in_specs=_tiled_specs(xs, kinds, t, axis) + ct_specs, out_specs=g_specs,
                      sem=("arbitrary" if any_whole else "parallel",))(*xs, *cts)

    @jax.custom_vjp
    def op(*xs):
        return tuple(run_fwd(*xs))

    def op_fwd(*xs):
        return tuple(run_fwd(*xs)), xs

    def op_bwd(xs, cts):
        grads = run_bwd(xs, cts)
        full = [None] * n_in
        for i, gr in zip(didx, grads):
            full[i] = gr
        return tuple(full)

    op.defvjp(op_fwd, op_bwd)
    return op(*args)


def _silu(x):
    return x * (1.0 / (1.0 + jnp.exp(-x)))


def _softplus(x):
    return jnp.maximum(x, 0.0) + jnp.log(1.0 + jnp.exp(-jnp.abs(x)))


def _rms_tile(x, gain, out_dtype):
    xf = x.astype(F32)
    y = xf * lax.rsqrt(jnp.mean(xf * xf, axis=-1, keepdims=True) + EPS)
    return (y * gain).astype(out_dtype)


def rms_norm(x, gain, name, out_dtype, tr=256):
    def fn(xt, gt):
        return (_rms_tile(xt, gt, out_dtype),)

    return tilewise(fn, name, [x, gain.reshape(1, -1)], ["tile", "whole"], [(x.shape[1], out_dtype)],
                    axis=0, t=_tile(x.shape[0], tr), diff=[True, True])[0]


def _shift_down_raw(x, s):
    rows = lax.broadcasted_iota(jnp.int32, x.shape, 0)
    return jnp.where(rows >= s, pltpu.roll(x, s, 0), 0.0)


def _shift_up_raw(x, s):
    n = x.shape[0]
    rows = lax.broadcasted_iota(jnp.int32, x.shape, 0)
    return jnp.where(rows < n - s, pltpu.roll(x, n - s, 0), 0.0)


@functools.partial(jax.custom_vjp, nondiff_argnums=(1,))
def _shift_down(x, s):
    return _shift_down_raw(x, s)


def _shift_down_fwd(x, s):
    return _shift_down_raw(x, s), None


def _shift_down_bwd(s, _, g):
    return (_shift_up_raw(g, s),)


_shift_down.defvjp(_shift_down_fwd, _shift_down_bwd)


def _causal_dwconv_tile(x, w):
    kk = w.shape[0]
    y = x * w[kk - 1:kk, :]
    for j in range(kk - 1):
        y = y + _shift_down(x, kk - 1 - j) * w[j:j + 1, :]
    return y


def dn_qkv(raw, conv_w, name):
    width = raw.shape[1] // 3
    is_qk = (jnp.arange(raw.shape[1]) < 2 * width).astype(F32).reshape(1, -1)

    def fn(x, w, flag):
        y = _silu(_causal_dwconv_tile(x, w))
        yn = y * lax.rsqrt(jnp.sum(y * y, axis=-1, keepdims=True) + EPS)
        return (jnp.where(flag > 0.5, yn, y),)

    return tilewise(fn, name, [raw, conv_w, is_qk], ["tile", "tile", "tile"], [(raw.shape[0], F32)],
                    axis=1, t=DN_HEAD_DIM, diff=[True, True, False])[0]


def dn_gates(b, a, a_log, dt_bias, name):
    hh = b.shape[1]
    tr = _tile(b.shape[0], 256)

    def fn(bt, at, al, db):
        beta = 1.0 / (1.0 + jnp.exp(-bt))
        g = -jnp.exp(al) * _softplus(at + db)
        pos = lax.broadcasted_iota(jnp.int32, g.shape, 0) & (CHUNK - 1)
        step = 1
        while step < CHUNK:
            g = g + jnp.where(pos >= step, _shift_down(g, step), 0.0)
            step *= 2
        return g, beta

    return tilewise(fn, name, [b, a, a_log.reshape(1, -1), dt_bias.reshape(1, -1)],
                    ["tile", "tile", "whole", "whole"], [(hh, F32), (hh, F32)], axis=0, t=tr, diff=[True] * 4)


def _unit_lower_inv_raw(low):
    c = low.shape[-1]
    ii = lax.broadcasted_iota(jnp.int32, (c, c), 0)
    jj = lax.broadcasted_iota(jnp.int32, (c, c), 1)
    inv = (ii == jj).astype(F32)[None] - low
    p = low
    n = 1
    while 2 * n < c:
        p = _ein_raw("hij,hjk->hik", p, p, True)
        inv = inv + _ein_raw("hij,hjk->hik", inv, p, True)
        n *= 2
    return inv


@jax.custom_vjp
def _unit_lower_inv(low):
    return _unit_lower_inv_raw(low)


def _unit_lower_inv_fwd(low):
    inv = _unit_lower_inv_raw(low)
    return inv, inv


def _unit_lower_inv_bwd(inv, g):
    t = _ein_raw("hji,hjk->hik", inv, g, True)
    return (-_ein_raw("hik,hlk->hil", t, inv, True),)


_unit_lower_inv.defvjp(_unit_lower_inv_fwd, _unit_lower_inv_bwd)


def _dn_chunk(state, q, k, v, gc, gr, gl, bc):
    c = q.shape[1]
    q = q * (q.shape[-1] ** -0.5)
    ii = lax.broadcasted_iota(jnp.int32, (c, c), 0)
    jj = lax.broadcasted_iota(jnp.int32, (c, c), 1)
    incl = (jj <= ii)[None]
    strict = (jj < ii)[None]
    decay = jnp.where(incl, jnp.exp(jnp.where(incl, gc - gr, 0.0)), 0.0)
    kb = k * bc
    low = jnp.where(strict, _bnt(kb, k) * decay, 0.0)
    ainv = _unit_lower_inv(low)
    eg = jnp.exp(gc)
    u = _bnn(ainv, v * bc)
    w = _bnn(ainv, kb * eg)
    attn = _bnt(q, k) * decay
    q_dec = q * eg
    k_dec = k * jnp.exp(gl - gc)
    v_new = u - _bnn(w, state)
    o = _bnn(q_dec, state) + _bnn(attn, v_new)
    new_state = state * jnp.exp(gl) + _btn(k_dec, v_new)
    return new_state, o


def _dn_heads(ref, base, hh):
    return jnp.stack([ref[:, base + h * DN_HEAD_DIM: base + (h + 1) * DN_HEAD_DIM] for h in range(hh)])


def _dn_fwd_call(qkv, gc, gr, gl, bc, name):
    s, w3 = qkv.shape
    width = w3 // 3
    hh = width // DN_HEAD_DIM
    d = DN_HEAD_DIM
    n = s // CHUNK

    def body(qkv_ref, gc_ref, gr_ref, gl_ref, bc_ref, o_ref, st_ref, state):
        @pl.when(pl.program_id(0) == 0)
        def _():
            state[...] = jnp.zeros_like(state)

        s_in = state[...]
        st_ref[0] = s_in
        new_s, o = _dn_chunk(s_in, _dn_heads(qkv_ref, 0, hh), _dn_heads(qkv_ref, width, hh),
                             _dn_heads(qkv_ref, 2 * width, hh), gc_ref[0], gr_ref[0], gl_ref[0], bc_ref[0])
        state[...] = new_s
        for h in range(hh):
            o_ref[:, h * d:(h + 1) * d] = o[h]

    g4 = lambda i: (i, 0, 0, 0)
    return _pcall(
        body, name=name + "_fwd",
        out_shape=[jax.ShapeDtypeStruct((s, width), F32), jax.ShapeDtypeStruct((n, hh, d, d), F32)],
        grid=(n,),
        in_specs=[pl.BlockSpec((CHUNK, w3), lambda i: (i, 0)), pl.BlockSpec((1, hh, CHUNK, 1), g4),
                  pl.BlockSpec((1, hh, 1, CHUNK), g4), pl.BlockSpec((1, hh, 1, 1), g4),
                  pl.BlockSpec((1, hh, CHUNK, 1), g4)],
        out_specs=[pl.BlockSpec((CHUNK, width), lambda i: (i, 0)), pl.BlockSpec((1, hh, d, d), g4)],
        scratch=[pltpu.VMEM((hh, d, d), F32)], sem=("arbitrary",))(qkv, gc, gr, gl, bc)


def _dn_bwd_call(qkv, gc, gr, gl, bc, states, do, name):
    s, w3 = qkv.shape
    width = w3 // 3
    hh = width // DN_HEAD_DIM
    d = DN_HEAD_DIM
    n = s // CHUNK

    def body(qkv_ref, gc_ref, gr_ref, gl_ref, bc_ref, st_ref, do_ref,
             dqkv_ref, dgc_ref, dgr_ref, dgl_ref, dbc_ref, dstate):
        @pl.when(pl.program_id(0) == 0)
        def _():
            dstate[...] = jnp.zeros_like(dstate)

        prim = (st_ref[0], _dn_heads(qkv_ref, 0, hh), _dn_heads(qkv_ref, width, hh),
                _dn_heads(qkv_ref, 2 * width, hh), gc_ref[0], gr_ref[0], gl_ref[0], bc_ref[0])
        _, vjp = jax.vjp(_dn_chunk, *prim)
        ds, dq, dk, dv, dgc, dgr, dgl, dbc = vjp((dstate[...], _dn_heads(do_ref, 0, hh)))
        dstate[...] = ds
        for h in range(hh):
            dqkv_ref[:, h * d:(h + 1) * d] = dq[h]
            dqkv_ref[:, width + h * d: width + (h + 1) * d] = dk[h]
            dqkv_ref[:, 2 * width + h * d: 2 * width + (h + 1) * d] = dv[h]
        dgc_ref[0] = dgc
        dgr_ref[0] = dgr
        dgl_ref[0] = dgl
        dbc_ref[0] = dbc

    r2 = lambda i: (n - 1 - i, 0)
    r4 = lambda i: (n - 1 - i, 0, 0, 0)
    spec_c = pl.BlockSpec((1, hh, CHUNK, 1), r4)
    spec_r = pl.BlockSpec((1, hh, 1, CHUNK), r4)
    spec_l = pl.BlockSpec((1, hh, 1, 1), r4)
    return _pcall(
        body, name=name + "_bwd",
        out_shape=[jax.ShapeDtypeStruct(qkv.shape, F32), jax.ShapeDtypeStruct(gc.shape, F32),
                   jax.ShapeDtypeStruct(gr.shape, F32), jax.ShapeDtypeStruct(gl.shape, F32),
                   jax.ShapeDtypeStruct(bc.shape, F32)],
        grid=(n,),
        in_specs=[pl.BlockSpec((CHUNK, w3), r2), spec_c, spec_r, spec_l, spec_c,
                  pl.BlockSpec((1, hh, d, d), r4), pl.BlockSpec((CHUNK, width), r2)],
        out_specs=[pl.BlockSpec((CHUNK, w3), r2), spec_c, spec_r, spec_l, spec_c],
        scratch=[pltpu.VMEM((hh, d, d), F32)], sem=("arbitrary",))(qkv, gc, gr, gl, bc, states, do)


@functools.partial(jax.custom_vjp, nondiff_argnums=(5,))
def dn_core(qkv, gc, gr, gl, bc, name):
    return _dn_fwd_call(qkv, gc, gr, gl, bc, name)[0]


def _dn_core_fwd(qkv, gc, gr, gl, bc, name):
    o, states = _dn_fwd_call(qkv, gc, gr, gl, bc, name)
    return o, (qkv, gc, gr, gl, bc, states)


def _dn_core_bwd(name, saved, do):
    return tuple(_dn_bwd_call(*saved, do, name))


dn_core.defvjp(_dn_core_fwd, _dn_core_bwd)


def dn_out_gate(o, z, out_norm, name):
    hh = o.shape[1] // DN_HEAD_DIM
    gain = jnp.tile(out_norm.reshape(1, -1), (1, hh))

    def fn(ot, zt, gt):
        y = ot * lax.rsqrt(jnp.mean(ot * ot, axis=-1, keepdims=True) + EPS) * gt
        return (y * _silu(zt),)

    return tilewise(fn, name, [o, z, gain], ["tile", "tile", "tile"], [(o.shape[0], BF16)],
                    axis=1, t=DN_HEAD_DIM, diff=[True, True, True])[0]


def _attn_tile(qn, kn, v, qp, kp, q0, scale, causal):
    s = _nt(qn, kn)
    if qp is not None:
        s = s + _nt(qp, kp)
    s = s * scale
    if causal:
        qpos = q0 + lax.broadcasted_iota(jnp.int32, s.shape, 0)
        kpos = lax.broadcasted_iota(jnp.int32, s.shape, 1)
        s = jnp.where((kpos >> _CHUNK_SHIFT) <= (qpos >> _CHUNK_SHIFT), s, -1e30)
    e = jnp.exp(s - jnp.max(s, axis=-1, keepdims=True))
    p = e / jnp.sum(e, axis=-1, keepdims=True)
    return _nn(p, v)


def _attn_specs(sq, sk, dh, dp, tb):
    q_spec = pl.BlockSpec((tb, dh), lambda h, i: (i, h))
    kv_spec = pl.BlockSpec((sk, dh), lambda h, i: (0, h))
    qp_spec = pl.BlockSpec((1, tb, dp), lambda h, i: (h, i, 0)) if dp else None
    kp_spec = pl.BlockSpec((sk, dp), lambda h, i: (0, 0)) if dp else None
    return q_spec, kv_spec, qp_spec, kp_spec


def _attn_plan(sq, sk, tq, causal):
    tb = sq if causal else tq
    subs = [(slice(i * tq, (i + 1) * tq), (i + 1) * tq if causal else sk) for i in range(tb // tq)]
    return tb, subs


def _attn_fwd_call(q, k, v, qp, kp, *, dh, scale, causal, tq, name):
    sq, sk = q.shape[0], k.shape[0]
    heads = q.shape[1] // dh
    dp = qp.shape[-1] if qp is not None else 0
    tb, subs = _attn_plan(sq, sk, tq, causal)
    q_spec, kv_spec, qp_spec, kp_spec = _attn_specs(sq, sk, dh, dp, tb)

    def body(*refs):
        q_ref, k_ref, v_ref = refs[:3]
        qp_ref, kp_ref = (refs[3], refs[4]) if dp else (None, None)
        o_ref = refs[-1]
        row0 = pl.program_id(1) * tb
        for rows, ke in subs:
            o = _attn_tile(q_ref[rows, :], k_ref[:ke, :], v_ref[:ke, :], qp_ref[0, rows, :] if dp else None,
                           kp_ref[:ke, :] if dp else None, row0 + rows.start, scale, causal)
            o_ref[rows, :] = o.astype(o_ref.dtype)

    in_specs = [q_spec, kv_spec, kv_spec] + ([qp_spec, kp_spec] if dp else [])
    args = (q, k, v) + ((qp, kp) if dp else ())
    return _pcall(body, name=name + "_fwd", out_shape=jax.ShapeDtypeStruct((sq, heads * dh), BF16),
                  grid=(heads, sq // tb), in_specs=in_specs, out_specs=q_spec,
                  sem=("parallel", "parallel"))(*args)


def _attn_bwd_call(q, k, v, qp, kp, do, *, dh, scale, causal, tq, name):
    sq, sk = q.shape[0], k.shape[0]
    heads = q.shape[1] // dh
    dp = qp.shape[-1] if qp is not None else 0
    tb, subs = _attn_plan(sq, sk, tq, causal)
    q_spec, kv_spec, qp_spec, kp_spec = _attn_specs(sq, sk, dh, dp, tb)

    def body(*refs):
        h, i = pl.program_id(0), pl.program_id(1)
        if dp:
            q_ref, k_ref, v_ref, qp_ref, kp_ref, do_ref, dq_ref, dk_ref, dv_ref, dqp_ref, dkp_ref = refs
        else:
            q_ref, k_ref, v_ref, do_ref, dq_ref, dk_ref, dv_ref = refs

        @pl.when(i == 0)
        def _():
            dk_ref[...] = jnp.zeros_like(dk_ref)
            dv_ref[...] = jnp.zeros_like(dv_ref)

        if dp:
            @pl.when(jnp.logical_and(h == 0, i == 0))
            def _():
                dkp_ref[...] = jnp.zeros_like(dkp_ref)

        for rows, ke in subs:
            q0 = i * tb + rows.start
            if dp:
                prim = (q_ref[rows, :], k_ref[:ke, :], v_ref[:ke, :], qp_ref[0, rows, :], kp_ref[:ke, :])
                f = lambda a, b, c, d, e, q0=q0: _attn_tile(a, b, c, d, e, q0, scale, causal)
            else:
                prim = (q_ref[rows, :], k_ref[:ke, :], v_ref[:ke, :])
                f = lambda a, b, c, q0=q0: _attn_tile(a, b, c, None, None, q0, scale, causal)
            _, vjp = jax.vjp(f, *prim)
            grads = vjp(do_ref[rows, :].astype(F32))
            dq_ref[rows, :] = grads[0]
            dk_ref[:ke, :] += grads[1]
            dv_ref[:ke, :] += grads[2]
            if dp:
                dqp_ref[0, rows, :] = grads[3]
                dkp_ref[:ke, :] += grads[4]

    in_specs = [q_spec, kv_spec, kv_spec] + ([qp_spec, kp_spec] if dp else []) + [q_spec]
    out_shape = [jax.ShapeDtypeStruct(q.shape, F32), jax.ShapeDtypeStruct(k.shape, F32),
                 jax.ShapeDtypeStruct(v.shape, F32)]
    out_specs = [q_spec, kv_spec, kv_spec]
    if dp:
        out_shape += [jax.ShapeDtypeStruct(qp.shape, F32), jax.ShapeDtypeStruct(kp.shape, F32)]
        out_specs += [qp_spec, kp_spec]
    args = (q, k, v) + ((qp, kp) if dp else ()) + (do,)
    return _pcall(body, name=name + "_bwd", out_shape=out_shape, grid=(heads, sq // tb), in_specs=in_specs,
                  out_specs=out_specs, sem=("arbitrary", "arbitrary"))(*args)


def attention(q, k, v, qp=None, kp=None, *, dh, scale, causal, name, tq=256):
    tq = _tile(q.shape[0], tq)
    kw = dict(dh=dh, scale=scale, causal=causal, tq=tq, name=name)
    has_pe = qp is not None

    @jax.custom_vjp
    def op(*xs):
        return _attn_fwd_call(*xs, **kw) if has_pe else _attn_fwd_call(*xs, None, None, **kw)

    def op_fwd(*xs):
        return (_attn_fwd_call(*xs, **kw) if has_pe else _attn_fwd_call(*xs, None, None, **kw)), xs

    def op_bwd(xs, do):
        full = xs if has_pe else xs + (None, None)
        return tuple(_attn_bwd_call(*full, do, **kw))

    op.defvjp(op_fwd, op_bwd)
    return op(q, k, v, qp, kp) if has_pe else op(q, k, v)


def _rope_tables(positions, reps):
    half = MLA_ROPE // 2
    inv = ROPE_BASE ** (-jnp.arange(0, MLA_ROPE, 2, dtype=F32) / MLA_ROPE)
    ang = positions.astype(F32)[:, None] * inv
    cos, sin = jnp.cos(ang), jnp.sin(ang)
    c = jnp.tile(jnp.concatenate([cos, cos], axis=-1), (1, reps))
    s = jnp.tile(jnp.concatenate([sin, sin], axis=-1), (1, reps))
    rot = np.zeros((MLA_ROPE, MLA_ROPE), np.float32)
    for i in range(half):
        rot[i + half, i] = -1.0
        rot[i, i + half] = 1.0
    return c, s, jnp.asarray(np.kron(np.eye(reps, dtype=np.float32), rot))


def rope(x, positions, name):
    c, s, rot = _rope_tables(positions, x.shape[1] // MLA_ROPE)

    def fn(xt, ct, st, rt):
        return (xt * ct + _nn_hi_const(xt, rt) * st,)

    return tilewise(fn, name, [x, c, s, rot], ["tile", "tile", "tile", "whole"], [(x.shape[1], F32)],
                    axis=0, t=_tile(x.shape[0], 256), diff=[True, False, False, False])[0]


def _ffn_gate_tile(g, u, wg, wu, bg, bu):
    return _silu(_causal_dwconv_tile(g, wg) + bg) * (_causal_dwconv_tile(u, wu) + bu)


def _ffn_gate_specs(pre4, conv4):
    _, groups, s, nb = pre4.shape
    tc = LANE if nb % LANE == 0 else nb
    per = nb // tc
    at = lambda g, j: (0, g, 0, j)
    specs = dict(pre=pl.BlockSpec((2, None, s, tc), at), conv=pl.BlockSpec((2, None, conv4.shape[2], tc), at),
                 bias=pl.BlockSpec((2, None, 1, tc), at), act=pl.BlockSpec((s, tc), lambda g, j: (0, g * per + j)))
    return specs, (groups, per)


def _ffn_gate_fwd_call(pre4, conv4, bias4, name):
    specs, grid = _ffn_gate_specs(pre4, conv4)
    _, groups, s, nb = pre4.shape

    def body(pre_ref, conv_ref, bias_ref, act_ref):
        act_ref[...] = _ffn_gate_tile(pre_ref[0], pre_ref[1], conv_ref[0], conv_ref[1], bias_ref[0],
                                      bias_ref[1]).astype(act_ref.dtype)

    return _pcall(body, name=name + "_fwd", out_shape=jax.ShapeDtypeStruct((s, groups * nb), BF16), grid=grid,
                  in_specs=[specs["pre"], specs["conv"], specs["bias"]], out_specs=specs["act"],
                  sem=("parallel", "parallel"))(pre4, conv4, bias4)


def _ffn_gate_bwd_call(pre4, conv4, bias4, dact, name):
    specs, grid = _ffn_gate_specs(pre4, conv4)

    def body(pre_ref, conv_ref, bias_ref, dact_ref, dpre_ref, dconv_ref, dbias_ref):
        prim = (pre_ref[0], pre_ref[1], conv_ref[0], conv_ref[1], bias_ref[0], bias_ref[1])
        _, vjp = jax.vjp(_ffn_gate_tile, *prim)
        dg, du, dwg, dwu, dbg, dbu = vjp(dact_ref[...].astype(F32))
        dpre_ref[0] = dg.astype(dpre_ref.dtype)
        dpre_ref[1] = du.astype(dpre_ref.dtype)
        dconv_ref[0] = dwg
        dconv_ref[1] = dwu
        dbias_ref[0] = dbg
        dbias_ref[1] = dbu

    return _pcall(body, name=name + "_bwd",
                  out_shape=[jax.ShapeDtypeStruct(pre4.shape, _MXU_DTYPE), jax.ShapeDtypeStruct(conv4.shape, F32),
                             jax.ShapeDtypeStruct(bias4.shape, F32)],
                  grid=grid, in_specs=[specs["pre"], specs["conv"], specs["bias"], specs["act"]],
                  out_specs=[specs["pre"], specs["conv"], specs["bias"]], sem=("parallel", "parallel"))(
                      pre4, conv4, bias4, dact)


@functools.partial(jax.custom_vjp, nondiff_argnums=(4,))
def ffn_up_gate(hn, w3, conv3, bias, name):
    return _ffn_up_gate_fwd(hn, w3, conv3, bias, name)[0]


def _ffn_up_gate_fwd(hn, w3, conv3, bias, name):
    groups, _, nb = w3.shape
    half = groups // 2
    pre4 = _mm_groups_out(hn, w3, name=name + "_up_fwd").reshape(2, half, hn.shape[0], nb)
    conv4 = conv3.reshape(2, half, conv3.shape[1], nb)
    bias4 = bias.reshape(2, half, 1, nb)
    return _ffn_gate_fwd_call(pre4, conv4, bias4, name + "_gate"), (hn, w3, pre4, conv4, bias4)


def _ffn_up_gate_bwd(name, saved, dact):
    hn, w3, pre4, conv4, bias4 = saved
    dpre4, dconv4, dbias4 = _ffn_gate_bwd_call(pre4, conv4, bias4, dact, name + "_gate")
    dpre3 = dpre4.reshape((w3.shape[0],) + dpre4.shape[2:])
    dw3 = _mm_groups_out(hn, dpre3, ta=True, out_dtype=w3.dtype, name=name + "_up_dw")
    dhn = _mm_groups_contract(dpre3, w3, out_dtype=hn.dtype, name=name + "_up_da")
    return dhn, dw3, dconv4.reshape((w3.shape[0],) + dconv4.shape[2:]), dbias4.reshape(-1)


ffn_up_gate.defvjp(_ffn_up_gate_fwd, _ffn_up_gate_bwd)


def loss_rows(h, gain, target, name):
    def fn(ht, tt, gt):
        err = _rms_tile(ht, gt, F32) - tt
        return (0.5 * jnp.mean(err * err, axis=-1, keepdims=True),)

    return tilewise(fn, name, [h, target, gain.reshape(1, -1)], ["tile", "tile", "whole"], [(1, F32)],
                    axis=0, t=_tile(h.shape[0], 256), diff=[True, False, True])[0]


def _col_groups(w, per_head, lo, hi):
    r = w.shape[0]
    return w.reshape(r, -1, per_head)[:, :, lo:hi].reshape(r, -1)


def _layer_mix(h, lw, rp, l, positions):
    s, d = h.shape
    width = d // 2
    dn_heads = width // DN_HEAD_DIM
    mla_heads = (d - width) // MLA_V
    nm = f"l{l}_"
    w_in = lw["w_in"]
    c0 = 4 * width
    c1 = c0 + 2 * dn_heads
    c2 = c1 + MLA_Q_RANK
    c3 = c2 + MLA_KV_RANK + MLA_ROPE
    u = rms_norm(h, rp["norm_mix"], nm + "norm_mix", BF16)
    qkv_raw, z, rest = in_proj(u, w_in, (3 * width, width, c3 - c0), nm + "in")

    qkv = dn_qkv(qkv_raw, lw["dn_conv"], nm + "dn_qkv")
    csum, beta = dn_gates(rest[:, :dn_heads], rest[:, dn_heads:2 * dn_heads], rp["dn_a_log"],
                          rp["dn_dt_bias"], nm + "dn_gates")
    n = s // CHUNK
    g3 = csum.reshape(n, CHUNK, dn_heads).transpose(0, 2, 1)
    b3 = beta.reshape(n, CHUNK, dn_heads).transpose(0, 2, 1)
    o_dn = dn_core(qkv, g3[..., None], g3[:, :, None, :], g3[:, :, CHUNK - 1][..., None, None], b3[..., None],
                   nm + "dn_core")
    o_dn = dn_out_gate(o_dn, z, rp["dn_out_norm"], nm + "dn_gate")

    mq = rest[:, c1 - c0:c2 - c0]
    mkv = rest[:, c2 - c0:c3 - c0]
    qn = rms_norm(mq, rp["mla_q_norm"], nm + "mla_qnorm", BF16)
    per_q = MLA_NOPE + MLA_ROPE
    q_nope = linear(qn, _col_groups(lw["mla_w_qb"], per_q, 0, MLA_NOPE), nm + "mla_qn")
    q_pe = linear(qn, _col_groups(lw["mla_w_qb"], per_q, MLA_NOPE, per_q), nm + "mla_qp")
    kvn = rms_norm(mkv[:, :MLA_KV_RANK], rp["mla_kv_norm"], nm + "mla_kvnorm", BF16)
    per_kv = MLA_NOPE + MLA_V
    k_nope = linear(kvn, _col_groups(lw["mla_w_kvb"], per_kv, 0, MLA_NOPE), nm + "mla_kn")
    v_mla = linear(kvn, _col_groups(lw["mla_w_kvb"], per_kv, MLA_NOPE, per_kv), nm + "mla_v")
    q_pe = rope(q_pe, positions, nm + "rope_q")
    k_pe = rope(mkv[:, MLA_KV_RANK:], positions, nm + "rope_k")
    q_pe = q_pe.reshape(s, mla_heads, MLA_ROPE).transpose(1, 0, 2)
    o_mla = attention(q_nope, k_nope, v_mla, q_pe, k_pe, dh=MLA_NOPE, scale=per_q ** -0.5, causal=True,
                      name=nm + "mla_attn")

    h = linear(o_dn, lw["w_out"][:width], nm + "out_dn", res=h)
    return linear(o_mla, lw["w_out"][width:], nm + "out_mla", res=h)


def _layer_xattn(h, lw, rp, l, mem_n):
    d = h.shape[1]
    nm = f"l{l}_"
    hn = rms_norm(h, rp["norm_xattn"], nm + "norm_xattn", BF16)
    xq = linear(hn, lw["xa_wq"], nm + "xa_q")
    xk = linear(mem_n, lw["xa_wk"], nm + "xa_k")
    xv = linear(mem_n, lw["xa_wv"], nm + "xa_v")
    xdh = d // XA_HEADS
    xo = attention(xq, xk, xv, dh=xdh, scale=xdh ** -0.5, causal=False, name=nm + "xattn")
    return linear(xo, lw["xa_wo"], nm + "xa_o", res=h)


def _layer_ffn(h, lw, rp, l):
    nm = f"l{l}_"
    hn = rms_norm(h, rp["norm_ffn"], nm + "norm_ffn", BF16)
    act = ffn_up_gate(hn, lw["ffn_w_up"], lw["ffn_conv"], rp["ffn_conv_bias"], nm + "ffn")
    return linear(act, lw["ffn_w_down"], nm + "ffn_down", res=h)


_COL_SHARDED = ("w_in", "mla_w_qb", "mla_w_kvb", "ffn_w_up", "dn_conv", "ffn_conv")
_ROW_SHARDED = ("w_out", "xa_wq", "xa_wk", "xa_wv", "xa_wo", "ffn_w_down")
_BIG = ("w_in", "mla_w_qb", "mla_w_kvb", "w_out", "xa_wq", "xa_wk", "xa_wv", "xa_wo", "ffn_w_up", "ffn_w_down")
_SMALL_SHARDED = ("dn_conv", "ffn_conv")
_STAGES = (("_mix", ("w_in", "mla_w_qb", "mla_w_kvb", "w_out"),
            ("norm_mix", "dn_a_log", "dn_dt_bias", "dn_out_norm", "mla_q_norm", "mla_kv_norm"), "norm_mix"),
           ("_xattn", ("xa_wq", "xa_wk", "xa_wv", "xa_wo"), ("norm_xattn",), "norm_xattn"),
           ("_ffn", ("ffn_w_up", "ffn_w_down"), ("norm_ffn", "ffn_conv_bias"), "norm_ffn"))
_REPLICATED = ("norm_mix", "dn_a_log", "dn_dt_bias", "dn_out_norm", "mla_q_norm", "mla_kv_norm", "mem_norm",
               "norm_xattn", "norm_ffn", "ffn_conv_bias", "norm_final")
_WEIGHTS = ("norm_mix", "w_in", "dn_conv", "dn_a_log", "dn_dt_bias", "dn_out_norm", "mla_q_norm", "mla_w_qb",
            "mla_kv_norm", "mla_w_kvb", "w_out", "mem_norm", "norm_xattn", "xa_wq", "xa_wk", "xa_wv", "xa_wo",
            "norm_ffn", "ffn_w_up", "ffn_conv", "ffn_conv_bias", "ffn_w_down", "norm_final")


def _assemble(name, g):
    if name in _ROW_SHARDED:
        return g.reshape((-1,) + g.shape[2:])
    return jnp.moveaxis(g, 0, -2).reshape(g.shape[1:-1] + (-1,))


def _mix(h, gw, conv, rp, l, positions):
    lw = {k: _assemble(k, v) for k, v in gw.items()}
    lw["dn_conv"] = _assemble("dn_conv", conv)
    return _layer_mix(h, lw, rp, l, positions)


def _xattn(h, gw, rp, l, mem_n):
    return _layer_xattn(h, {k: _assemble(k, v) for k, v in gw.items()}, rp, l, mem_n)


def _ffn(h, gw, conv, rp, l):
    lw = {"ffn_w_up": gw["ffn_w_up"], "ffn_w_down": _assemble("ffn_w_down", gw["ffn_w_down"]), "ffn_conv": conv}
    return _layer_ffn(h, lw, rp, l)


def _my_index():
    return 4 * lax.axis_index("x") + 2 * lax.axis_index("y") + lax.axis_index("c")


def _peer(k):
    x, y, c = lax.axis_index("x"), lax.axis_index("y"), lax.axis_index("c")
    px = (1 - x) if k & 4 else x
    py = (1 - y) if k & 2 else y
    pc = (1 - c) if k & 1 else c
    return (px, py, pc), 4 * px + 2 * py + pc


_ANY = pl.BlockSpec(memory_space=pl.ANY)


def all_gather(shards, name):
    n = len(shards)

    def body(*refs):
        x_refs, o_refs = refs[:n], refs[n:2 * n]
        send_sems, recv_sems, local_sems = refs[2 * n:]
        me = _my_index()
        sib_id, sib = _peer(1)
        chips = [_peer(k) for k in (4, 2, 6)]

        def copy(a, k, block, to, src=None):
            return pltpu.make_async_remote_copy(
                src_ref=o_refs[a].at[block] if src is None else src, dst_ref=o_refs[a].at[block],
                send_sem=send_sems.at[a * 7 + k], recv_sem=recv_sems.at[a * 7 + k],
                device_id=to, device_id_type=MESH)

        mine = [pltpu.make_async_copy(x_refs[a], o_refs[a].at[me], local_sems.at[a]) for a in range(n)]
        for cp in mine:
            cp.start()
        first = []
        for a in range(n):
            first.append(copy(a, 0, me, sib_id, src=x_refs[a]))
            first += [copy(a, 1 + j, me, cid, src=x_refs[a]) for j, (cid, _) in enumerate(chips)]
        for cp in first:
            cp.start()
        passed = []
        for a in range(n):
            for j, (cid, cidx) in enumerate(chips):
                copy(a, 1 + j, cidx, cid).wait_recv()
                fwd = copy(a, 4 + j, cidx, sib_id)
                fwd.start()
                passed.append(fwd)
        for a in range(n):
            copy(a, 0, sib, sib_id).wait_recv()
            for j, (_, cidx) in enumerate(chips):
                copy(a, 4 + j, cidx ^ 1, sib_id).wait_recv()
        for cp in first + passed:
            cp.wait_send()
        for cp in mine:
            cp.wait()

    out_shape = [jax.ShapeDtypeStruct((N_DEV,) + s.shape, s.dtype) for s in shards]
    return _pcall(body, name=name, out_shape=out_shape, in_specs=[_ANY] * n, out_specs=[_ANY] * n,
                  scratch=[pltpu.SemaphoreType.DMA((7 * n,)), pltpu.SemaphoreType.DMA((7 * n,)),
                           pltpu.SemaphoreType.DMA((n,))])(*shards)


_HBM = pl.BlockSpec(memory_space=pltpu.HBM)
_SEM = pl.BlockSpec(memory_space=pltpu.SEMAPHORE)
_EFFECT = pltpu.SideEffectType.DATAFLOW_SIDE_EFFECTING


def _plan_gather_first(n):
    def plan(refs):
        me = _my_index()
        sib_id, sib = _peer(1)
        chips = [_peer(k) for k in (4, 2, 6)]
        out = []
        for a in range(n):
            x, land = refs[a], refs[n + a]
            out.append((x, land.at[me], sib_id, land.at[sib]))
            out += [(x, land.at[me], cid, land.at[cidx]) for cid, cidx in chips]
        return out

    return plan, 4 * n


def _plan_gather_pass(n):
    def plan(refs):
        sib_id, _ = _peer(1)
        chips = [_peer(k) for k in (4, 2, 6)]
        return [(refs[a].at[cidx], refs[a].at[cidx], sib_id, refs[a].at[cidx ^ 1])
                for a in range(n) for _, cidx in chips]

    return plan, 3 * n


def _plan_gather_direct(n):
    def plan(refs):
        me = _my_index()
        out = []
        for a in range(n):
            for k in range(1, N_DEV):
                pid, pidx = _peer(k)
                out.append((refs[a], refs[n + a].at[me], pid, refs[n + a].at[pidx]))
        return out

    return plan, 7 * n


def _plan_scatter(n):
    def plan(refs):
        me = _my_index()
        out = []
        for a in range(n):
            for k in range(1, N_DEV):
                pid, pidx = _peer(k)
                out.append((refs[a].at[pidx], refs[n + a].at[me], pid, refs[n + a].at[pidx]))
        return out

    return plan, 7 * n


def _remote_copy(src, dst, send_sems, recv_sems, i, dev):
    return pltpu.make_async_remote_copy(src_ref=src, dst_ref=dst, send_sem=send_sems.at[i], recv_sem=recv_sems.at[i],
                                        device_id=dev, device_id_type=MESH)


def _after(x, tokens):
    for token in tokens:
        x = x + token[0, 0]
    return x


def exchange_start(bufs, plan_n, name, after=None):
    plan, n = plan_n
    nb = len(bufs)
    n_in = nb + (after is not None)

    def body(*refs):
        send_sems, recv_sems = refs[n_in], refs[n_in + 1]
        for i, (src, dst, dev, _) in enumerate(plan(refs[:nb])):
            _remote_copy(src, dst, send_sems, recv_sems, i, dev).start()
        refs[-1][...] = jnp.zeros_like(refs[-1])

    out_shape = (pltpu.SemaphoreType.DMA((n,)), pltpu.SemaphoreType.DMA((n,)),
                 *[pltpu.HBM(b.shape, b.dtype) for b in bufs], jax.ShapeDtypeStruct((8, LANE), F32))
    args = [pltpu.with_memory_space_constraint(b, pltpu.HBM) for b in bufs] + ([after] if after is not None else [])
    res = pl.pallas_call(
        body, name=name, out_shape=out_shape,
        in_specs=[_HBM] * nb + ([_ANY] if after is not None else []),
        out_specs=(_SEM, _SEM, *[_HBM] * nb, pl.BlockSpec(memory_space=pltpu.VMEM)),
        input_output_aliases={i: 2 + i for i in range(nb)},
        compiler_params=pltpu.CompilerParams(has_side_effects=_EFFECT), interpret=_INTERPRET)(*args)
    return (res[0], res[1], list(res[2:2 + nb])), res[-1]


def exchange_wait(flight, plan_n, name, after):
    plan, _ = plan_n
    send_sems, recv_sems, bufs = flight
    nb = len(bufs)

    def body(*refs):
        s_sems, r_sems = refs[nb], refs[nb + 1]
        for i, (src, _, dev, arrival) in enumerate(plan(refs[:nb])):
            cp = _remote_copy(src, arrival, s_sems, r_sems, i, dev)
            cp.wait_send()
            cp.wait_recv()

    return list(pl.pallas_call(
        body, name=name, out_shape=tuple(pltpu.HBM(b.shape, b.dtype) for b in bufs),
        in_specs=[_HBM] * nb + [_SEM, _SEM, _ANY], out_specs=tuple([_HBM] * nb),
        input_output_aliases={i: i for i in range(nb)},
        compiler_params=pltpu.CompilerParams(has_side_effects=_EFFECT), interpret=_INTERPRET)(
            *bufs, send_sems, recv_sems, after))


def _adam_tile(contribs, w, m, v):
    g = contribs[0].astype(F32)
    for part in contribs[1:]:
        g = g + part.astype(F32)
    m_new = ADAM_B1 * m + (1.0 - ADAM_B1) * g
    v_new = ADAM_B2 * v + (1.0 - ADAM_B2) * (g * g)
    m_hat = m_new / (1.0 - ADAM_B1 ** ADAM_STEP)
    v_hat = v_new / (1.0 - ADAM_B2 ** ADAM_STEP)
    return g, -ADAM_LR * (m_hat / (jnp.sqrt(v_hat) + ADAM_EPS) + ADAM_WD * w), m_new, v_new


def adam_update(recv, w, m, v, name):
    ll, _, r, c = recv.shape
    tr = _tile(r, max(8, 1 << ((_ADAM_TILE_ELEMS // c).bit_length() - 1)))

    def body(g_ref, w_ref, m_ref, v_ref, go_ref, d_ref, mo_ref, vo_ref):
        go_ref[0], d_ref[0], mo_ref[0], vo_ref[0] = _adam_tile([g_ref[0, j] for j in range(N_DEV)], w_ref[0],
                                                               m_ref[0], v_ref[0])

    spec = pl.BlockSpec((1, tr, c), lambda l, i: (l, i, 0))
    sds = jax.ShapeDtypeStruct((ll, r, c), F32)
    return _pcall(body, name=name, out_shape=[sds] * 4, grid=(ll, r // tr),
                  in_specs=[pl.BlockSpec((1, N_DEV, tr, c), lambda l, i: (l, 0, i, 0)), spec, spec, spec],
                  out_specs=[spec] * 4, sem=("parallel", "parallel"))(recv, w, m, v)


def adam_layer(recv, w, m, v, prev, l, name):
    _, r, c = recv.shape
    tr = _tile(r, max(8, 1 << ((_ADAM_TILE_ELEMS // c).bit_length() - 1)))
    if prev is None:
        prev = [lax.empty(w.shape, F32) for _ in range(4)]

    def body(g_ref, w_ref, m_ref, v_ref, *rest):
        go_ref, d_ref, mo_ref, vo_ref = rest[4:]
        go_ref[0], d_ref[0], mo_ref[0], vo_ref[0] = _adam_tile([g_ref[j] for j in range(N_DEV)], w_ref[0], m_ref[0],
                                                               v_ref[0])

    spec = pl.BlockSpec((1, tr, c), lambda i: (l, i, 0))
    return _pcall(body, name=name, out_shape=[jax.ShapeDtypeStruct(w.shape, F32)] * 4, grid=(r // tr,),
                  in_specs=[pl.BlockSpec((N_DEV, tr, c), lambda i: (0, i, 0)), spec, spec, spec] + [_ANY] * 4,
                  out_specs=[spec] * 4, sem=("parallel",),
                  input_output_aliases={4 + j: j for j in range(4)})(recv, w, m, v, *prev)


def _as3d(a):
    if a.ndim == 1:
        return a.reshape(1, 1, -1)
    if a.ndim == 2:
        return a.reshape(1, a.shape[0], a.shape[1])
    return a.reshape(a.shape[0], -1, a.shape[-1])


def kernel(x, mem, positions, norm_mix, w_in, dn_conv, dn_a_log, dn_dt_bias, dn_out_norm, mla_q_norm, mla_w_qb, mla_kv_norm, mla_w_kvb, w_out, mem_norm, norm_xattn, xa_wq, xa_wk, xa_wv, xa_wo, norm_ffn, ffn_w_up, ffn_conv, ffn_conv_bias, ffn_w_down, norm_final, loss_target, m_norm_mix, m_w_in, m_dn_conv, m_dn_a_log, m_dn_dt_bias, m_dn_out_norm, m_mla_q_norm, m_mla_w_qb, m_mla_kv_norm, m_mla_w_kvb, m_w_out, m_mem_norm, m_norm_xattn, m_xa_wq, m_xa_wk, m_xa_wv, m_xa_wo, m_norm_ffn, m_ffn_w_up, m_ffn_conv, m_ffn_conv_bias, m_ffn_w_down, m_norm_final, v_norm_mix, v_w_in, v_dn_conv, v_dn_a_log, v_dn_dt_bias, v_dn_out_norm, v_mla_q_norm, v_mla_w_qb, v_mla_kv_norm, v_mla_w_kvb, v_w_out, v_mem_norm, v_norm_xattn, v_xa_wq, v_xa_wk, v_xa_wv, v_xa_wo, v_norm_ffn, v_ffn_w_up, v_ffn_conv, v_ffn_conv_bias, v_ffn_w_down, v_norm_final):
    w = dict(norm_mix=norm_mix, w_in=w_in, dn_conv=dn_conv, dn_a_log=dn_a_log, dn_dt_bias=dn_dt_bias,
             dn_out_norm=dn_out_norm, mla_q_norm=mla_q_norm, mla_w_qb=mla_w_qb, mla_kv_norm=mla_kv_norm,
             mla_w_kvb=mla_w_kvb, w_out=w_out, mem_norm=mem_norm, norm_xattn=norm_xattn, xa_wq=xa_wq, xa_wk=xa_wk,
             xa_wv=xa_wv, xa_wo=xa_wo, norm_ffn=norm_ffn, ffn_w_up=ffn_w_up, ffn_conv=ffn_conv,
             ffn_conv_bias=ffn_conv_bias, ffn_w_down=ffn_w_down, norm_final=norm_final)
    mom = dict(norm_mix=m_norm_mix, w_in=m_w_in, dn_conv=m_dn_conv, dn_a_log=m_dn_a_log, dn_dt_bias=m_dn_dt_bias,
               dn_out_norm=m_dn_out_norm, mla_q_norm=m_mla_q_norm, mla_w_qb=m_mla_w_qb, mla_kv_norm=m_mla_kv_norm,
               mla_w_kvb=m_mla_w_kvb, w_out=m_w_out, mem_norm=m_mem_norm, norm_xattn=m_norm_xattn, xa_wq=m_xa_wq,
               xa_wk=m_xa_wk, xa_wv=m_xa_wv, xa_wo=m_xa_wo, norm_ffn=m_norm_ffn, ffn_w_up=m_ffn_w_up,
               ffn_conv=m_ffn_conv, ffn_conv_bias=m_ffn_conv_bias, ffn_w_down=m_ffn_w_down, norm_final=m_norm_final)
    var = dict(norm_mix=v_norm_mix, w_in=v_w_in, dn_conv=v_dn_conv, dn_a_log=v_dn_a_log, dn_dt_bias=v_dn_dt_bias,
               dn_out_norm=v_dn_out_norm, mla_q_norm=v_mla_q_norm, mla_w_qb=v_mla_w_qb, mla_kv_norm=v_mla_kv_norm,
               mla_w_kvb=v_mla_w_kvb, w_out=v_w_out, mem_norm=v_mem_norm, norm_xattn=v_norm_xattn, xa_wq=v_xa_wq,
               xa_wk=v_xa_wk, xa_wv=v_xa_wv, xa_wo=v_xa_wo, norm_ffn=v_norm_ffn, ffn_w_up=v_ffn_w_up,
               ffn_conv=v_ffn_conv, ffn_conv_bias=v_ffn_conv_bias, ffn_w_down=v_ffn_w_down, norm_final=v_norm_final)
    depth = w_in.shape[0]
    me = _my_index()
    rep = {k: w[k] for k in _REPLICATED}
    x0, mem0, pos0, target0 = x[0], mem[0], positions[0], loss_target[0]

    def with_own_block(block):
        return lax.dynamic_update_index_in_dim(lax.empty((N_DEV,) + block.shape, block.dtype), block, me, 0)

    def gather_start(l, keys, tag, after):
        shards = [w[k][l].astype(BF16) for k in keys]
        plan = _plan_gather_first(len(keys))
        flight, token = exchange_start(shards + [with_own_block(sh) for sh in shards], plan,
                                       f"gather_l{l}{tag}_first_start", after)
        return (flight, plan, keys, f"gather_l{l}{tag}"), token

    def gather_pass(state, after):
        flight, plan, keys, name = state
        lands = exchange_wait(flight, plan, name + "_first_wait", after)[len(keys):]
        plan = _plan_gather_pass(len(keys))
        flight, token = exchange_start(lands, plan, name + "_pass_start")
        return (flight, plan, keys, name), token

    def gather_finish(state, after):
        flight, plan, keys, name = state
        return dict(zip(keys, exchange_wait(flight, plan, name + "_pass_wait", after)))

    small = dict(zip(_SMALL_SHARDED, all_gather([w[k] for k in _SMALL_SHARDED], "gather_small")))
    mem_n, vjp_mem = jax.vjp(lambda g: rms_norm(mem0, g, "mem_norm", BF16), rep["mem_norm"])

    n_kind = len(_STAGES)
    groups = [(l, keys, tag) for l in range(depth) for tag, keys, _, _ in _STAGES]
    n_stage = len(groups)
    states, ready = {}, {}
    token = small["dn_conv"]
    for s in range(min(3, n_stage)):
        states[s], token = gather_start(*groups[s], token)
    states[0], token = gather_pass(states[0], token)
    ready[0] = gather_finish(states[0], token)

    h = x0
    tapes = []
    for s in range(n_stage):
        l, kind = divmod(s, n_kind)
        tokens = []
        if s >= 2 and s + 1 < n_stage:
            states[s + 1], token = gather_pass(states[s + 1], h)
            tokens.append(token)
        if s + 3 < n_stage:
            states[s + 3], token = gather_start(*groups[s + 3], h)
            tokens.append(token)
        _, _, rep_keys, gain = _STAGES[kind]
        rp = {k: rep[k][l] for k in rep_keys}
        rp[gain] = _after(rp[gain], tokens)
        if kind == 0:
            h_new, tape = jax.vjp(lambda hh, gw, cv, rr: _mix(hh, gw, cv, rr, l, pos0), h, ready.pop(s),
                                  small["dn_conv"][:, l], rp)
        elif kind == 1:
            h_new, tape = jax.vjp(lambda hh, gw, rr, mn: _xattn(hh, gw, rr, l, mn), h, ready.pop(s), rp, mem_n)
        else:
            h_new, tape = jax.vjp(lambda hh, gw, cv, rr: _ffn(hh, gw, cv, rr, l), h, ready.pop(s),
                                  small["ffn_conv"][:, l], rp)
        if s + 1 < n_stage:
            if s < 2:
                states[s + 1], token = gather_pass(states[s + 1], h_new)
                ready[s + 1] = gather_finish(states[s + 1], token)
            else:
                ready[s + 1] = gather_finish(states[s + 1], h_new)
        tapes.append(tape)
        h = h_new

    rows, vjp_loss = jax.vjp(lambda hh, g: loss_rows(hh, g, target0, "loss"), h, rep["norm_final"])
    loss = lax.psum(jnp.sum(rows), ("x", "y", "c"))
    dh, d_norm_final = vjp_loss(jnp.ones_like(rows))

    d_rep = {k: [None] * depth for _, _, rep_keys, _ in _STAGES for k in rep_keys}
    d_conv = {k: [None] * depth for k in _SMALL_SHARDED}
    recv_big = [dict() for _ in range(depth)]
    d_mem_n = None

    def scatter_start(l, grads, keys, tag):
        parts = [grads[k] for k in keys]
        own = [with_own_block(lax.dynamic_index_in_dim(p, me, 0, keepdims=False)) for p in parts]
        plan = _plan_scatter(len(keys))
        flight, token = exchange_start(parts + own, plan, f"scatter_l{l}{tag}_start")
        return (flight, plan, keys, l, f"scatter_l{l}{tag}_wait"), token

    def scatter_finish(state, after):
        flight, plan, keys, l, name = state
        recv_big[l].update(zip(keys, exchange_wait(flight, plan, name, after)[len(keys):]))

    pending = {}
    token = None
    for s in reversed(range(n_stage)):
        l, keys, tag = groups[s]
        if token is not None:
            dh = _after(dh, [token])
        if s % n_kind == 0:
            dh, dg, d_conv["dn_conv"][l], d_rp = tapes[s](dh)
        elif s % n_kind == 1:
            dh, dg, d_rp, d_mn = tapes[s](dh)
            d_mem_n = d_mn if d_mem_n is None else d_mem_n + d_mn
        else:
            dh, dg, d_conv["ffn_conv"][l], d_rp = tapes[s](dh)
        for k, v in d_rp.items():
            d_rep[k][l] = v
        if s + n_kind in pending:
            scatter_finish(pending.pop(s + n_kind), dh)
        pending[s], token = scatter_start(l, dg, keys, tag)

    (d_mem_norm,) = vjp_mem(d_mem_n)
    d_rep_all = {k: jnp.stack(v) for k, v in d_rep.items()}
    d_rep_all["mem_norm"] = d_mem_norm
    d_rep_all["norm_final"] = _after(d_norm_final, [token])
    parts = [jnp.stack(d_conv[k], axis=1) for k in _SMALL_SHARDED]
    own = [with_own_block(lax.dynamic_index_in_dim(p, me, 0, keepdims=False)) for p in parts]
    plan_small = _plan_scatter(len(parts))
    flight_small, token = exchange_start(parts + own, plan_small, "scatter_small_start")
    blocks = [_as3d(d_rep_all[k])[0] for k in _REPLICATED]
    blocks[0] = _after(blocks[0], [token])
    plan_rep = _plan_gather_direct(len(blocks))
    flight_rep, token = exchange_start(blocks + [with_own_block(b) for b in blocks], plan_rep, "gather_rep_start")

    big = {k: None for k in _BIG}

    def update_big(l, keys):
        for k in keys:
            recv = recv_big[l][k]
            big[k] = adam_layer(recv.reshape((N_DEV, -1, recv.shape[-1])), _as3d(w[k]), _as3d(mom[k]),
                                _as3d(var[k]), big[k], l, f"adam_{k}_l{l}")

    for l in reversed(range(1, depth)):
        update_big(l, _BIG)
    after = token if depth == 1 else big[_BIG[-1]][0]
    for s in sorted(pending, reverse=True):
        scatter_finish(pending[s], after)
        update_big(0, groups[s][1])
        after = big[groups[s][1][-1]][0]
    out = {}
    recv_small = exchange_wait(flight_small, plan_small, "scatter_small_wait", after)[len(parts):]
    for k, recv in zip(_SMALL_SHARDED, recv_small):
        r3 = _as3d(w[k])
        out[k] = adam_update(recv.reshape((1, N_DEV, -1, recv.shape[-1])), r3.reshape((1, -1, r3.shape[-1])),
                             _as3d(mom[k]).reshape((1, -1, r3.shape[-1])), _as3d(var[k]).reshape((1, -1, r3.shape[-1])),
                             "adam_" + k)
    recv_rep = exchange_wait(flight_rep, plan_rep, "gather_rep_wait", after)[len(blocks):]
    for k, recv in zip(_REPLICATED, recv_rep):
        r3 = _as3d(w[k])
        flat = lambda a, r3=r3: a.reshape((1, -1, r3.shape[-1]))
        out[k] = adam_update(recv[None], flat(w[k]), flat(mom[k]), flat(var[k]), "adam_" + k)
    out.update(big)

    res = [loss, dh[None]]
    for j in range(4):
        res += [out[k][j].reshape(w[k].shape) for k in _WEIGHTS]
    return tuple(res)
```
